```python
import jax
import jax.numpy as jnp
from jax import lax
import numpy as np

D_MODEL = 1024
BATCH = 2
SEQ = 8192
DEPTH = 1

MEM_LEN = 256
NORM_EPS = 1e-6

A_GROUPS = ((128, 1), (512, 4), (2048, 16))
A_N_GROUPS = len(A_GROUPS)
A_HEADS = 4
A_HEAD_DIM = 128
A_WIDTH = A_HEADS * A_HEAD_DIM
BLOCK = 128

R_HEAD = 64
R_HEADS = 8
R_WIDTH = R_HEADS * R_HEAD
R_DECAY_LORA = 64
R_AAA_LORA = 64
R_GATE_LORA = 128
R_GN_EPS = 64e-5

C_HEADS = 4
C_HEAD_DIM = 128
C_WIDTH = C_HEADS * C_HEAD_DIM

ROPE_THETA = 500000.0
ROT_DIM = A_HEAD_DIM // 4

N_BRANCH = 3
BRANCH_WIDTH = 512

N_EXPERTS = 32
TOP_K = 4
D_FF = D_MODEL
SWIGLU_ALPHA = 1.702
SWIGLU_LIMIT = 7.0
MOE_ROWS = 256

A_QKV_COLS = A_N_GROUPS * A_WIDTH
R_COLS = 3 * R_WIDTH + R_DECAY_LORA + R_AAA_LORA + R_GATE_LORA
R_SPLITS = (R_WIDTH, 2 * R_WIDTH, 3 * R_WIDTH, 3 * R_WIDTH + R_DECAY_LORA,
            3 * R_WIDTH + R_DECAY_LORA + R_AAA_LORA)
GATE_COLS = N_BRANCH * D_MODEL
OFF_AQ = 0
OFF_AK = OFF_AQ + A_QKV_COLS
OFF_AV = OFF_AK + A_QKV_COLS
OFF_R = OFF_AV + A_QKV_COLS
OFF_CQ = OFF_R + R_COLS
OFF_GATE = OFF_CQ + C_WIDTH
IN_COLS = OFF_GATE + GATE_COLS

kernel_name = 'hybrid_dilated_rwkv7_memxattn_moe_layer'


def rms_norm(t, gain, eps=NORM_EPS):
    tf = t.astype(jnp.float32)
    y = tf * lax.rsqrt(jnp.mean(tf * tf, axis=-1, keepdims=True) + eps)
    return (y * gain.astype(jnp.float32)).astype(t.dtype)


def partial_rotary(t, positions):
    half = ROT_DIM // 2
    inv_freq = ROPE_THETA ** (-jnp.arange(half, dtype=jnp.float32) / half)
    ang = positions.astype(jnp.float32)[..., None] * inv_freq
    ang = ang.reshape(ang.shape[:2] + (1,) * (t.ndim - 3) + (half,))
    cos, sin = jnp.cos(ang), jnp.sin(ang)
    tf = t.astype(jnp.float32)
    t1, t2, rest = tf[..., :half], tf[..., half:ROT_DIM], tf[..., ROT_DIM:]
    out = jnp.concatenate([t1 * cos - t2 * sin, t2 * cos + t1 * sin, rest], axis=-1)
    return out.astype(t.dtype)


def banded_causal_attention(q, k, v, span):
    n, length, heads, dh = q.shape
    n_blk = -(-length // BLOCK)
    pad = n_blk * BLOCK - length
    qp = jnp.pad(q, ((0, 0), (0, pad), (0, 0), (0, 0)))
    kp = jnp.pad(k, ((0, 0), (span, pad), (0, 0), (0, 0)))
    vp = jnp.pad(v, ((0, 0), (span, pad), (0, 0), (0, 0)))
    scale = dh ** -0.5
    qi = jnp.arange(BLOCK)[:, None]
    ki = jnp.arange(BLOCK + span)[None, :]
    rel = qi + span - ki

    def one_block(blk):
        s0 = blk * BLOCK
        qb = lax.dynamic_slice_in_dim(qp, s0, BLOCK, axis=1)
        kb = lax.dynamic_slice_in_dim(kp, s0, BLOCK + span, axis=1)
        vb = lax.dynamic_slice_in_dim(vp, s0, BLOCK + span, axis=1)
        s = jnp.einsum('nqhd,nkhd->nhqk', qb, kb, preferred_element_type=jnp.float32) * scale
        valid = (rel >= 0) & (rel <= span) & (s0 - span + ki >= 0)
        s = jnp.where(valid, s, -jnp.inf)
        m = jnp.max(s, axis=-1, keepdims=True)
        p = jnp.exp(s - m)
        denom = jnp.sum(p, axis=-1, keepdims=True)
        o = jnp.einsum('nhqk,nkhd->nqhd', p, vb.astype(jnp.float32))
        o = o / jnp.transpose(denom, (0, 2, 1, 3))
        lse = jnp.swapaxes((m + jnp.log(denom))[..., 0], 1, 2)
        return o.astype(q.dtype), lse

    o, lse = lax.map(one_block, jnp.arange(n_blk))
    o = jnp.moveaxis(o, 0, 1).reshape(n, n_blk * BLOCK, heads, dh)[:, :length]
    lse = jnp.moveaxis(lse, 0, 1).reshape(n, n_blk * BLOCK, heads)[:, :length]
    return o, lse


def dilated_attention(q, k, v, window, dilation):
    b, s, h, dh = q.shape
    sub = s // dilation

    def to_residues(t):
        return t.reshape(b, sub, dilation, h, dh).transpose(0, 2, 1, 3, 4).reshape(b * dilation, sub, h, dh)

    o, lse = banded_causal_attention(to_residues(q), to_residues(k), to_residues(v), window // dilation)
    o = o.reshape(b, dilation, sub, h, dh).transpose(0, 2, 1, 3, 4).reshape(b, s, h, dh)
    lse = lse.reshape(b, dilation, sub, h).transpose(0, 2, 1, 3).reshape(b, s, h)
    return o, lse


def dilated_mixture_attention(zq, zk, zv, q_gain, k_gain, positions):
    b, s, _ = zq.shape
    shape = (b, s, A_N_GROUPS, A_HEADS, A_HEAD_DIM)
    q = partial_rotary(rms_norm(zq.reshape(shape), q_gain), positions)
    k = partial_rotary(rms_norm(zk.reshape(shape), k_gain), positions)
    v = zv.reshape(shape)
    outs, lses = [], []
    for g, (window, dilation) in enumerate(A_GROUPS):
        o, l = dilated_attention(q[:, :, g], k[:, :, g], v[:, :, g], window, dilation)
        outs.append(o)
        lses.append(l)
    alpha = jax.nn.softmax(jnp.stack(lses, axis=0), axis=0)
    out = jnp.einsum('gbsh,gbshd->bshd', alpha, jnp.stack(outs, axis=0).astype(jnp.float32))
    return out.reshape(b, s, A_WIDTH).astype(zq.dtype)


def rwkv7_step(state, inp):
    r_t, w_t, k_t, v_t, a_t, b_t = inp
    sa = jnp.einsum('bhvk,bhk->bhv', state, a_t)
    state = state * w_t[:, :, None, :] + sa[..., None] * b_t[:, :, None, :] + v_t[..., None] * k_t[:, :, None, :]
    y = jnp.einsum('bhvk,bhk->bhv', state, r_t)
    return state, y


def rwkv7_time_mix(zr, mu, w0, w2, a0, a2, g2, k_k, k_a, r_k, ln_w, ln_b):
    b, s, _ = zr.shape
    f32 = jnp.float32
    prev = jnp.pad(zr, ((0, 0), (1, 0), (0, 0)))[:, :-1]
    zr = zr + (prev - zr) * mu
    r, k, v, w_lo, a_lo, g_lo = jnp.split(zr, R_SPLITS, axis=-1)
    w = -jax.nn.softplus(-(w0 + jnp.tanh(w_lo) @ w2)) - 0.5
    decay = jnp.exp(-jnp.exp(w.astype(f32)))
    a = jax.nn.sigmoid(a0 + a_lo @ a2)
    g = jax.nn.sigmoid(g_lo) @ g2
    heads = lambda t: t.reshape(b, s, R_HEADS, R_HEAD).astype(f32)
    kk = heads(k * k_k)
    kk = kk / jnp.maximum(jnp.sqrt(jnp.sum(kk * kk, axis=-1, keepdims=True)), 1e-12)
    k = k * (1.0 + (a - 1.0) * k_a)
    r_h, k_h, v_h, a_h, w_h = heads(r), heads(k), heads(v), heads(a), heads(decay)
    seq_first = lambda t: jnp.moveaxis(t, 1, 0)
    xs = (seq_first(r_h), seq_first(w_h), seq_first(k_h), seq_first(v_h), seq_first(-kk), seq_first(kk * a_h))
    state0 = jnp.zeros((b, R_HEADS, R_HEAD, R_HEAD), f32)
    _, y = lax.scan(rwkv7_step, state0, xs)
    y = jnp.moveaxis(y, 0, 1)
    mean = jnp.mean(y, axis=-1, keepdims=True)
    var = jnp.mean(jnp.square(y - mean), axis=-1, keepdims=True)
    y = ((y - mean) * lax.rsqrt(var + R_GN_EPS)).reshape(b, s, R_WIDTH)
    y = y * ln_w.astype(f32) + ln_b.astype(f32)
    bonus = jnp.sum(r_h * k_h * r_k.astype(f32), axis=-1, keepdims=True) * v_h
    y = (y + bonus.reshape(b, s, R_WIDTH)) * g.astype(f32)
    return y.astype(zr.dtype)


def memory_cross_attention(zq, mem_n, w_mem_kv, q_gain, k_gain):
    b, s, _ = zq.shape
    m = mem_n.shape[1]
    q = rms_norm(zq.reshape(b, s, C_HEADS, C_HEAD_DIM), q_gain)
    k, v = jnp.split(mem_n @ w_mem_kv, 2, axis=-1)
    k = rms_norm(k.reshape(b, m, C_HEADS, C_HEAD_DIM), k_gain)
    v = v.reshape(b, m, C_HEADS, C_HEAD_DIM)
    sc = jnp.einsum('bshd,bmhd->bhsm', q, k, preferred_element_type=jnp.float32) * (C_HEAD_DIM ** -0.5)
    p = jax.nn.softmax(sc, axis=-1)
    o = jnp.einsum('bhsm,bmhd->bshd', p, v.astype(jnp.float32))
    return o.reshape(b, s, C_WIDTH).astype(zq.dtype)


def moe_ffn(h, router_w, router_b, w1, b1, w2, b2):
    b, s, d = h.shape
    n_tok = b * s
    n_assign = n_tok * TOP_K
    xt = h.reshape(n_tok, d)
    logits = (xt @ router_w + router_b).astype(jnp.float32)
    top_val, top_idx = lax.top_k(logits, TOP_K)
    gate = jax.nn.softmax(top_val, axis=-1)
    flat_e = top_idx.reshape(-1)
    flat_g = gate.reshape(-1)
    flat_tok = jnp.arange(n_assign, dtype=jnp.int32) // TOP_K
    order = jnp.argsort(flat_e)
    sorted_e = flat_e[order]
    counts = jnp.bincount(flat_e, length=N_EXPERTS)
    padded = (counts + MOE_ROWS - 1) // MOE_ROWS * MOE_ROWS
    pad_end = jnp.cumsum(padded)
    pad_start = pad_end - padded
    raw_start = jnp.cumsum(counts) - counts
    dest = pad_start[sorted_e] + jnp.arange(n_assign, dtype=jnp.int32) - raw_start[sorted_e]
    n_blocks = -(-n_assign // MOE_ROWS) + N_EXPERTS
    n_rows = n_blocks * MOE_ROWS
    row_tok = jnp.full((n_rows,), n_tok, jnp.int32).at[dest].set(flat_tok[order])
    row_gate = jnp.zeros((n_rows,), jnp.float32).at[dest].set(flat_g[order])
    block_expert = jnp.minimum(
        jnp.searchsorted(pad_end, jnp.arange(n_blocks, dtype=jnp.int32) * MOE_ROWS, side='right'),
        N_EXPERTS - 1)
    x_pad = jnp.concatenate([xt, jnp.zeros((1, d), xt.dtype)], axis=0)

    def expert_block(args):
        tok, e = args
        u = x_pad[tok] @ w1[e] + b1[e]
        glu = jnp.minimum(u[:, :D_FF], SWIGLU_LIMIT)
        lin = jnp.clip(u[:, D_FF:], -SWIGLU_LIMIT, SWIGLU_LIMIT)
        act = glu * jax.nn.sigmoid(SWIGLU_ALPHA * glu) * (lin + 1.0)
        return act @ w2[e] + b2[e]

    y_rows = lax.map(expert_block, (row_tok.reshape(n_blocks, MOE_ROWS), block_expert))
    y_rows = y_rows.reshape(n_rows, d).astype(jnp.float32) * row_gate[:, None]
    y = jax.ops.segment_sum(y_rows, row_tok, num_segments=n_tok + 1)[:n_tok]
    return y.reshape(b, s, d).astype(h.dtype)


def setup_inputs(seed: int = 0) -> dict:
    key = jax.random.key(seed)
    ks = iter(jax.random.split(key, 40))
    L, D = DEPTH, D_MODEL

    def nrm(shape, scale):
        return scale * jax.random.normal(next(ks), shape, jnp.float32)

    x = nrm((BATCH, SEQ, D), 1.0)
    mem = nrm((BATCH, MEM_LEN, D), 1.0)
    positions = (jnp.arange(SEQ, dtype=jnp.int32)[None, :]
                 + jax.random.randint(next(ks), (BATCH, 1), 0, 4096, dtype=jnp.int32))
    norm_mix = 1.0 + nrm((L, D), 0.02)
    w_in = nrm((L, D, IN_COLS), D ** -0.5)
    b_gate = nrm((L, GATE_COLS), 0.02)
    a_q_gain = 1.0 + nrm((L, A_HEAD_DIM), 0.02)
    a_k_gain = 1.0 + nrm((L, A_HEAD_DIM), 0.02)
    r_mu = jax.random.uniform(next(ks), (L, R_COLS), jnp.float32, 0.0, 1.0)
    r_w0 = jnp.broadcast_to(jnp.linspace(-6.0, -1.0, R_WIDTH, dtype=jnp.float32), (L, R_WIDTH)) + nrm((L, R_WIDTH), 0.1)
    r_w2 = nrm((L, R_DECAY_LORA, R_WIDTH), 0.1 * R_DECAY_LORA ** -0.5)
    r_a0 = nrm((L, R_WIDTH), 0.1)
    r_a2 = nrm((L, R_AAA_LORA, R_WIDTH), 0.1 * R_AAA_LORA ** -0.5)
    r_g2 = nrm((L, R_GATE_LORA, R_WIDTH), R_GATE_LORA ** -0.5)
    r_k_k = 0.85 + nrm((L, R_WIDTH), 0.02)
    r_k_a = 1.0 + nrm((L, R_WIDTH), 0.02)
    r_r_k = nrm((L, R_HEADS, R_HEAD), 0.1)
    r_ln_w = 1.0 + nrm((L, R_WIDTH), 0.02)
    r_ln_b = nrm((L, R_WIDTH), 0.01)
    mem_norm = 1.0 + nrm((L, D), 0.02)
    w_mem_kv = nrm((L, D, 2 * C_WIDTH), D ** -0.5)
    c_q_gain = 1.0 + nrm((L, C_HEAD_DIM), 0.02)
    c_k_gain = 1.0 + nrm((L, C_HEAD_DIM), 0.02)
    w_branch = nrm((L, N_BRANCH, BRANCH_WIDTH, D), BRANCH_WIDTH ** -0.5)
    w_out = nrm((L, D, D), D ** -0.5)
    norm_ffn = 1.0 + nrm((L, D), 0.02)
    router_w = nrm((L, D, N_EXPERTS), D ** -0.5)
    router_b = nrm((L, N_EXPERTS), 0.01)
    exp_w1 = nrm((L, N_EXPERTS, D, 2 * D_FF), D ** -0.5)
    exp_b1 = nrm((L, N_EXPERTS, 2 * D_FF), 0.01)
    exp_w2 = nrm((L, N_EXPERTS, D_FF, D), D_FF ** -0.5)
    exp_b2 = nrm((L, N_EXPERTS, D), 0.01)
    return {'x': x, 'mem': mem, 'positions': positions, 'norm_mix': norm_mix, 'w_in': w_in,
            'b_gate': b_gate, 'a_q_gain': a_q_gain, 'a_k_gain': a_k_gain, 'r_mu': r_mu,
            'r_w0': r_w0, 'r_w2': r_w2, 'r_a0': r_a0, 'r_a2': r_a2, 'r_g2': r_g2,
            'r_k_k': r_k_k, 'r_k_a': r_k_a, 'r_r_k': r_r_k, 'r_ln_w': r_ln_w, 'r_ln_b': r_ln_b,
            'mem_norm': mem_norm, 'w_mem_kv': w_mem_kv, 'c_q_gain': c_q_gain, 'c_k_gain': c_k_gain,
            'w_branch': w_branch, 'w_out': w_out, 'norm_ffn': norm_ffn, 'router_w': router_w,
            'router_b': router_b, 'exp_w1': exp_w1, 'exp_b1': exp_b1, 'exp_w2': exp_w2, 'exp_b2': exp_b2}


def reference(x, mem, positions, norm_mix, w_in, b_gate, a_q_gain, a_k_gain, r_mu, r_w0, r_w2,
              r_a0, r_a2, r_g2, r_k_k, r_k_a, r_r_k, r_ln_w, r_ln_b, mem_norm, w_mem_kv,
              c_q_gain, c_k_gain, w_branch, w_out, norm_ffn, router_w, router_b,
              exp_w1, exp_b1, exp_w2, exp_b2):
    b, s, d = x.shape
    for l in range(DEPTH):
        h = rms_norm(x, norm_mix[l])
        z = h @ w_in[l]
        y_a = dilated_mixture_attention(z[..., OFF_AQ:OFF_AK], z[..., OFF_AK:OFF_AV],
                                        z[..., OFF_AV:OFF_R], a_q_gain[l], a_k_gain[l], positions)
        y_b = rwkv7_time_mix(z[..., OFF_R:OFF_CQ], r_mu[l], r_w0[l], r_w2[l], r_a0[l], r_a2[l],
                             r_g2[l], r_k_k[l], r_k_a[l], r_r_k[l], r_ln_w[l], r_ln_b[l])
        y_c = memory_cross_attention(z[..., OFF_CQ:OFF_GATE], rms_norm(mem, mem_norm[l]),
                                     w_mem_kv[l], c_q_gain[l], c_k_gain[l])
        gates = jax.nn.sigmoid(z[..., OFF_GATE:] + b_gate[l]).reshape(b, s, N_BRANCH, d)
        branches = jnp.stack([y_a, y_b, y_c], axis=2)
        proj = jnp.einsum('bsnc,ncd->bsnd', branches, w_branch[l])
        merged = jnp.sum(gates * proj, axis=2)
        x = x + merged @ w_out[l]
        x = x + moe_ffn(rms_norm(x, norm_ffn[l]), router_w[l], router_b[l],
                        exp_w1[l], exp_b1[l], exp_w2[l], exp_b2[l])
    return x
```

```python
import functools

import jax
import jax.numpy as jnp
from jax import lax
from jax.experimental import pallas as pl
from jax.experimental.pallas import tpu as pltpu

F32 = jnp.float32
BF16 = jnp.bfloat16

NORM_EPS = 1e-6
LANES = 128
HEAD_DIM = 128
A_GROUPS = ((128, 1), (512, 4), (2048, 16))
A_HEADS = 4
A_WIDTH = A_HEADS * HEAD_DIM
ATT_BLOCK = 128
ROT_DIM = 32
ROPE_THETA = 500000.0
R_HEAD = 64
R_HEADS = 8
R_WIDTH = R_HEADS * R_HEAD
R_DECAY_LORA = 64
R_AAA_LORA = 64
R_GATE_LORA = 128
R_COLS = 3 * R_WIDTH + R_DECAY_LORA + R_AAA_LORA + R_GATE_LORA
R_GN_EPS = 64e-5
R_CHUNK = 64
C_HEADS = 4
C_WIDTH = C_HEADS * HEAD_DIM
N_BRANCH = 3
N_EXPERTS = 32
TOP_K = 4
SWIGLU_ALPHA = 1.702
SWIGLU_LIMIT = 7.0
MOE_ROWS = 256
NEG_BIG = -1e30

VMEM_LIMIT = 56 * 1024 * 1024


def _cparams(sem):
    return pltpu.CompilerParams(dimension_semantics=sem, vmem_limit_bytes=VMEM_LIMIT)


def _rmsnorm_kernel(x_ref, g_ref, o_ref):
    x = x_ref[...]
    ms = jnp.mean(x * x, axis=-1, keepdims=True)
    o_ref[...] = (x * lax.rsqrt(ms + NORM_EPS) * g_ref[...]).astype(o_ref.dtype)


def rmsnorm_rows(x2d, gain, tm=512):
    n, d = x2d.shape
    tm = min(tm, n)
    return pl.pallas_call(
        _rmsnorm_kernel,
        grid=(n // tm,),
        in_specs=[pl.BlockSpec((tm, d), lambda i: (i, 0)),
                  pl.BlockSpec((1, d), lambda i: (0, 0))],
        out_specs=pl.BlockSpec((tm, d), lambda i: (i, 0)),
        out_shape=jax.ShapeDtypeStruct((n, d), BF16),
        compiler_params=_cparams(("arbitrary",)),
        name="rmsnorm_rows",
    )(x2d, gain.reshape(1, d))


def _rope_table_kernel(pos_ref, freq_ref, cos_ref, sin_ref):
    ang = pos_ref[...].astype(F32) * freq_ref[...]
    lane = lax.broadcasted_iota(jnp.int32, ang.shape, 1)
    cos_ref[...] = jnp.cos(ang)
    s = jnp.sin(ang)
    half = ROT_DIM // 2
    sin_ref[...] = jnp.where(lane < half, -s, jnp.where(lane < ROT_DIM, s, 0.0))


def rope_tables(pos_col, tm=1024):
    n = pos_col.shape[0]
    half = ROT_DIM // 2
    inv_freq = ROPE_THETA ** (-jnp.arange(half, dtype=F32) / half)
    freq_row = jnp.concatenate([inv_freq, inv_freq, jnp.zeros((LANES - ROT_DIM,), F32)]).reshape(1, LANES)
    return pl.pallas_call(
        _rope_table_kernel,
        grid=(n // tm,),
        in_specs=[pl.BlockSpec((tm, 1), lambda i: (i, 0)),
                  pl.BlockSpec((1, LANES), lambda i: (0, 0))],
        out_specs=[pl.BlockSpec((tm, LANES), lambda i: (i, 0)),
                   pl.BlockSpec((tm, LANES), lambda i: (i, 0))],
        out_shape=[jax.ShapeDtypeStruct((n, LANES), F32)] * 2,
        compiler_params=_cparams(("arbitrary",)),
        name="rope_tables",
    )(pos_col, freq_row)


def _head_rmsnorm(zh, gain_row):
    ms = jnp.mean(zh * zh, axis=-1, keepdims=True)
    return zh * lax.rsqrt(ms + NORM_EPS) * gain_row


def _proj_kernel(*refs, mode):
    h_ref, w_ref = refs[0], refs[1]
    o_ref = refs[-1]
    z = jnp.dot(h_ref[...], w_ref[...], preferred_element_type=F32)
    if mode == "plain":
        o_ref[...] = z.astype(o_ref.dtype)
    elif mode == "gate":
        o_ref[...] = jax.nn.sigmoid(z + refs[2][...]).astype(o_ref.dtype)
    else:
        gain = refs[2][...]
        tn = z.shape[1]
        if mode == "headnorm_rope":
            cos = refs[3][...]
            sin = refs[4][...]
            lane = lax.broadcasted_iota(jnp.int32, cos.shape, 1)
        half = ROT_DIM // 2
        for c in range(tn // HEAD_DIM):
            zn = _head_rmsnorm(z[:, c * HEAD_DIM:(c + 1) * HEAD_DIM], gain)
            if mode == "headnorm_rope":
                partner = jnp.where(lane < half,
                                    pltpu.roll(zn, HEAD_DIM - half, 1),
                                    pltpu.roll(zn, half, 1))
                zn = zn * cos + partner * sin
            o_ref[:, c * HEAD_DIM:(c + 1) * HEAD_DIM] = zn.astype(o_ref.dtype)


def project(h, w, mode, extras=(), out_dtype=BF16, tm=1024, tn=512):
    n, k = h.shape
    m = w.shape[1]
    tm = min(tm, n)
    tn = min(tn, m)
    assert n % tm == 0 and m % tn == 0
    in_specs = [pl.BlockSpec((tm, k), lambda j, i: (i, 0)),
                pl.BlockSpec((k, tn), lambda j, i: (0, j))]
    if mode == "gate":
        in_specs.append(pl.BlockSpec((1, tn), lambda j, i: (0, j)))
    elif mode in ("headnorm", "headnorm_rope"):
        in_specs.append(pl.BlockSpec((1, HEAD_DIM), lambda j, i: (0, 0)))
        if mode == "headnorm_rope":
            in_specs += [pl.BlockSpec((tm, LANES), lambda j, i: (i, 0))] * 2
    return pl.pallas_call(
        functools.partial(_proj_kernel, mode=mode),
        grid=(m // tn, n // tm),
        in_specs=in_specs,
        out_specs=pl.BlockSpec((tm, tn), lambda j, i: (i, j)),
        out_shape=jax.ShapeDtypeStruct((n, m), out_dtype),
        compiler_params=_cparams(("arbitrary", "arbitrary")),
        name="proj_" + mode,
    )(h, w, *extras)


def _band_attn_kernel(q_ref, kp_ref, kc_ref, vp_ref, vc_ref, o_ref, lse_ref):
    blk = pl.program_id(2)
    scale = HEAD_DIM ** -0.5
    nq = ATT_BLOCK
    qi = lax.broadcasted_iota(jnp.int32, (nq, 2 * nq), 0)
    ki = lax.broadcasted_iota(jnp.int32, (nq, 2 * nq), 1)
    rel = qi + nq - ki
    valid = (rel >= 0) & (rel <= nq) & ((blk * nq - nq + ki) >= 0)
    lses = []
    for h in range(A_HEADS):
        sl = slice(h * HEAD_DIM, (h + 1) * HEAD_DIM)
        qh = q_ref[0, :, sl]
        kh = jnp.concatenate([kp_ref[0, :, sl], kc_ref[0, :, sl]], axis=0)
        vh = jnp.concatenate([vp_ref[0, :, sl], vc_ref[0, :, sl]], axis=0)
        s = lax.dot_general(qh, kh, (((1,), (1,)), ((), ())), preferred_element_type=F32) * scale
        s = jnp.where(valid, s, NEG_BIG)
        m = jnp.max(s, axis=-1, keepdims=True)
        p = jnp.exp(s - m)
        l = jnp.sum(p, axis=-1, keepdims=True)
        o = jnp.dot(p.astype(BF16), vh, preferred_element_type=F32) / l
        o_ref[0, :, sl] = o.astype(o_ref.dtype)
        lses.append(jnp.broadcast_to(m + jnp.log(l), (nq, LANES // A_HEADS)))
    lse_ref[0] = jnp.concatenate(lses, axis=1)


def band_attention_group(qn, kn, v, g, dilation):
    b, s, cols = qn.shape
    d = dilation
    sub = s // d
    nblk = sub // ATT_BLOCK
    ng = cols // A_WIDTH
    view = lambda t: t.reshape(b, sub, d * cols)
    cur = lambda bi, r, j: (bi, j, r * ng + g)
    prev = lambda bi, r, j: (bi, jnp.maximum(j - 1, 0), r * ng + g)
    blk = (1, ATT_BLOCK, A_WIDTH)
    o, lse = pl.pallas_call(
        _band_attn_kernel,
        grid=(b, d, nblk),
        in_specs=[pl.BlockSpec(blk, cur), pl.BlockSpec(blk, prev), pl.BlockSpec(blk, cur),
                  pl.BlockSpec(blk, prev), pl.BlockSpec(blk, cur)],
        out_specs=[pl.BlockSpec(blk, lambda bi, r, j: (bi, j, r)),
                   pl.BlockSpec((1, ATT_BLOCK, LANES), lambda bi, r, j: (bi, j, r))],
        out_shape=[jax.ShapeDtypeStruct((b, sub, d * A_WIDTH), BF16),
                   jax.ShapeDtypeStruct((b, sub, d * LANES), F32)],
        compiler_params=_cparams(("arbitrary", "arbitrary", "arbitrary")),
        name=f"band_attn_g{g}",
    )(view(qn), view(kn), view(kn), view(v), view(v))
    return o.reshape(b, s, A_WIDTH), lse.reshape(b, s, LANES)


def _cross_attn_kernel(q_ref, k_ref, v_ref, o_ref):
    scale = HEAD_DIM ** -0.5
    for h in range(C_HEADS):
        sl = slice(h * HEAD_DIM, (h + 1) * HEAD_DIM)
        s = lax.dot_general(q_ref[0, :, sl], k_ref[0, :, sl], (((1,), (1,)), ((), ())),
                            preferred_element_type=F32) * scale
        m = jnp.max(s, axis=-1, keepdims=True)
        p = jnp.exp(s - m)
        l = jnp.sum(p, axis=-1, keepdims=True)
        o = jnp.dot(p.astype(BF16), v_ref[0, :, sl], preferred_element_type=F32) / l
        o_ref[0, :, sl] = o.astype(o_ref.dtype)


def cross_attention(qn, kn, v, tm=512):
    b, s, w = qn.shape
    m = kn.shape[1]
    return pl.pallas_call(
        _cross_attn_kernel,
        grid=(b, s // tm),
        in_specs=[pl.BlockSpec((1, tm, w), lambda bi, i: (bi, i, 0)),
                  pl.BlockSpec((1, m, w), lambda bi, i: (bi, 0, 0)),
                  pl.BlockSpec((1, m, w), lambda bi, i: (bi, 0, 0))],
        out_specs=pl.BlockSpec((1, tm, w), lambda bi, i: (bi, i, 0)),
        out_shape=jax.ShapeDtypeStruct((b, s, w), BF16),
        compiler_params=_cparams(("arbitrary", "arbitrary")),
        name="cross_attn",
    )(qn, kn, v)


def _split_dot(x, m_bf16):
    hi = x.astype(BF16)
    lo = (x - hi.astype(F32)).astype(BF16)
    return (jnp.dot(hi, m_bf16, preferred_element_type=F32)
            + jnp.dot(lo, m_bf16, preferred_element_type=F32))


def _dot_t(a, b):
    return lax.dot_general(a, b, (((0,), (0,)), ((), ())), preferred_element_type=F32)


def _dot_nt(a, b):
    return lax.dot_general(a, b, (((1,), (1,)), ((), ())), preferred_element_type=F32)


def _rwkv_kernel(zr_ref, mu_ref, w0_ref, wwa_ref, a0_ref, g2_ref, kk_ref, ka_ref, rk_ref,
                 lnw_ref, lnb_ref, seg_ref, y_ref,
                 state_ref, carry_ref, ops_ref, yh_ref):
    t = pl.program_id(1)
    tt = zr_ref.shape[1]
    nch = tt // R_CHUNK
    c = R_CHUNK

    @pl.when(t == 0)
    def _():
        state_ref[...] = jnp.zeros_like(state_ref)
        carry_ref[...] = jnp.zeros_like(carry_ref)

    z = zr_ref[0]
    row = lax.broadcasted_iota(jnp.int32, z.shape, 0)
    prev = jnp.where(row == 0, carry_ref[...], pltpu.roll(z, 1, 0))
    carry_ref[...] = z[tt - 1:tt, :]
    xs = z + (prev - z) * mu_ref[...]

    w3 = 3 * R_WIDTH
    r = xs[:, 0:R_WIDTH]
    k = xs[:, R_WIDTH:2 * R_WIDTH]
    v = xs[:, 2 * R_WIDTH:w3]
    wa_lo = xs[:, w3:w3 + LANES]
    g_lo = xs[:, w3 + LANES:w3 + 2 * LANES]
    lane = lax.broadcasted_iota(jnp.int32, wa_lo.shape, 1)
    wa_in = jnp.where(lane < R_DECAY_LORA, jnp.tanh(wa_lo), wa_lo)
    wa = jnp.dot(wa_in.astype(BF16), wwa_ref[...], preferred_element_type=F32)
    u = -(w0_ref[...] + wa[:, :R_WIDTH])
    softplus = jnp.maximum(u, 0.0) + jnp.log(1.0 + jnp.exp(-jnp.abs(u)))
    w_raw = -softplus - 0.5
    ld = -jnp.exp(w_raw)
    a = jax.nn.sigmoid(a0_ref[...] + wa[:, R_WIDTH:])
    g = jnp.dot(jax.nn.sigmoid(g_lo).astype(BF16), g2_ref[...], preferred_element_type=F32)

    seg = seg_ref[...]
    kk = k * kk_ref[...]
    kk = kk / jnp.maximum(jnp.sqrt(_split_dot(kk * kk, seg)), 1e-12)
    k2 = k * (1.0 + (a - 1.0) * ka_ref[...])
    bonus = _split_dot(r * k2 * rk_ref[...], seg) * v

    ri = lax.broadcasted_iota(jnp.int32, (c, c), 0)
    ci = lax.broadcasted_iota(jnp.int32, (c, c), 1)
    tri = (ci <= ri).astype(BF16)
    ld_hi = ld.astype(BF16)
    ld_lo = (ld - ld_hi.astype(F32)).astype(BF16)
    lcs = []
    for ch in range(nch):
        rs = slice(ch * c, (ch + 1) * c)
        lcs.append(jnp.dot(tri, ld_hi[rs], preferred_element_type=F32)
                   + jnp.dot(tri, ld_lo[rs], preferred_element_type=F32))
    lc = jnp.concatenate(lcs, axis=0)
    e_inc = jnp.exp(lc)
    e_exc = jnp.exp(lc - ld)
    e_inv = jnp.exp(-lc)
    a_t = -kk * e_exc
    r_t = r * e_inc
    b_t = kk * a * e_inv
    k_t = k2 * e_inv
    for h in range(R_HEADS):
        hs = slice(h * R_HEAD, (h + 1) * R_HEAD)
        ops_ref[0, h] = a_t[:, hs]
        ops_ref[1, h] = r_t[:, hs]
        ops_ref[2, h] = b_t[:, hs]
        ops_ref[3, h] = k_t[:, hs]
        ops_ref[4, h] = v[:, hs]
        ops_ref[5, h] = e_inc[:, hs]

    strict = ci < ri
    incl = ci <= ri
    eye = (ci == ri)

    def head_body(h, _):
        st = state_ref[h]
        for ch in range(nch):
            rs = pl.ds(ch * c, c)
            at = ops_ref[0, h, rs, :]
            rt = ops_ref[1, h, rs, :]
            bt = ops_ref[2, h, rs, :]
            kt = ops_ref[3, h, rs, :]
            vv = ops_ref[4, h, rs, :]
            pc = ops_ref[5, h, pl.ds(ch * c + c - 1, 1), :]
            at16, rt16, bt16, kt16, v16 = (x.astype(BF16) for x in (at, rt, bt, kt, vv))
            a_ab = jnp.where(strict, _dot_nt(at16, bt16), 0.0)
            a_ak = jnp.where(strict, _dot_nt(at16, kt16), 0.0)
            a_rb = jnp.where(incl, _dot_nt(rt16, bt16), 0.0)
            a_rk = jnp.where(incl, _dot_nt(rt16, kt16), 0.0)
            npow = a_ab
            tinv = jnp.where(eye, 1.0, 0.0) + a_ab
            for _i in range(5):
                np16 = npow.astype(BF16)
                npow = jnp.dot(np16, np16, preferred_element_type=F32)
                tinv = tinv + jnp.dot(tinv.astype(BF16), npow.astype(BF16), preferred_element_type=F32)
            tinv16 = tinv.astype(BF16)
            ap = jnp.dot(tinv16, at16, preferred_element_type=F32)
            w1 = jnp.dot(tinv16, jnp.dot(a_ak.astype(BF16), v16, preferred_element_type=F32).astype(BF16),
                         preferred_element_type=F32)
            ap16, w116 = ap.astype(BF16), w1.astype(BF16)
            bh16 = (bt * pc).astype(BF16)
            kh16 = (kt * pc).astype(BF16)
            gm = jnp.where(eye, jnp.broadcast_to(pc, (c, c)), 0.0) + _dot_t(bh16, ap16)
            hm = _dot_t(bh16, w116) + _dot_t(kh16, v16)
            qp = rt + jnp.dot(a_rb.astype(BF16), ap16, preferred_element_type=F32)
            y0 = (jnp.dot(a_rb.astype(BF16), w116, preferred_element_type=F32)
                  + jnp.dot(a_rk.astype(BF16), v16, preferred_element_type=F32))
            st16 = st.astype(BF16)
            yh_ref[h, rs, :] = jnp.dot(qp.astype(BF16), st16, preferred_element_type=F32) + y0
            st = jnp.dot(gm.astype(BF16), st16, preferred_element_type=F32) + hm
        state_ref[h] = st
        return 0

    lax.fori_loop(0, R_HEADS, head_body, 0)

    y = jnp.concatenate([yh_ref[h] for h in range(R_HEADS)], axis=1)
    mean = _split_dot(y, seg) * (1.0 / R_HEAD)
    dlt = y - mean
    var = _split_dot(dlt * dlt, seg) * (1.0 / R_HEAD)
    yn = dlt * lax.rsqrt(var + R_GN_EPS) * lnw_ref[...] + lnb_ref[...]
    y_ref[0] = ((yn + bonus) * g).astype(y_ref.dtype)


def rwkv7_mix(zr, mu, w0, w2, a0, a2, g2, k_k, k_a, r_k, ln_w, ln_b, tt=256):
    b, s, cols = zr.shape
    row = lambda x: x.reshape(1, -1).astype(F32)
    wwa = jnp.zeros((LANES, 2 * R_WIDTH), F32)
    wwa = wwa.at[:R_DECAY_LORA, :R_WIDTH].set(w2).at[R_DECAY_LORA:, R_WIDTH:].set(a2).astype(BF16)
    hid = jnp.arange(R_WIDTH) // R_HEAD
    seg = (hid[:, None] == hid[None, :]).astype(BF16)
    full = lambda shape: pl.BlockSpec(shape, lambda bi, t: (0,) * len(shape))
    return pl.pallas_call(
        _rwkv_kernel,
        grid=(b, s // tt),
        in_specs=[pl.BlockSpec((1, tt, cols), lambda bi, t: (bi, t, 0)),
                  full((1, cols)), full((1, R_WIDTH)), full((LANES, 2 * R_WIDTH)), full((1, R_WIDTH)),
                  full((R_GATE_LORA, R_WIDTH)), full((1, R_WIDTH)), full((1, R_WIDTH)), full((1, R_WIDTH)),
                  full((1, R_WIDTH)), full((1, R_WIDTH)), full((R_WIDTH, R_WIDTH))],
        out_specs=pl.BlockSpec((1, tt, R_WIDTH), lambda bi, t: (bi, t, 0)),
        out_shape=jax.ShapeDtypeStruct((b, s, R_WIDTH), BF16),
        scratch_shapes=[pltpu.VMEM((R_HEADS, R_HEAD, R_HEAD), F32),
                        pltpu.VMEM((1, cols), F32),
                        pltpu.VMEM((6, R_HEADS, tt, R_HEAD), F32),
                        pltpu.VMEM((R_HEADS, tt, R_HEAD), F32)],
        compiler_params=_cparams(("arbitrary", "arbitrary")),
        name="rwkv7_mix",
    )(zr, row(mu), row(w0), wwa, row(a0), g2.astype(BF16), row(k_k), row(k_a), row(r_k),
      row(ln_w), row(ln_b), seg)


def _merge_kernel(x_ref, o0_ref, o1_ref, o2_ref, l0_ref, l1_ref, l2_ref, yb_ref, yc_ref, gt_ref,
                  wb_ref, wo_ref, gn_ref, rw_ref, rb_ref,
                  x1_ref, h2_ref, idx_ref, gate_ref):
    o_refs = (o0_ref, o1_ref, o2_ref)
    lses = [l[...] for l in (l0_ref, l1_ref, l2_ref)]
    lmax = jnp.maximum(jnp.maximum(lses[0], lses[1]), lses[2])
    es = [jnp.exp(l - lmax) for l in lses]
    inv = 1.0 / (es[0] + es[1] + es[2])
    qw = LANES // A_HEADS
    heads = []
    for h in range(A_HEADS):
        sl = slice(h * HEAD_DIM, (h + 1) * HEAD_DIM)
        acc = None
        for gi in range(3):
            alpha = (es[gi] * inv)[:, h * qw:h * qw + 1]
            term = alpha * o_refs[gi][:, sl].astype(F32)
            acc = term if acc is None else acc + term
        heads.append(acc)
    ya = jnp.concatenate(heads, axis=1).astype(BF16)
    d = x_ref.shape[1]
    merged = None
    for n, yn in enumerate((ya, yb_ref[...], yc_ref[...])):
        proj = jnp.dot(yn, wb_ref[n], preferred_element_type=F32)
        term = gt_ref[:, n * d:(n + 1) * d].astype(F32) * proj
        merged = term if merged is None else merged + term
    x1 = x_ref[...] + jnp.dot(merged.astype(BF16), wo_ref[...], preferred_element_type=F32)
    x1_ref[...] = x1
    ms = jnp.mean(x1 * x1, axis=-1, keepdims=True)
    h2 = x1 * lax.rsqrt(ms + NORM_EPS) * gn_ref[...]
    h2_ref[...] = h2.astype(h2_ref.dtype)
    logits = jnp.dot(h2, rw_ref[...], preferred_element_type=F32,
                     precision=lax.Precision.HIGHEST) + rb_ref[...]
    lane = lax.broadcasted_iota(jnp.int32, logits.shape, 1)
    vals, idxs = [], []
    cur = logits
    for _k in range(TOP_K):
        m = jnp.max(cur, axis=-1, keepdims=True)
        ik = jnp.min(jnp.where(cur == m, lane, N_EXPERTS), axis=-1, keepdims=True)
        vals.append(m)
        idxs.append(ik)
        cur = jnp.where(lane == ik, -jnp.inf, cur)
    exps = [jnp.exp(vk - vals[0]) for vk in vals]
    tot = exps[0] + exps[1] + exps[2] + exps[3]
    for kk in range(TOP_K):
        idx_ref[:, kk:kk + 1] = idxs[kk]
        gate_ref[:, kk:kk + 1] = exps[kk] / tot


def merge_and_route(x2d, outs, lses, yb, yc, gates, w_branch, w_out, norm_ffn, router_w, router_b, tm=512):
    n, d = x2d.shape
    rows = lambda w: pl.BlockSpec((tm, w), lambda i: (i, 0))
    full = lambda shape: pl.BlockSpec(shape, lambda i: (0,) * len(shape))
    return pl.pallas_call(
        _merge_kernel,
        grid=(n // tm,),
        in_specs=[rows(d)] + [rows(A_WIDTH)] * 3 + [rows(LANES)] * 3 + [rows(R_WIDTH), rows(C_WIDTH), rows(N_BRANCH * d),
                  full((N_BRANCH, A_WIDTH, d)), full((d, d)), full((1, d)), full((d, N_EXPERTS)), full((1, N_EXPERTS))],
        out_specs=[rows(d), rows(d), rows(TOP_K), rows(TOP_K)],
        out_shape=[jax.ShapeDtypeStruct((n, d), F32), jax.ShapeDtypeStruct((n, d), F32),
                   jax.ShapeDtypeStruct((n, TOP_K), jnp.int32), jax.ShapeDtypeStruct((n, TOP_K), F32)],
        compiler_params=_cparams(("arbitrary",)),
        name="merge_route",
    )(x2d, *outs, *lses, yb, yc, gates, w_branch.astype(BF16), w_out.astype(BF16),
      norm_ffn.reshape(1, d), router_w, router_b.reshape(1, N_EXPERTS))


def _gather_rows(src_hbm, idx_ref, dst_ref, sem, nrows):
    def body(i, _):
        pltpu.make_async_copy(src_hbm.at[pl.ds(idx_ref[i], 1)], dst_ref.at[pl.ds(i, 1)], sem).start()
        return 0
    lax.fori_loop(0, nrows, body, 0)


def _expert_kernel(be_ref, tok0_ref, tokn_ref, h2_hbm, w1_ref, b1_ref, w2_ref, b2_ref, gate_ref,
                   y_ref, xbuf, sems):
    i = pl.program_id(0)
    nblk = pl.num_programs(0)
    slot = lax.rem(i, 2)

    @pl.when(i == 0)
    def _():
        _gather_rows(h2_hbm, tok0_ref.at[0, 0], xbuf.at[0], sems.at[0], MOE_ROWS)

    @pl.when(i + 1 < nblk)
    def _():
        _gather_rows(h2_hbm, tokn_ref.at[0, 0], xbuf.at[1 - slot], sems.at[1 - slot], MOE_ROWS)

    pltpu.make_async_copy(h2_hbm.at[pl.ds(0, MOE_ROWS)], xbuf.at[slot], sems.at[slot]).wait()
    xb = xbuf[slot].astype(BF16)
    dff = w2_ref.shape[1]
    u = jnp.dot(xb, w1_ref[0], preferred_element_type=F32) + b1_ref[0]
    glu = jnp.minimum(u[:, :dff], SWIGLU_LIMIT)
    lin = jnp.clip(u[:, dff:], -SWIGLU_LIMIT, SWIGLU_LIMIT)
    act = glu * jax.nn.sigmoid(SWIGLU_ALPHA * glu) * (lin + 1.0)
    y = jnp.dot(act.astype(BF16), w2_ref[0], preferred_element_type=F32) + b2_ref[0]
    y_ref[...] = (y * gate_ref[...]).astype(y_ref.dtype)


def expert_ffn(h2, block_expert, row_tok, row_gate, w1, b1, w2, b2):
    n, d = h2.shape
    n_rows = row_tok.shape[0]
    nblk = n_rows // MOE_ROWS
    dff2 = w1.shape[2]
    tok3 = row_tok.reshape(nblk, 1, MOE_ROWS)
    grid_spec = pltpu.PrefetchScalarGridSpec(
        num_scalar_prefetch=1,
        grid=(nblk,),
        in_specs=[
            pl.BlockSpec((1, 1, MOE_ROWS), lambda i, be: (0, 0, 0), memory_space=pltpu.SMEM),
            pl.BlockSpec((1, 1, MOE_ROWS), lambda i, be: (jnp.minimum(i + 1, nblk - 1), 0, 0),
                         memory_space=pltpu.SMEM),
            pl.BlockSpec(memory_space=pl.ANY),
            pl.BlockSpec((1, d, dff2), lambda i, be: (be[i], 0, 0)),
            pl.BlockSpec((1, 1, dff2), lambda i, be: (be[i], 0, 0)),
            pl.BlockSpec((1, dff2 // 2, d), lambda i, be: (be[i], 0, 0)),
            pl.BlockSpec((1, 1, d), lambda i, be: (be[i], 0, 0)),
            pl.BlockSpec((MOE_ROWS, 1), lambda i, be: (i, 0)),
        ],
        out_specs=pl.BlockSpec((MOE_ROWS, d), lambda i, be: (i, 0)),
        scratch_shapes=[pltpu.VMEM((2, MOE_ROWS, d), F32), pltpu.SemaphoreType.DMA((2,))],
    )
    return pl.pallas_call(
        _expert_kernel,
        grid_spec=grid_spec,
        out_shape=jax.ShapeDtypeStruct((n_rows, d), F32),
        compiler_params=_cparams(("arbitrary",)),
        name="expert_ffn",
    )(block_expert, tok3, tok3, h2, w1, b1.reshape(N_EXPERTS, 1, dff2), w2, b2.reshape(N_EXPERTS, 1, d),
      row_gate.reshape(n_rows, 1))


def _combine_kernel(pos0_ref, posn_ref, x1_ref, y_hbm, o_ref, ybuf, sems):
    i = pl.program_id(0)
    nblk = pl.num_programs(0)
    slot = lax.rem(i, 2)
    nrows = ybuf.shape[1]

    @pl.when(i == 0)
    def _():
        _gather_rows(y_hbm, pos0_ref.at[0, 0], ybuf.at[0], sems.at[0], nrows)

    @pl.when(i + 1 < nblk)
    def _():
        _gather_rows(y_hbm, posn_ref.at[0, 0], ybuf.at[1 - slot], sems.at[1 - slot], nrows)

    pltpu.make_async_copy(y_hbm.at[pl.ds(0, nrows)], ybuf.at[slot], sems.at[slot]).wait()
    tm = o_ref.shape[0]
    acc = x1_ref[...]
    for kk in range(TOP_K):
        acc = acc + ybuf[slot, pl.ds(kk * tm, tm), :].astype(F32)
    o_ref[...] = acc


def moe_combine(x1, y_sorted, pos_kmajor, tm=256):
    n, d = x1.shape
    nblk = n // tm
    return pl.pallas_call(
        _combine_kernel,
        grid=(nblk,),
        in_specs=[pl.BlockSpec((1, 1, TOP_K * tm), lambda i: (0, 0, 0), memory_space=pltpu.SMEM),
                  pl.BlockSpec((1, 1, TOP_K * tm), lambda i: (jnp.minimum(i + 1, nblk - 1), 0, 0),
                               memory_space=pltpu.SMEM),
                  pl.BlockSpec((tm, d), lambda i: (i, 0)),
                  pl.BlockSpec(memory_space=pl.ANY)],
        out_specs=pl.BlockSpec((tm, d), lambda i: (i, 0)),
        out_shape=jax.ShapeDtypeStruct((n, d), F32),
        scratch_shapes=[pltpu.VMEM((2, TOP_K * tm, d), F32), pltpu.SemaphoreType.DMA((2,))],
        compiler_params=_cparams(("arbitrary",)),
        name="moe_combine",
    )(pos_kmajor, pos_kmajor, x1, y_sorted)


def routing_metadata(top_idx, gate, n_tok, tm_combine):
    n_assign = n_tok * TOP_K
    flat_e = top_idx.reshape(-1)
    flat_g = gate.reshape(-1)
    flat_tok = jnp.arange(n_assign, dtype=jnp.int32) // TOP_K
    order = jnp.argsort(flat_e)
    sorted_e = flat_e[order]
    counts = jnp.bincount(flat_e, length=N_EXPERTS)
    padded = (counts + MOE_ROWS - 1) // MOE_ROWS * MOE_ROWS
    pad_end = jnp.cumsum(padded)
    pad_start = pad_end - padded
    raw_start = jnp.cumsum(counts) - counts
    dest = (pad_start[sorted_e] + jnp.arange(n_assign, dtype=jnp.int32) - raw_start[sorted_e]).astype(jnp.int32)
    n_blocks = -(-n_assign // MOE_ROWS) + N_EXPERTS
    n_rows = n_blocks * MOE_ROWS
    row_tok = jnp.zeros((n_rows,), jnp.int32).at[dest].set(flat_tok[order])
    row_gate = jnp.zeros((n_rows,), F32).at[dest].set(flat_g[order])
    block_expert = jnp.minimum(
        jnp.searchsorted(pad_end, jnp.arange(n_blocks, dtype=jnp.int32) * MOE_ROWS, side='right'),
        N_EXPERTS - 1).astype(jnp.int32)
    pos = jnp.zeros((n_assign,), jnp.int32).at[order].set(dest).reshape(n_tok, TOP_K)
    pos_kmajor = pos.reshape(n_tok // tm_combine, tm_combine, TOP_K).transpose(0, 2, 1)
    pos_kmajor = pos_kmajor.reshape(n_tok // tm_combine, 1, TOP_K * tm_combine)
    return block_expert, row_tok, row_gate, pos_kmajor


def kernel(x, mem, positions, norm_mix, w_in, b_gate, a_q_gain, a_k_gain, r_mu, r_w0, r_w2, r_a0, r_a2,
           r_g2, r_k_k, r_k_a, r_r_k, r_ln_w, r_ln_b, mem_norm, w_mem_kv, c_q_gain, c_k_gain, w_branch,
           w_out, norm_ffn, router_w, router_b, exp_w1, exp_b1, exp_w2, exp_b2):
    b, s, d = x.shape
    n = b * s
    depth = norm_mix.shape[0]
    n_groups = len(A_GROUPS)
    qkv_cols = n_groups * A_WIDTH
    off_k, off_v, off_r = qkv_cols, 2 * qkv_cols, 3 * qkv_cols
    off_cq = off_r + R_COLS
    off_gate = off_cq + C_WIDTH
    tm_combine = 256

    x2d = x.reshape(n, d)
    cos_t, sin_t = rope_tables(positions.reshape(n, 1).astype(jnp.int32))
    for l in range(depth):
        w_l = w_in[l].astype(BF16)
        h = rmsnorm_rows(x2d, norm_mix[l])
        gq = a_q_gain[l].reshape(1, HEAD_DIM)
        gk = a_k_gain[l].reshape(1, HEAD_DIM)
        qn = project(h, w_l[:, :off_k], "headnorm_rope", (gq, cos_t, sin_t))
        kn = project(h, w_l[:, off_k:off_v], "headnorm_rope", (gk, cos_t, sin_t))
        vv = project(h, w_l[:, off_v:off_r], "plain")
        zr = project(h, w_l[:, off_r:off_cq], "plain", out_dtype=F32, tn=R_COLS // 2)
        cq = project(h, w_l[:, off_cq:off_gate], "headnorm", (c_q_gain[l].reshape(1, HEAD_DIM),))
        gates = project(h, w_l[:, off_gate:], "gate", (b_gate[l].reshape(1, -1),))

        shp = lambda t: t.reshape(b, s, -1)
        outs, lses = [], []
        for g, (window, dilation) in enumerate(A_GROUPS):
            assert window // dilation == ATT_BLOCK
            o, lse = band_attention_group(shp(qn), shp(kn), shp(vv), g, dilation)
            outs.append(o.reshape(n, A_WIDTH))
            lses.append(lse.reshape(n, LANES))

        yb = rwkv7_mix(shp(zr), r_mu[l], r_w0[l], r_w2[l], r_a0[l], r_a2[l], r_g2[l], r_k_k[l], r_k_a[l],
                       r_r_k[l].reshape(-1), r_ln_w[l], r_ln_b[l]).reshape(n, R_WIDTH)

        mlen = mem.shape[1]
        mem_n = rmsnorm_rows(mem.reshape(b * mlen, d), mem_norm[l])
        wkv = w_mem_kv[l].astype(BF16)
        ck = project(mem_n, wkv[:, :C_WIDTH], "headnorm", (c_k_gain[l].reshape(1, HEAD_DIM),))
        cv = project(mem_n, wkv[:, C_WIDTH:], "plain")
        yc = cross_attention(shp(cq), ck.reshape(b, mlen, C_WIDTH), cv.reshape(b, mlen, C_WIDTH)).reshape(n, C_WIDTH)

        x1, h2, top_idx, gate = merge_and_route(x2d, outs, lses, yb, yc, gates, w_branch[l], w_out[l],
                                                norm_ffn[l], router_w[l], router_b[l])
        block_expert, row_tok, row_gate, pos_kmajor = routing_metadata(top_idx, gate, n, tm_combine)
        y_sorted = expert_ffn(h2, block_expert, row_tok, row_gate, exp_w1[l].astype(BF16), exp_b1[l],
                              exp_w2[l].astype(BF16), exp_b2[l])
        x2d = moe_combine(x1, y_sorted, pos_kmajor, tm_combine)
    return x2d.reshape(b, s, d)
```

```python
import functools

import jax
import jax.numpy as jnp
from jax import lax
from jax.experimental import pallas as pl
from jax.experimental.pallas import tpu as pltpu

F32 = jnp.float32
BF16 = jnp.bfloat16

NORM_EPS = 1e-6
LANES = 128
HEAD_DIM = 128
A_GROUPS = ((128, 1), (512, 4), (2048, 16))
A_HEADS = 4
A_WIDTH = A_HEADS * HEAD_DIM
ATT_BLOCK = 128
ROT_DIM = 32
ROPE_THETA = 500000.0
R_HEAD = 64
R_HEADS = 8
R_WIDTH = R_HEADS * R_HEAD
R_DECAY_LORA = 64
R_AAA_LORA = 64
R_GATE_LORA = 128
R_COLS = 3 * R_WIDTH + R_DECAY_LORA + R_AAA_LORA + R_GATE_LORA
R_GN_EPS = 64e-5
R_CHUNK = 64
R_CPI = 4
R_SEG = 256
C_HEADS = 4
C_WIDTH = C_HEADS * HEAD_DIM
N_BRANCH = 3
N_EXPERTS = 32
TOP_K = 4
SWIGLU_ALPHA = 1.702
SWIGLU_LIMIT = 7.0
MOE_ROWS = 256
NEG_BIG = -1e30

VMEM_LIMIT = 56 * 1024 * 1024


def _cparams(sem):
    return pltpu.CompilerParams(dimension_semantics=sem, vmem_limit_bytes=VMEM_LIMIT)


def _rmsnorm_kernel(x_ref, g_ref, o_ref):
    x = x_ref[...]
    ms = jnp.mean(x * x, axis=-1, keepdims=True)
    o_ref[...] = (x * lax.rsqrt(ms + NORM_EPS) * g_ref[...]).astype(o_ref.dtype)


def rmsnorm_rows(x2d, gain, tm=512):
    n, d = x2d.shape
    tm = min(tm, n)
    return pl.pallas_call(
        _rmsnorm_kernel,
        grid=(n // tm,),
        in_specs=[pl.BlockSpec((tm, d), lambda i: (i, 0)),
                  pl.BlockSpec((1, d), lambda i: (0, 0))],
        out_specs=pl.BlockSpec((tm, d), lambda i: (i, 0)),
        out_shape=jax.ShapeDtypeStruct((n, d), BF16),
        compiler_params=_cparams(("arbitrary",)),
        name="rmsnorm_rows",
    )(x2d, gain.reshape(1, d))


def _rope_table_kernel(pos_ref, freq_ref, cos_ref, sin_ref):
    ang = pos_ref[...].astype(F32) * freq_ref[...]
    lane = lax.broadcasted_iota(jnp.int32, ang.shape, 1)
    cos_ref[...] = jnp.cos(ang)
    s = jnp.sin(ang)
    half = ROT_DIM // 2
    sin_ref[...] = jnp.where(lane < half, -s, jnp.where(lane < ROT_DIM, s, 0.0))


def rope_tables(pos_col, tm=1024):
    n = pos_col.shape[0]
    half = ROT_DIM // 2
    inv_freq = ROPE_THETA ** (-jnp.arange(half, dtype=F32) / half)
    freq_row = jnp.concatenate([inv_freq, inv_freq, jnp.zeros((LANES - ROT_DIM,), F32)]).reshape(1, LANES)
    return pl.pallas_call(
        _rope_table_kernel,
        grid=(n // tm,),
        in_specs=[pl.BlockSpec((tm, 1), lambda i: (i, 0)),
                  pl.BlockSpec((1, LANES), lambda i: (0, 0))],
        out_specs=[pl.BlockSpec((tm, LANES), lambda i: (i, 0)),
                   pl.BlockSpec((tm, LANES), lambda i: (i, 0))],
        out_shape=[jax.ShapeDtypeStruct((n, LANES), F32)] * 2,
        compiler_params=_cparams(("arbitrary",)),
        name="rope_tables",
    )(pos_col, freq_row)


def _head_rmsnorm(zh, gain_row):
    ms = jnp.mean(zh * zh, axis=-1, keepdims=True)
    return zh * lax.rsqrt(ms + NORM_EPS) * gain_row


def _proj_kernel(*refs, mode):
    h_ref, w_ref = refs[0], refs[1]
    o_ref = refs[-1]
    z = jnp.dot(h_ref[...], w_ref[...], preferred_element_type=F32)
    if mode == "plain":
        o_ref[...] = z.astype(o_ref.dtype)
    elif mode == "gate":
        o_ref[...] = jax.nn.sigmoid(z + refs[2][...]).astype(o_ref.dtype)
    else:
        gain = refs[2][...]
        tn = z.shape[1]
        if mode == "headnorm_rope":
            cos = refs[3][...]
            sin = refs[4][...]
            lane = lax.broadcasted_iota(jnp.int32, cos.shape, 1)
        half = ROT_DIM // 2
        for c in range(tn // HEAD_DIM):
            zn = _head_rmsnorm(z[:, c * HEAD_DIM:(c + 1) * HEAD_DIM], gain)
            if mode == "headnorm_rope":
                partner = jnp.where(lane < half,
                                    pltpu.roll(zn, HEAD_DIM - half, 1),
                                    pltpu.roll(zn, half, 1))
                zn = zn * cos + partner * sin
            o_ref[:, c * HEAD_DIM:(c + 1) * HEAD_DIM] = zn.astype(o_ref.dtype)


def project(h, w, mode, extras=(), out_dtype=BF16, tm=1024, tn=512):
    n, k = h.shape
    m = w.shape[1]
    tm = min(tm, n)
    tn = min(tn, m)
    assert n % tm == 0 and m % tn == 0
    in_specs = [pl.BlockSpec((tm, k), lambda j, i: (i, 0)),
                pl.BlockSpec((k, tn), lambda j, i: (0, j))]
    if mode == "gate":
        in_specs.append(pl.BlockSpec((1, tn), lambda j, i: (0, j)))
    elif mode in ("headnorm", "headnorm_rope"):
        in_specs.append(pl.BlockSpec((1, HEAD_DIM), lambda j, i: (0, 0)))
        if mode == "headnorm_rope":
            in_specs += [pl.BlockSpec((tm, LANES), lambda j, i: (i, 0))] * 2
    return pl.pallas_call(
        functools.partial(_proj_kernel, mode=mode),
        grid=(m // tn, n // tm),
        in_specs=in_specs,
        out_specs=pl.BlockSpec((tm, tn), lambda j, i: (i, j)),
        out_shape=jax.ShapeDtypeStruct((n, m), out_dtype),
        compiler_params=_cparams(("arbitrary", "arbitrary")),
        name="proj_" + mode,
    )(h, w, *extras)


def _qkv_proj_kernel(h_ref, w_ref, gain_ref, cos_ref, sin_ref, o_ref, z_scr, *, dilation):
    j = pl.program_id(0)
    z = jnp.dot(h_ref[...], w_ref[...], preferred_element_type=F32)
    half = ROT_DIM // 2

    @pl.when(j < 2)
    def _():
        gain = gain_ref[0]
        cos = cos_ref[...]
        sin = sin_ref[...]
        lane = lax.broadcasted_iota(jnp.int32, cos.shape, 1)
        for c in range(A_HEADS):
            zn = _head_rmsnorm(z[:, c * HEAD_DIM:(c + 1) * HEAD_DIM], gain)
            partner = jnp.where(lane < half, pltpu.roll(zn, HEAD_DIM - half, 1), pltpu.roll(zn, half, 1))
            z_scr[c] = zn * cos + partner * sin

    @pl.when(j == 2)
    def _():
        for c in range(A_HEADS):
            z_scr[c] = z[:, c * HEAD_DIM:(c + 1) * HEAD_DIM]

    rows = z_scr.shape[1] // dilation
    for r in range(dilation):
        for c in range(A_HEADS):
            src = pl.ds(r, rows, stride=dilation) if dilation > 1 else slice(None)
            lo = r * A_WIDTH + c * HEAD_DIM
            o_ref[:, lo:lo + HEAD_DIM] = z_scr[c, src, :].astype(o_ref.dtype)


def project_qkv(h, w_qkv, gains, cos_t, sin_t, dilation, tm=1024):
    n, k = h.shape
    d = dilation
    return pl.pallas_call(
        functools.partial(_qkv_proj_kernel, dilation=d),
        grid=(3, n // tm),
        in_specs=[pl.BlockSpec((tm, k), lambda j, i: (i, 0)),
                  pl.BlockSpec((k, A_WIDTH), lambda j, i: (0, j)),
                  pl.BlockSpec((1, 1, HEAD_DIM), lambda j, i: (jnp.minimum(j, 1), 0, 0)),
                  pl.BlockSpec((tm, LANES), lambda j, i: (i, 0)),
                  pl.BlockSpec((tm, LANES), lambda j, i: (i, 0))],
        out_specs=pl.BlockSpec((tm // d, d * A_WIDTH), lambda j, i: (i, j)),
        out_shape=jax.ShapeDtypeStruct((n // d, 3 * d * A_WIDTH), BF16),
        scratch_shapes=[pltpu.VMEM((A_HEADS, tm, HEAD_DIM), F32)],
        compiler_params=_cparams(("arbitrary", "arbitrary")),
        name=f"proj_qkv_d{d}",
    )(h, w_qkv, gains, cos_t, sin_t)


def _band_attn_kernel(q_ref, kp_ref, kc_ref, vp_ref, vc_ref, o_ref, lse_ref):
    blk = pl.program_id(2)
    scale = HEAD_DIM ** -0.5
    nq = ATT_BLOCK
    qi = lax.broadcasted_iota(jnp.int32, (nq, 2 * nq), 0)
    ki = lax.broadcasted_iota(jnp.int32, (nq, 2 * nq), 1)
    rel = qi + nq - ki
    valid = (rel >= 0) & (rel <= nq) & ((blk * nq - nq + ki) >= 0)
    lses = []
    for h in range(A_HEADS):
        sl = slice(h * HEAD_DIM, (h + 1) * HEAD_DIM)
        qh = q_ref[0, :, sl]
        kh = jnp.concatenate([kp_ref[0, :, sl], kc_ref[0, :, sl]], axis=0)
        vh = jnp.concatenate([vp_ref[0, :, sl], vc_ref[0, :, sl]], axis=0)
        s = lax.dot_general(qh, kh, (((1,), (1,)), ((), ())), preferred_element_type=F32) * scale
        s = jnp.where(valid, s, NEG_BIG)
        m = jnp.max(s, axis=-1, keepdims=True)
        p = jnp.exp(s - m)
        l = jnp.sum(p, axis=-1, keepdims=True)
        o = jnp.dot(p.astype(BF16), vh, preferred_element_type=F32) / l
        o_ref[0, :, sl] = o.astype(o_ref.dtype)
        lses.append(jnp.broadcast_to(m + jnp.log(l), (nq, LANES // A_HEADS)))
    lse_ref[0] = jnp.concatenate(lses, axis=1)


def band_attention_group(qkv, b, g, dilation):
    d = dilation
    sub = qkv.shape[0] // b
    nblk = sub // ATT_BLOCK
    view = qkv.reshape(b, sub, 3 * d * A_WIDTH)
    cur = lambda t: (lambda bi, r, j: (bi, j, t * d + r))
    prev = lambda t: (lambda bi, r, j: (bi, jnp.maximum(j - 1, 0), t * d + r))
    blk = (1, ATT_BLOCK, A_WIDTH)
    o, lse = pl.pallas_call(
        _band_attn_kernel,
        grid=(b, d, nblk),
        in_specs=[pl.BlockSpec(blk, cur(0)), pl.BlockSpec(blk, prev(1)), pl.BlockSpec(blk, cur(1)),
                  pl.BlockSpec(blk, prev(2)), pl.BlockSpec(blk, cur(2))],
        out_specs=[pl.BlockSpec(blk, lambda bi, r, j: (bi, j, r)),
                   pl.BlockSpec((1, ATT_BLOCK, LANES), lambda bi, r, j: (bi, j, r))],
        out_shape=[jax.ShapeDtypeStruct((b, sub, d * A_WIDTH), BF16),
                   jax.ShapeDtypeStruct((b, sub, d * LANES), F32)],
        compiler_params=_cparams(("arbitrary", "arbitrary", "arbitrary")),
        name=f"band_attn_g{g}",
    )(view, view, view, view, view)
    return o.reshape(b * sub, d * A_WIDTH), lse.reshape(b * sub, d * LANES)


def _cross_attn_kernel(q_ref, k_ref, v_ref, o_ref):
    scale = HEAD_DIM ** -0.5
    for h in range(C_HEADS):
        sl = slice(h * HEAD_DIM, (h + 1) * HEAD_DIM)
        s = lax.dot_general(q_ref[0, :, sl], k_ref[0, :, sl], (((1,), (1,)), ((), ())),
                            preferred_element_type=F32) * scale
        m = jnp.max(s, axis=-1, keepdims=True)
        p = jnp.exp(s - m)
        l = jnp.sum(p, axis=-1, keepdims=True)
        o = jnp.dot(p.astype(BF16), v_ref[0, :, sl], preferred_element_type=F32) / l
        o_ref[0, :, sl] = o.astype(o_ref.dtype)


def cross_attention(qn, kn, v, tm=512):
    b, s, w = qn.shape
    m = kn.shape[1]
    return pl.pallas_call(
        _cross_attn_kernel,
        grid=(b, s // tm),
        in_specs=[pl.BlockSpec((1, tm, w), lambda bi, i: (bi, i, 0)),
                  pl.BlockSpec((1, m, w), lambda bi, i: (bi, 0, 0)),
                  pl.BlockSpec((1, m, w), lambda bi, i: (bi, 0, 0))],
        out_specs=pl.BlockSpec((1, tm, w), lambda bi, i: (bi, i, 0)),
        out_shape=jax.ShapeDtypeStruct((b, s, w), BF16),
        compiler_params=_cparams(("arbitrary", "arbitrary")),
        name="cross_attn",
    )(qn, kn, v)


def _head_sums(x, seg):
    w = seg.shape[0]
    x16 = x.astype(BF16)
    return jnp.concatenate(
        [jnp.dot(x16[:, j:j + w], seg, preferred_element_type=F32) for j in range(0, x.shape[1], w)], axis=1)


def _dot_t(a, b):
    return lax.dot_general(a, b, (((0,), (0,)), ((), ())), preferred_element_type=F32)


def _dot_nt(a, b):
    return lax.dot_general(a, b, (((1,), (1,)), ((), ())), preferred_element_type=F32)


def _rwkv_kernel(zr_ref, mu_ref, w0_ref, wwa_ref, a0_ref, g2_ref, kk_ref, ka_ref, rk_ref,
                 lnw_ref, lnb_ref, seg_ref, y_ref,
                 state_ref, carry_ref, ops_ref, yh_ref):
    t = pl.program_id(1)
    tt = zr_ref.shape[1]
    nch = tt // R_CHUNK
    c = R_CHUNK

    @pl.when(t == 0)
    def _():
        state_ref[...] = jnp.zeros_like(state_ref)
        carry_ref[...] = jnp.zeros_like(carry_ref)

    z = zr_ref[0]
    row = lax.broadcasted_iota(jnp.int32, z.shape, 0)
    prev = jnp.where(row == 0, carry_ref[...], pltpu.roll(z, 1, 0))
    carry_ref[...] = z[tt - 1:tt, :]
    xs = z + (prev - z) * mu_ref[...]

    w3 = 3 * R_WIDTH
    r = xs[:, 0:R_WIDTH]
    k = xs[:, R_WIDTH:2 * R_WIDTH]
    v = xs[:, 2 * R_WIDTH:w3]
    wa_lo = xs[:, w3:w3 + LANES]
    g_lo = xs[:, w3 + LANES:w3 + 2 * LANES]
    lane = lax.broadcasted_iota(jnp.int32, wa_lo.shape, 1)
    wa_in = jnp.where(lane < R_DECAY_LORA, jnp.tanh(wa_lo), wa_lo)
    wa = jnp.dot(wa_in.astype(BF16), wwa_ref[...], preferred_element_type=F32)
    u = -(w0_ref[...] + wa[:, :R_WIDTH])
    softplus = jnp.maximum(u, 0.0) + jnp.log(1.0 + jnp.exp(-jnp.abs(u)))
    w_raw = -softplus - 0.5
    ld = -jnp.exp(w_raw)
    a = jax.nn.sigmoid(a0_ref[...] + wa[:, R_WIDTH:])
    g = jnp.dot(jax.nn.sigmoid(g_lo).astype(BF16), g2_ref[...], preferred_element_type=F32)

    seg = seg_ref[...]
    kk = k * kk_ref[...]
    kk = kk / jnp.maximum(jnp.sqrt(_head_sums(kk * kk, seg)), 1e-12)
    k2 = k * (1.0 + (a - 1.0) * ka_ref[...])
    bonus = _head_sums(r * k2 * rk_ref[...], seg) * v

    ri = lax.broadcasted_iota(jnp.int32, (c, c), 0)
    ci = lax.broadcasted_iota(jnp.int32, (c, c), 1)
    tri = (ci <= ri).astype(BF16)
    ld_hi = ld.astype(BF16)
    ld_lo = (ld - ld_hi.astype(F32)).astype(BF16)
    lcs = []
    for ch in range(nch):
        rs = slice(ch * c, (ch + 1) * c)
        lcs.append(jnp.dot(tri, ld_hi[rs], preferred_element_type=F32)
                   + jnp.dot(tri, ld_lo[rs], preferred_element_type=F32))
    lc = jnp.concatenate(lcs, axis=0)
    e_inc = jnp.exp(lc)
    e_exc = jnp.exp(lc - ld)
    e_inv = jnp.exp(-lc)
    a_t = -kk * e_exc
    r_t = r * e_inc
    b_t = kk * a * e_inv
    k_t = k2 * e_inv
    for h in range(R_HEADS):
        hs = slice(h * R_HEAD, (h + 1) * R_HEAD)
        ops_ref[0, h] = a_t[:, hs]
        ops_ref[1, h] = r_t[:, hs]
        ops_ref[2, h] = b_t[:, hs]
        ops_ref[3, h] = k_t[:, hs]
        ops_ref[4, h] = v[:, hs]
        ops_ref[5, h] = e_inc[:, hs]

    strict = ci < ri
    incl = ci <= ri
    eye = (ci == ri)

    ri2 = lax.broadcasted_iota(jnp.int32, (c, 2 * c), 0)
    ci2 = lax.broadcasted_iota(jnp.int32, (c, 2 * c), 1)
    incl2 = jnp.bitwise_and(ci2, c - 1) <= ri2
    eye_f = jnp.where(eye, 1.0, 0.0)
    heads = range(R_HEADS)
    dot = functools.partial(jnp.dot, preferred_element_type=F32)

    def chunk_body(ch, _):
        starts = [pl.multiple_of((ch * R_CPI + sub) * c, c) for sub in range(R_CPI)]
        rows = [pl.ds(r0, c) for r0 in starts]
        items = [(sub, h) for sub in range(R_CPI) for h in heads]
        idx = range(len(items))
        at = [ops_ref[0, h, rows[sub], :] for sub, h in items]
        rt = [ops_ref[1, h, rows[sub], :] for sub, h in items]
        bt = [ops_ref[2, h, rows[sub], :] for sub, h in items]
        kt = [ops_ref[3, h, rows[sub], :] for sub, h in items]
        pc = [ops_ref[5, h, pl.ds(starts[sub] + c - 1, 1), :] for sub, h in items]
        at16 = [x.astype(BF16) for x in at]
        rt16 = [x.astype(BF16) for x in rt]
        bt16 = [x.astype(BF16) for x in bt]
        kt16 = [x.astype(BF16) for x in kt]
        v16 = [ops_ref[4, h, rows[sub], :].astype(BF16) for sub, h in items]
        bk16 = [jnp.concatenate([bt16[i], kt16[i]], axis=0) for i in idx]
        nmat = [jnp.where(strict, _dot_nt(at16[i], bt16[i]), 0.0) for i in idx]
        a_ak = [jnp.where(strict, _dot_nt(at16[i], kt16[i]), 0.0).astype(BF16) for i in idx]
        a_rbk = [jnp.where(incl2, _dot_nt(rt16[i], bk16[i]), 0.0).astype(BF16) for i in idx]
        npow = nmat
        tinv = [eye_f + nmat[i] for i in idx]
        for _i in range(5):
            np16 = [x.astype(BF16) for x in npow]
            npow = [dot(np16[i], np16[i]) for i in idx]
            tinv = [tinv[i] + dot(tinv[i].astype(BF16), npow[i].astype(BF16)) for i in idx]
        akv = [dot(a_ak[i], v16[i]).astype(BF16) for i in idx]
        apw1 = [dot(tinv[i].astype(BF16), jnp.concatenate([at16[i], akv[i]], axis=1)).astype(BF16)
                for i in idx]
        zero = jnp.zeros((c, R_HEAD), BF16)
        rhs2 = [jnp.concatenate([apw1[i], jnp.concatenate([zero, v16[i]], axis=1)], axis=0)
                for i in idx]
        bkh = [jnp.concatenate([bt[i] * pc[i], kt[i] * pc[i]], axis=0).astype(BF16) for i in idx]
        gh = [_dot_t(bkh[i], rhs2[i]) for i in idx]
        qy = [dot(a_rbk[i], rhs2[i]) for i in idx]
        for i, (sub, h) in enumerate(items):
            gm = jnp.where(eye, jnp.broadcast_to(pc[i], (c, c)), 0.0) + gh[i][:, :R_HEAD]
            qp = rt[i] + qy[i][:, :R_HEAD]
            st = state_ref[h]
            res = dot(jnp.concatenate([qp, gm], axis=0).astype(BF16), st.astype(BF16))
            yh_ref[h, rows[sub], :] = res[:c] + qy[i][:, R_HEAD:]
            state_ref[h] = res[c:] + gh[i][:, R_HEAD:]
        return 0

    lax.fori_loop(0, nch // R_CPI, chunk_body, 0)

    y = jnp.concatenate([yh_ref[h] for h in range(R_HEADS)], axis=1)
    mean = _head_sums(y, seg) * (1.0 / R_HEAD)
    dlt = y - mean
    var = _head_sums(dlt * dlt, seg) * (1.0 / R_HEAD)
    yn = dlt * lax.rsqrt(var + R_GN_EPS) * lnw_ref[...] + lnb_ref[...]
    y_ref[0] = ((yn + bonus) * g).astype(y_ref.dtype)


def rwkv7_mix(zr, mu, w0, w2, a0, a2, g2, k_k, k_a, r_k, ln_w, ln_b, tt=256):
    b, s, cols = zr.shape
    row = lambda x: x.reshape(1, -1).astype(F32)
    wwa = jnp.zeros((LANES, 2 * R_WIDTH), F32)
    wwa = wwa.at[:R_DECAY_LORA, :R_WIDTH].set(w2).at[R_DECAY_LORA:, R_WIDTH:].set(a2).astype(BF16)
    hid = jnp.arange(R_SEG) // R_HEAD
    seg = (hid[:, None] == hid[None, :]).astype(BF16)
    full = lambda shape: pl.BlockSpec(shape, lambda bi, t: (0,) * len(shape))
    return pl.pallas_call(
        _rwkv_kernel,
        grid=(b, s // tt),
        in_specs=[pl.BlockSpec((1, tt, cols), lambda bi, t: (bi, t, 0)),
                  full((1, cols)), full((1, R_WIDTH)), full((LANES, 2 * R_WIDTH)), full((1, R_WIDTH)),
                  full((R_GATE_LORA, R_WIDTH)), full((1, R_WIDTH)), full((1, R_WIDTH)), full((1, R_WIDTH)),
                  full((1, R_WIDTH)), full((1, R_WIDTH)), full((R_SEG, R_SEG))],
        out_specs=pl.BlockSpec((1, tt, R_WIDTH), lambda bi, t: (bi, t, 0)),
        out_shape=jax.ShapeDtypeStruct((b, s, R_WIDTH), BF16),
        scratch_shapes=[pltpu.VMEM((R_HEADS, R_HEAD, R_HEAD), F32),
                        pltpu.VMEM((1, cols), F32),
                        pltpu.VMEM((6, R_HEADS, tt, R_HEAD), F32),
                        pltpu.VMEM((R_HEADS, tt, R_HEAD), F32)],
        compiler_params=_cparams(("arbitrary", "arbitrary")),
        name="rwkv7_mix",
    )(zr, row(mu), row(w0), wwa, row(a0), g2.astype(BF16), row(k_k), row(k_a), row(r_k),
      row(ln_w), row(ln_b), seg)


def _pack_bf16_pairs(x):
    w = x.shape[1] // 2
    as_bits = lambda t: lax.bitcast_convert_type(t.astype(BF16).astype(F32), jnp.uint32)
    return (as_bits(x[:, w:]) & jnp.uint32(0xFFFF0000)) | (as_bits(x[:, :w]) >> 16)


def _unpack_bf16_pairs(p):
    lo = lax.bitcast_convert_type(p << 16, F32)
    hi = lax.bitcast_convert_type(p & jnp.uint32(0xFFFF0000), F32)
    return lo, hi


def _merge_kernel(x_ref, o0_ref, o1_ref, o2_ref, l0_ref, l1_ref, l2_ref, yb_ref, yc_ref, gt_ref,
                  wb_ref, wo_ref, gn_ref, rw_ref, rb_ref,
                  x1_ref, h2_ref, idx_ref, gate_ref, rank_ref, cnt_ref, base_ref, o_scr, l_scr):
    @pl.when(pl.program_id(0) == 0)
    def _():
        base_ref[...] = jnp.zeros_like(base_ref)

    tm_rows = x_ref.shape[0]
    for gi, (o_ref, l_ref) in enumerate(((o0_ref, l0_ref), (o1_ref, l1_ref), (o2_ref, l2_ref))):
        dil = A_GROUPS[gi][1]
        for r in range(dil):
            dst = pl.ds(r, tm_rows // dil, stride=dil) if dil > 1 else slice(None)
            for h in range(A_HEADS):
                lo = r * A_WIDTH + h * HEAD_DIM
                o_scr[gi, h, dst, :] = o_ref[:, lo:lo + HEAD_DIM].astype(F32)
            l_scr[gi, dst, :] = l_ref[:, r * LANES:(r + 1) * LANES]
    lses = [l_scr[gi] for gi in range(3)]
    lmax = jnp.maximum(jnp.maximum(lses[0], lses[1]), lses[2])
    es = [jnp.exp(l - lmax) for l in lses]
    inv = 1.0 / (es[0] + es[1] + es[2])
    qw = LANES // A_HEADS
    heads = []
    for h in range(A_HEADS):
        sl = slice(h * HEAD_DIM, (h + 1) * HEAD_DIM)
        acc = None
        for gi in range(3):
            alpha = (es[gi] * inv)[:, h * qw:h * qw + 1]
            term = alpha * o_scr[gi, h]
            acc = term if acc is None else acc + term
        heads.append(acc)
    ya = jnp.concatenate(heads, axis=1).astype(BF16)
    d = x_ref.shape[1]
    merged = None
    for n, yn in enumerate((ya, yb_ref[...], yc_ref[...])):
        proj = jnp.dot(yn, wb_ref[n], preferred_element_type=F32)
        term = gt_ref[:, n * d:(n + 1) * d].astype(F32) * proj
        merged = term if merged is None else merged + term
    x1 = x_ref[...] + jnp.dot(merged.astype(BF16), wo_ref[...], preferred_element_type=F32)
    x1_ref[...] = x1
    ms = jnp.mean(x1 * x1, axis=-1, keepdims=True)
    h2 = x1 * lax.rsqrt(ms + NORM_EPS) * gn_ref[...]
    h2_ref[...] = _pack_bf16_pairs(h2)
    logits = jnp.dot(h2, rw_ref[...], preferred_element_type=F32,
                     precision=lax.Precision.HIGHEST) + rb_ref[...]
    tm = logits.shape[0]
    lane = lax.broadcasted_iota(jnp.int32, logits.shape, 1)
    vals, idxs = [], []
    cur = logits
    for _k in range(TOP_K):
        m = jnp.max(cur, axis=-1, keepdims=True)
        ik = jnp.min(jnp.where(cur == m, lane, N_EXPERTS), axis=-1, keepdims=True)
        vals.append(m)
        idxs.append(ik)
        cur = jnp.where(lane == ik, -jnp.inf, cur)
    exps = [jnp.exp(vk - vals[0]) for vk in vals]
    tot = exps[0] + exps[1] + exps[2] + exps[3]
    onehots = [lane == ik for ik in idxs]
    hits = sum(jnp.where(oh, 1.0, 0.0) for oh in onehots)
    ri = lax.broadcasted_iota(jnp.int32, (tm, tm), 0)
    ci = lax.broadcasted_iota(jnp.int32, (tm, tm), 1)
    before = jnp.dot((ci < ri).astype(BF16), hits.astype(BF16), preferred_element_type=F32) + base_ref[...]
    for kk in range(TOP_K):
        idx_ref[:, kk:kk + 1] = idxs[kk]
        gate_ref[:, kk:kk + 1] = exps[kk] / tot
        rank_ref[:, kk:kk + 1] = jnp.sum(jnp.where(onehots[kk], before, 0.0), axis=-1,
                                         keepdims=True).astype(jnp.int32)
    base_ref[...] = base_ref[...] + jnp.sum(hits, axis=0, keepdims=True)
    cnt_ref[...] = base_ref[...].astype(jnp.int32)


def merge_and_route(x2d, outs, lses, yb, yc, gates, w_branch, w_out, norm_ffn, router_w, router_b, tm=512):
    n, d = x2d.shape
    rows = lambda w: pl.BlockSpec((tm, w), lambda i: (i, 0))
    packed = lambda w, dil: pl.BlockSpec((tm // dil, dil * w), lambda i: (i, 0))
    full = lambda shape: pl.BlockSpec(shape, lambda i: (0,) * len(shape))
    return pl.pallas_call(
        _merge_kernel,
        grid=(n // tm,),
        in_specs=[rows(d)] + [packed(A_WIDTH, dil) for _w, dil in A_GROUPS] + [packed(LANES, dil) for _w, dil in A_GROUPS]
                 + [rows(R_WIDTH), rows(C_WIDTH), rows(N_BRANCH * d),
                    full((N_BRANCH, A_WIDTH, d)), full((d, d)), full((1, d)), full((d, N_EXPERTS)), full((1, N_EXPERTS))],
        out_specs=[rows(d), rows(d // 2), rows(TOP_K), rows(TOP_K), rows(TOP_K), full((1, N_EXPERTS))],
        out_shape=[jax.ShapeDtypeStruct((n, d), F32), jax.ShapeDtypeStruct((n, d // 2), jnp.uint32),
                   jax.ShapeDtypeStruct((n, TOP_K), jnp.int32), jax.ShapeDtypeStruct((n, TOP_K), F32),
                   jax.ShapeDtypeStruct((n, TOP_K), jnp.int32), jax.ShapeDtypeStruct((1, N_EXPERTS), jnp.int32)],
        scratch_shapes=[pltpu.VMEM((1, N_EXPERTS), F32), pltpu.VMEM((len(A_GROUPS), A_HEADS, tm, HEAD_DIM), F32),
                        pltpu.VMEM((len(A_GROUPS), tm, LANES), F32)],
        compiler_params=_cparams(("arbitrary",)),
        name="merge_route",
    )(x2d, *outs, *lses, yb, yc, gates, w_branch.astype(BF16), w_out.astype(BF16),
      norm_ffn.reshape(1, d), router_w, router_b.reshape(1, N_EXPERTS))


def block_layout(counts, n_assign):
    counts = counts.reshape(-1)
    padded = (counts + MOE_ROWS - 1) // MOE_ROWS * MOE_ROWS
    pad_end = jnp.cumsum(padded)
    pad_start = (pad_end - padded).astype(jnp.int32)
    n_blocks = -(-n_assign // MOE_ROWS) + N_EXPERTS
    blk_row = jnp.arange(n_blocks, dtype=jnp.int32) * MOE_ROWS
    block_expert = jnp.minimum(jnp.searchsorted(pad_end, blk_row, side='right'), N_EXPERTS - 1).astype(jnp.int32)
    unused = blk_row >= pad_end[-1]
    zero_flag = (unused | (blk_row + MOE_ROWS == pad_end[block_expert])).astype(jnp.int32)
    n_used = (pad_end[-1:] // MOE_ROWS).astype(jnp.int32)
    return pad_start, block_expert, zero_flag, n_used


def _dest_kernel(ps_ref, idx_ref, rank_ref, dest_ref):
    idx = idx_ref[...]
    dest = rank_ref[...]
    for e in range(N_EXPERTS):
        dest = dest + jnp.where(idx == e, ps_ref[e], 0)
    dest_ref[...] = dest


def assignment_rows(top_idx, rank, pad_start):
    n = top_idx.shape[0]
    rows = n * TOP_K // LANES
    flat = lambda t: t.reshape(rows, LANES)
    spec = pl.BlockSpec((rows, LANES), lambda i, ps: (0, 0))
    out = pl.pallas_call(
        _dest_kernel,
        grid_spec=pltpu.PrefetchScalarGridSpec(num_scalar_prefetch=1, grid=(1,), in_specs=[spec, spec],
                                               out_specs=spec),
        out_shape=jax.ShapeDtypeStruct((rows, LANES), jnp.int32),
        compiler_params=_cparams(("arbitrary",)),
        name="assignment_rows",
    )(pad_start, flat(top_idx), flat(rank))
    return out.reshape(n, TOP_K)


def _scatter_kernel(zf_ref, dest_ref, h2_ref, xs_hbm, zeros_ref, sem, zsem):
    i = pl.program_id(0)
    tm = h2_ref.shape[0]
    nblk = zf_ref.shape[0]

    def zero_block(j):
        return pltpu.make_async_copy(zeros_ref, xs_hbm.at[pl.ds(j * MOE_ROWS, MOE_ROWS)], zsem)

    @pl.when(i == 0)
    def _():
        zeros_ref[...] = jnp.zeros_like(zeros_ref)

        def start(j, _):
            @pl.when(zf_ref[j] != 0)
            def _():
                zero_block(j).start()
            return 0

        def wait(j, _):
            @pl.when(zf_ref[j] != 0)
            def _():
                zero_block(j).wait()
            return 0

        lax.fori_loop(0, nblk, start, 0)
        lax.fori_loop(0, nblk, wait, 0)

    def body(t, _):
        for kk in range(TOP_K):
            row = dest_ref[0, 0, t * TOP_K + kk]
            pltpu.make_async_copy(h2_ref.at[pl.ds(t, 1)], xs_hbm.at[pl.ds(row, 1)], sem).start()
        return 0

    lax.fori_loop(0, tm, body, 0)
    pltpu.make_async_copy(xs_hbm.at[pl.ds(0, tm * TOP_K)], xs_hbm.at[pl.ds(0, tm * TOP_K)], sem).wait()


def scatter_rows(h2p, dest, zero_flag, n_rows, tm=256):
    n, w = h2p.shape
    dest3 = dest.reshape(n // tm, 1, tm * TOP_K)
    grid_spec = pltpu.PrefetchScalarGridSpec(
        num_scalar_prefetch=1,
        grid=(n // tm,),
        in_specs=[pl.BlockSpec((1, 1, tm * TOP_K), lambda i, zf: (i, 0, 0), memory_space=pltpu.SMEM),
                  pl.BlockSpec((tm, w), lambda i, zf: (i, 0))],
        out_specs=pl.BlockSpec(memory_space=pl.ANY),
        scratch_shapes=[pltpu.VMEM((MOE_ROWS, w), jnp.uint32), pltpu.SemaphoreType.DMA(()),
                        pltpu.SemaphoreType.DMA(())],
    )
    return pl.pallas_call(
        _scatter_kernel,
        grid_spec=grid_spec,
        out_shape=jax.ShapeDtypeStruct((n_rows, w), jnp.uint32),
        compiler_params=_cparams(("arbitrary",)),
        name="scatter_rows",
    )(zero_flag, dest3, h2p)


def _expert_kernel(be_ref, nu_ref, xs_ref, w1_ref, b1_ref, w2_ref, b2_ref, y_ref):
    i = pl.program_id(0)

    @pl.when(i < nu_ref[0])
    def _():
        lo, hi = _unpack_bf16_pairs(xs_ref[...])
        xb = jnp.concatenate([lo, hi], axis=1).astype(BF16)
        dff = w2_ref.shape[1]
        u = jnp.dot(xb, w1_ref[0], preferred_element_type=F32) + b1_ref[0]
        glu = jnp.minimum(u[:, :dff], SWIGLU_LIMIT)
        lin = jnp.clip(u[:, dff:], -SWIGLU_LIMIT, SWIGLU_LIMIT)
        act = glu * jax.nn.sigmoid(SWIGLU_ALPHA * glu) * (lin + 1.0)
        y = jnp.dot(act.astype(BF16), w2_ref[0], preferred_element_type=F32) + b2_ref[0]
        y_ref[...] = _pack_bf16_pairs(y)

    @pl.when(i >= nu_ref[0])
    def _():
        y_ref[...] = jnp.zeros_like(y_ref)


def expert_ffn(x_sorted, block_expert, n_used, w1, b1, w2, b2):
    n_rows, w = x_sorted.shape
    d = 2 * w
    nblk = n_rows // MOE_ROWS
    dff2 = w1.shape[2]
    used = lambda i, nu: jnp.minimum(i, nu[0] - 1)
    grid_spec = pltpu.PrefetchScalarGridSpec(
        num_scalar_prefetch=2,
        grid=(nblk,),
        in_specs=[
            pl.BlockSpec((MOE_ROWS, w), lambda i, be, nu: (used(i, nu), 0)),
            pl.BlockSpec((1, d, dff2), lambda i, be, nu: (be[used(i, nu)], 0, 0)),
            pl.BlockSpec((1, 1, dff2), lambda i, be, nu: (be[used(i, nu)], 0, 0)),
            pl.BlockSpec((1, dff2 // 2, d), lambda i, be, nu: (be[used(i, nu)], 0, 0)),
            pl.BlockSpec((1, 1, d), lambda i, be, nu: (be[used(i, nu)], 0, 0)),
        ],
        out_specs=pl.BlockSpec((MOE_ROWS, w), lambda i, be, nu: (i, 0)),
    )
    return pl.pallas_call(
        _expert_kernel,
        grid_spec=grid_spec,
        out_shape=jax.ShapeDtypeStruct((n_rows, w), jnp.uint32),
        compiler_params=_cparams(("arbitrary",)),
        name="expert_ffn",
    )(block_expert, n_used, x_sorted, w1, b1.reshape(N_EXPERTS, 1, dff2), w2, b2.reshape(N_EXPERTS, 1, d))


def _gather_assigned_rows(y_hbm, dest_ref, dst_ref, sem, tm):
    def body(t, _):
        for kk in range(TOP_K):
            row = dest_ref[t * TOP_K + kk]
            pltpu.make_async_copy(y_hbm.at[pl.ds(row, 1)], dst_ref.at[pl.ds(kk * tm + t, 1)], sem).start()
        return 0
    lax.fori_loop(0, tm, body, 0)


def _combine_kernel(d0_ref, dn_ref, x1_ref, gate_ref, y_hbm, o_ref, ybuf, sems):
    i = pl.program_id(0)
    nblk = pl.num_programs(0)
    slot = lax.rem(i, 2)
    tm = o_ref.shape[0]

    @pl.when(i == 0)
    def _():
        _gather_assigned_rows(y_hbm, d0_ref.at[0, 0], ybuf.at[0], sems.at[0], tm)

    @pl.when(i + 1 < nblk)
    def _():
        _gather_assigned_rows(y_hbm, dn_ref.at[0, 0], ybuf.at[1 - slot], sems.at[1 - slot], tm)

    pltpu.make_async_copy(y_hbm.at[pl.ds(0, TOP_K * tm)], ybuf.at[slot], sems.at[slot]).wait()
    acc = x1_ref[...]
    for kk in range(TOP_K):
        lo, hi = _unpack_bf16_pairs(ybuf[slot, pl.ds(kk * tm, tm), :])
        acc = acc + gate_ref[:, kk:kk + 1] * jnp.concatenate([lo, hi], axis=1)
    o_ref[...] = acc


def moe_combine(x1, gate, y_sorted, dest, tm=256):
    n, d = x1.shape
    w = y_sorted.shape[1]
    nblk = n // tm
    dest3 = dest.reshape(nblk, 1, tm * TOP_K)
    return pl.pallas_call(
        _combine_kernel,
        grid=(nblk,),
        in_specs=[pl.BlockSpec((1, 1, TOP_K * tm), lambda i: (0, 0, 0), memory_space=pltpu.SMEM),
                  pl.BlockSpec((1, 1, TOP_K * tm), lambda i: (jnp.minimum(i + 1, nblk - 1), 0, 0),
                               memory_space=pltpu.SMEM),
                  pl.BlockSpec((tm, d), lambda i: (i, 0)),
                  pl.BlockSpec((tm, TOP_K), lambda i: (i, 0)),
                  pl.BlockSpec(memory_space=pl.ANY)],
        out_specs=pl.BlockSpec((tm, d), lambda i: (i, 0)),
        out_shape=jax.ShapeDtypeStruct((n, d), F32),
        scratch_shapes=[pltpu.VMEM((2, TOP_K * tm, w), jnp.uint32), pltpu.SemaphoreType.DMA((2,))],
        compiler_params=_cparams(("arbitrary",)),
        name="moe_combine",
    )(dest3, dest3, x1, gate, y_sorted)


def kernel(x, mem, positions, norm_mix, w_in, b_gate, a_q_gain, a_k_gain, r_mu, r_w0, r_w2, r_a0, r_a2,
           r_g2, r_k_k, r_k_a, r_r_k, r_ln_w, r_ln_b, mem_norm, w_mem_kv, c_q_gain, c_k_gain, w_branch,
           w_out, norm_ffn, router_w, router_b, exp_w1, exp_b1, exp_w2, exp_b2):
    b, s, d = x.shape
    n = b * s
    depth = norm_mix.shape[0]
    n_groups = len(A_GROUPS)
    qkv_cols = n_groups * A_WIDTH
    off_k, off_v, off_r = qkv_cols, 2 * qkv_cols, 3 * qkv_cols
    off_cq = off_r + R_COLS
    off_gate = off_cq + C_WIDTH

    x2d = x.reshape(n, d)
    cos_t, sin_t = rope_tables(positions.reshape(n, 1).astype(jnp.int32))
    for l in range(depth):
        w_l = w_in[l].astype(BF16)
        h = rmsnorm_rows(x2d, norm_mix[l])
        qk_gains = jnp.stack([a_q_gain[l], a_k_gain[l]]).reshape(2, 1, HEAD_DIM)
        zr = project(h, w_l[:, off_r:off_cq], "plain", out_dtype=F32, tn=R_COLS // 2)
        cq = project(h, w_l[:, off_cq:off_gate], "headnorm", (c_q_gain[l].reshape(1, HEAD_DIM),))
        gates = project(h, w_l[:, off_gate:], "gate", (b_gate[l].reshape(1, -1),))

        shp = lambda t: t.reshape(b, s, -1)
        outs, lses = [], []
        for g, (window, dilation) in enumerate(A_GROUPS):
            assert window // dilation == ATT_BLOCK
            cols = slice(g * A_WIDTH, (g + 1) * A_WIDTH)
            w_g = jnp.concatenate([w_l[:, :off_k][:, cols], w_l[:, off_k:off_v][:, cols],
                                   w_l[:, off_v:off_r][:, cols]], axis=1)
            qkv = project_qkv(h, w_g, qk_gains, cos_t, sin_t, dilation)
            o, lse = band_attention_group(qkv, b, g, dilation)
            outs.append(o)
            lses.append(lse)

        yb = rwkv7_mix(shp(zr), r_mu[l], r_w0[l], r_w2[l], r_a0[l], r_a2[l], r_g2[l], r_k_k[l], r_k_a[l],
                       r_r_k[l].reshape(-1), r_ln_w[l], r_ln_b[l]).reshape(n, R_WIDTH)

        mlen = mem.shape[1]
        mem_n = rmsnorm_rows(mem.reshape(b * mlen, d), mem_norm[l])
        wkv = w_mem_kv[l].astype(BF16)
        ck = project(mem_n, wkv[:, :C_WIDTH], "headnorm", (c_k_gain[l].reshape(1, HEAD_DIM),))
        cv = project(mem_n, wkv[:, C_WIDTH:], "plain")
        yc = cross_attention(shp(cq), ck.reshape(b, mlen, C_WIDTH), cv.reshape(b, mlen, C_WIDTH)).reshape(n, C_WIDTH)

        x1, h2p, top_idx, gate, rank, counts = merge_and_route(
            x2d, outs, lses, yb, yc, gates, w_branch[l], w_out[l], norm_ffn[l], router_w[l], router_b[l])
        pad_start, block_expert, zero_flag, n_used = block_layout(counts, n * TOP_K)
        dest = assignment_rows(top_idx, rank, pad_start)
        x_sorted = scatter_rows(h2p, dest, zero_flag, block_expert.shape[0] * MOE_ROWS)
        y_sorted = expert_ffn(x_sorted, block_expert, n_used, exp_w1[l].astype(BF16), exp_b1[l],
                              exp_w2[l].astype(BF16), exp_b2[l])
        x2d = moe_combine(x1, gate, y_sorted, dest)
    return x2d.reshape(b, s, d)
```

```python
import functools

import jax
import jax.numpy as jnp
from jax import lax
from jax.experimental import pallas as pl
from jax.experimental.pallas import tpu as pltpu

F32 = jnp.float32
BF16 = jnp.bfloat16

NORM_EPS = 1e-6
LANES = 128
HEAD_DIM = 128
A_GROUPS = ((128, 1), (512, 4), (2048, 16))
A_HEADS = 4
A_WIDTH = A_HEADS * HEAD_DIM
ATT_BLOCK = 128
ROT_DIM = 32
ROPE_THETA = 500000.0
R_HEAD = 64
R_HEADS = 8
R_WIDTH = R_HEADS * R_HEAD
R_DECAY_LORA = 64
R_AAA_LORA = 64
R_GATE_LORA = 128
R_COLS = 3 * R_WIDTH + R_DECAY_LORA + R_AAA_LORA + R_GATE_LORA
R_GN_EPS = 64e-5
R_CHUNK = 64
R_CPI = 4
R_SEG = 256
C_HEADS = 4
C_WIDTH = C_HEADS * HEAD_DIM
N_BRANCH = 3
N_EXPERTS = 32
TOP_K = 4
SWIGLU_ALPHA = 1.702
SWIGLU_LIMIT = 7.0
MOE_ROWS = 256
NEG_BIG = -1e30

VMEM_LIMIT = 56 * 1024 * 1024


def _cparams(sem):
    return pltpu.CompilerParams(dimension_semantics=sem, vmem_limit_bytes=VMEM_LIMIT)


def _rmsnorm_kernel(x_ref, g_ref, o_ref):
    x = x_ref[...]
    ms = jnp.mean(x * x, axis=-1, keepdims=True)
    o_ref[...] = (x * lax.rsqrt(ms + NORM_EPS) * g_ref[...]).astype(o_ref.dtype)


def rmsnorm_rows(x2d, gain, tm=512):
    n, d = x2d.shape
    tm = min(tm, n)
    return pl.pallas_call(
        _rmsnorm_kernel,
        grid=(n // tm,),
        in_specs=[pl.BlockSpec((tm, d), lambda i: (i, 0)),
                  pl.BlockSpec((1, d), lambda i: (0, 0))],
        out_specs=pl.BlockSpec((tm, d), lambda i: (i, 0)),
        out_shape=jax.ShapeDtypeStruct((n, d), BF16),
        compiler_params=_cparams(("arbitrary",)),
        name="rmsnorm_rows",
    )(x2d, gain.reshape(1, d))


def _rope_table_kernel(pos_ref, freq_ref, cos_ref, sin_ref):
    ang = pos_ref[...].astype(F32) * freq_ref[...]
    lane = lax.broadcasted_iota(jnp.int32, ang.shape, 1)
    cos_ref[...] = jnp.cos(ang)
    s = jnp.sin(ang)
    half = ROT_DIM // 2
    sin_ref[...] = jnp.where(lane < half, -s, jnp.where(lane < ROT_DIM, s, 0.0))


def rope_tables(pos_col, tm=1024):
    n = pos_col.shape[0]
    half = ROT_DIM // 2
    inv_freq = ROPE_THETA ** (-jnp.arange(half, dtype=F32) / half)
    freq_row = jnp.concatenate([inv_freq, inv_freq, jnp.zeros((LANES - ROT_DIM,), F32)]).reshape(1, LANES)
    return pl.pallas_call(
        _rope_table_kernel,
        grid=(n // tm,),
        in_specs=[pl.BlockSpec((tm, 1), lambda i: (i, 0)),
                  pl.BlockSpec((1, LANES), lambda i: (0, 0))],
        out_specs=[pl.BlockSpec((tm, LANES), lambda i: (i, 0)),
                   pl.BlockSpec((tm, LANES), lambda i: (i, 0))],
        out_shape=[jax.ShapeDtypeStruct((n, LANES), F32)] * 2,
        compiler_params=_cparams(("arbitrary",)),
        name="rope_tables",
    )(pos_col, freq_row)


def _head_mean_sq(zh):
    avg = jnp.full((HEAD_DIM, HEAD_DIM), 1.0 / HEAD_DIM, BF16)
    return jnp.dot((zh * zh).astype(BF16), avg, preferred_element_type=F32)


def _cast_weight_once(w_ref, w_scr):
    @pl.when(pl.program_id(1) == 0)
    def _():
        w_scr[...] = w_ref[...].astype(w_scr.dtype)


def _proj_kernel(*refs, mode):
    h_ref, w_ref = refs[0], refs[1]
    o_ref, w_scr = refs[-2], refs[-1]
    _cast_weight_once(w_ref, w_scr)
    z = jnp.dot(h_ref[...], w_scr[...], preferred_element_type=F32)
    if mode == "plain":
        o_ref[...] = z.astype(o_ref.dtype)
    elif mode == "gate":
        o_ref[...] = jax.nn.sigmoid(z + refs[2][...]).astype(o_ref.dtype)
    else:
        gain = refs[2][...]
        for c in range(z.shape[1] // HEAD_DIM):
            zh = z[:, c * HEAD_DIM:(c + 1) * HEAD_DIM]
            zn = zh * lax.rsqrt(_head_mean_sq(zh) + NORM_EPS) * gain
            o_ref[:, c * HEAD_DIM:(c + 1) * HEAD_DIM] = zn.astype(o_ref.dtype)


def project(h, w, mode, extras=(), out_dtype=BF16, tm=1024, tn=512):
    n, k = h.shape
    m = w.shape[1]
    tm = min(tm, n)
    tn = min(tn, m)
    assert n % tm == 0 and m % tn == 0
    in_specs = [pl.BlockSpec((tm, k), lambda j, i: (i, 0)),
                pl.BlockSpec((k, tn), lambda j, i: (0, j))]
    if mode == "gate":
        in_specs.append(pl.BlockSpec((1, tn), lambda j, i: (0, j)))
    elif mode == "headnorm":
        in_specs.append(pl.BlockSpec((1, HEAD_DIM), lambda j, i: (0, 0)))
    return pl.pallas_call(
        functools.partial(_proj_kernel, mode=mode),
        grid=(m // tn, n // tm),
        in_specs=in_specs,
        out_specs=pl.BlockSpec((tm, tn), lambda j, i: (i, j)),
        out_shape=jax.ShapeDtypeStruct((n, m), out_dtype),
        scratch_shapes=[pltpu.VMEM((k, tn), BF16)],
        compiler_params=_cparams(("arbitrary", "arbitrary")),
        name="proj_" + mode,
    )(h, w, *extras)


def _qkv_proj_kernel(h_ref, w_ref, gain_ref, cos_ref, sin_ref, o_ref, w_scr, z_scr, *, dilation):
    j = pl.program_id(0)
    _cast_weight_once(w_ref, w_scr)
    z = jnp.dot(h_ref[...], w_scr[...], preferred_element_type=F32)
    half = ROT_DIM // 2

    @pl.when(j < 2)
    def _():
        mi = lax.broadcasted_iota(jnp.int32, (HEAD_DIM, HEAD_DIM), 0)
        li = lax.broadcasted_iota(jnp.int32, (HEAD_DIM, HEAD_DIM), 1)
        perm = (((li < half) & (mi == li + half)) | ((li >= half) & (li < ROT_DIM) & (mi == li - half))).astype(BF16)
        gcos = gain_ref[0, 0:1, :] * cos_ref[...]
        gsin = gain_ref[0, 1:2, :] * sin_ref[...]
        for c in range(A_HEADS):
            zh = z[:, c * HEAD_DIM:(c + 1) * HEAD_DIM]
            partner = jnp.dot(zh.astype(BF16), perm, preferred_element_type=F32)
            z_scr[c] = lax.rsqrt(_head_mean_sq(zh) + NORM_EPS) * (zh * gcos + partner * gsin)

    @pl.when(j == 2)
    def _():
        for c in range(A_HEADS):
            z_scr[c] = z[:, c * HEAD_DIM:(c + 1) * HEAD_DIM]

    rows = z_scr.shape[1] // dilation
    for r in range(dilation):
        for c in range(A_HEADS):
            src = pl.ds(r, rows, stride=dilation) if dilation > 1 else slice(None)
            lo = r * A_WIDTH + c * HEAD_DIM
            o_ref[:, lo:lo + HEAD_DIM] = z_scr[c, src, :].astype(o_ref.dtype)


def rotary_gains(q_gain, k_gain):
    half = ROT_DIM // 2
    lane = jnp.arange(HEAD_DIM)
    partner = jnp.where(lane < half, lane + half, jnp.where(lane < ROT_DIM, lane - half, lane))
    return jnp.stack([jnp.stack([g, g[partner]]) for g in (q_gain, k_gain)])


def project_qkv(h, w_qkv, gains, cos_t, sin_t, dilation, tm=1024):
    n, k = h.shape
    d = dilation
    return pl.pallas_call(
        functools.partial(_qkv_proj_kernel, dilation=d),
        grid=(3, n // tm),
        in_specs=[pl.BlockSpec((tm, k), lambda j, i: (i, 0)),
                  pl.BlockSpec((k, A_WIDTH), lambda j, i: (0, j)),
                  pl.BlockSpec((1, 2, HEAD_DIM), lambda j, i: (jnp.minimum(j, 1), 0, 0)),
                  pl.BlockSpec((tm, LANES), lambda j, i: (i, 0)),
                  pl.BlockSpec((tm, LANES), lambda j, i: (i, 0))],
        out_specs=pl.BlockSpec((tm // d, d * A_WIDTH), lambda j, i: (i, j)),
        out_shape=jax.ShapeDtypeStruct((n // d, 3 * d * A_WIDTH), BF16),
        scratch_shapes=[pltpu.VMEM((k, A_WIDTH), BF16), pltpu.VMEM((A_HEADS, tm, HEAD_DIM), F32)],
        compiler_params=_cparams(("arbitrary", "arbitrary")),
        name=f"proj_qkv_d{d}",
    )(h, w_qkv, gains, cos_t, sin_t)


def _band_attn_kernel(q_ref, kp_ref, kc_ref, vp_ref, vc_ref, o_ref, lse_ref):
    blk = pl.program_id(2)
    scale = HEAD_DIM ** -0.5
    nq = ATT_BLOCK
    qi = lax.broadcasted_iota(jnp.int32, (nq, 2 * nq), 0)
    ki = lax.broadcasted_iota(jnp.int32, (nq, 2 * nq), 1)
    rel = qi + nq - ki
    valid = (rel >= 0) & (rel <= nq) & ((blk * nq - nq + ki) >= 0)
    lses = []
    for h in range(A_HEADS):
        sl = slice(h * HEAD_DIM, (h + 1) * HEAD_DIM)
        qh = q_ref[0, :, sl]
        kh = jnp.concatenate([kp_ref[0, :, sl], kc_ref[0, :, sl]], axis=0)
        vh = jnp.concatenate([vp_ref[0, :, sl], vc_ref[0, :, sl]], axis=0)
        s = lax.dot_general(qh, kh, (((1,), (1,)), ((), ())), preferred_element_type=F32) * scale
        s = jnp.where(valid, s, NEG_BIG)
        m = jnp.max(s, axis=-1, keepdims=True)
        p = jnp.exp(s - m)
        l = jnp.sum(p, axis=-1, keepdims=True)
        o = jnp.dot(p.astype(BF16), vh, preferred_element_type=F32) / l
        o_ref[0, :, sl] = o.astype(o_ref.dtype)
        lses.append(jnp.broadcast_to(m + jnp.log(l), (nq, LANES // A_HEADS)))
    lse_ref[0] = jnp.concatenate(lses, axis=1)


def band_attention_group(qkv, b, g, dilation):
    d = dilation
    sub = qkv.shape[0] // b
    nblk = sub // ATT_BLOCK
    view = qkv.reshape(b, sub, 3 * d * A_WIDTH)
    cur = lambda t: (lambda bi, r, j: (bi, j, t * d + r))
    prev = lambda t: (lambda bi, r, j: (bi, jnp.maximum(j - 1, 0), t * d + r))
    blk = (1, ATT_BLOCK, A_WIDTH)
    o, lse = pl.pallas_call(
        _band_attn_kernel,
        grid=(b, d, nblk),
        in_specs=[pl.BlockSpec(blk, cur(0)), pl.BlockSpec(blk, prev(1)), pl.BlockSpec(blk, cur(1)),
                  pl.BlockSpec(blk, prev(2)), pl.BlockSpec(blk, cur(2))],
        out_specs=[pl.BlockSpec(blk, lambda bi, r, j: (bi, j, r)),
                   pl.BlockSpec((1, ATT_BLOCK, LANES), lambda bi, r, j: (bi, j, r))],
        out_shape=[jax.ShapeDtypeStruct((b, sub, d * A_WIDTH), BF16),
                   jax.ShapeDtypeStruct((b, sub, d * LANES), F32)],
        compiler_params=_cparams(("arbitrary", "arbitrary", "arbitrary")),
        name=f"band_attn_g{g}",
    )(view, view, view, view, view)
    return o.reshape(b * sub, d * A_WIDTH), lse.reshape(b * sub, d * LANES)


def _cross_attn_kernel(q_ref, k_ref, v_ref, o_ref):
    scale = HEAD_DIM ** -0.5
    for h in range(C_HEADS):
        sl = slice(h * HEAD_DIM, (h + 1) * HEAD_DIM)
        s = lax.dot_general(q_ref[0, :, sl], k_ref[0, :, sl], (((1,), (1,)), ((), ())),
                            preferred_element_type=F32) * scale
        m = jnp.max(s, axis=-1, keepdims=True)
        p = jnp.exp(s - m)
        l = jnp.sum(p, axis=-1, keepdims=True)
        o = jnp.dot(p.astype(BF16), v_ref[0, :, sl], preferred_element_type=F32) / l
        o_ref[0, :, sl] = o.astype(o_ref.dtype)


def cross_attention(qn, kn, v, tm=512):
    b, s, w = qn.shape
    m = kn.shape[1]
    return pl.pallas_call(
        _cross_attn_kernel,
        grid=(b, s // tm),
        in_specs=[pl.BlockSpec((1, tm, w), lambda bi, i: (bi, i, 0)),
                  pl.BlockSpec((1, m, w), lambda bi, i: (bi, 0, 0)),
                  pl.BlockSpec((1, m, w), lambda bi, i: (bi, 0, 0))],
        out_specs=pl.BlockSpec((1, tm, w), lambda bi, i: (bi, i, 0)),
        out_shape=jax.ShapeDtypeStruct((b, s, w), BF16),
        compiler_params=_cparams(("arbitrary", "arbitrary")),
        name="cross_attn",
    )(qn, kn, v)


def _head_sums(x, seg):
    w = seg.shape[0]
    x16 = x.astype(BF16)
    return jnp.concatenate(
        [jnp.dot(x16[:, j:j + w], seg, preferred_element_type=F32) for j in range(0, x.shape[1], w)], axis=1)


def _dot_t(a, b):
    return lax.dot_general(a, b, (((0,), (0,)), ((), ())), preferred_element_type=F32)


def _dot_nt(a, b):
    return lax.dot_general(a, b, (((1,), (1,)), ((), ())), preferred_element_type=F32)


def _rwkv_kernel(zr_ref, mu_ref, w0_ref, wwa_ref, a0_ref, g2_ref, kk_ref, ka_ref, rk_ref,
                 lnw_ref, lnb_ref, seg_ref, y_ref,
                 state_ref, carry_ref, ops_ref, yh_ref):
    t = pl.program_id(1)
    tt = zr_ref.shape[1]
    nch = tt // R_CHUNK
    c = R_CHUNK

    @pl.when(t == 0)
    def _():
        state_ref[...] = jnp.zeros_like(state_ref)
        carry_ref[...] = jnp.zeros_like(carry_ref)

    z = zr_ref[0]
    row = lax.broadcasted_iota(jnp.int32, z.shape, 0)
    prev = jnp.where(row == 0, carry_ref[...], pltpu.roll(z, 1, 0))
    carry_ref[...] = z[tt - 1:tt, :]
    xs = z + (prev - z) * mu_ref[...]

    w3 = 3 * R_WIDTH
    r = xs[:, 0:R_WIDTH]
    k = xs[:, R_WIDTH:2 * R_WIDTH]
    v = xs[:, 2 * R_WIDTH:w3]
    wa_lo = xs[:, w3:w3 + LANES]
    g_lo = xs[:, w3 + LANES:w3 + 2 * LANES]
    lane = lax.broadcasted_iota(jnp.int32, wa_lo.shape, 1)
    wa_in = jnp.where(lane < R_DECAY_LORA, jnp.tanh(wa_lo), wa_lo)
    wa = jnp.dot(wa_in.astype(BF16), wwa_ref[...], preferred_element_type=F32)
    u = -(w0_ref[...] + wa[:, :R_WIDTH])
    softplus = jnp.maximum(u, 0.0) + jnp.log(1.0 + jnp.exp(-jnp.abs(u)))
    w_raw = -softplus - 0.5
    ld = -jnp.exp(w_raw)
    a = jax.nn.sigmoid(a0_ref[...] + wa[:, R_WIDTH:])
    g = jnp.dot(jax.nn.sigmoid(g_lo).astype(BF16), g2_ref[...], preferred_element_type=F32)

    seg = seg_ref[...]
    kk = k * kk_ref[...]
    kk = kk / jnp.maximum(jnp.sqrt(_head_sums(kk * kk, seg)), 1e-12)
    k2 = k * (1.0 + (a - 1.0) * ka_ref[...])
    bonus = _head_sums(r * k2 * rk_ref[...], seg) * v

    ri = lax.broadcasted_iota(jnp.int32, (c, c), 0)
    ci = lax.broadcasted_iota(jnp.int32, (c, c), 1)
    tri = (ci <= ri).astype(BF16)
    ld_hi = ld.astype(BF16)
    ld_lo = (ld - ld_hi.astype(F32)).astype(BF16)
    lcs = []
    for ch in range(nch):
        rs = slice(ch * c, (ch + 1) * c)
        lcs.append(jnp.dot(tri, ld_hi[rs], preferred_element_type=F32)
                   + jnp.dot(tri, ld_lo[rs], preferred_element_type=F32))
    lc = jnp.concatenate(lcs, axis=0)
    e_inc = jnp.exp(lc)
    e_exc = jnp.exp(lc - ld)
    e_inv = jnp.exp(-lc)
    a_t = -kk * e_exc
    r_t = r * e_inc
    b_t = kk * a * e_inv
    k_t = k2 * e_inv
    for h in range(R_HEADS):
        hs = slice(h * R_HEAD, (h + 1) * R_HEAD)
        ops_ref[0, h] = a_t[:, hs]
        ops_ref[1, h] = r_t[:, hs]
        ops_ref[2, h] = b_t[:, hs]
        ops_ref[3, h] = k_t[:, hs]
        ops_ref[4, h] = v[:, hs]
        ops_ref[5, h] = e_inc[:, hs]

    strict = ci < ri
    incl = ci <= ri
    eye = (ci == ri)

    ri2 = lax.broadcasted_iota(jnp.int32, (c, 2 * c), 0)
    ci2 = lax.broadcasted_iota(jnp.int32, (c, 2 * c), 1)
    incl2 = jnp.bitwise_and(ci2, c - 1) <= ri2
    eye_f = jnp.where(eye, 1.0, 0.0)
    heads = range(R_HEADS)
    dot = functools.partial(jnp.dot, preferred_element_type=F32)

    def chunk_body(ch, _):
        starts = [pl.multiple_of((ch * R_CPI + sub) * c, c) for sub in range(R_CPI)]
        rows = [pl.ds(r0, c) for r0 in starts]
        items = [(sub, h) for sub in range(R_CPI) for h in heads]
        idx = range(len(items))
        at = [ops_ref[0, h, rows[sub], :] for sub, h in items]
        rt = [ops_ref[1, h, rows[sub], :] for sub, h in items]
        bt = [ops_ref[2, h, rows[sub], :] for sub, h in items]
        kt = [ops_ref[3, h, rows[sub], :] for sub, h in items]
        pc = [ops_ref[5, h, pl.ds(starts[sub] + c - 1, 1), :] for sub, h in items]
        at16 = [x.astype(BF16) for x in at]
        rt16 = [x.astype(BF16) for x in rt]
        bt16 = [x.astype(BF16) for x in bt]
        kt16 = [x.astype(BF16) for x in kt]
        v16 = [ops_ref[4, h, rows[sub], :].astype(BF16) for sub, h in items]
        bk16 = [jnp.concatenate([bt16[i], kt16[i]], axis=0) for i in idx]
        nmat = [jnp.where(strict, _dot_nt(at16[i], bt16[i]), 0.0) for i in idx]
        a_ak = [jnp.where(strict, _dot_nt(at16[i], kt16[i]), 0.0).astype(BF16) for i in idx]
        a_rbk = [jnp.where(incl2, _dot_nt(rt16[i], bk16[i]), 0.0).astype(BF16) for i in idx]
        npow = nmat
        tinv = [eye_f + nmat[i] for i in idx]
        for _i in range(5):
            np16 = [x.astype(BF16) for x in npow]
            npow = [dot(np16[i], np16[i]) for i in idx]
            tinv = [tinv[i] + dot(tinv[i].astype(BF16), npow[i].astype(BF16)) for i in idx]
        akv = [dot(a_ak[i], v16[i]).astype(BF16) for i in idx]
        apw1 = [dot(tinv[i].astype(BF16), jnp.concatenate([at16[i], akv[i]], axis=1)).astype(BF16)
                for i in idx]
        zero = jnp.zeros((c, R_HEAD), BF16)
        rhs2 = [jnp.concatenate([apw1[i], jnp.concatenate([zero, v16[i]], axis=1)], axis=0)
                for i in idx]
        bkh = [jnp.concatenate([bt[i] * pc[i], kt[i] * pc[i]], axis=0).astype(BF16) for i in idx]
        gh = [_dot_t(bkh[i], rhs2[i]) for i in idx]
        qy = [dot(a_rbk[i], rhs2[i]) for i in idx]
        for i, (sub, h) in enumerate(items):
            gm = jnp.where(eye, jnp.broadcast_to(pc[i], (c, c)), 0.0) + gh[i][:, :R_HEAD]
            qp = rt[i] + qy[i][:, :R_HEAD]
            st = state_ref[h]
            res = dot(jnp.concatenate([qp, gm], axis=0).astype(BF16), st.astype(BF16))
            yh_ref[h, rows[sub], :] = res[:c] + qy[i][:, R_HEAD:]
            state_ref[h] = res[c:] + gh[i][:, R_HEAD:]
        return 0

    lax.fori_loop(0, nch // R_CPI, chunk_body, 0)

    y = jnp.concatenate([yh_ref[h] for h in range(R_HEADS)], axis=1)
    mean = _head_sums(y, seg) * (1.0 / R_HEAD)
    dlt = y - mean
    var = _head_sums(dlt * dlt, seg) * (1.0 / R_HEAD)
    yn = dlt * lax.rsqrt(var + R_GN_EPS) * lnw_ref[...] + lnb_ref[...]
    y_ref[0] = ((yn + bonus) * g).astype(y_ref.dtype)


def rwkv7_mix(zr, mu, w0, w2, a0, a2, g2, k_k, k_a, r_k, ln_w, ln_b, tt=256):
    b, s, cols = zr.shape
    row = lambda x: x.reshape(1, -1).astype(F32)
    wwa = jnp.zeros((LANES, 2 * R_WIDTH), F32)
    wwa = wwa.at[:R_DECAY_LORA, :R_WIDTH].set(w2).at[R_DECAY_LORA:, R_WIDTH:].set(a2).astype(BF16)
    hid = jnp.arange(R_SEG) // R_HEAD
    seg = (hid[:, None] == hid[None, :]).astype(BF16)
    full = lambda shape: pl.BlockSpec(shape, lambda bi, t: (0,) * len(shape))
    return pl.pallas_call(
        _rwkv_kernel,
        grid=(b, s // tt),
        in_specs=[pl.BlockSpec((1, tt, cols), lambda bi, t: (bi, t, 0)),
                  full((1, cols)), full((1, R_WIDTH)), full((LANES, 2 * R_WIDTH)), full((1, R_WIDTH)),
                  full((R_GATE_LORA, R_WIDTH)), full((1, R_WIDTH)), full((1, R_WIDTH)), full((1, R_WIDTH)),
                  full((1, R_WIDTH)), full((1, R_WIDTH)), full((R_SEG, R_SEG))],
        out_specs=pl.BlockSpec((1, tt, R_WIDTH), lambda bi, t: (bi, t, 0)),
        out_shape=jax.ShapeDtypeStruct((b, s, R_WIDTH), BF16),
        scratch_shapes=[pltpu.VMEM((R_HEADS, R_HEAD, R_HEAD), F32),
                        pltpu.VMEM((1, cols), F32),
                        pltpu.VMEM((6, R_HEADS, tt, R_HEAD), F32),
                        pltpu.VMEM((R_HEADS, tt, R_HEAD), F32)],
        compiler_params=_cparams(("arbitrary", "arbitrary")),
        name="rwkv7_mix",
    )(zr, row(mu), row(w0), wwa, row(a0), g2.astype(BF16), row(k_k), row(k_a), row(r_k),
      row(ln_w), row(ln_b), seg)


def _merge_kernel(x_ref, o0_ref, o1_ref, o2_ref, l0_ref, l1_ref, l2_ref, yb_ref, yc_ref, gt_ref,
                  wb_ref, wo_ref, gn_ref, rw_ref, rb_ref,
                  x1_ref, h2_ref, idx_ref, gate_ref, rank_ref, cnt_ref, base_ref, o_scr, l_scr):
    @pl.when(pl.program_id(0) == 0)
    def _():
        base_ref[...] = jnp.zeros_like(base_ref)

    tm_rows = x_ref.shape[0]
    for gi, (o_ref, l_ref) in enumerate(((o0_ref, l0_ref), (o1_ref, l1_ref), (o2_ref, l2_ref))):
        dil = A_GROUPS[gi][1]
        for r in range(dil):
            dst = pl.ds(r, tm_rows // dil, stride=dil) if dil > 1 else slice(None)
            for h in range(A_HEADS):
                lo = r * A_WIDTH + h * HEAD_DIM
                o_scr[gi, h, dst, :] = o_ref[:, lo:lo + HEAD_DIM].astype(F32)
            l_scr[gi, dst, :] = l_ref[:, r * LANES:(r + 1) * LANES]
    lses = [l_scr[gi] for gi in range(3)]
    lmax = jnp.maximum(jnp.maximum(lses[0], lses[1]), lses[2])
    es = [jnp.exp(l - lmax) for l in lses]
    inv = 1.0 / (es[0] + es[1] + es[2])
    qw = LANES // A_HEADS
    heads = []
    for h in range(A_HEADS):
        sl = slice(h * HEAD_DIM, (h + 1) * HEAD_DIM)
        acc = None
        for gi in range(3):
            alpha = (es[gi] * inv)[:, h * qw:h * qw + 1]
            term = alpha * o_scr[gi, h]
            acc = term if acc is None else acc + term
        heads.append(acc)
    ya = jnp.concatenate(heads, axis=1).astype(BF16)
    d = x_ref.shape[1]
    merged = None
    for n, yn in enumerate((ya, yb_ref[...], yc_ref[...])):
        proj = jnp.dot(yn, wb_ref[n], preferred_element_type=F32)
        term = gt_ref[:, n * d:(n + 1) * d].astype(F32) * proj
        merged = term if merged is None else merged + term
    x1 = x_ref[...] + jnp.dot(merged.astype(BF16), wo_ref[...], preferred_element_type=F32)
    x1_ref[...] = x1
    ms = jnp.mean(x1 * x1, axis=-1, keepdims=True)
    h2 = x1 * lax.rsqrt(ms + NORM_EPS) * gn_ref[...]
    h2_ref[...] = h2
    logits = jnp.dot(h2, rw_ref[...], preferred_element_type=F32,
                     precision=lax.Precision.HIGHEST) + rb_ref[...]
    tm = logits.shape[0]
    lane = lax.broadcasted_iota(jnp.int32, logits.shape, 1)
    vals, idxs = [], []
    cur = logits
    for _k in range(TOP_K):
        m = jnp.max(cur, axis=-1, keepdims=True)
        ik = jnp.min(jnp.where(cur == m, lane, N_EXPERTS), axis=-1, keepdims=True)
        vals.append(m)
        idxs.append(ik)
        cur = jnp.where(lane == ik, -jnp.inf, cur)
    exps = [jnp.exp(vk - vals[0]) for vk in vals]
    tot = exps[0] + exps[1] + exps[2] + exps[3]
    onehots = [lane == ik for ik in idxs]
    hits = sum(jnp.where(oh, 1.0, 0.0) for oh in onehots)
    ri = lax.broadcasted_iota(jnp.int32, (tm, tm), 0)
    ci = lax.broadcasted_iota(jnp.int32, (tm, tm), 1)
    before = jnp.dot((ci < ri).astype(BF16), hits.astype(BF16), preferred_element_type=F32) + base_ref[...]
    for kk in range(TOP_K):
        idx_ref[:, kk:kk + 1] = idxs[kk]
        gate_ref[:, kk:kk + 1] = exps[kk] / tot
        rank_ref[:, kk:kk + 1] = jnp.sum(jnp.where(onehots[kk], before, 0.0), axis=-1,
                                         keepdims=True).astype(jnp.int32)
    base_ref[...] = base_ref[...] + jnp.sum(hits, axis=0, keepdims=True)
    cnt_ref[...] = base_ref[...].astype(jnp.int32)


def merge_and_route(x2d, outs, lses, yb, yc, gates, w_branch, w_out, norm_ffn, router_w, router_b, tm=512):
    n, d = x2d.shape
    rows = lambda w: pl.BlockSpec((tm, w), lambda i: (i, 0))
    packed = lambda w, dil: pl.BlockSpec((tm // dil, dil * w), lambda i: (i, 0))
    full = lambda shape: pl.BlockSpec(shape, lambda i: (0,) * len(shape))
    return pl.pallas_call(
        _merge_kernel,
        grid=(n // tm,),
        in_specs=[rows(d)] + [packed(A_WIDTH, dil) for _w, dil in A_GROUPS] + [packed(LANES, dil) for _w, dil in A_GROUPS]
                 + [rows(R_WIDTH), rows(C_WIDTH), rows(N_BRANCH * d),
                    full((N_BRANCH, A_WIDTH, d)), full((d, d)), full((1, d)), full((d, N_EXPERTS)), full((1, N_EXPERTS))],
        out_specs=[rows(d), rows(d), rows(TOP_K), rows(TOP_K), rows(TOP_K), full((1, N_EXPERTS))],
        out_shape=[jax.ShapeDtypeStruct((n, d), F32), jax.ShapeDtypeStruct((n, d), F32),
                   jax.ShapeDtypeStruct((n, TOP_K), jnp.int32), jax.ShapeDtypeStruct((n, TOP_K), F32),
                   jax.ShapeDtypeStruct((n, TOP_K), jnp.int32), jax.ShapeDtypeStruct((1, N_EXPERTS), jnp.int32)],
        scratch_shapes=[pltpu.VMEM((1, N_EXPERTS), F32), pltpu.VMEM((len(A_GROUPS), A_HEADS, tm, HEAD_DIM), F32),
                        pltpu.VMEM((len(A_GROUPS), tm, LANES), F32)],
        compiler_params=_cparams(("arbitrary",)),
        name="merge_route",
    )(x2d, *outs, *lses, yb, yc, gates, w_branch.astype(BF16), w_out.astype(BF16),
      norm_ffn.reshape(1, d), router_w, router_b.reshape(1, N_EXPERTS))


def block_layout(counts, n_assign):
    counts = counts.reshape(-1)
    padded = (counts + MOE_ROWS - 1) // MOE_ROWS * MOE_ROWS
    pad_end = jnp.cumsum(padded)
    pad_start = (pad_end - padded).astype(jnp.int32)
    n_blocks = -(-n_assign // MOE_ROWS) + N_EXPERTS
    blk_row = jnp.arange(n_blocks, dtype=jnp.int32) * MOE_ROWS
    owner = jnp.sum((blk_row[:, None] >= pad_end[None, :]).astype(jnp.int32), axis=1)
    block_expert = jnp.minimum(owner, N_EXPERTS - 1).astype(jnp.int32)
    unused = blk_row >= pad_end[-1]
    zero_flag = (unused | (blk_row + MOE_ROWS == pad_end[block_expert])).astype(jnp.int32)
    n_used = (pad_end[-1:] // MOE_ROWS).astype(jnp.int32)
    return pad_start, block_expert, zero_flag, n_used


def _dest_kernel(ps_ref, idx_ref, rank_ref, dest_ref):
    idx = idx_ref[...]
    dest = rank_ref[...]
    for e in range(N_EXPERTS):
        dest = dest + jnp.where(idx == e, ps_ref[e], 0)
    dest_ref[...] = dest


def assignment_rows(top_idx, rank, pad_start):
    n = top_idx.shape[0]
    rows = n * TOP_K // LANES
    flat = lambda t: t.reshape(rows, LANES)
    spec = pl.BlockSpec((rows, LANES), lambda i, ps: (0, 0))
    out = pl.pallas_call(
        _dest_kernel,
        grid_spec=pltpu.PrefetchScalarGridSpec(num_scalar_prefetch=1, grid=(1,), in_specs=[spec, spec],
                                               out_specs=spec),
        out_shape=jax.ShapeDtypeStruct((rows, LANES), jnp.int32),
        compiler_params=_cparams(("arbitrary",)),
        name="assignment_rows",
    )(pad_start, flat(top_idx), flat(rank))
    return out.reshape(n, TOP_K)


def _scatter_kernel(zf_ref, dest_ref, h2_ref, xs_hbm, zeros_ref, sem, zsem):
    i = pl.program_id(0)
    tm = h2_ref.shape[0]
    nblk = zf_ref.shape[0]

    def zero_block(j):
        return pltpu.make_async_copy(zeros_ref, xs_hbm.at[pl.ds(j * MOE_ROWS, MOE_ROWS)], zsem)

    @pl.when(i == 0)
    def _():
        zeros_ref[...] = jnp.zeros_like(zeros_ref)

        def start(j, _):
            @pl.when(zf_ref[j] != 0)
            def _():
                zero_block(j).start()
            return 0

        def wait(j, _):
            @pl.when(zf_ref[j] != 0)
            def _():
                zero_block(j).wait()
            return 0

        lax.fori_loop(0, nblk, start, 0)
        lax.fori_loop(0, nblk, wait, 0)

    def body(t, _):
        for kk in range(TOP_K):
            row = dest_ref[0, 0, t * TOP_K + kk]
            pltpu.make_async_copy(h2_ref.at[pl.ds(t, 1)], xs_hbm.at[pl.ds(row, 1)], sem).start()
        return 0

    lax.fori_loop(0, tm, body, 0)
    pltpu.make_async_copy(xs_hbm.at[pl.ds(0, tm * TOP_K)], xs_hbm.at[pl.ds(0, tm * TOP_K)], sem).wait()


def scatter_rows(h2, dest, zero_flag, n_rows, tm=256):
    n, w = h2.shape
    dest3 = dest.reshape(n // tm, 1, tm * TOP_K)
    grid_spec = pltpu.PrefetchScalarGridSpec(
        num_scalar_prefetch=1,
        grid=(n // tm,),
        in_specs=[pl.BlockSpec((1, 1, tm * TOP_K), lambda i, zf: (i, 0, 0), memory_space=pltpu.SMEM),
                  pl.BlockSpec((tm, w), lambda i, zf: (i, 0))],
        out_specs=pl.BlockSpec(memory_space=pl.ANY),
        scratch_shapes=[pltpu.VMEM((MOE_ROWS, w), h2.dtype), pltpu.SemaphoreType.DMA(()),
                        pltpu.SemaphoreType.DMA(())],
    )
    return pl.pallas_call(
        _scatter_kernel,
        grid_spec=grid_spec,
        out_shape=jax.ShapeDtypeStruct((n_rows, w), h2.dtype),
        compiler_params=_cparams(("arbitrary",)),
        name="scatter_rows",
    )(zero_flag, dest3, h2)


def _expert_kernel(be_ref, nu_ref, xs_ref, w1_ref, b1_ref, w2_ref, b2_ref, y_ref, w1_scr, w2_scr):
    i = pl.program_id(0)
    prev = be_ref[jnp.maximum(i - 1, 0)]

    @pl.when((i < nu_ref[0]) & ((i == 0) | (be_ref[i] != prev)))
    def _():
        w1_scr[...] = w1_ref[0].astype(BF16)
        w2_scr[...] = w2_ref[0].astype(BF16)

    @pl.when(i < nu_ref[0])
    def _():
        xb = xs_ref[...].astype(BF16)
        dff = w2_ref.shape[1]
        u = jnp.dot(xb, w1_scr[...], preferred_element_type=F32) + b1_ref[0]
        glu = jnp.minimum(u[:, :dff], SWIGLU_LIMIT)
        lin = jnp.clip(u[:, dff:], -SWIGLU_LIMIT, SWIGLU_LIMIT)
        act = glu * jax.nn.sigmoid(SWIGLU_ALPHA * glu) * (lin + 1.0)
        y = jnp.dot(act.astype(BF16), w2_scr[...], preferred_element_type=F32) + b2_ref[0]
        y_ref[...] = y

    @pl.when(i >= nu_ref[0])
    def _():
        y_ref[...] = jnp.zeros_like(y_ref)


def expert_ffn(x_sorted, block_expert, n_used, w1, b1, w2, b2):
    n_rows, d = x_sorted.shape
    w = d
    nblk = n_rows // MOE_ROWS
    dff2 = w1.shape[2]
    used = lambda i, nu: jnp.minimum(i, nu[0] - 1)
    grid_spec = pltpu.PrefetchScalarGridSpec(
        num_scalar_prefetch=2,
        grid=(nblk,),
        in_specs=[
            pl.BlockSpec((MOE_ROWS, w), lambda i, be, nu: (used(i, nu), 0)),
            pl.BlockSpec((1, d, dff2), lambda i, be, nu: (be[used(i, nu)], 0, 0)),
            pl.BlockSpec((1, 1, dff2), lambda i, be, nu: (be[used(i, nu)], 0, 0)),
            pl.BlockSpec((1, dff2 // 2, d), lambda i, be, nu: (be[used(i, nu)], 0, 0)),
            pl.BlockSpec((1, 1, d), lambda i, be, nu: (be[used(i, nu)], 0, 0)),
        ],
        out_specs=pl.BlockSpec((MOE_ROWS, w), lambda i, be, nu: (i, 0)),
        scratch_shapes=[pltpu.VMEM((d, dff2), BF16), pltpu.VMEM((dff2 // 2, d), BF16)],
    )
    return pl.pallas_call(
        _expert_kernel,
        grid_spec=grid_spec,
        out_shape=jax.ShapeDtypeStruct((n_rows, w), F32),
        compiler_params=_cparams(("arbitrary",)),
        name="expert_ffn",
    )(block_expert, n_used, x_sorted, w1, b1.reshape(N_EXPERTS, 1, dff2), w2, b2.reshape(N_EXPERTS, 1, d))


def _gather_assigned_rows(y_hbm, dest_ref, dst_ref, sem, tm):
    def body(t, _):
        for kk in range(TOP_K):
            row = dest_ref[t * TOP_K + kk]
            pltpu.make_async_copy(y_hbm.at[pl.ds(row, 1)], dst_ref.at[pl.ds(kk * tm + t, 1)], sem).start()
        return 0
    lax.fori_loop(0, tm, body, 0)


def _combine_kernel(d0_ref, dn_ref, x1_ref, gate_ref, y_hbm, o_ref, ybuf, sems):
    i = pl.program_id(0)
    nblk = pl.num_programs(0)
    slot = lax.rem(i, 2)
    tm = o_ref.shape[0]

    @pl.when(i == 0)
    def _():
        _gather_assigned_rows(y_hbm, d0_ref.at[0, 0], ybuf.at[0], sems.at[0], tm)

    @pl.when(i + 1 < nblk)
    def _():
        _gather_assigned_rows(y_hbm, dn_ref.at[0, 0], ybuf.at[1 - slot], sems.at[1 - slot], tm)

    pltpu.make_async_copy(y_hbm.at[pl.ds(0, TOP_K * tm)], ybuf.at[slot], sems.at[slot]).wait()
    acc = x1_ref[...]
    for kk in range(TOP_K):
        acc = acc + gate_ref[:, kk:kk + 1] * ybuf[slot, pl.ds(kk * tm, tm), :]
    o_ref[...] = acc


def moe_combine(x1, gate, y_sorted, dest, tm=256):
    n, d = x1.shape
    w = y_sorted.shape[1]
    nblk = n // tm
    dest3 = dest.reshape(nblk, 1, tm * TOP_K)
    return pl.pallas_call(
        _combine_kernel,
        grid=(nblk,),
        in_specs=[pl.BlockSpec((1, 1, TOP_K * tm), lambda i: (0, 0, 0), memory_space=pltpu.SMEM),
                  pl.BlockSpec((1, 1, TOP_K * tm), lambda i: (jnp.minimum(i + 1, nblk - 1), 0, 0),
                               memory_space=pltpu.SMEM),
                  pl.BlockSpec((tm, d), lambda i: (i, 0)),
                  pl.BlockSpec((tm, TOP_K), lambda i: (i, 0)),
                  pl.BlockSpec(memory_space=pl.ANY)],
        out_specs=pl.BlockSpec((tm, d), lambda i: (i, 0)),
        out_shape=jax.ShapeDtypeStruct((n, d), F32),
        scratch_shapes=[pltpu.VMEM((2, TOP_K * tm, w), y_sorted.dtype), pltpu.SemaphoreType.DMA((2,))],
        compiler_params=_cparams(("arbitrary",)),
        name="moe_combine",
    )(dest3, dest3, x1, gate, y_sorted)


def kernel(x, mem, positions, norm_mix, w_in, b_gate, a_q_gain, a_k_gain, r_mu, r_w0, r_w2, r_a0, r_a2,
           r_g2, r_k_k, r_k_a, r_r_k, r_ln_w, r_ln_b, mem_norm, w_mem_kv, c_q_gain, c_k_gain, w_branch,
           w_out, norm_ffn, router_w, router_b, exp_w1, exp_b1, exp_w2, exp_b2):
    b, s, d = x.shape
    n = b * s
    depth = norm_mix.shape[0]
    n_groups = len(A_GROUPS)
    qkv_cols = n_groups * A_WIDTH
    off_k, off_v, off_r = qkv_cols, 2 * qkv_cols, 3 * qkv_cols
    off_cq = off_r + R_COLS
    off_gate = off_cq + C_WIDTH

    x2d = x.reshape(n, d)
    cos_t, sin_t = rope_tables(positions.reshape(n, 1).astype(jnp.int32))
    for l in range(depth):
        w_l = w_in[l]
        h = rmsnorm_rows(x2d, norm_mix[l])
        qk_gains = rotary_gains(a_q_gain[l], a_k_gain[l])
        zr = project(h, w_l[:, off_r:off_cq], "plain", out_dtype=F32, tn=R_COLS // 2)
        cq = project(h, w_l[:, off_cq:off_gate], "headnorm", (c_q_gain[l].reshape(1, HEAD_DIM),))
        gates = project(h, w_l[:, off_gate:], "gate", (b_gate[l].reshape(1, -1),))

        shp = lambda t: t.reshape(b, s, -1)
        outs, lses = [], []
        for g, (window, dilation) in enumerate(A_GROUPS):
            assert window // dilation == ATT_BLOCK
            cols = slice(g * A_WIDTH, (g + 1) * A_WIDTH)
            w_g = jnp.concatenate([w_l[:, :off_k][:, cols], w_l[:, off_k:off_v][:, cols],
                                   w_l[:, off_v:off_r][:, cols]], axis=1)
            qkv = project_qkv(h, w_g, qk_gains, cos_t, sin_t, dilation)
            o, lse = band_attention_group(qkv, b, g, dilation)
            outs.append(o)
            lses.append(lse)

        yb = rwkv7_mix(shp(zr), r_mu[l], r_w0[l], r_w2[l], r_a0[l], r_a2[l], r_g2[l], r_k_k[l], r_k_a[l],
                       r_r_k[l].reshape(-1), r_ln_w[l], r_ln_b[l]).reshape(n, R_WIDTH)

        mlen = mem.shape[1]
        mem_n = rmsnorm_rows(mem.reshape(b * mlen, d), mem_norm[l])
        wkv = w_mem_kv[l]
        ck = project(mem_n, wkv[:, :C_WIDTH], "headnorm", (c_k_gain[l].reshape(1, HEAD_DIM),))
        cv = project(mem_n, wkv[:, C_WIDTH:], "plain")
        yc = cross_attention(shp(cq), ck.reshape(b, mlen, C_WIDTH), cv.reshape(b, mlen, C_WIDTH)).reshape(n, C_WIDTH)

        x1, h2p, top_idx, gate, rank, counts = merge_and_route(
            x2d, outs, lses, yb, yc, gates, w_branch[l], w_out[l], norm_ffn[l], router_w[l], router_b[l])
        pad_start, block_expert, zero_flag, n_used = block_layout(counts, n * TOP_K)
        dest = assignment_rows(top_idx, rank, pad_start)
        x_sorted = scatter_rows(h2p, dest, zero_flag, block_expert.shape[0] * MOE_ROWS)
        y_sorted = expert_ffn(x_sorted, block_expert, n_used, exp_w1[l], exp_b1[l], exp_w2[l], exp_b2[l])
        x2d = moe_combine(x1, gate, y_sorted, dest)
    return x2d.reshape(b, s, d)
```

```python
import functools

import jax
import jax.numpy as jnp
from jax import lax
from jax.experimental import pallas as pl
from jax.experimental.pallas import tpu as pltpu

F32 = jnp.float32
BF16 = jnp.bfloat16

NORM_EPS = 1e-6
LANES = 128
HEAD_DIM = 128
A_GROUPS = ((128, 1), (512, 4), (2048, 16))
A_HEADS = 4
A_WIDTH = A_HEADS * HEAD_DIM
ATT_BLOCK = 128
ATT_QB = 4
ROT_DIM = 32
ROPE_THETA = 500000.0
R_HEAD = 64
R_HEADS = 8
R_WIDTH = R_HEADS * R_HEAD
R_DECAY_LORA = 64
R_AAA_LORA = 64
R_GATE_LORA = 128
R_COLS = 3 * R_WIDTH + R_DECAY_LORA + R_AAA_LORA + R_GATE_LORA
R_GN_EPS = 64e-5
R_CHUNK = 64
R_CPI = 4
R_SEG = 256
C_HEADS = 4
C_WIDTH = C_HEADS * HEAD_DIM
N_BRANCH = 3
N_EXPERTS = 32
TOP_K = 4
SWIGLU_ALPHA = 1.702
SWIGLU_LIMIT = 7.0
MOE_ROWS = 256
NEG_BIG = -1e30

VMEM_LIMIT = 56 * 1024 * 1024


def _cparams(sem):
    return pltpu.CompilerParams(dimension_semantics=sem, vmem_limit_bytes=VMEM_LIMIT)


def _rmsnorm_kernel(x_ref, g_ref, o_ref):
    x = x_ref[...]
    ms = jnp.mean(x * x, axis=-1, keepdims=True)
    o_ref[...] = (x * lax.rsqrt(ms + NORM_EPS) * g_ref[...]).astype(o_ref.dtype)


def rmsnorm_rows(x2d, gain, tm=512):
    n, d = x2d.shape
    tm = min(tm, n)
    return pl.pallas_call(
        _rmsnorm_kernel,
        grid=(n // tm,),
        in_specs=[pl.BlockSpec((tm, d), lambda i: (i, 0)),
                  pl.BlockSpec((1, d), lambda i: (0, 0))],
        out_specs=pl.BlockSpec((tm, d), lambda i: (i, 0)),
        out_shape=jax.ShapeDtypeStruct((n, d), BF16),
        compiler_params=_cparams(("arbitrary",)),
        name="rmsnorm_rows",
    )(x2d, gain.reshape(1, d))


def _rope_table_kernel(pos_ref, freq_ref, cos_ref, sin_ref):
    ang = pos_ref[...].astype(F32) * freq_ref[...]
    lane = lax.broadcasted_iota(jnp.int32, ang.shape, 1)
    cos_ref[...] = jnp.cos(ang)
    s = jnp.sin(ang)
    half = ROT_DIM // 2
    sin_ref[...] = jnp.where(lane < half, -s, jnp.where(lane < ROT_DIM, s, 0.0))


def rope_tables(pos_col, tm=1024):
    n = pos_col.shape[0]
    half = ROT_DIM // 2
    inv_freq = ROPE_THETA ** (-jnp.arange(half, dtype=F32) / half)
    freq_row = jnp.concatenate([inv_freq, inv_freq, jnp.zeros((LANES - ROT_DIM,), F32)]).reshape(1, LANES)
    return pl.pallas_call(
        _rope_table_kernel,
        grid=(n // tm,),
        in_specs=[pl.BlockSpec((tm, 1), lambda i: (i, 0)),
                  pl.BlockSpec((1, LANES), lambda i: (0, 0))],
        out_specs=[pl.BlockSpec((tm, LANES), lambda i: (i, 0)),
                   pl.BlockSpec((tm, LANES), lambda i: (i, 0))],
        out_shape=[jax.ShapeDtypeStruct((n, LANES), F32)] * 2,
        compiler_params=_cparams(("arbitrary",)),
        name="rope_tables",
    )(pos_col, freq_row)


def _head_mean_sq(zh):
    avg = jnp.full((HEAD_DIM, HEAD_DIM), 1.0 / HEAD_DIM, BF16)
    return jnp.dot((zh * zh).astype(BF16), avg, preferred_element_type=F32)


def _cast_weight_once(w_ref, w_scr):
    @pl.when(pl.program_id(1) == 0)
    def _():
        w_scr[...] = w_ref[...].astype(w_scr.dtype)


def _proj_kernel(*refs, mode):
    h_ref, w_ref = refs[0], refs[1]
    o_ref, w_scr = refs[-2], refs[-1]
    _cast_weight_once(w_ref, w_scr)
    z = jnp.dot(h_ref[...], w_scr[...], preferred_element_type=F32)
    if mode == "plain":
        o_ref[...] = z.astype(o_ref.dtype)
    elif mode == "gate":
        o_ref[...] = (0.5 * jnp.tanh(0.5 * (z + refs[2][...])) + 0.5).astype(o_ref.dtype)
    else:
        gain = refs[2][...]
        for c in range(z.shape[1] // HEAD_DIM):
            zh = z[:, c * HEAD_DIM:(c + 1) * HEAD_DIM]
            zn = zh * lax.rsqrt(_head_mean_sq(zh) + NORM_EPS) * gain
            o_ref[:, c * HEAD_DIM:(c + 1) * HEAD_DIM] = zn.astype(o_ref.dtype)


def project(h, w, mode, extras=(), out_dtype=BF16, tm=1024, tn=512):
    n, k = h.shape
    m = w.shape[1]
    tm = min(tm, n)
    tn = min(tn, m)
    assert n % tm == 0 and m % tn == 0
    in_specs = [pl.BlockSpec((tm, k), lambda j, i: (i, 0)),
                pl.BlockSpec((k, tn), lambda j, i: (0, j))]
    if mode == "gate":
        in_specs.append(pl.BlockSpec((1, tn), lambda j, i: (0, j)))
    elif mode == "headnorm":
        in_specs.append(pl.BlockSpec((1, HEAD_DIM), lambda j, i: (0, 0)))
    return pl.pallas_call(
        functools.partial(_proj_kernel, mode=mode),
        grid=(m // tn, n // tm),
        in_specs=in_specs,
        out_specs=pl.BlockSpec((tm, tn), lambda j, i: (i, j)),
        out_shape=jax.ShapeDtypeStruct((n, m), out_dtype),
        scratch_shapes=[pltpu.VMEM((k, tn), BF16)],
        compiler_params=_cparams(("arbitrary", "arbitrary")),
        name="proj_" + mode,
    )(h, w, *extras)


def _qkv_proj_kernel(h_ref, w_ref, gain_ref, cos_ref, sin_ref, o_ref, w_scr, z_scr, *, dilation):
    j = pl.program_id(0)
    _cast_weight_once(w_ref, w_scr)
    z = jnp.dot(h_ref[...], w_scr[...], preferred_element_type=F32)
    half = ROT_DIM // 2

    @pl.when(j < 2)
    def _():
        mi = lax.broadcasted_iota(jnp.int32, (HEAD_DIM, HEAD_DIM), 0)
        li = lax.broadcasted_iota(jnp.int32, (HEAD_DIM, HEAD_DIM), 1)
        perm = (((li < half) & (mi == li + half)) | ((li >= half) & (li < ROT_DIM) & (mi == li - half))).astype(BF16)
        gcos = gain_ref[0, 0:1, :] * cos_ref[...]
        gsin = gain_ref[0, 1:2, :] * sin_ref[...]
        for c in range(A_HEADS):
            zh = z[:, c * HEAD_DIM:(c + 1) * HEAD_DIM]
            partner = jnp.dot(zh.astype(BF16), perm, preferred_element_type=F32)
            z_scr[c] = lax.rsqrt(_head_mean_sq(zh) + NORM_EPS) * (zh * gcos + partner * gsin)

    @pl.when(j == 2)
    def _():
        for c in range(A_HEADS):
            z_scr[c] = z[:, c * HEAD_DIM:(c + 1) * HEAD_DIM]

    rows = z_scr.shape[1] // dilation
    for r in range(dilation):
        for c in range(A_HEADS):
            src = pl.ds(r, rows, stride=dilation) if dilation > 1 else slice(None)
            lo = r * A_WIDTH + c * HEAD_DIM
            o_ref[:, lo:lo + HEAD_DIM] = z_scr[c, src, :].astype(o_ref.dtype)


def rotary_gains(q_gain, k_gain):
    half = ROT_DIM // 2
    lane = jnp.arange(HEAD_DIM)
    partner = jnp.where(lane < half, lane + half, jnp.where(lane < ROT_DIM, lane - half, lane))
    return jnp.stack([jnp.stack([g, g[partner]]) for g in (q_gain, k_gain)])


def project_qkv(h, w_qkv, gains, cos_t, sin_t, dilation, tm=1024):
    n, k = h.shape
    d = dilation
    return pl.pallas_call(
        functools.partial(_qkv_proj_kernel, dilation=d),
        grid=(3, n // tm),
        in_specs=[pl.BlockSpec((tm, k), lambda j, i: (i, 0)),
                  pl.BlockSpec((k, A_WIDTH), lambda j, i: (0, j)),
                  pl.BlockSpec((1, 2, HEAD_DIM), lambda j, i: (jnp.minimum(j, 1), 0, 0)),
                  pl.BlockSpec((tm, LANES), lambda j, i: (i, 0)),
                  pl.BlockSpec((tm, LANES), lambda j, i: (i, 0))],
        out_specs=pl.BlockSpec((tm // d, d * A_WIDTH), lambda j, i: (i, j)),
        out_shape=jax.ShapeDtypeStruct((n // d, 3 * d * A_WIDTH), BF16),
        scratch_shapes=[pltpu.VMEM((k, A_WIDTH), BF16), pltpu.VMEM((A_HEADS, tm, HEAD_DIM), F32)],
        compiler_params=_cparams(("arbitrary", "arbitrary")),
        name=f"proj_qkv_d{d}",
    )(h, w_qkv, gains, cos_t, sin_t)


def _band_attn_kernel(*refs, qb):
    q_ref = refs[0]
    k_refs = refs[1:qb + 2]
    v_refs = refs[qb + 2:2 * qb + 3]
    o_ref, lse_ref = refs[-2:]
    step = pl.program_id(2)
    scale = HEAD_DIM ** -0.5
    nq = ATT_BLOCK
    qi = lax.broadcasted_iota(jnp.int32, (nq, 2 * nq), 0)
    ki = lax.broadcasted_iota(jnp.int32, (nq, 2 * nq), 1)
    rel = qi + nq - ki
    band = (rel >= 0) & (rel <= nq)
    for a in range(qb):
        blk = step * qb + a
        valid = band & ((blk * nq - nq + ki) >= 0)
        rows = slice(a * nq, (a + 1) * nq)
        lses = []
        for h in range(A_HEADS):
            sl = slice(h * HEAD_DIM, (h + 1) * HEAD_DIM)
            qh = q_ref[0, rows, sl]
            kh = jnp.concatenate([k_refs[a][0, :, sl], k_refs[a + 1][0, :, sl]], axis=0)
            vh = jnp.concatenate([v_refs[a][0, :, sl], v_refs[a + 1][0, :, sl]], axis=0)
            s = lax.dot_general(qh, kh, (((1,), (1,)), ((), ())), preferred_element_type=F32) * scale
            s = jnp.where(valid, s, NEG_BIG)
            m = jnp.max(s, axis=-1, keepdims=True)
            p = jnp.exp(s - m)
            l = jnp.sum(p, axis=-1, keepdims=True)
            o = jnp.dot(p.astype(BF16), vh, preferred_element_type=F32) / l
            o_ref[0, rows, sl] = o.astype(o_ref.dtype)
            lses.append(jnp.broadcast_to(m + jnp.log(l), (nq, LANES // A_HEADS)))
        lse_ref[0, rows, :] = jnp.concatenate(lses, axis=1)


def band_attention_group(qkv, b, g, dilation):
    d = dilation
    sub = qkv.shape[0] // b
    nblk = sub // ATT_BLOCK
    qb = min(ATT_QB, nblk)
    assert nblk % qb == 0
    view = qkv.reshape(b, sub, 3 * d * A_WIDTH)
    qrows = qb * ATT_BLOCK
    key_spec = lambda t, m: pl.BlockSpec(
        (1, ATT_BLOCK, A_WIDTH), lambda bi, r, j: (bi, jnp.maximum(j * qb - 1 + m, 0), t * d + r))
    o, lse = pl.pallas_call(
        functools.partial(_band_attn_kernel, qb=qb),
        grid=(b, d, nblk // qb),
        in_specs=[pl.BlockSpec((1, qrows, A_WIDTH), lambda bi, r, j: (bi, j, r))]
                 + [key_spec(1, m) for m in range(qb + 1)] + [key_spec(2, m) for m in range(qb + 1)],
        out_specs=[pl.BlockSpec((1, qrows, A_WIDTH), lambda bi, r, j: (bi, j, r)),
                   pl.BlockSpec((1, qrows, LANES), lambda bi, r, j: (bi, j, r))],
        out_shape=[jax.ShapeDtypeStruct((b, sub, d * A_WIDTH), BF16),
                   jax.ShapeDtypeStruct((b, sub, d * LANES), F32)],
        compiler_params=_cparams(("arbitrary", "arbitrary", "arbitrary")),
        name=f"band_attn_g{g}",
    )(*([view] * (2 * qb + 3)))
    return o.reshape(b * sub, d * A_WIDTH), lse.reshape(b * sub, d * LANES)


def _cross_attn_kernel(q_ref, k_ref, v_ref, o_ref):
    scale = HEAD_DIM ** -0.5
    for h in range(C_HEADS):
        sl = slice(h * HEAD_DIM, (h + 1) * HEAD_DIM)
        s = lax.dot_general(q_ref[0, :, sl], k_ref[0, :, sl], (((1,), (1,)), ((), ())),
                            preferred_element_type=F32) * scale
        m = jnp.max(s, axis=-1, keepdims=True)
        p = jnp.exp(s - m)
        l = jnp.sum(p, axis=-1, keepdims=True)
        o = jnp.dot(p.astype(BF16), v_ref[0, :, sl], preferred_element_type=F32) / l
        o_ref[0, :, sl] = o.astype(o_ref.dtype)


def cross_attention(qn, kn, v, tm=512):
    b, s, w = qn.shape
    m = kn.shape[1]
    return pl.pallas_call(
        _cross_attn_kernel,
        grid=(b, s // tm),
        in_specs=[pl.BlockSpec((1, tm, w), lambda bi, i: (bi, i, 0)),
                  pl.BlockSpec((1, m, w), lambda bi, i: (bi, 0, 0)),
                  pl.BlockSpec((1, m, w), lambda bi, i: (bi, 0, 0))],
        out_specs=pl.BlockSpec((1, tm, w), lambda bi, i: (bi, i, 0)),
        out_shape=jax.ShapeDtypeStruct((b, s, w), BF16),
        compiler_params=_cparams(("arbitrary", "arbitrary")),
        name="cross_attn",
    )(qn, kn, v)


def _head_sums(x, seg):
    w = seg.shape[0]
    x16 = x.astype(BF16)
    return jnp.concatenate(
        [jnp.dot(x16[:, j:j + w], seg, preferred_element_type=F32) for j in range(0, x.shape[1], w)], axis=1)


def _dot_t(a, b):
    return lax.dot_general(a, b, (((0,), (0,)), ((), ())), preferred_element_type=F32)


def _dot_nt(a, b):
    return lax.dot_general(a, b, (((1,), (1,)), ((), ())), preferred_element_type=F32)


def _rwkv_kernel(zr_ref, mu_ref, w0_ref, wwa_ref, a0_ref, g2_ref, kk_ref, ka_ref, rk_ref,
                 lnw_ref, lnb_ref, seg_ref, y_ref,
                 state_ref, carry_ref, ops_ref, yh_ref):
    t = pl.program_id(1)
    tt = zr_ref.shape[1]
    nch = tt // R_CHUNK
    c = R_CHUNK

    @pl.when(t == 0)
    def _():
        state_ref[...] = jnp.zeros_like(state_ref)
        carry_ref[...] = jnp.zeros_like(carry_ref)

    z = zr_ref[0]
    row = lax.broadcasted_iota(jnp.int32, z.shape, 0)
    prev = jnp.where(row == 0, carry_ref[...], pltpu.roll(z, 1, 0))
    carry_ref[...] = z[tt - 1:tt, :]
    xs = z + (prev - z) * mu_ref[...]

    w3 = 3 * R_WIDTH
    r = xs[:, 0:R_WIDTH]
    k = xs[:, R_WIDTH:2 * R_WIDTH]
    v = xs[:, 2 * R_WIDTH:w3]
    wa_lo = xs[:, w3:w3 + LANES]
    g_lo = xs[:, w3 + LANES:w3 + 2 * LANES]
    lane = lax.broadcasted_iota(jnp.int32, wa_lo.shape, 1)
    wa_in = jnp.where(lane < R_DECAY_LORA, jnp.tanh(wa_lo), wa_lo)
    wa = jnp.dot(wa_in.astype(BF16), wwa_ref[...], preferred_element_type=F32)
    u = -(w0_ref[...] + wa[:, :R_WIDTH])
    softplus = jnp.maximum(u, 0.0) + jnp.log(1.0 + jnp.exp(-jnp.abs(u)))
    w_raw = -softplus - 0.5
    ld = -jnp.exp(w_raw)
    a = jax.nn.sigmoid(a0_ref[...] + wa[:, R_WIDTH:])
    g = jnp.dot(jax.nn.sigmoid(g_lo).astype(BF16), g2_ref[...], preferred_element_type=F32)

    seg = seg_ref[...]
    kk = k * kk_ref[...]
    kk = kk / jnp.maximum(jnp.sqrt(_head_sums(kk * kk, seg)), 1e-12)
    k2 = k * (1.0 + (a - 1.0) * ka_ref[...])
    bonus = _head_sums(r * k2 * rk_ref[...], seg) * v

    ri = lax.broadcasted_iota(jnp.int32, (c, c), 0)
    ci = lax.broadcasted_iota(jnp.int32, (c, c), 1)
    tri = (ci <= ri).astype(BF16)
    ld_hi = ld.astype(BF16)
    ld_lo = (ld - ld_hi.astype(F32)).astype(BF16)
    lcs = []
    for ch in range(nch):
        rs = slice(ch * c, (ch + 1) * c)
        lcs.append(jnp.dot(tri, ld_hi[rs], preferred_element_type=F32)
                   + jnp.dot(tri, ld_lo[rs], preferred_element_type=F32))
    lc = jnp.concatenate(lcs, axis=0)
    e_inc = jnp.exp(lc)
    e_exc = jnp.exp(lc - ld)
    e_inv = jnp.exp(-lc)
    a_t = -kk * e_exc
    r_t = r * e_inc
    b_t = kk * a * e_inv
    k_t = k2 * e_inv
    for h in range(R_HEADS):
        hs = slice(h * R_HEAD, (h + 1) * R_HEAD)
        ops_ref[0, h] = a_t[:, hs]
        ops_ref[1, h] = r_t[:, hs]
        ops_ref[2, h] = b_t[:, hs]
        ops_ref[3, h] = k_t[:, hs]
        ops_ref[4, h] = v[:, hs]
        ops_ref[5, h] = e_inc[:, hs]

    strict = ci < ri
    incl = ci <= ri
    eye = (ci == ri)

    ri2 = lax.broadcasted_iota(jnp.int32, (c, 2 * c), 0)
    ci2 = lax.broadcasted_iota(jnp.int32, (c, 2 * c), 1)
    incl2 = jnp.bitwise_and(ci2, c - 1) <= ri2
    eye_f = jnp.where(eye, 1.0, 0.0)
    heads = range(R_HEADS)
    dot = functools.partial(jnp.dot, preferred_element_type=F32)

    def chunk_body(ch, _):
        starts = [pl.multiple_of((ch * R_CPI + sub) * c, c) for sub in range(R_CPI)]
        rows = [pl.ds(r0, c) for r0 in starts]
        items = [(sub, h) for sub in range(R_CPI) for h in heads]
        idx = range(len(items))
        at = [ops_ref[0, h, rows[sub], :] for sub, h in items]
        rt = [ops_ref[1, h, rows[sub], :] for sub, h in items]
        bt = [ops_ref[2, h, rows[sub], :] for sub, h in items]
        kt = [ops_ref[3, h, rows[sub], :] for sub, h in items]
        pc = [ops_ref[5, h, pl.ds(starts[sub] + c - 1, 1), :] for sub, h in items]
        at16 = [x.astype(BF16) for x in at]
        rt16 = [x.astype(BF16) for x in rt]
        bt16 = [x.astype(BF16) for x in bt]
        kt16 = [x.astype(BF16) for x in kt]
        v16 = [ops_ref[4, h, rows[sub], :].astype(BF16) for sub, h in items]
        bk16 = [jnp.concatenate([bt16[i], kt16[i]], axis=0) for i in idx]
        nmat = [jnp.where(strict, _dot_nt(at16[i], bt16[i]), 0.0) for i in idx]
        a_ak = [jnp.where(strict, _dot_nt(at16[i], kt16[i]), 0.0).astype(BF16) for i in idx]
        a_rbk = [jnp.where(incl2, _dot_nt(rt16[i], bk16[i]), 0.0).astype(BF16) for i in idx]
        npow = nmat
        tinv = [eye_f + nmat[i] for i in idx]
        for _i in range(5):
            np16 = [x.astype(BF16) for x in npow]
            npow = [dot(np16[i], np16[i]) for i in idx]
            tinv = [tinv[i] + dot(tinv[i].astype(BF16), npow[i].astype(BF16)) for i in idx]
        akv = [dot(a_ak[i], v16[i]).astype(BF16) for i in idx]
        apw1 = [dot(tinv[i].astype(BF16), jnp.concatenate([at16[i], akv[i]], axis=1)).astype(BF16)
                for i in idx]
        zero = jnp.zeros((c, R_HEAD), BF16)
        rhs2 = [jnp.concatenate([apw1[i], jnp.concatenate([zero, v16[i]], axis=1)], axis=0)
                for i in idx]
        bkh = [jnp.concatenate([bt[i] * pc[i], kt[i] * pc[i]], axis=0).astype(BF16) for i in idx]
        gh = [_dot_t(bkh[i], rhs2[i]) for i in idx]
        qy = [dot(a_rbk[i], rhs2[i]) for i in idx]
        for i, (sub, h) in enumerate(items):
            gm = jnp.where(eye, jnp.broadcast_to(pc[i], (c, c)), 0.0) + gh[i][:, :R_HEAD]
            qp = rt[i] + qy[i][:, :R_HEAD]
            st = state_ref[h]
            res = dot(jnp.concatenate([qp, gm], axis=0).astype(BF16), st.astype(BF16))
            yh_ref[h, rows[sub], :] = res[:c] + qy[i][:, R_HEAD:]
            state_ref[h] = res[c:] + gh[i][:, R_HEAD:]
        return 0

    lax.fori_loop(0, nch // R_CPI, chunk_body, 0)

    y = jnp.concatenate([yh_ref[h] for h in range(R_HEADS)], axis=1)
    mean = _head_sums(y, seg) * (1.0 / R_HEAD)
    dlt = y - mean
    var = _head_sums(dlt * dlt, seg) * (1.0 / R_HEAD)
    yn = dlt * lax.rsqrt(var + R_GN_EPS) * lnw_ref[...] + lnb_ref[...]
    y_ref[0] = ((yn + bonus) * g).astype(y_ref.dtype)


def rwkv7_mix(zr, mu, w0, w2, a0, a2, g2, k_k, k_a, r_k, ln_w, ln_b, tt=256):
    b, s, cols = zr.shape
    row = lambda x: x.reshape(1, -1).astype(F32)
    wwa = jnp.zeros((LANES, 2 * R_WIDTH), F32)
    wwa = wwa.at[:R_DECAY_LORA, :R_WIDTH].set(w2).at[R_DECAY_LORA:, R_WIDTH:].set(a2).astype(BF16)
    hid = jnp.arange(R_SEG) // R_HEAD
    seg = (hid[:, None] == hid[None, :]).astype(BF16)
    full = lambda shape: pl.BlockSpec(shape, lambda bi, t: (0,) * len(shape))
    return pl.pallas_call(
        _rwkv_kernel,
        grid=(b, s // tt),
        in_specs=[pl.BlockSpec((1, tt, cols), lambda bi, t: (bi, t, 0)),
                  full((1, cols)), full((1, R_WIDTH)), full((LANES, 2 * R_WIDTH)), full((1, R_WIDTH)),
                  full((R_GATE_LORA, R_WIDTH)), full((1, R_WIDTH)), full((1, R_WIDTH)), full((1, R_WIDTH)),
                  full((1, R_WIDTH)), full((1, R_WIDTH)), full((R_SEG, R_SEG))],
        out_specs=pl.BlockSpec((1, tt, R_WIDTH), lambda bi, t: (bi, t, 0)),
        out_shape=jax.ShapeDtypeStruct((b, s, R_WIDTH), BF16),
        scratch_shapes=[pltpu.VMEM((R_HEADS, R_HEAD, R_HEAD), F32),
                        pltpu.VMEM((1, cols), F32),
                        pltpu.VMEM((6, R_HEADS, tt, R_HEAD), F32),
                        pltpu.VMEM((R_HEADS, tt, R_HEAD), F32)],
        compiler_params=_cparams(("arbitrary", "arbitrary")),
        name="rwkv7_mix",
    )(zr, row(mu), row(w0), wwa, row(a0), g2.astype(BF16), row(k_k), row(k_a), row(r_k),
      row(ln_w), row(ln_b), seg)


def _merge_kernel(x_ref, o0_ref, o1_ref, o2_ref, l0_ref, l1_ref, l2_ref, yb_ref, yc_ref, gt_ref,
                  wb_ref, wo_ref, gn_ref, rw_ref, rb_ref,
                  x1_ref, h2_ref, idx_ref, gate_ref, rank_ref, cnt_ref, base_ref, o_scr, l_scr):
    @pl.when(pl.program_id(0) == 0)
    def _():
        base_ref[...] = jnp.zeros_like(base_ref)

    tm_rows = x_ref.shape[0]
    for gi, (o_ref, l_ref) in enumerate(((o0_ref, l0_ref), (o1_ref, l1_ref), (o2_ref, l2_ref))):
        dil = A_GROUPS[gi][1]
        for r in range(dil):
            dst = pl.ds(r, tm_rows // dil, stride=dil) if dil > 1 else slice(None)
            for h in range(A_HEADS):
                lo = r * A_WIDTH + h * HEAD_DIM
                o_scr[gi, h, dst, :] = o_ref[:, lo:lo + HEAD_DIM].astype(F32)
            l_scr[gi, dst, :] = l_ref[:, r * LANES:(r + 1) * LANES]
    lses = [l_scr[gi] for gi in range(3)]
    lmax = jnp.maximum(jnp.maximum(lses[0], lses[1]), lses[2])
    es = [jnp.exp(l - lmax) for l in lses]
    inv = 1.0 / (es[0] + es[1] + es[2])
    qw = LANES // A_HEADS
    heads = []
    for h in range(A_HEADS):
        sl = slice(h * HEAD_DIM, (h + 1) * HEAD_DIM)
        acc = None
        for gi in range(3):
            alpha = (es[gi] * inv)[:, h * qw:h * qw + 1]
            term = alpha * o_scr[gi, h]
            acc = term if acc is None else acc + term
        heads.append(acc)
    ya = jnp.concatenate(heads, axis=1).astype(BF16)
    d = x_ref.shape[1]
    merged = None
    for n, yn in enumerate((ya, yb_ref[...], yc_ref[...])):
        proj = jnp.dot(yn, wb_ref[n], preferred_element_type=F32)
        term = gt_ref[:, n * d:(n + 1) * d].astype(F32) * proj
        merged = term if merged is None else merged + term
    x1 = x_ref[...] + jnp.dot(merged.astype(BF16), wo_ref[...], preferred_element_type=F32)
    x1_ref[...] = x1
    ms = jnp.mean(x1 * x1, axis=-1, keepdims=True)
    h2 = x1 * lax.rsqrt(ms + NORM_EPS) * gn_ref[...]
    h2_ref[...] = h2
    logits = jnp.dot(h2, rw_ref[...], preferred_element_type=F32,
                     precision=lax.Precision.HIGHEST) + rb_ref[...]
    tm = logits.shape[0]
    lane = lax.broadcasted_iota(jnp.int32, logits.shape, 1)
    vals, idxs = [], []
    cur = logits
    for _k in range(TOP_K):
        m = jnp.max(cur, axis=-1, keepdims=True)
        ik = jnp.min(jnp.where(cur == m, lane, N_EXPERTS), axis=-1, keepdims=True)
        vals.append(m)
        idxs.append(ik)
        cur = jnp.where(lane == ik, -jnp.inf, cur)
    exps = [jnp.exp(vk - vals[0]) for vk in vals]
    tot = exps[0] + exps[1] + exps[2] + exps[3]
    onehots = [lane == ik for ik in idxs]
    hits = sum(jnp.where(oh, 1.0, 0.0) for oh in onehots)
    ri = lax.broadcasted_iota(jnp.int32, (tm, tm), 0)
    ci = lax.broadcasted_iota(jnp.int32, (tm, tm), 1)
    before = jnp.dot((ci < ri).astype(BF16), hits.astype(BF16), preferred_element_type=F32) + base_ref[...]
    for kk in range(TOP_K):
        idx_ref[:, kk:kk + 1] = idxs[kk]
        gate_ref[:, kk:kk + 1] = exps[kk] / tot
        rank_ref[:, kk:kk + 1] = jnp.sum(jnp.where(onehots[kk], before, 0.0), axis=-1,
                                         keepdims=True).astype(jnp.int32)
    base_ref[...] = base_ref[...] + jnp.sum(hits, axis=0, keepdims=True)
    cnt_ref[...] = base_ref[...].astype(jnp.int32)


def merge_and_route(x2d, outs, lses, yb, yc, gates, w_branch, w_out, norm_ffn, router_w, router_b, tm=512):
    n, d = x2d.shape
    rows = lambda w: pl.BlockSpec((tm, w), lambda i: (i, 0))
    packed = lambda w, dil: pl.BlockSpec((tm // dil, dil * w), lambda i: (i, 0))
    full = lambda shape: pl.BlockSpec(shape, lambda i: (0,) * len(shape))
    return pl.pallas_call(
        _merge_kernel,
        grid=(n // tm,),
        in_specs=[rows(d)] + [packed(A_WIDTH, dil) for _w, dil in A_GROUPS] + [packed(LANES, dil) for _w, dil in A_GROUPS]
                 + [rows(R_WIDTH), rows(C_WIDTH), rows(N_BRANCH * d),
                    full((N_BRANCH, A_WIDTH, d)), full((d, d)), full((1, d)), full((d, N_EXPERTS)), full((1, N_EXPERTS))],
        out_specs=[rows(d), rows(d), rows(TOP_K), rows(TOP_K), rows(TOP_K), full((1, N_EXPERTS))],
        out_shape=[jax.ShapeDtypeStruct((n, d), F32), jax.ShapeDtypeStruct((n, d), F32),
                   jax.ShapeDtypeStruct((n, TOP_K), jnp.int32), jax.ShapeDtypeStruct((n, TOP_K), F32),
                   jax.ShapeDtypeStruct((n, TOP_K), jnp.int32), jax.ShapeDtypeStruct((1, N_EXPERTS), jnp.int32)],
        scratch_shapes=[pltpu.VMEM((1, N_EXPERTS), F32), pltpu.VMEM((len(A_GROUPS), A_HEADS, tm, HEAD_DIM), F32),
                        pltpu.VMEM((len(A_GROUPS), tm, LANES), F32)],
        compiler_params=_cparams(("arbitrary",)),
        name="merge_route",
    )(x2d, *outs, *lses, yb, yc, gates, w_branch.astype(BF16), w_out.astype(BF16),
      norm_ffn.reshape(1, d), router_w, router_b.reshape(1, N_EXPERTS))


def block_layout(counts, n_assign):
    counts = counts.reshape(-1)
    padded = (counts + MOE_ROWS - 1) // MOE_ROWS * MOE_ROWS
    pad_end = jnp.cumsum(padded)
    pad_start = (pad_end - padded).astype(jnp.int32)
    n_blocks = -(-n_assign // MOE_ROWS) + N_EXPERTS
    blk_row = jnp.arange(n_blocks, dtype=jnp.int32) * MOE_ROWS
    owner = jnp.sum((blk_row[:, None] >= pad_end[None, :]).astype(jnp.int32), axis=1)
    block_expert = jnp.minimum(owner, N_EXPERTS - 1).astype(jnp.int32)
    unused = blk_row >= pad_end[-1]
    zero_flag = (unused | (blk_row + MOE_ROWS == pad_end[block_expert])).astype(jnp.int32)
    n_used = (pad_end[-1:] // MOE_ROWS).astype(jnp.int32)
    return pad_start, block_expert, zero_flag, n_used


def _dest_kernel(ps_ref, idx_ref, rank_ref, dest_ref):
    idx = idx_ref[...]
    dest = rank_ref[...]
    for e in range(N_EXPERTS):
        dest = dest + jnp.where(idx == e, ps_ref[e], 0)
    dest_ref[...] = dest


def assignment_rows(top_idx, rank, pad_start):
    n = top_idx.shape[0]
    rows = n * TOP_K // LANES
    flat = lambda t: t.reshape(rows, LANES)
    spec = pl.BlockSpec((rows, LANES), lambda i, ps: (0, 0))
    out = pl.pallas_call(
        _dest_kernel,
        grid_spec=pltpu.PrefetchScalarGridSpec(num_scalar_prefetch=1, grid=(1,), in_specs=[spec, spec],
                                               out_specs=spec),
        out_shape=jax.ShapeDtypeStruct((rows, LANES), jnp.int32),
        compiler_params=_cparams(("arbitrary",)),
        name="assignment_rows",
    )(pad_start, flat(top_idx), flat(rank))
    return out.reshape(n, TOP_K)


def _scatter_kernel(zf_ref, dest_ref, h2_ref, xs_hbm, zeros_ref, sem, zsem):
    i = pl.program_id(0)
    tm = h2_ref.shape[0]
    nblk = zf_ref.shape[0]

    def zero_block(j):
        return pltpu.make_async_copy(zeros_ref, xs_hbm.at[pl.ds(j * MOE_ROWS, MOE_ROWS)], zsem)

    @pl.when(i == 0)
    def _():
        zeros_ref[...] = jnp.zeros_like(zeros_ref)

        def start(j, _):
            @pl.when(zf_ref[j] != 0)
            def _():
                zero_block(j).start()
            return 0

        def wait(j, _):
            @pl.when(zf_ref[j] != 0)
            def _():
                zero_block(j).wait()
            return 0

        lax.fori_loop(0, nblk, start, 0)
        lax.fori_loop(0, nblk, wait, 0)

    def body(t, _):
        for kk in range(TOP_K):
            row = dest_ref[0, 0, t * TOP_K + kk]
            pltpu.make_async_copy(h2_ref.at[pl.ds(t, 1)], xs_hbm.at[pl.ds(row, 1)], sem).start(priority=kk % 2)
        return 0

    lax.fori_loop(0, tm, body, 0)
    pltpu.make_async_copy(xs_hbm.at[pl.ds(0, tm * TOP_K)], xs_hbm.at[pl.ds(0, tm * TOP_K)], sem).wait()


def scatter_rows(h2, dest, zero_flag, n_rows, tm=256):
    n, w = h2.shape
    dest3 = dest.reshape(n // tm, 1, tm * TOP_K)
    grid_spec = pltpu.PrefetchScalarGridSpec(
        num_scalar_prefetch=1,
        grid=(n // tm,),
        in_specs=[pl.BlockSpec((1, 1, tm * TOP_K), lambda i, zf: (i, 0, 0), memory_space=pltpu.SMEM),
                  pl.BlockSpec((tm, w), lambda i, zf: (i, 0))],
        out_specs=pl.BlockSpec(memory_space=pl.ANY),
        scratch_shapes=[pltpu.VMEM((MOE_ROWS, w), h2.dtype), pltpu.SemaphoreType.DMA(()),
                        pltpu.SemaphoreType.DMA(())],
    )
    return pl.pallas_call(
        _scatter_kernel,
        grid_spec=grid_spec,
        out_shape=jax.ShapeDtypeStruct((n_rows, w), h2.dtype),
        compiler_params=_cparams(("arbitrary",)),
        name="scatter_rows",
    )(zero_flag, dest3, h2)


def _expert_kernel(be_ref, nu_ref, xs_ref, w1_ref, b1_ref, w2_ref, b2_ref, y_ref, w1_scr, w2_scr):
    i = pl.program_id(0)
    prev = be_ref[jnp.maximum(i - 1, 0)]

    @pl.when((i < nu_ref[0]) & ((i == 0) | (be_ref[i] != prev)))
    def _():
        w1_scr[...] = w1_ref[0].astype(BF16)
        w2_scr[...] = w2_ref[0].astype(BF16)

    @pl.when(i < nu_ref[0])
    def _():
        xb = xs_ref[...].astype(BF16)
        dff = w2_ref.shape[1]
        u = jnp.dot(xb, w1_scr[...], preferred_element_type=F32) + b1_ref[0]
        glu = jnp.minimum(u[:, :dff], SWIGLU_LIMIT)
        lin = jnp.clip(u[:, dff:], -SWIGLU_LIMIT, SWIGLU_LIMIT)
        act = glu * jax.nn.sigmoid(SWIGLU_ALPHA * glu) * (lin + 1.0)
        y = jnp.dot(act.astype(BF16), w2_scr[...], preferred_element_type=F32) + b2_ref[0]
        y_ref[...] = y

    @pl.when(i >= nu_ref[0])
    def _():
        y_ref[...] = jnp.zeros_like(y_ref)


def expert_ffn(x_sorted, block_expert, n_used, w1, b1, w2, b2):
    n_rows, d = x_sorted.shape
    w = d
    nblk = n_rows // MOE_ROWS
    dff2 = w1.shape[2]
    used = lambda i, nu: jnp.minimum(i, nu[0] - 1)
    grid_spec = pltpu.PrefetchScalarGridSpec(
        num_scalar_prefetch=2,
        grid=(nblk,),
        in_specs=[
            pl.BlockSpec((MOE_ROWS, w), lambda i, be, nu: (used(i, nu), 0)),
            pl.BlockSpec((1, d, dff2), lambda i, be, nu: (be[used(i, nu)], 0, 0)),
            pl.BlockSpec((1, 1, dff2), lambda i, be, nu: (be[used(i, nu)], 0, 0)),
            pl.BlockSpec((1, dff2 // 2, d), lambda i, be, nu: (be[used(i, nu)], 0, 0)),
            pl.BlockSpec((1, 1, d), lambda i, be, nu: (be[used(i, nu)], 0, 0)),
        ],
        out_specs=pl.BlockSpec((MOE_ROWS, w), lambda i, be, nu: (i, 0)),
        scratch_shapes=[pltpu.VMEM((d, dff2), BF16), pltpu.VMEM((dff2 // 2, d), BF16)],
    )
    return pl.pallas_call(
        _expert_kernel,
        grid_spec=grid_spec,
        out_shape=jax.ShapeDtypeStruct((n_rows, w), F32),
        compiler_params=_cparams(("arbitrary",)),
        name="expert_ffn",
    )(block_expert, n_used, x_sorted, w1, b1.reshape(N_EXPERTS, 1, dff2), w2, b2.reshape(N_EXPERTS, 1, d))


def _gather_assigned_rows(y_hbm, dest_ref, dst_ref, sem, tm):
    def body(t, _):
        for kk in range(TOP_K):
            row = dest_ref[t * TOP_K + kk]
            pltpu.make_async_copy(y_hbm.at[pl.ds(row, 1)], dst_ref.at[pl.ds(kk * tm + t, 1)], sem).start(
                priority=kk % 2)
        return 0
    lax.fori_loop(0, tm, body, 0)


def _combine_kernel(d0_ref, dn_ref, x1_ref, gate_ref, y_hbm, o_ref, ybuf, sems):
    i = pl.program_id(0)
    nblk = pl.num_programs(0)
    slot = lax.rem(i, 2)
    tm = o_ref.shape[0]

    @pl.when(i == 0)
    def _():
        _gather_assigned_rows(y_hbm, d0_ref.at[0, 0], ybuf.at[0], sems.at[0], tm)

    @pl.when(i + 1 < nblk)
    def _():
        _gather_assigned_rows(y_hbm, dn_ref.at[0, 0], ybuf.at[1 - slot], sems.at[1 - slot], tm)

    pltpu.make_async_copy(y_hbm.at[pl.ds(0, TOP_K * tm)], ybuf.at[slot], sems.at[slot]).wait()
    acc = x1_ref[...]
    for kk in range(TOP_K):
        acc = acc + gate_ref[:, kk:kk + 1] * ybuf[slot, pl.ds(kk * tm, tm), :]
    o_ref[...] = acc


def moe_combine(x1, gate, y_sorted, dest, tm=256):
    n, d = x1.shape
    w = y_sorted.shape[1]
    nblk = n // tm
    dest3 = dest.reshape(nblk, 1, tm * TOP_K)
    return pl.pallas_call(
        _combine_kernel,
        grid=(nblk,),
        in_specs=[pl.BlockSpec((1, 1, TOP_K * tm), lambda i: (0, 0, 0), memory_space=pltpu.SMEM),
                  pl.BlockSpec((1, 1, TOP_K * tm), lambda i: (jnp.minimum(i + 1, nblk - 1), 0, 0),
                               memory_space=pltpu.SMEM),
                  pl.BlockSpec((tm, d), lambda i: (i, 0)),
                  pl.BlockSpec((tm, TOP_K), lambda i: (i, 0)),
                  pl.BlockSpec(memory_space=pl.ANY)],
        out_specs=pl.BlockSpec((tm, d), lambda i: (i, 0)),
        out_shape=jax.ShapeDtypeStruct((n, d), F32),
        scratch_shapes=[pltpu.VMEM((2, TOP_K * tm, w), y_sorted.dtype), pltpu.SemaphoreType.DMA((2,))],
        compiler_params=_cparams(("arbitrary",)),
        name="moe_combine",
    )(dest3, dest3, x1, gate, y_sorted)


def kernel(x, mem, positions, norm_mix, w_in, b_gate, a_q_gain, a_k_gain, r_mu, r_w0, r_w2, r_a0, r_a2,
           r_g2, r_k_k, r_k_a, r_r_k, r_ln_w, r_ln_b, mem_norm, w_mem_kv, c_q_gain, c_k_gain, w_branch,
           w_out, norm_ffn, router_w, router_b, exp_w1, exp_b1, exp_w2, exp_b2):
    b, s, d = x.shape
    n = b * s
    depth = norm_mix.shape[0]
    n_groups = len(A_GROUPS)
    qkv_cols = n_groups * A_WIDTH
    off_k, off_v, off_r = qkv_cols, 2 * qkv_cols, 3 * qkv_cols
    off_cq = off_r + R_COLS
    off_gate = off_cq + C_WIDTH

    x2d = x.reshape(n, d)
    cos_t, sin_t = rope_tables(positions.reshape(n, 1).astype(jnp.int32))
    for l in range(depth):
        w_l = w_in[l]
        h = rmsnorm_rows(x2d, norm_mix[l])
        qk_gains = rotary_gains(a_q_gain[l], a_k_gain[l])
        zr = project(h, w_l[:, off_r:off_cq], "plain", out_dtype=F32, tn=R_COLS // 2)
        cq = project(h, w_l[:, off_cq:off_gate], "headnorm", (c_q_gain[l].reshape(1, HEAD_DIM),))
        gates = project(h, w_l[:, off_gate:], "gate", (b_gate[l].reshape(1, -1),))

        shp = lambda t: t.reshape(b, s, -1)
        outs, lses = [], []
        for g, (window, dilation) in enumerate(A_GROUPS):
            assert window // dilation == ATT_BLOCK
            cols = slice(g * A_WIDTH, (g + 1) * A_WIDTH)
            w_g = jnp.concatenate([w_l[:, :off_k][:, cols], w_l[:, off_k:off_v][:, cols],
                                   w_l[:, off_v:off_r][:, cols]], axis=1)
            qkv = project_qkv(h, w_g, qk_gains, cos_t, sin_t, dilation)
            o, lse = band_attention_group(qkv, b, g, dilation)
            outs.append(o)
            lses.append(lse)

        yb = rwkv7_mix(shp(zr), r_mu[l], r_w0[l], r_w2[l], r_a0[l], r_a2[l], r_g2[l], r_k_k[l], r_k_a[l],
                       r_r_k[l].reshape(-1), r_ln_w[l], r_ln_b[l]).reshape(n, R_WIDTH)

        mlen = mem.shape[1]
        mem_n = rmsnorm_rows(mem.reshape(b * mlen, d), mem_norm[l])
        wkv = w_mem_kv[l]
        ck = project(mem_n, wkv[:, :C_WIDTH], "headnorm", (c_k_gain[l].reshape(1, HEAD_DIM),))
        cv = project(mem_n, wkv[:, C_WIDTH:], "plain")
        yc = cross_attention(shp(cq), ck.reshape(b, mlen, C_WIDTH), cv.reshape(b, mlen, C_WIDTH)).reshape(n, C_WIDTH)

        x1, h2p, top_idx, gate, rank, counts = merge_and_route(
            x2d, outs, lses, yb, yc, gates, w_branch[l], w_out[l], norm_ffn[l], router_w[l], router_b[l])
        pad_start, block_expert, zero_flag, n_used = block_layout(counts, n * TOP_K)
        dest = assignment_rows(top_idx, rank, pad_start)
        x_sorted = scatter_rows(h2p, dest, zero_flag, block_expert.shape[0] * MOE_ROWS)
        y_sorted = expert_ffn(x_sorted, block_expert, n_used, exp_w1[l], exp_b1[l], exp_w2[l], exp_b2[l])
        x2d = moe_combine(x1, gate, y_sorted, dest)
    return x2d.reshape(b, s, d)
```

```python
import functools

import jax
import jax.numpy as jnp
from jax import lax
from jax.experimental import pallas as pl
from jax.experimental.pallas import tpu as pltpu

F32 = jnp.float32
BF16 = jnp.bfloat16

NORM_EPS = 1e-6
LANES = 128
SUBLANES = 8
HEAD_DIM = 128
A_GROUPS = ((128, 1), (512, 4), (2048, 16))
A_HEADS = 4
A_WIDTH = A_HEADS * HEAD_DIM
ATT_BLOCK = 128
ATT_QB = 4
ROT_DIM = 32
ROPE_THETA = 500000.0
R_HEAD = 64
R_HEADS = 8
R_WIDTH = R_HEADS * R_HEAD
R_DECAY_LORA = 64
R_AAA_LORA = 64
R_GATE_LORA = 128
R_COLS = 3 * R_WIDTH + R_DECAY_LORA + R_AAA_LORA + R_GATE_LORA
R_GN_EPS = 64e-5
R_CHUNK = 64
R_CPI = 4
R_SEG = 256
C_HEADS = 4
C_WIDTH = C_HEADS * HEAD_DIM
N_BRANCH = 3
N_EXPERTS = 32
TOP_K = 4
SWIGLU_ALPHA = 1.702
SWIGLU_LIMIT = 7.0
MOE_ROWS = 256
NEG_BIG = -1e30

VMEM_LIMIT = 56 * 1024 * 1024


def _cparams(sem):
    return pltpu.CompilerParams(dimension_semantics=sem, vmem_limit_bytes=VMEM_LIMIT)


def _rmsnorm_kernel(x_ref, g_ref, o_ref):
    x = x_ref[...]
    ms = jnp.mean(x * x, axis=-1, keepdims=True)
    o_ref[...] = (x * lax.rsqrt(ms + NORM_EPS) * g_ref[...]).astype(o_ref.dtype)


def rmsnorm_rows(x2d, gain, tm=512):
    n, d = x2d.shape
    tm = min(tm, n)
    return pl.pallas_call(
        _rmsnorm_kernel,
        grid=(n // tm,),
        in_specs=[pl.BlockSpec((tm, d), lambda i: (i, 0)),
                  pl.BlockSpec((1, d), lambda i: (0, 0))],
        out_specs=pl.BlockSpec((tm, d), lambda i: (i, 0)),
        out_shape=jax.ShapeDtypeStruct((n, d), BF16),
        compiler_params=_cparams(("arbitrary",)),
        name="rmsnorm_rows",
    )(x2d, gain.reshape(1, d))


def _rope_table_kernel(pos_ref, freq_ref, cos_ref, sin_ref):
    ang = pos_ref[...].astype(F32) * freq_ref[...]
    lane = lax.broadcasted_iota(jnp.int32, ang.shape, 1)
    cos_ref[...] = jnp.cos(ang)
    s = jnp.sin(ang)
    half = ROT_DIM // 2
    sin_ref[...] = jnp.where(lane < half, -s, jnp.where(lane < ROT_DIM, s, 0.0))


def rope_tables(pos_col, tm=1024):
    n = pos_col.shape[0]
    half = ROT_DIM // 2
    inv_freq = ROPE_THETA ** (-jnp.arange(half, dtype=F32) / half)
    freq_row = jnp.concatenate([inv_freq, inv_freq, jnp.zeros((LANES - ROT_DIM,), F32)]).reshape(1, LANES)
    return pl.pallas_call(
        _rope_table_kernel,
        grid=(n // tm,),
        in_specs=[pl.BlockSpec((tm, 1), lambda i: (i, 0)),
                  pl.BlockSpec((1, LANES), lambda i: (0, 0))],
        out_specs=[pl.BlockSpec((tm, LANES), lambda i: (i, 0)),
                   pl.BlockSpec((tm, LANES), lambda i: (i, 0))],
        out_shape=[jax.ShapeDtypeStruct((n, LANES), F32)] * 2,
        compiler_params=_cparams(("arbitrary",)),
        name="rope_tables",
    )(pos_col, freq_row)


def _head_mean_sq(zh):
    avg = jnp.full((HEAD_DIM, HEAD_DIM), 1.0 / HEAD_DIM, BF16)
    return jnp.dot((zh * zh).astype(BF16), avg, preferred_element_type=F32)


def _cast_weight_once(w_ref, w_scr):
    @pl.when(pl.program_id(1) == 0)
    def _():
        w_scr[...] = w_ref[...].astype(w_scr.dtype)


def _proj_kernel(*refs, mode):
    h_ref, w_ref = refs[0], refs[1]
    o_ref, w_scr = refs[-2], refs[-1]
    _cast_weight_once(w_ref, w_scr)
    z = jnp.dot(h_ref[...], w_scr[...], preferred_element_type=F32)
    if mode == "plain":
        o_ref[...] = z.astype(o_ref.dtype)
    elif mode == "gate":
        o_ref[...] = (0.5 * jnp.tanh(0.5 * (z + refs[2][...])) + 0.5).astype(o_ref.dtype)
    else:
        gain = refs[2][...]
        for c in range(z.shape[1] // HEAD_DIM):
            zh = z[:, c * HEAD_DIM:(c + 1) * HEAD_DIM]
            zn = zh * lax.rsqrt(_head_mean_sq(zh) + NORM_EPS) * gain
            o_ref[:, c * HEAD_DIM:(c + 1) * HEAD_DIM] = zn.astype(o_ref.dtype)


def project(h, w, mode, extras=(), out_dtype=BF16, tm=1024, tn=512):
    n, k = h.shape
    m = w.shape[1]
    tm = min(tm, n)
    tn = min(tn, m)
    assert n % tm == 0 and m % tn == 0
    in_specs = [pl.BlockSpec((tm, k), lambda j, i: (i, 0)),
                pl.BlockSpec((k, tn), lambda j, i: (0, j))]
    if mode == "gate":
        in_specs.append(pl.BlockSpec((1, tn), lambda j, i: (0, j)))
    elif mode == "headnorm":
        in_specs.append(pl.BlockSpec((1, HEAD_DIM), lambda j, i: (0, 0)))
    return pl.pallas_call(
        functools.partial(_proj_kernel, mode=mode),
        grid=(m // tn, n // tm),
        in_specs=in_specs,
        out_specs=pl.BlockSpec((tm, tn), lambda j, i: (i, j)),
        out_shape=jax.ShapeDtypeStruct((n, m), out_dtype),
        scratch_shapes=[pltpu.VMEM((k, tn), BF16)],
        compiler_params=_cparams(("arbitrary", "arbitrary")),
        name="proj_" + mode,
    )(h, w, *extras)


def _qkv_proj_kernel(h_ref, w_ref, gain_ref, cos_ref, sin_ref, o_ref, w_scr, z_scr, *, dilation):
    j = pl.program_id(0)
    _cast_weight_once(w_ref, w_scr)
    z = jnp.dot(h_ref[...], w_scr[...], preferred_element_type=F32)
    half = ROT_DIM // 2

    @pl.when(j < 2)
    def _():
        mi = lax.broadcasted_iota(jnp.int32, (HEAD_DIM, HEAD_DIM), 0)
        li = lax.broadcasted_iota(jnp.int32, (HEAD_DIM, HEAD_DIM), 1)
        perm = (((li < half) & (mi == li + half)) | ((li >= half) & (li < ROT_DIM) & (mi == li - half))).astype(BF16)
        gcos = gain_ref[0, 0:1, :] * cos_ref[...]
        gsin = gain_ref[0, 1:2, :] * sin_ref[...]
        for c in range(A_HEADS):
            zh = z[:, c * HEAD_DIM:(c + 1) * HEAD_DIM]
            partner = jnp.dot(zh.astype(BF16), perm, preferred_element_type=F32)
            z_scr[c] = lax.rsqrt(_head_mean_sq(zh) + NORM_EPS) * (zh * gcos + partner * gsin)

    @pl.when(j == 2)
    def _():
        for c in range(A_HEADS):
            z_scr[c] = z[:, c * HEAD_DIM:(c + 1) * HEAD_DIM]

    rows = z_scr.shape[1] // dilation
    for r in range(dilation):
        for c in range(A_HEADS):
            src = pl.ds(r, rows, stride=dilation) if dilation > 1 else slice(None)
            lo = r * A_WIDTH + c * HEAD_DIM
            o_ref[:, lo:lo + HEAD_DIM] = z_scr[c, src, :].astype(o_ref.dtype)


def rotary_gains(q_gain, k_gain):
    half = ROT_DIM // 2
    lane = jnp.arange(HEAD_DIM)
    partner = jnp.where(lane < half, lane + half, jnp.where(lane < ROT_DIM, lane - half, lane))
    return jnp.stack([jnp.stack([g, g[partner]]) for g in (q_gain, k_gain)])


def project_qkv(h, w_qkv, gains, cos_t, sin_t, dilation, tm=1024):
    n, k = h.shape
    d = dilation
    return pl.pallas_call(
        functools.partial(_qkv_proj_kernel, dilation=d),
        grid=(3, n // tm),
        in_specs=[pl.BlockSpec((tm, k), lambda j, i: (i, 0)),
                  pl.BlockSpec((k, A_WIDTH), lambda j, i: (0, j)),
                  pl.BlockSpec((1, 2, HEAD_DIM), lambda j, i: (jnp.minimum(j, 1), 0, 0)),
                  pl.BlockSpec((tm, LANES), lambda j, i: (i, 0)),
                  pl.BlockSpec((tm, LANES), lambda j, i: (i, 0))],
        out_specs=pl.BlockSpec((tm // d, d * A_WIDTH), lambda j, i: (i, j)),
        out_shape=jax.ShapeDtypeStruct((n // d, 3 * d * A_WIDTH), BF16),
        scratch_shapes=[pltpu.VMEM((k, A_WIDTH), BF16), pltpu.VMEM((A_HEADS, tm, HEAD_DIM), F32)],
        compiler_params=_cparams(("arbitrary", "arbitrary")),
        name=f"proj_qkv_d{d}",
    )(h, w_qkv, gains, cos_t, sin_t)


def _band_attn_kernel(*refs, qb):
    q_ref = refs[0]
    k_refs = refs[1:qb + 2]
    v_refs = refs[qb + 2:2 * qb + 3]
    o_ref, lse_ref = refs[-2:]
    step = pl.program_id(2)
    scale = HEAD_DIM ** -0.5
    nq = ATT_BLOCK
    qi = lax.broadcasted_iota(jnp.int32, (nq, 2 * nq), 0)
    ki = lax.broadcasted_iota(jnp.int32, (nq, 2 * nq), 1)
    rel = qi + nq - ki
    band = (rel >= 0) & (rel <= nq)
    for a in range(qb):
        blk = step * qb + a
        valid = band & ((blk * nq - nq + ki) >= 0)
        rows = slice(a * nq, (a + 1) * nq)
        lses = []
        for h in range(A_HEADS):
            sl = slice(h * HEAD_DIM, (h + 1) * HEAD_DIM)
            qh = q_ref[0, rows, sl]
            kh = jnp.concatenate([k_refs[a][0, :, sl], k_refs[a + 1][0, :, sl]], axis=0)
            vh = jnp.concatenate([v_refs[a][0, :, sl], v_refs[a + 1][0, :, sl]], axis=0)
            s = lax.dot_general(qh, kh, (((1,), (1,)), ((), ())), preferred_element_type=F32) * scale
            s = jnp.where(valid, s, NEG_BIG)
            m = jnp.max(s, axis=-1, keepdims=True)
            p = jnp.exp(s - m)
            l = jnp.sum(p, axis=-1, keepdims=True)
            o = jnp.dot(p.astype(BF16), vh, preferred_element_type=F32) / l
            o_ref[0, rows, sl] = o.astype(o_ref.dtype)
            lses.append(jnp.broadcast_to(m + jnp.log(l), (nq, LANES // A_HEADS)))
        lse_ref[0, rows, :] = jnp.concatenate(lses, axis=1)


def band_attention_group(qkv, b, g, dilation):
    d = dilation
    sub = qkv.shape[0] // b
    nblk = sub // ATT_BLOCK
    qb = min(ATT_QB, nblk)
    assert nblk % qb == 0
    view = qkv.reshape(b, sub, 3 * d * A_WIDTH)
    qrows = qb * ATT_BLOCK
    key_spec = lambda t, m: pl.BlockSpec(
        (1, ATT_BLOCK, A_WIDTH), lambda bi, r, j: (bi, jnp.maximum(j * qb - 1 + m, 0), t * d + r))
    o, lse = pl.pallas_call(
        functools.partial(_band_attn_kernel, qb=qb),
        grid=(b, d, nblk // qb),
        in_specs=[pl.BlockSpec((1, qrows, A_WIDTH), lambda bi, r, j: (bi, j, r))]
                 + [key_spec(1, m) for m in range(qb + 1)] + [key_spec(2, m) for m in range(qb + 1)],
        out_specs=[pl.BlockSpec((1, qrows, A_WIDTH), lambda bi, r, j: (bi, j, r)),
                   pl.BlockSpec((1, qrows, LANES), lambda bi, r, j: (bi, j, r))],
        out_shape=[jax.ShapeDtypeStruct((b, sub, d * A_WIDTH), BF16),
                   jax.ShapeDtypeStruct((b, sub, d * LANES), F32)],
        compiler_params=_cparams(("arbitrary", "arbitrary", "arbitrary")),
        name=f"band_attn_g{g}",
    )(*([view] * (2 * qb + 3)))
    return o.reshape(b * sub, d * A_WIDTH), lse.reshape(b * sub, d * LANES)


def _cross_attn_kernel(q_ref, k_ref, v_ref, o_ref):
    scale = HEAD_DIM ** -0.5
    for h in range(C_HEADS):
        sl = slice(h * HEAD_DIM, (h + 1) * HEAD_DIM)
        s = lax.dot_general(q_ref[0, :, sl], k_ref[0, :, sl], (((1,), (1,)), ((), ())),
                            preferred_element_type=F32) * scale
        m = jnp.max(s, axis=-1, keepdims=True)
        p = jnp.exp(s - m)
        l = jnp.sum(p, axis=-1, keepdims=True)
        o = jnp.dot(p.astype(BF16), v_ref[0, :, sl], preferred_element_type=F32) / l
        o_ref[0, :, sl] = o.astype(o_ref.dtype)


def cross_attention(qn, kn, v, tm=512):
    b, s, w = qn.shape
    m = kn.shape[1]
    return pl.pallas_call(
        _cross_attn_kernel,
        grid=(b, s // tm),
        in_specs=[pl.BlockSpec((1, tm, w), lambda bi, i: (bi, i, 0)),
                  pl.BlockSpec((1, m, w), lambda bi, i: (bi, 0, 0)),
                  pl.BlockSpec((1, m, w), lambda bi, i: (bi, 0, 0))],
        out_specs=pl.BlockSpec((1, tm, w), lambda bi, i: (bi, i, 0)),
        out_shape=jax.ShapeDtypeStruct((b, s, w), BF16),
        compiler_params=_cparams(("arbitrary", "arbitrary")),
        name="cross_attn",
    )(qn, kn, v)


def _head_sums(x, seg):
    w = seg.shape[0]
    x16 = x.astype(BF16)
    return jnp.concatenate(
        [jnp.dot(x16[:, j:j + w], seg, preferred_element_type=F32) for j in range(0, x.shape[1], w)], axis=1)


def _dot_t(a, b):
    return lax.dot_general(a, b, (((0,), (0,)), ((), ())), preferred_element_type=F32)


def _dot_nt(a, b):
    return lax.dot_general(a, b, (((1,), (1,)), ((), ())), preferred_element_type=F32)


def _rwkv_kernel(zr_ref, mu_ref, w0_ref, wwa_ref, a0_ref, g2_ref, kk_ref, ka_ref, rk_ref,
                 lnw_ref, lnb_ref, seg_ref, y_ref,
                 state_ref, carry_ref, ops_ref, yh_ref):
    t = pl.program_id(1)
    tt = zr_ref.shape[1]
    nch = tt // R_CHUNK
    c = R_CHUNK

    @pl.when(t == 0)
    def _():
        state_ref[...] = jnp.zeros_like(state_ref)
        carry_ref[...] = jnp.zeros_like(carry_ref)

    z = zr_ref[0]
    row = lax.broadcasted_iota(jnp.int32, z.shape, 0)
    prev = jnp.where(row == 0, carry_ref[...], pltpu.roll(z, 1, 0))
    carry_ref[...] = z[tt - 1:tt, :]
    xs = z + (prev - z) * mu_ref[...]

    w3 = 3 * R_WIDTH
    r = xs[:, 0:R_WIDTH]
    k = xs[:, R_WIDTH:2 * R_WIDTH]
    v = xs[:, 2 * R_WIDTH:w3]
    wa_lo = xs[:, w3:w3 + LANES]
    g_lo = xs[:, w3 + LANES:w3 + 2 * LANES]
    lane = lax.broadcasted_iota(jnp.int32, wa_lo.shape, 1)
    wa_in = jnp.where(lane < R_DECAY_LORA, jnp.tanh(wa_lo), wa_lo)
    wa = jnp.dot(wa_in.astype(BF16), wwa_ref[...], preferred_element_type=F32)
    u = -(w0_ref[...] + wa[:, :R_WIDTH])
    softplus = jnp.maximum(u, 0.0) + jnp.log(1.0 + jnp.exp(-jnp.abs(u)))
    w_raw = -softplus - 0.5
    ld = -jnp.exp(w_raw)
    a = jax.nn.sigmoid(a0_ref[...] + wa[:, R_WIDTH:])
    g = jnp.dot(jax.nn.sigmoid(g_lo).astype(BF16), g2_ref[...], preferred_element_type=F32)

    seg = seg_ref[...]
    kk = k * kk_ref[...]
    kk = kk * jnp.minimum(lax.rsqrt(_head_sums(kk * kk, seg)), 1e12)
    k2 = k * (1.0 + (a - 1.0) * ka_ref[...])
    bonus = _head_sums(r * k2 * rk_ref[...], seg) * v

    ri = lax.broadcasted_iota(jnp.int32, (c, c), 0)
    ci = lax.broadcasted_iota(jnp.int32, (c, c), 1)
    tri = (ci <= ri).astype(BF16)
    ld_hi = ld.astype(BF16)
    ld_lo = (ld - ld_hi.astype(F32)).astype(BF16)
    lcs = []
    for ch in range(nch):
        rs = slice(ch * c, (ch + 1) * c)
        lcs.append(jnp.dot(tri, ld_hi[rs], preferred_element_type=F32)
                   + jnp.dot(tri, ld_lo[rs], preferred_element_type=F32))
    lc = jnp.concatenate(lcs, axis=0)
    e_inc = jnp.exp(lc)
    e_exc = jnp.exp(lc - ld)
    e_inv = jnp.exp(-lc)
    a_t = -kk * e_exc
    r_t = r * e_inc
    b_t = kk * a * e_inv
    k_t = k2 * e_inv
    for h in range(R_HEADS):
        hs = slice(h * R_HEAD, (h + 1) * R_HEAD)
        ops_ref[0, h] = a_t[:, hs]
        ops_ref[1, h] = r_t[:, hs]
        ops_ref[2, h] = b_t[:, hs]
        ops_ref[3, h] = k_t[:, hs]
        ops_ref[4, h] = v[:, hs]
        ops_ref[5, h] = e_inc[:, hs]

    strict = ci < ri
    incl = ci <= ri
    eye = (ci == ri)

    ri2 = lax.broadcasted_iota(jnp.int32, (c, 2 * c), 0)
    ci2 = lax.broadcasted_iota(jnp.int32, (c, 2 * c), 1)
    incl2 = jnp.bitwise_and(ci2, c - 1) <= ri2
    eye_f = jnp.where(eye, 1.0, 0.0)
    heads = range(R_HEADS)
    dot = functools.partial(jnp.dot, preferred_element_type=F32)

    def chunk_body(ch, _):
        starts = [pl.multiple_of((ch * R_CPI + sub) * c, c) for sub in range(R_CPI)]
        rows = [pl.ds(r0, c) for r0 in starts]
        items = [(sub, h) for sub in range(R_CPI) for h in heads]
        idx = range(len(items))
        at = [ops_ref[0, h, rows[sub], :] for sub, h in items]
        rt = [ops_ref[1, h, rows[sub], :] for sub, h in items]
        bt = [ops_ref[2, h, rows[sub], :] for sub, h in items]
        kt = [ops_ref[3, h, rows[sub], :] for sub, h in items]
        pc = [ops_ref[5, h, pl.ds(starts[sub] + c - 1, 1), :] for sub, h in items]
        at16 = [x.astype(BF16) for x in at]
        rt16 = [x.astype(BF16) for x in rt]
        bt16 = [x.astype(BF16) for x in bt]
        kt16 = [x.astype(BF16) for x in kt]
        v16 = [ops_ref[4, h, rows[sub], :].astype(BF16) for sub, h in items]
        bk16 = [jnp.concatenate([bt16[i], kt16[i]], axis=0) for i in idx]
        nmat = [jnp.where(strict, _dot_nt(at16[i], bt16[i]), 0.0) for i in idx]
        a_ak = [jnp.where(strict, _dot_nt(at16[i], kt16[i]), 0.0).astype(BF16) for i in idx]
        a_rbk = [jnp.where(incl2, _dot_nt(rt16[i], bk16[i]), 0.0).astype(BF16) for i in idx]
        npow = nmat
        tinv = [eye_f + nmat[i] for i in idx]
        for _i in range(5):
            np16 = [x.astype(BF16) for x in npow]
            npow = [dot(np16[i], np16[i]) for i in idx]
            tinv = [tinv[i] + dot(tinv[i].astype(BF16), npow[i].astype(BF16)) for i in idx]
        akv = [dot(a_ak[i], v16[i]).astype(BF16) for i in idx]
        apw1 = [dot(tinv[i].astype(BF16), jnp.concatenate([at16[i], akv[i]], axis=1)).astype(BF16)
                for i in idx]
        zero = jnp.zeros((c, R_HEAD), BF16)
        rhs2 = [jnp.concatenate([apw1[i], jnp.concatenate([zero, v16[i]], axis=1)], axis=0)
                for i in idx]
        bkh = [jnp.concatenate([bt[i] * pc[i], kt[i] * pc[i]], axis=0).astype(BF16) for i in idx]
        gh = [_dot_t(bkh[i], rhs2[i]) for i in idx]
        qy = [dot(a_rbk[i], rhs2[i]) for i in idx]
        for i, (sub, h) in enumerate(items):
            gm = jnp.where(eye, jnp.broadcast_to(pc[i], (c, c)), 0.0) + gh[i][:, :R_HEAD]
            qp = rt[i] + qy[i][:, :R_HEAD]
            st = state_ref[h]
            res = dot(jnp.concatenate([qp, gm], axis=0).astype(BF16), st.astype(BF16))
            yh_ref[h, rows[sub], :] = res[:c] + qy[i][:, R_HEAD:]
            state_ref[h] = res[c:] + gh[i][:, R_HEAD:]
        return 0

    lax.fori_loop(0, nch // R_CPI, chunk_body, 0)

    y = jnp.concatenate([yh_ref[h] for h in range(R_HEADS)], axis=1)
    mean = _head_sums(y, seg) * (1.0 / R_HEAD)
    dlt = y - mean
    var = _head_sums(dlt * dlt, seg) * (1.0 / R_HEAD)
    yn = dlt * lax.rsqrt(var + R_GN_EPS) * lnw_ref[...] + lnb_ref[...]
    y_ref[0] = ((yn + bonus) * g).astype(y_ref.dtype)


def rwkv7_mix(zr, mu, w0, w2, a0, a2, g2, k_k, k_a, r_k, ln_w, ln_b, tt=256):
    b, s, cols = zr.shape
    row = lambda x: x.reshape(1, -1).astype(F32)
    wwa = jnp.zeros((LANES, 2 * R_WIDTH), F32)
    wwa = wwa.at[:R_DECAY_LORA, :R_WIDTH].set(w2).at[R_DECAY_LORA:, R_WIDTH:].set(a2).astype(BF16)
    hid = jnp.arange(R_SEG) // R_HEAD
    seg = (hid[:, None] == hid[None, :]).astype(BF16)
    full = lambda shape: pl.BlockSpec(shape, lambda bi, t: (0,) * len(shape))
    return pl.pallas_call(
        _rwkv_kernel,
        grid=(b, s // tt),
        in_specs=[pl.BlockSpec((1, tt, cols), lambda bi, t: (bi, t, 0)),
                  full((1, cols)), full((1, R_WIDTH)), full((LANES, 2 * R_WIDTH)), full((1, R_WIDTH)),
                  full((R_GATE_LORA, R_WIDTH)), full((1, R_WIDTH)), full((1, R_WIDTH)), full((1, R_WIDTH)),
                  full((1, R_WIDTH)), full((1, R_WIDTH)), full((R_SEG, R_SEG))],
        out_specs=pl.BlockSpec((1, tt, R_WIDTH), lambda bi, t: (bi, t, 0)),
        out_shape=jax.ShapeDtypeStruct((b, s, R_WIDTH), BF16),
        scratch_shapes=[pltpu.VMEM((R_HEADS, R_HEAD, R_HEAD), F32),
                        pltpu.VMEM((1, cols), F32),
                        pltpu.VMEM((6, R_HEADS, tt, R_HEAD), F32),
                        pltpu.VMEM((R_HEADS, tt, R_HEAD), F32)],
        compiler_params=_cparams(("arbitrary", "arbitrary")),
        name="rwkv7_mix",
    )(zr, row(mu), row(w0), wwa, row(a0), g2.astype(BF16), row(k_k), row(k_a), row(r_k),
      row(ln_w), row(ln_b), seg)


def _merge_kernel(x_ref, o0_ref, o1_ref, o2_ref, l0_ref, l1_ref, l2_ref, yb_ref, yc_ref, gt_ref,
                  wb_ref, wo_ref, gn_ref, rw_ref, rb_ref,
                  x1_ref, h2_ref, idx_ref, gate_ref, rank_ref, cnt_ref, base_ref, o_scr, l_scr):
    @pl.when(pl.program_id(0) == 0)
    def _():
        base_ref[...] = jnp.zeros_like(base_ref)

    tm_rows = x_ref.shape[0]
    for gi, (o_ref, l_ref) in enumerate(((o0_ref, l0_ref), (o1_ref, l1_ref), (o2_ref, l2_ref))):
        dil = A_GROUPS[gi][1]
        for r in range(dil):
            dst = pl.ds(r, tm_rows // dil, stride=dil) if dil > 1 else slice(None)
            for h in range(A_HEADS):
                lo = r * A_WIDTH + h * HEAD_DIM
                o_scr[gi, h, dst, :] = o_ref[:, lo:lo + HEAD_DIM].astype(F32)
            l_scr[gi, dst, :] = l_ref[:, r * LANES:(r + 1) * LANES]
    lses = [l_scr[gi] for gi in range(3)]
    lmax = jnp.maximum(jnp.maximum(lses[0], lses[1]), lses[2])
    es = [jnp.exp(l - lmax) for l in lses]
    inv = 1.0 / (es[0] + es[1] + es[2])
    qw = LANES // A_HEADS
    heads = []
    for h in range(A_HEADS):
        sl = slice(h * HEAD_DIM, (h + 1) * HEAD_DIM)
        acc = None
        for gi in range(3):
            alpha = (es[gi] * inv)[:, h * qw:h * qw + 1]
            term = alpha * o_scr[gi, h]
            acc = term if acc is None else acc + term
        heads.append(acc)
    ya = jnp.concatenate(heads, axis=1).astype(BF16)
    d = x_ref.shape[1]
    merged = None
    for n, yn in enumerate((ya, yb_ref[...], yc_ref[...])):
        proj = jnp.dot(yn, wb_ref[n], preferred_element_type=F32)
        term = gt_ref[:, n * d:(n + 1) * d].astype(F32) * proj
        merged = term if merged is None else merged + term
    x1 = x_ref[...] + jnp.dot(merged.astype(BF16), wo_ref[...], preferred_element_type=F32)
    x1_ref[...] = x1
    ms = jnp.mean(x1 * x1, axis=-1, keepdims=True)
    h2 = x1 * lax.rsqrt(ms + NORM_EPS) * gn_ref[...]
    h2_ref[...] = h2
    rw = rw_ref[...]
    h2_hi, rw_hi = h2.astype(BF16), rw.astype(BF16)
    h2_lo = (h2 - h2_hi.astype(F32)).astype(BF16)
    rw_lo = (rw - rw_hi.astype(F32)).astype(BF16)
    logits = (jnp.dot(h2_hi, rw_hi, preferred_element_type=F32) + jnp.dot(h2_hi, rw_lo, preferred_element_type=F32)
              + jnp.dot(h2_lo, rw_hi, preferred_element_type=F32)) + rb_ref[...]
    tm = logits.shape[0]
    lane = lax.broadcasted_iota(jnp.int32, logits.shape, 1)
    vals, idxs = [], []
    cur = logits
    for _k in range(TOP_K):
        m = jnp.max(cur, axis=-1, keepdims=True)
        ik = jnp.min(jnp.where(cur == m, lane, N_EXPERTS), axis=-1, keepdims=True)
        vals.append(m)
        idxs.append(ik)
        cur = jnp.where(lane == ik, -jnp.inf, cur)
    exps = [jnp.exp(vk - vals[0]) for vk in vals]
    tot = exps[0] + exps[1] + exps[2] + exps[3]
    onehots = [lane == ik for ik in idxs]
    hits = sum(jnp.where(oh, 1.0, 0.0) for oh in onehots)
    ri = lax.broadcasted_iota(jnp.int32, (tm, tm), 0)
    ci = lax.broadcasted_iota(jnp.int32, (tm, tm), 1)
    before = jnp.dot((ci < ri).astype(BF16), hits.astype(BF16), preferred_element_type=F32) + base_ref[...]
    for kk in range(TOP_K):
        idx_ref[:, kk:kk + 1] = idxs[kk]
        gate_ref[:, kk:kk + 1] = exps[kk] / tot
        rank_ref[:, kk:kk + 1] = jnp.sum(jnp.where(onehots[kk], before, 0.0), axis=-1,
                                         keepdims=True).astype(jnp.int32)
    base_ref[...] = base_ref[...] + jnp.sum(hits, axis=0, keepdims=True)
    cnt_ref[...] = base_ref[...].astype(jnp.int32)


def merge_and_route(x2d, outs, lses, yb, yc, gates, w_branch, w_out, norm_ffn, router_w, router_b, tm=512):
    n, d = x2d.shape
    rows = lambda w: pl.BlockSpec((tm, w), lambda i: (i, 0))
    packed = lambda w, dil: pl.BlockSpec((tm // dil, dil * w), lambda i: (i, 0))
    full = lambda shape: pl.BlockSpec(shape, lambda i: (0,) * len(shape))
    return pl.pallas_call(
        _merge_kernel,
        grid=(n // tm,),
        in_specs=[rows(d)] + [packed(A_WIDTH, dil) for _w, dil in A_GROUPS] + [packed(LANES, dil) for _w, dil in A_GROUPS]
                 + [rows(R_WIDTH), rows(C_WIDTH), rows(N_BRANCH * d),
                    full((N_BRANCH, A_WIDTH, d)), full((d, d)), full((1, d)), full((d, N_EXPERTS)), full((1, N_EXPERTS))],
        out_specs=[rows(d), rows(d), rows(TOP_K), rows(TOP_K), rows(TOP_K), full((1, N_EXPERTS))],
        out_shape=[jax.ShapeDtypeStruct((n, d), F32), jax.ShapeDtypeStruct((n, d), F32),
                   jax.ShapeDtypeStruct((n, TOP_K), jnp.int32), jax.ShapeDtypeStruct((n, TOP_K), F32),
                   jax.ShapeDtypeStruct((n, TOP_K), jnp.int32), jax.ShapeDtypeStruct((1, N_EXPERTS), jnp.int32)],
        scratch_shapes=[pltpu.VMEM((1, N_EXPERTS), F32), pltpu.VMEM((len(A_GROUPS), A_HEADS, tm, HEAD_DIM), F32),
                        pltpu.VMEM((len(A_GROUPS), tm, LANES), F32)],
        compiler_params=_cparams(("arbitrary",)),
        name="merge_route",
    )(x2d, *outs, *lses, yb, yc, gates, w_branch.astype(BF16), w_out.astype(BF16),
      norm_ffn.reshape(1, d), router_w, router_b.reshape(1, N_EXPERTS))


def block_layout(counts, n_assign):
    counts = counts.reshape(-1)
    padded = (counts + MOE_ROWS - 1) // MOE_ROWS * MOE_ROWS
    pad_end = jnp.cumsum(padded)
    pad_start = (pad_end - padded).astype(jnp.int32)
    n_blocks = -(-n_assign // MOE_ROWS) + N_EXPERTS
    blk_row = jnp.arange(n_blocks, dtype=jnp.int32) * MOE_ROWS
    owner = jnp.sum((blk_row[:, None] >= pad_end[None, :]).astype(jnp.int32), axis=1)
    block_expert = jnp.minimum(owner, N_EXPERTS - 1).astype(jnp.int32)
    unused = blk_row >= pad_end[-1]
    zero_flag = (unused | (blk_row + MOE_ROWS == pad_end[block_expert])).astype(jnp.int32)
    n_used = (pad_end[-1:] // MOE_ROWS).astype(jnp.int32)
    has_rows = counts > 0
    eid = jnp.arange(N_EXPERTS, dtype=jnp.int32)
    later = jnp.where(has_rows[None, :] & (eid[None, :] > eid[:, None]), eid[None, :], N_EXPERTS)
    next_expert = jnp.min(later, axis=1)
    next_expert = jnp.where(next_expert == N_EXPERTS, -1, next_expert).astype(jnp.int32)
    run_parity = ((jnp.cumsum(has_rows) - has_rows) % 2).astype(jnp.int32)
    return pad_start, block_expert, zero_flag, n_used, next_expert, run_parity


def _dest_kernel(ps_ref, idx_ref, rank_ref, dest_ref):
    idx = idx_ref[...]
    dest = rank_ref[...]
    for e in range(N_EXPERTS):
        dest = dest + jnp.where(idx == e, ps_ref[e], 0)
    dest_ref[...] = dest


def assignment_rows(top_idx, rank, pad_start):
    n = top_idx.shape[0]
    rows = n * TOP_K // LANES
    flat = lambda t: t.reshape(rows, LANES)
    spec = pl.BlockSpec((rows, LANES), lambda i, ps: (0, 0))
    out = pl.pallas_call(
        _dest_kernel,
        grid_spec=pltpu.PrefetchScalarGridSpec(num_scalar_prefetch=1, grid=(1,), in_specs=[spec, spec],
                                               out_specs=spec),
        out_shape=jax.ShapeDtypeStruct((rows, LANES), jnp.int32),
        compiler_params=_cparams(("arbitrary",)),
        name="assignment_rows",
    )(pad_start, flat(top_idx), flat(rank))
    return out.reshape(n, TOP_K)


def _scatter_kernel(zf_ref, dest_ref, h2_ref, xs_hbm, zeros_ref, sem, zsem):
    i = pl.program_id(0)
    tm = h2_ref.shape[0]
    nblk = zf_ref.shape[0]

    def zero_block(j):
        return pltpu.make_async_copy(zeros_ref, xs_hbm.at[pl.ds(j * MOE_ROWS, MOE_ROWS)], zsem)

    @pl.when(i == 0)
    def _():
        zeros_ref[...] = jnp.zeros_like(zeros_ref)

        def start(j, _):
            @pl.when(zf_ref[j] != 0)
            def _():
                zero_block(j).start()
            return 0

        def wait(j, _):
            @pl.when(zf_ref[j] != 0)
            def _():
                zero_block(j).wait()
            return 0

        lax.fori_loop(0, nblk, start, 0)
        lax.fori_loop(0, nblk, wait, 0)

    def body(t, _):
        for kk in range(TOP_K):
            row = dest_ref[0, 0, t * TOP_K + kk]
            pltpu.make_async_copy(h2_ref.at[pl.ds(t, 1)], xs_hbm.at[pl.ds(row, 1)], sem).start(priority=kk % 2)
        return 0

    lax.fori_loop(0, tm, body, 0)
    pltpu.make_async_copy(xs_hbm.at[pl.ds(0, tm * TOP_K)], xs_hbm.at[pl.ds(0, tm * TOP_K)], sem).wait()


def scatter_rows(h2, dest, zero_flag, n_rows, tm=1024):
    n, w = h2.shape
    dest3 = dest.reshape(n // tm, 1, tm * TOP_K)
    grid_spec = pltpu.PrefetchScalarGridSpec(
        num_scalar_prefetch=1,
        grid=(n // tm,),
        in_specs=[pl.BlockSpec((1, 1, tm * TOP_K), lambda i, zf: (i, 0, 0), memory_space=pltpu.SMEM),
                  pl.BlockSpec((tm, w), lambda i, zf: (i, 0))],
        out_specs=pl.BlockSpec(memory_space=pl.ANY),
        scratch_shapes=[pltpu.VMEM((MOE_ROWS, w), h2.dtype), pltpu.SemaphoreType.DMA(()),
                        pltpu.SemaphoreType.DMA(())],
    )
    return pl.pallas_call(
        _scatter_kernel,
        grid_spec=grid_spec,
        out_shape=jax.ShapeDtypeStruct((n_rows, w), h2.dtype),
        compiler_params=_cparams(("arbitrary",)),
        name="scatter_rows",
    )(zero_flag, dest3, h2)


def _expert_kernel(be_ref, nu_ref, nxt_ref, par_ref, xs_ref, w1_hbm, b1_ref, w2_hbm, b2_ref, y_ref,
                   w1_f32, w2_f32, w1_scr, w2_scr, wsem):
    i = pl.program_id(0)
    e = be_ref[i]
    prev = be_ref[jnp.maximum(i - 1, 0)]

    def fetch(expert, slot):
        return (pltpu.make_async_copy(w1_hbm.at[expert], w1_f32.at[slot], wsem.at[slot]),
                pltpu.make_async_copy(w2_hbm.at[expert], w2_f32.at[slot], wsem.at[slot]))

    @pl.when((i < nu_ref[0]) & ((i == 0) | (e != prev)))
    def _():
        slot = par_ref[e]

        @pl.when(i == 0)
        def _():
            for c in fetch(e, slot):
                c.start()

        for c in fetch(e, slot):
            c.wait()

        @pl.when(nxt_ref[e] >= 0)
        def _():
            for c in fetch(nxt_ref[e], 1 - slot):
                c.start()

        w1_scr[...] = w1_f32[slot].astype(BF16)
        w2_scr[...] = w2_f32[slot].astype(BF16)

    @pl.when(i < nu_ref[0])
    def _():
        xb = xs_ref[...].astype(BF16)
        dff = w2_scr.shape[0]
        u = jnp.dot(xb, w1_scr[...], preferred_element_type=F32) + b1_ref[0]
        glu = jnp.minimum(u[:, :dff], SWIGLU_LIMIT)
        lin = jnp.clip(u[:, dff:], -SWIGLU_LIMIT, SWIGLU_LIMIT)
        act = glu * jax.nn.sigmoid(SWIGLU_ALPHA * glu) * (lin + 1.0)
        y = jnp.dot(act.astype(BF16), w2_scr[...], preferred_element_type=F32) + b2_ref[0]
        y_ref[...] = y

    @pl.when(i >= nu_ref[0])
    def _():
        y_ref[...] = jnp.zeros_like(y_ref)


def expert_ffn(x_sorted, block_expert, n_used, next_expert, run_parity, w1, b1, w2, b2):
    n_rows, d = x_sorted.shape
    w = d
    nblk = n_rows // MOE_ROWS
    dff2 = w1.shape[2]
    used = lambda i, nu: jnp.minimum(i, nu[0] - 1)
    grid_spec = pltpu.PrefetchScalarGridSpec(
        num_scalar_prefetch=4,
        grid=(nblk,),
        in_specs=[
            pl.BlockSpec((MOE_ROWS, w), lambda i, be, nu, nx, pa: (used(i, nu), 0)),
            pl.BlockSpec(memory_space=pl.ANY),
            pl.BlockSpec((1, 1, dff2), lambda i, be, nu, nx, pa: (be[used(i, nu)], 0, 0)),
            pl.BlockSpec(memory_space=pl.ANY),
            pl.BlockSpec((1, 1, d), lambda i, be, nu, nx, pa: (be[used(i, nu)], 0, 0)),
        ],
        out_specs=pl.BlockSpec((MOE_ROWS, w), lambda i, be, nu, nx, pa: (i, 0)),
        scratch_shapes=[pltpu.VMEM((2, d, dff2), F32), pltpu.VMEM((2, dff2 // 2, d), F32),
                        pltpu.VMEM((d, dff2), BF16), pltpu.VMEM((dff2 // 2, d), BF16),
                        pltpu.SemaphoreType.DMA((2,))],
    )
    return pl.pallas_call(
        _expert_kernel,
        grid_spec=grid_spec,
        out_shape=jax.ShapeDtypeStruct((n_rows, w), F32),
        compiler_params=_cparams(("arbitrary",)),
        name="expert_ffn",
    )(block_expert, n_used, next_expert, run_parity, x_sorted, w1, b1.reshape(N_EXPERTS, 1, dff2), w2,
      b2.reshape(N_EXPERTS, 1, d))


def _gather_assigned_rows(y_hbm, dest_ref, dst_ref, sem, tm):
    def body(g, _):
        base = pl.multiple_of(g * SUBLANES, SUBLANES)
        for u in range(SUBLANES):
            for kk in range(TOP_K):
                row = dest_ref[(base + u) * TOP_K + kk]
                pltpu.make_async_copy(y_hbm.at[pl.ds(row, 1)], dst_ref.at[pl.ds(kk * tm + base + u, 1)],
                                      sem).start(priority=kk % 2)
        return 0
    lax.fori_loop(0, tm // SUBLANES, body, 0)


def _combine_kernel(d0_ref, dn_ref, x1_ref, gate_ref, y_hbm, o_ref, ybuf, sems):
    i = pl.program_id(0)
    nblk = pl.num_programs(0)
    slot = lax.rem(i, 2)
    tm = o_ref.shape[0]

    @pl.when(i == 0)
    def _():
        _gather_assigned_rows(y_hbm, d0_ref.at[0, 0], ybuf.at[0], sems.at[0], tm)

    @pl.when(i + 1 < nblk)
    def _():
        _gather_assigned_rows(y_hbm, dn_ref.at[0, 0], ybuf.at[1 - slot], sems.at[1 - slot], tm)

    pltpu.make_async_copy(y_hbm.at[pl.ds(0, TOP_K * tm)], ybuf.at[slot], sems.at[slot]).wait()
    acc = x1_ref[...]
    for kk in range(TOP_K):
        acc = acc + gate_ref[:, kk:kk + 1] * ybuf[slot, pl.ds(kk * tm, tm), :]
    o_ref[...] = acc


def moe_combine(x1, gate, y_sorted, dest, tm=256):
    n, d = x1.shape
    w = y_sorted.shape[1]
    nblk = n // tm
    dest3 = dest.reshape(nblk, 1, tm * TOP_K)
    return pl.pallas_call(
        _combine_kernel,
        grid=(nblk,),
        in_specs=[pl.BlockSpec((1, 1, TOP_K * tm), lambda i: (0, 0, 0), memory_space=pltpu.SMEM),
                  pl.BlockSpec((1, 1, TOP_K * tm), lambda i: (jnp.minimum(i + 1, nblk - 1), 0, 0),
                               memory_space=pltpu.SMEM),
                  pl.BlockSpec((tm, d), lambda i: (i, 0)),
                  pl.BlockSpec((tm, TOP_K), lambda i: (i, 0)),
                  pl.BlockSpec(memory_space=pl.ANY)],
        out_specs=pl.BlockSpec((tm, d), lambda i: (i, 0)),
        out_shape=jax.ShapeDtypeStruct((n, d), F32),
        scratch_shapes=[pltpu.VMEM((2, TOP_K * tm, w), y_sorted.dtype), pltpu.SemaphoreType.DMA((2,))],
        compiler_params=_cparams(("arbitrary",)),
        name="moe_combine",
    )(dest3, dest3, x1, gate, y_sorted)


def kernel(x, mem, positions, norm_mix, w_in, b_gate, a_q_gain, a_k_gain, r_mu, r_w0, r_w2, r_a0, r_a2,
           r_g2, r_k_k, r_k_a, r_r_k, r_ln_w, r_ln_b, mem_norm, w_mem_kv, c_q_gain, c_k_gain, w_branch,
           w_out, norm_ffn, router_w, router_b, exp_w1, exp_b1, exp_w2, exp_b2):
    b, s, d = x.shape
    n = b * s
    depth = norm_mix.shape[0]
    n_groups = len(A_GROUPS)
    qkv_cols = n_groups * A_WIDTH
    off_k, off_v, off_r = qkv_cols, 2 * qkv_cols, 3 * qkv_cols
    off_cq = off_r + R_COLS
    off_gate = off_cq + C_WIDTH

    x2d = x.reshape(n, d)
    cos_t, sin_t = rope_tables(positions.reshape(n, 1).astype(jnp.int32))
    for l in range(depth):
        w_l = w_in[l]
        h = rmsnorm_rows(x2d, norm_mix[l])
        qk_gains = rotary_gains(a_q_gain[l], a_k_gain[l])
        zr = project(h, w_l[:, off_r:off_cq], "plain", out_dtype=F32, tn=R_COLS // 2)
        cq = project(h, w_l[:, off_cq:off_gate], "headnorm", (c_q_gain[l].reshape(1, HEAD_DIM),))
        gates = project(h, w_l[:, off_gate:], "gate", (b_gate[l].reshape(1, -1),))

        shp = lambda t: t.reshape(b, s, -1)
        outs, lses = [], []
        for g, (window, dilation) in enumerate(A_GROUPS):
            assert window // dilation == ATT_BLOCK
            cols = slice(g * A_WIDTH, (g + 1) * A_WIDTH)
            w_g = jnp.concatenate([w_l[:, :off_k][:, cols], w_l[:, off_k:off_v][:, cols],
                                   w_l[:, off_v:off_r][:, cols]], axis=1)
            qkv = project_qkv(h, w_g, qk_gains, cos_t, sin_t, dilation)
            o, lse = band_attention_group(qkv, b, g, dilation)
            outs.append(o)
            lses.append(lse)

        yb = rwkv7_mix(shp(zr), r_mu[l], r_w0[l], r_w2[l], r_a0[l], r_a2[l], r_g2[l], r_k_k[l], r_k_a[l],
                       r_r_k[l].reshape(-1), r_ln_w[l], r_ln_b[l]).reshape(n, R_WIDTH)

        mlen = mem.shape[1]
        mem_n = rmsnorm_rows(mem.reshape(b * mlen, d), mem_norm[l])
        wkv = w_mem_kv[l]
        ck = project(mem_n, wkv[:, :C_WIDTH], "headnorm", (c_k_gain[l].reshape(1, HEAD_DIM),))
        cv = project(mem_n, wkv[:, C_WIDTH:], "plain")
        yc = cross_attention(shp(cq), ck.reshape(b, mlen, C_WIDTH), cv.reshape(b, mlen, C_WIDTH)).reshape(n, C_WIDTH)

        x1, h2p, top_idx, gate, rank, counts = merge_and_route(
            x2d, outs, lses, yb, yc, gates, w_branch[l], w_out[l], norm_ffn[l], router_w[l], router_b[l])
        pad_start, block_expert, zero_flag, n_used, next_expert, run_parity = block_layout(counts, n * TOP_K)
        dest = assignment_rows(top_idx, rank, pad_start)
        x_sorted = scatter_rows(h2p, dest, zero_flag, block_expert.shape[0] * MOE_ROWS)
        y_sorted = expert_ffn(x_sorted, block_expert, n_used, next_expert, run_parity,
                              exp_w1[l], exp_b1[l], exp_w2[l], exp_b2[l])
        x2d = moe_combine(x1, gate, y_sorted, dest)
    return x2d.reshape(b, s, d)
```

```python
import functools

import jax
import jax.numpy as jnp
from jax import lax
from jax.experimental import pallas as pl
from jax.experimental.pallas import tpu as pltpu

F32 = jnp.float32
BF16 = jnp.bfloat16

NORM_EPS = 1e-6
LANES = 128
SUBLANES = 8
MXU_COLS = 256
HEAD_DIM = 128
A_GROUPS = ((128, 1), (512, 4), (2048, 16))
A_HEADS = 4
A_WIDTH = A_HEADS * HEAD_DIM
ATT_BLOCK = 128
ATT_QB = 4
ROT_DIM = 32
ROPE_THETA = 500000.0
R_HEAD = 64
R_HEADS = 8
R_WIDTH = R_HEADS * R_HEAD
R_DECAY_LORA = 64
R_AAA_LORA = 64
R_GATE_LORA = 128
R_COLS = 3 * R_WIDTH + R_DECAY_LORA + R_AAA_LORA + R_GATE_LORA
R_GN_EPS = 64e-5
R_CHUNK = 64
R_CPI = 4
R_SEG = 256
C_HEADS = 4
C_WIDTH = C_HEADS * HEAD_DIM
N_BRANCH = 3
N_EXPERTS = 32
TOP_K = 4
SWIGLU_ALPHA = 1.702
SWIGLU_LIMIT = 7.0
MOE_ROWS = 256
NEG_BIG = -1e30

VMEM_LIMIT = 56 * 1024 * 1024


def _cparams(sem):
    return pltpu.CompilerParams(dimension_semantics=sem, vmem_limit_bytes=VMEM_LIMIT)


def _rmsnorm_kernel(x_ref, g_ref, o_ref):
    x = x_ref[...]
    ms = jnp.mean(x * x, axis=-1, keepdims=True)
    o_ref[...] = (x * lax.rsqrt(ms + NORM_EPS) * g_ref[...]).astype(o_ref.dtype)


def rmsnorm_rows(x2d, gain, tm=512):
    n, d = x2d.shape
    tm = min(tm, n)
    return pl.pallas_call(
        _rmsnorm_kernel,
        grid=(n // tm,),
        in_specs=[pl.BlockSpec((tm, d), lambda i: (i, 0)),
                  pl.BlockSpec((1, d), lambda i: (0, 0))],
        out_specs=pl.BlockSpec((tm, d), lambda i: (i, 0)),
        out_shape=jax.ShapeDtypeStruct((n, d), BF16),
        compiler_params=_cparams(("arbitrary",)),
        name="rmsnorm_rows",
    )(x2d, gain.reshape(1, d))


def _rope_table_kernel(pos_ref, freq_ref, cos_ref, sin_ref):
    ang = pos_ref[...].astype(F32) * freq_ref[...]
    lane = lax.broadcasted_iota(jnp.int32, ang.shape, 1)
    cos_ref[...] = jnp.cos(ang)
    s = jnp.sin(ang)
    half = ROT_DIM // 2
    sin_ref[...] = jnp.where(lane < half, -s, jnp.where(lane < ROT_DIM, s, 0.0))


def rope_tables(pos_col, tm=1024):
    n = pos_col.shape[0]
    half = ROT_DIM // 2
    inv_freq = ROPE_THETA ** (-jnp.arange(half, dtype=F32) / half)
    freq_row = jnp.concatenate([inv_freq, inv_freq, jnp.zeros((LANES - ROT_DIM,), F32)]).reshape(1, LANES)
    return pl.pallas_call(
        _rope_table_kernel,
        grid=(n // tm,),
        in_specs=[pl.BlockSpec((tm, 1), lambda i: (i, 0)),
                  pl.BlockSpec((1, LANES), lambda i: (0, 0))],
        out_specs=[pl.BlockSpec((tm, LANES), lambda i: (i, 0)),
                   pl.BlockSpec((tm, LANES), lambda i: (i, 0))],
        out_shape=[jax.ShapeDtypeStruct((n, LANES), F32)] * 2,
        compiler_params=_cparams(("arbitrary",)),
        name="rope_tables",
    )(pos_col, freq_row)


def _head_mean_sq(zh):
    avg = jnp.full((HEAD_DIM, HEAD_DIM), 1.0 / HEAD_DIM, BF16)
    return jnp.dot((zh * zh).astype(BF16), avg, preferred_element_type=F32)


def _cast_weight_once(w_ref, w_scr):
    @pl.when(pl.program_id(1) == 0)
    def _():
        w_scr[...] = w_ref[...].astype(w_scr.dtype)


def _proj_kernel(*refs, mode):
    h_ref, w_ref = refs[0], refs[1]
    o_ref, w_scr = refs[-2], refs[-1]
    _cast_weight_once(w_ref, w_scr)
    z = jnp.dot(h_ref[...], w_scr[...], preferred_element_type=F32)
    if mode == "plain":
        o_ref[...] = z.astype(o_ref.dtype)
    elif mode == "gate":
        o_ref[...] = (0.5 * jnp.tanh(0.5 * (z + refs[2][...])) + 0.5).astype(o_ref.dtype)
    else:
        gain = refs[2][...]
        for c in range(z.shape[1] // HEAD_DIM):
            zh = z[:, c * HEAD_DIM:(c + 1) * HEAD_DIM]
            zn = zh * lax.rsqrt(_head_mean_sq(zh) + NORM_EPS) * gain
            o_ref[:, c * HEAD_DIM:(c + 1) * HEAD_DIM] = zn.astype(o_ref.dtype)


def project(h, w, mode, extras=(), out_dtype=BF16, tm=1024, tn=512):
    n, k = h.shape
    m = w.shape[1]
    tm = min(tm, n)
    tn = min(tn, m)
    assert n % tm == 0 and m % tn == 0
    in_specs = [pl.BlockSpec((tm, k), lambda j, i: (i, 0)),
                pl.BlockSpec((k, tn), lambda j, i: (0, j))]
    if mode == "gate":
        in_specs.append(pl.BlockSpec((1, tn), lambda j, i: (0, j)))
    elif mode == "headnorm":
        in_specs.append(pl.BlockSpec((1, HEAD_DIM), lambda j, i: (0, 0)))
    return pl.pallas_call(
        functools.partial(_proj_kernel, mode=mode),
        grid=(m // tn, n // tm),
        in_specs=in_specs,
        out_specs=pl.BlockSpec((tm, tn), lambda j, i: (i, j)),
        out_shape=jax.ShapeDtypeStruct((n, m), out_dtype),
        scratch_shapes=[pltpu.VMEM((k, tn), BF16)],
        compiler_params=_cparams(("arbitrary", "arbitrary")),
        name="proj_" + mode,
    )(h, w, *extras)


def _qkv_proj_kernel(h_ref, w_ref, gain_ref, cos_ref, sin_ref, o_ref, w_scr, z_scr, *, dilation):
    j = pl.program_id(0)
    _cast_weight_once(w_ref, w_scr)
    z = jnp.dot(h_ref[...], w_scr[...], preferred_element_type=F32)
    half = ROT_DIM // 2

    @pl.when(j < 2)
    def _():
        mi = lax.broadcasted_iota(jnp.int32, (HEAD_DIM, HEAD_DIM), 0)
        li = lax.broadcasted_iota(jnp.int32, (HEAD_DIM, HEAD_DIM), 1)
        perm = (((li < half) & (mi == li + half)) | ((li >= half) & (li < ROT_DIM) & (mi == li - half))).astype(BF16)
        gcos = gain_ref[0, 0:1, :] * cos_ref[...]
        gsin = gain_ref[0, 1:2, :] * sin_ref[...]
        for c in range(A_HEADS):
            zh = z[:, c * HEAD_DIM:(c + 1) * HEAD_DIM]
            partner = jnp.dot(zh.astype(BF16), perm, preferred_element_type=F32)
            z_scr[c] = lax.rsqrt(_head_mean_sq(zh) + NORM_EPS) * (zh * gcos + partner * gsin)

    @pl.when(j == 2)
    def _():
        for c in range(A_HEADS):
            z_scr[c] = z[:, c * HEAD_DIM:(c + 1) * HEAD_DIM]

    rows = z_scr.shape[1] // dilation
    for r in range(dilation):
        for c in range(A_HEADS):
            src = pl.ds(r, rows, stride=dilation) if dilation > 1 else slice(None)
            lo = r * A_WIDTH + c * HEAD_DIM
            o_ref[:, lo:lo + HEAD_DIM] = z_scr[c, src, :].astype(o_ref.dtype)


def rotary_gains(q_gain, k_gain):
    half = ROT_DIM // 2
    lane = jnp.arange(HEAD_DIM)
    partner = jnp.where(lane < half, lane + half, jnp.where(lane < ROT_DIM, lane - half, lane))
    return jnp.stack([jnp.stack([g, g[partner]]) for g in (q_gain, k_gain)])


def project_qkv(h, w_qkv, gains, cos_t, sin_t, dilation, tm=1024):
    n, k = h.shape
    d = dilation
    return pl.pallas_call(
        functools.partial(_qkv_proj_kernel, dilation=d),
        grid=(3, n // tm),
        in_specs=[pl.BlockSpec((tm, k), lambda j, i: (i, 0)),
                  pl.BlockSpec((k, A_WIDTH), lambda j, i: (0, j)),
                  pl.BlockSpec((1, 2, HEAD_DIM), lambda j, i: (jnp.minimum(j, 1), 0, 0)),
                  pl.BlockSpec((tm, LANES), lambda j, i: (i, 0)),
                  pl.BlockSpec((tm, LANES), lambda j, i: (i, 0))],
        out_specs=pl.BlockSpec((tm // d, d * A_WIDTH), lambda j, i: (i, j)),
        out_shape=jax.ShapeDtypeStruct((n // d, 3 * d * A_WIDTH), BF16),
        scratch_shapes=[pltpu.VMEM((k, A_WIDTH), BF16), pltpu.VMEM((A_HEADS, tm, HEAD_DIM), F32)],
        compiler_params=_cparams(("arbitrary", "arbitrary")),
        name=f"proj_qkv_d{d}",
    )(h, w_qkv, gains, cos_t, sin_t)


def _band_attn_kernel(*refs, qb):
    q_ref = refs[0]
    k_refs = refs[1:qb + 2]
    v_refs = refs[qb + 2:2 * qb + 3]
    o_ref, lse_ref = refs[-2:]
    step = pl.program_id(2)
    scale = HEAD_DIM ** -0.5
    nq = ATT_BLOCK
    qi = lax.broadcasted_iota(jnp.int32, (nq, 2 * nq), 0)
    ki = lax.broadcasted_iota(jnp.int32, (nq, 2 * nq), 1)
    rel = qi + nq - ki
    band = (rel >= 0) & (rel <= nq)
    for a in range(qb):
        blk = step * qb + a
        valid = band & ((blk * nq - nq + ki) >= 0)
        rows = slice(a * nq, (a + 1) * nq)
        lses = []
        for h in range(A_HEADS):
            sl = slice(h * HEAD_DIM, (h + 1) * HEAD_DIM)
            qh = q_ref[0, rows, sl]
            kh = jnp.concatenate([k_refs[a][0, :, sl], k_refs[a + 1][0, :, sl]], axis=0)
            vh = jnp.concatenate([v_refs[a][0, :, sl], v_refs[a + 1][0, :, sl]], axis=0)
            s = lax.dot_general(qh, kh, (((1,), (1,)), ((), ())), preferred_element_type=F32) * scale
            s = jnp.where(valid, s, NEG_BIG)
            m = jnp.max(s, axis=-1, keepdims=True)
            p = jnp.exp(s - m)
            l = jnp.sum(p, axis=-1, keepdims=True)
            o = jnp.dot(p.astype(BF16), vh, preferred_element_type=F32) / l
            o_ref[0, rows, sl] = o.astype(o_ref.dtype)
            lses.append(jnp.broadcast_to(m + jnp.log(l), (nq, LANES // A_HEADS)))
        lse_ref[0, rows, :] = jnp.concatenate(lses, axis=1)


def band_attention_group(qkv, b, g, dilation):
    d = dilation
    sub = qkv.shape[0] // b
    nblk = sub // ATT_BLOCK
    qb = min(ATT_QB, nblk)
    assert nblk % qb == 0
    view = qkv.reshape(b, sub, 3 * d * A_WIDTH)
    qrows = qb * ATT_BLOCK
    key_spec = lambda t, m: pl.BlockSpec(
        (1, ATT_BLOCK, A_WIDTH), lambda bi, r, j: (bi, jnp.maximum(j * qb - 1 + m, 0), t * d + r))
    o, lse = pl.pallas_call(
        functools.partial(_band_attn_kernel, qb=qb),
        grid=(b, d, nblk // qb),
        in_specs=[pl.BlockSpec((1, qrows, A_WIDTH), lambda bi, r, j: (bi, j, r))]
                 + [key_spec(1, m) for m in range(qb + 1)] + [key_spec(2, m) for m in range(qb + 1)],
        out_specs=[pl.BlockSpec((1, qrows, A_WIDTH), lambda bi, r, j: (bi, j, r)),
                   pl.BlockSpec((1, qrows, LANES), lambda bi, r, j: (bi, j, r))],
        out_shape=[jax.ShapeDtypeStruct((b, sub, d * A_WIDTH), BF16),
                   jax.ShapeDtypeStruct((b, sub, d * LANES), F32)],
        compiler_params=_cparams(("arbitrary", "arbitrary", "arbitrary")),
        name=f"band_attn_g{g}",
    )(*([view] * (2 * qb + 3)))
    return o.reshape(b * sub, d * A_WIDTH), lse.reshape(b * sub, d * LANES)


def _cross_attn_kernel(q_ref, k_ref, v_ref, o_ref):
    scale = HEAD_DIM ** -0.5
    for h in range(C_HEADS):
        sl = slice(h * HEAD_DIM, (h + 1) * HEAD_DIM)
        s = lax.dot_general(q_ref[0, :, sl], k_ref[0, :, sl], (((1,), (1,)), ((), ())),
                            preferred_element_type=F32) * scale
        m = jnp.max(s, axis=-1, keepdims=True)
        p = jnp.exp(s - m)
        l = jnp.sum(p, axis=-1, keepdims=True)
        o = jnp.dot(p.astype(BF16), v_ref[0, :, sl], preferred_element_type=F32) / l
        o_ref[0, :, sl] = o.astype(o_ref.dtype)


def cross_attention(qn, kn, v, tm=512):
    b, s, w = qn.shape
    m = kn.shape[1]
    return pl.pallas_call(
        _cross_attn_kernel,
        grid=(b, s // tm),
        in_specs=[pl.BlockSpec((1, tm, w), lambda bi, i: (bi, i, 0)),
                  pl.BlockSpec((1, m, w), lambda bi, i: (bi, 0, 0)),
                  pl.BlockSpec((1, m, w), lambda bi, i: (bi, 0, 0))],
        out_specs=pl.BlockSpec((1, tm, w), lambda bi, i: (bi, i, 0)),
        out_shape=jax.ShapeDtypeStruct((b, s, w), BF16),
        compiler_params=_cparams(("arbitrary", "arbitrary")),
        name="cross_attn",
    )(qn, kn, v)


def _head_sums(x, seg):
    w = seg.shape[0]
    x16 = x.astype(BF16)
    return jnp.concatenate(
        [jnp.dot(x16[:, j:j + w], seg, preferred_element_type=F32) for j in range(0, x.shape[1], w)], axis=1)


def _dot_t(a, b):
    return lax.dot_general(a, b, (((0,), (0,)), ((), ())), preferred_element_type=F32)


def _dot_nt(a, b):
    return lax.dot_general(a, b, (((1,), (1,)), ((), ())), preferred_element_type=F32)


def _rwkv_kernel(zr_ref, mu_ref, w0_ref, wwa_ref, a0_ref, g2_ref, kk_ref, ka_ref, rk_ref,
                 lnw_ref, lnb_ref, seg_ref, y_ref,
                 state_ref, carry_ref, ops_ref, yh_ref):
    t = pl.program_id(1)
    tt = zr_ref.shape[1]
    nch = tt // R_CHUNK
    c = R_CHUNK

    @pl.when(t == 0)
    def _():
        state_ref[...] = jnp.zeros_like(state_ref)
        carry_ref[...] = jnp.zeros_like(carry_ref)

    z = zr_ref[0]
    row = lax.broadcasted_iota(jnp.int32, z.shape, 0)
    prev = jnp.where(row == 0, carry_ref[...], pltpu.roll(z, 1, 0))
    carry_ref[...] = z[tt - 1:tt, :]
    xs = z + (prev - z) * mu_ref[...]

    w3 = 3 * R_WIDTH
    r = xs[:, 0:R_WIDTH]
    k = xs[:, R_WIDTH:2 * R_WIDTH]
    v = xs[:, 2 * R_WIDTH:w3]
    wa_lo = xs[:, w3:w3 + LANES]
    g_lo = xs[:, w3 + LANES:w3 + 2 * LANES]
    lane = lax.broadcasted_iota(jnp.int32, wa_lo.shape, 1)
    wa_in = jnp.where(lane < R_DECAY_LORA, jnp.tanh(wa_lo), wa_lo)
    wa = jnp.dot(wa_in.astype(BF16), wwa_ref[...], preferred_element_type=F32)
    u = -(w0_ref[...] + wa[:, :R_WIDTH])
    softplus = jnp.maximum(u, 0.0) + jnp.log(1.0 + jnp.exp(-jnp.abs(u)))
    w_raw = -softplus - 0.5
    ld = -jnp.exp(w_raw)
    a = jax.nn.sigmoid(a0_ref[...] + wa[:, R_WIDTH:])
    g = jnp.dot(jax.nn.sigmoid(g_lo).astype(BF16), g2_ref[...], preferred_element_type=F32)

    seg = seg_ref[...]
    kk = k * kk_ref[...]
    kk = kk * jnp.minimum(lax.rsqrt(_head_sums(kk * kk, seg)), 1e12)
    k2 = k * (1.0 + (a - 1.0) * ka_ref[...])
    bonus = _head_sums(r * k2 * rk_ref[...], seg) * v

    ri = lax.broadcasted_iota(jnp.int32, (c, c), 0)
    ci = lax.broadcasted_iota(jnp.int32, (c, c), 1)
    tri = (ci <= ri).astype(BF16)
    ld_hi = ld.astype(BF16)
    ld_lo = (ld - ld_hi.astype(F32)).astype(BF16)
    lcs = []
    for ch in range(nch):
        rs = slice(ch * c, (ch + 1) * c)
        lcs.append(jnp.dot(tri, ld_hi[rs], preferred_element_type=F32)
                   + jnp.dot(tri, ld_lo[rs], preferred_element_type=F32))
    lc = jnp.concatenate(lcs, axis=0)
    e_inc = jnp.exp(lc)
    e_exc = jnp.exp(lc - ld)
    e_inv = jnp.exp(-lc)
    a_t = -kk * e_exc
    r_t = r * e_inc
    b_t = kk * a * e_inv
    k_t = k2 * e_inv
    for h in range(R_HEADS):
        hs = slice(h * R_HEAD, (h + 1) * R_HEAD)
        ops_ref[0, h] = a_t[:, hs]
        ops_ref[1, h] = r_t[:, hs]
        ops_ref[2, h] = b_t[:, hs]
        ops_ref[3, h] = k_t[:, hs]
        ops_ref[4, h] = v[:, hs]
        ops_ref[5, h] = e_inc[:, hs]

    strict = ci < ri
    incl = ci <= ri
    eye = (ci == ri)

    ri2 = lax.broadcasted_iota(jnp.int32, (c, 2 * c), 0)
    ci2 = lax.broadcasted_iota(jnp.int32, (c, 2 * c), 1)
    incl2 = jnp.bitwise_and(ci2, c - 1) <= ri2
    eye_f = jnp.where(eye, 1.0, 0.0)
    heads = range(R_HEADS)
    dot = functools.partial(jnp.dot, preferred_element_type=F32)

    def chunk_body(ch, _):
        starts = [pl.multiple_of((ch * R_CPI + sub) * c, c) for sub in range(R_CPI)]
        rows = [pl.ds(r0, c) for r0 in starts]
        items = [(sub, h) for sub in range(R_CPI) for h in heads]
        idx = range(len(items))
        at = [ops_ref[0, h, rows[sub], :] for sub, h in items]
        rt = [ops_ref[1, h, rows[sub], :] for sub, h in items]
        bt = [ops_ref[2, h, rows[sub], :] for sub, h in items]
        kt = [ops_ref[3, h, rows[sub], :] for sub, h in items]
        pc = [ops_ref[5, h, pl.ds(starts[sub] + c - 1, 1), :] for sub, h in items]
        at16 = [x.astype(BF16) for x in at]
        rt16 = [x.astype(BF16) for x in rt]
        bt16 = [x.astype(BF16) for x in bt]
        kt16 = [x.astype(BF16) for x in kt]
        v16 = [ops_ref[4, h, rows[sub], :].astype(BF16) for sub, h in items]
        bk16 = [jnp.concatenate([bt16[i], kt16[i]], axis=0) for i in idx]
        nmat = [jnp.where(strict, _dot_nt(at16[i], bt16[i]), 0.0) for i in idx]
        a_ak = [jnp.where(strict, _dot_nt(at16[i], kt16[i]), 0.0).astype(BF16) for i in idx]
        a_rbk = [jnp.where(incl2, _dot_nt(rt16[i], bk16[i]), 0.0).astype(BF16) for i in idx]
        npow = nmat
        tinv = [eye_f + nmat[i] for i in idx]
        for _i in range(5):
            np16 = [x.astype(BF16) for x in npow]
            npow = [dot(np16[i], np16[i]) for i in idx]
            tinv = [tinv[i] + dot(tinv[i].astype(BF16), npow[i].astype(BF16)) for i in idx]
        akv = [dot(a_ak[i], v16[i]).astype(BF16) for i in idx]
        apw1 = [dot(tinv[i].astype(BF16), jnp.concatenate([at16[i], akv[i]], axis=1)).astype(BF16)
                for i in idx]
        zero = jnp.zeros((c, R_HEAD), BF16)
        rhs2 = [jnp.concatenate([apw1[i], jnp.concatenate([zero, v16[i]], axis=1)], axis=0)
                for i in idx]
        bkh = [jnp.concatenate([bt[i] * pc[i], kt[i] * pc[i]], axis=0).astype(BF16) for i in idx]
        gh = [_dot_t(bkh[i], rhs2[i]) for i in idx]
        qy = [dot(a_rbk[i], rhs2[i]) for i in idx]
        for i, (sub, h) in enumerate(items):
            gm = jnp.where(eye, jnp.broadcast_to(pc[i], (c, c)), 0.0) + gh[i][:, :R_HEAD]
            qp = rt[i] + qy[i][:, :R_HEAD]
            st = state_ref[h]
            res = dot(jnp.concatenate([qp, gm], axis=0).astype(BF16), st.astype(BF16))
            yh_ref[h, rows[sub], :] = res[:c] + qy[i][:, R_HEAD:]
            state_ref[h] = res[c:] + gh[i][:, R_HEAD:]
        return 0

    lax.fori_loop(0, nch // R_CPI, chunk_body, 0)

    y = jnp.concatenate([yh_ref[h] for h in range(R_HEADS)], axis=1)
    mean = _head_sums(y, seg) * (1.0 / R_HEAD)
    dlt = y - mean
    var = _head_sums(dlt * dlt, seg) * (1.0 / R_HEAD)
    yn = dlt * lax.rsqrt(var + R_GN_EPS) * lnw_ref[...] + lnb_ref[...]
    y_ref[0] = ((yn + bonus) * g).astype(y_ref.dtype)


def rwkv7_mix(zr, mu, w0, w2, a0, a2, g2, k_k, k_a, r_k, ln_w, ln_b, tt=256):
    b, s, cols = zr.shape
    row = lambda x: x.reshape(1, -1).astype(F32)
    wwa = jnp.zeros((LANES, 2 * R_WIDTH), F32)
    wwa = wwa.at[:R_DECAY_LORA, :R_WIDTH].set(w2).at[R_DECAY_LORA:, R_WIDTH:].set(a2).astype(BF16)
    hid = jnp.arange(R_SEG) // R_HEAD
    seg = (hid[:, None] == hid[None, :]).astype(BF16)
    full = lambda shape: pl.BlockSpec(shape, lambda bi, t: (0,) * len(shape))
    return pl.pallas_call(
        _rwkv_kernel,
        grid=(b, s // tt),
        in_specs=[pl.BlockSpec((1, tt, cols), lambda bi, t: (bi, t, 0)),
                  full((1, cols)), full((1, R_WIDTH)), full((LANES, 2 * R_WIDTH)), full((1, R_WIDTH)),
                  full((R_GATE_LORA, R_WIDTH)), full((1, R_WIDTH)), full((1, R_WIDTH)), full((1, R_WIDTH)),
                  full((1, R_WIDTH)), full((1, R_WIDTH)), full((R_SEG, R_SEG))],
        out_specs=pl.BlockSpec((1, tt, R_WIDTH), lambda bi, t: (bi, t, 0)),
        out_shape=jax.ShapeDtypeStruct((b, s, R_WIDTH), BF16),
        scratch_shapes=[pltpu.VMEM((R_HEADS, R_HEAD, R_HEAD), F32),
                        pltpu.VMEM((1, cols), F32),
                        pltpu.VMEM((6, R_HEADS, tt, R_HEAD), F32),
                        pltpu.VMEM((R_HEADS, tt, R_HEAD), F32)],
        compiler_params=_cparams(("arbitrary", "arbitrary")),
        name="rwkv7_mix",
    )(zr, row(mu), row(w0), wwa, row(a0), g2.astype(BF16), row(k_k), row(k_a), row(r_k),
      row(ln_w), row(ln_b), seg)


def _merge_kernel(x_ref, o0_ref, o1_ref, o2_ref, l0_ref, l1_ref, l2_ref, yb_ref, yc_ref, gt_ref,
                  wb_ref, wo_ref, gn_ref, rw_ref, rb_ref,
                  x1_ref, h2_ref, idx_ref, gate_ref, rank_ref, cnt_ref, base_ref, o_scr, l_scr):
    @pl.when(pl.program_id(0) == 0)
    def _():
        base_ref[...] = jnp.zeros_like(base_ref)

    tm_rows = x_ref.shape[0]
    for gi, (o_ref, l_ref) in enumerate(((o0_ref, l0_ref), (o1_ref, l1_ref), (o2_ref, l2_ref))):
        dil = A_GROUPS[gi][1]
        for r in range(dil):
            dst = pl.ds(r, tm_rows // dil, stride=dil) if dil > 1 else slice(None)
            for h in range(A_HEADS):
                lo = r * A_WIDTH + h * HEAD_DIM
                o_scr[gi, h, dst, :] = o_ref[:, lo:lo + HEAD_DIM].astype(F32)
            l_scr[gi, dst, :] = l_ref[:, r * LANES:(r + 1) * LANES]
    lses = [l_scr[gi] for gi in range(3)]
    lmax = jnp.maximum(jnp.maximum(lses[0], lses[1]), lses[2])
    es = [jnp.exp(l - lmax) for l in lses]
    inv = 1.0 / (es[0] + es[1] + es[2])
    qw = LANES // A_HEADS
    heads = []
    for h in range(A_HEADS):
        sl = slice(h * HEAD_DIM, (h + 1) * HEAD_DIM)
        acc = None
        for gi in range(3):
            alpha = (es[gi] * inv)[:, h * qw:h * qw + 1]
            term = alpha * o_scr[gi, h]
            acc = term if acc is None else acc + term
        heads.append(acc)
    ya = jnp.concatenate(heads, axis=1).astype(BF16)
    d = x_ref.shape[1]
    merged = None
    for n, yn in enumerate((ya, yb_ref[...], yc_ref[...])):
        proj = jnp.dot(yn, wb_ref[n], preferred_element_type=F32)
        term = gt_ref[:, n * d:(n + 1) * d].astype(F32) * proj
        merged = term if merged is None else merged + term
    x1 = x_ref[...] + jnp.dot(merged.astype(BF16), wo_ref[...], preferred_element_type=F32)
    x1_ref[...] = x1
    ms = jnp.mean(x1 * x1, axis=-1, keepdims=True)
    h2 = x1 * lax.rsqrt(ms + NORM_EPS) * gn_ref[...]
    h2_ref[...] = h2
    rw = rw_ref[...]
    h2_hi, rw_hi = h2.astype(BF16), rw.astype(BF16)
    h2_lo = (h2 - h2_hi.astype(F32)).astype(BF16)
    rw_lo = (rw - rw_hi.astype(F32)).astype(BF16)
    logits = (jnp.dot(h2_hi, rw_hi, preferred_element_type=F32) + jnp.dot(h2_hi, rw_lo, preferred_element_type=F32)
              + jnp.dot(h2_lo, rw_hi, preferred_element_type=F32)) + rb_ref[...]
    tm = logits.shape[0]
    lane = lax.broadcasted_iota(jnp.int32, logits.shape, 1)
    vals, idxs = [], []
    cur = logits
    for _k in range(TOP_K):
        m = jnp.max(cur, axis=-1, keepdims=True)
        ik = jnp.min(jnp.where(cur == m, lane, N_EXPERTS), axis=-1, keepdims=True)
        vals.append(m)
        idxs.append(ik)
        cur = jnp.where(lane == ik, -jnp.inf, cur)
    exps = [jnp.exp(vk - vals[0]) for vk in vals]
    tot = exps[0] + exps[1] + exps[2] + exps[3]
    onehots = [lane == ik for ik in idxs]
    hits = sum(jnp.where(oh, 1.0, 0.0) for oh in onehots)
    ri = lax.broadcasted_iota(jnp.int32, (tm, tm), 0)
    ci = lax.broadcasted_iota(jnp.int32, (tm, tm), 1)
    before = jnp.dot((ci < ri).astype(BF16), hits.astype(BF16), preferred_element_type=F32) + base_ref[...]
    for kk in range(TOP_K):
        idx_ref[:, kk:kk + 1] = idxs[kk]
        gate_ref[:, kk:kk + 1] = exps[kk] / tot
        rank_ref[:, kk:kk + 1] = jnp.sum(jnp.where(onehots[kk], before, 0.0), axis=-1,
                                         keepdims=True).astype(jnp.int32)
    base_ref[...] = base_ref[...] + jnp.sum(hits, axis=0, keepdims=True)
    cnt_ref[...] = base_ref[...].astype(jnp.int32)


def merge_and_route(x2d, outs, lses, yb, yc, gates, w_branch, w_out, norm_ffn, router_w, router_b, tm=512):
    n, d = x2d.shape
    rows = lambda w: pl.BlockSpec((tm, w), lambda i: (i, 0))
    packed = lambda w, dil: pl.BlockSpec((tm // dil, dil * w), lambda i: (i, 0))
    full = lambda shape: pl.BlockSpec(shape, lambda i: (0,) * len(shape))
    return pl.pallas_call(
        _merge_kernel,
        grid=(n // tm,),
        in_specs=[rows(d)] + [packed(A_WIDTH, dil) for _w, dil in A_GROUPS] + [packed(LANES, dil) for _w, dil in A_GROUPS]
                 + [rows(R_WIDTH), rows(C_WIDTH), rows(N_BRANCH * d),
                    full((N_BRANCH, A_WIDTH, d)), full((d, d)), full((1, d)), full((d, N_EXPERTS)), full((1, N_EXPERTS))],
        out_specs=[rows(d), rows(d), rows(TOP_K), rows(TOP_K), rows(TOP_K), full((1, N_EXPERTS))],
        out_shape=[jax.ShapeDtypeStruct((n, d), F32), jax.ShapeDtypeStruct((n, d), F32),
                   jax.ShapeDtypeStruct((n, TOP_K), jnp.int32), jax.ShapeDtypeStruct((n, TOP_K), F32),
                   jax.ShapeDtypeStruct((n, TOP_K), jnp.int32), jax.ShapeDtypeStruct((1, N_EXPERTS), jnp.int32)],
        scratch_shapes=[pltpu.VMEM((1, N_EXPERTS), F32), pltpu.VMEM((len(A_GROUPS), A_HEADS, tm, HEAD_DIM), F32),
                        pltpu.VMEM((len(A_GROUPS), tm, LANES), F32)],
        compiler_params=_cparams(("arbitrary",)),
        name="merge_route",
    )(x2d, *outs, *lses, yb, yc, gates, w_branch.astype(BF16), w_out.astype(BF16),
      norm_ffn.reshape(1, d), router_w, router_b.reshape(1, N_EXPERTS))


def block_layout(counts, n_assign):
    counts = counts.reshape(-1)
    padded = (counts + MOE_ROWS - 1) // MOE_ROWS * MOE_ROWS
    pad_end = jnp.cumsum(padded)
    pad_start = (pad_end - padded).astype(jnp.int32)
    n_blocks = -(-n_assign // MOE_ROWS) + N_EXPERTS
    blk_row = jnp.arange(n_blocks, dtype=jnp.int32) * MOE_ROWS
    owner = jnp.sum((blk_row[:, None] >= pad_end[None, :]).astype(jnp.int32), axis=1)
    block_expert = jnp.minimum(owner, N_EXPERTS - 1).astype(jnp.int32)
    unused = blk_row >= pad_end[-1]
    zero_flag = (unused | (blk_row + MOE_ROWS == pad_end[block_expert])).astype(jnp.int32)
    n_used = (pad_end[-1:] // MOE_ROWS).astype(jnp.int32)
    has_rows = counts > 0
    eid = jnp.arange(N_EXPERTS, dtype=jnp.int32)
    later = jnp.where(has_rows[None, :] & (eid[None, :] > eid[:, None]), eid[None, :], N_EXPERTS)
    next_expert = jnp.min(later, axis=1)
    next_expert = jnp.where(next_expert == N_EXPERTS, -1, next_expert).astype(jnp.int32)
    run_parity = ((jnp.cumsum(has_rows) - has_rows) % 2).astype(jnp.int32)
    return pad_start, block_expert, zero_flag, n_used, next_expert, run_parity


def _dest_kernel(ps_ref, idx_ref, rank_ref, dest_ref):
    idx = idx_ref[...]
    dest = rank_ref[...]
    for e in range(N_EXPERTS):
        dest = dest + jnp.where(idx == e, ps_ref[e], 0)
    dest_ref[...] = dest


def assignment_rows(top_idx, rank, pad_start):
    n = top_idx.shape[0]
    rows = n * TOP_K // LANES
    flat = lambda t: t.reshape(rows, LANES)
    spec = pl.BlockSpec((rows, LANES), lambda i, ps: (0, 0))
    out = pl.pallas_call(
        _dest_kernel,
        grid_spec=pltpu.PrefetchScalarGridSpec(num_scalar_prefetch=1, grid=(1,), in_specs=[spec, spec],
                                               out_specs=spec),
        out_shape=jax.ShapeDtypeStruct((rows, LANES), jnp.int32),
        compiler_params=_cparams(("arbitrary",)),
        name="assignment_rows",
    )(pad_start, flat(top_idx), flat(rank))
    return out.reshape(n, TOP_K)


def _scatter_kernel(zf_ref, dest_ref, h2_ref, xs_hbm, zeros_ref, sem, zsem):
    i = pl.program_id(0)
    tm = h2_ref.shape[0]
    nblk = zf_ref.shape[0]

    def zero_block(j):
        return pltpu.make_async_copy(zeros_ref, xs_hbm.at[pl.ds(j * MOE_ROWS, MOE_ROWS)], zsem)

    @pl.when(i == 0)
    def _():
        zeros_ref[...] = jnp.zeros_like(zeros_ref)

        def start(j, _):
            @pl.when(zf_ref[j] != 0)
            def _():
                zero_block(j).start()
            return 0

        def wait(j, _):
            @pl.when(zf_ref[j] != 0)
            def _():
                zero_block(j).wait()
            return 0

        lax.fori_loop(0, nblk, start, 0)
        lax.fori_loop(0, nblk, wait, 0)

    def body(t, _):
        for kk in range(TOP_K):
            row = dest_ref[0, 0, t * TOP_K + kk]
            pltpu.make_async_copy(h2_ref.at[pl.ds(t, 1)], xs_hbm.at[pl.ds(row, 1)], sem).start(priority=kk % 2)
        return 0

    lax.fori_loop(0, tm, body, 0)
    pltpu.make_async_copy(xs_hbm.at[pl.ds(0, tm * TOP_K)], xs_hbm.at[pl.ds(0, tm * TOP_K)], sem).wait()


def scatter_rows(h2, dest, zero_flag, n_rows, tm=1024):
    n, w = h2.shape
    dest3 = dest.reshape(n // tm, 1, tm * TOP_K)
    grid_spec = pltpu.PrefetchScalarGridSpec(
        num_scalar_prefetch=1,
        grid=(n // tm,),
        in_specs=[pl.BlockSpec((1, 1, tm * TOP_K), lambda i, zf: (i, 0, 0), memory_space=pltpu.SMEM),
                  pl.BlockSpec((tm, w), lambda i, zf: (i, 0))],
        out_specs=pl.BlockSpec(memory_space=pl.ANY),
        scratch_shapes=[pltpu.VMEM((MOE_ROWS, w), h2.dtype), pltpu.SemaphoreType.DMA(()),
                        pltpu.SemaphoreType.DMA(())],
    )
    return pl.pallas_call(
        _scatter_kernel,
        grid_spec=grid_spec,
        out_shape=jax.ShapeDtypeStruct((n_rows, w), h2.dtype),
        compiler_params=_cparams(("arbitrary",)),
        name="scatter_rows",
    )(zero_flag, dest3, h2)


def _expert_kernel(be_ref, nu_ref, nxt_ref, par_ref, xs_ref, w1_hbm, b1_ref, w2_hbm, b2_ref, y_ref,
                   w1_f32, w2_f32, w1_scr, w2_scr, wsem):
    i = pl.program_id(0)
    e = be_ref[i]
    prev = be_ref[jnp.maximum(i - 1, 0)]

    def fetch(expert, slot):
        return (pltpu.make_async_copy(w1_hbm.at[expert], w1_f32.at[slot], wsem.at[slot]),
                pltpu.make_async_copy(w2_hbm.at[expert], w2_f32.at[slot], wsem.at[slot]))

    @pl.when((i < nu_ref[0]) & ((i == 0) | (e != prev)))
    def _():
        slot = par_ref[e]

        @pl.when(i == 0)
        def _():
            for c in fetch(e, slot):
                c.start()

        for c in fetch(e, slot):
            c.wait()

        @pl.when(nxt_ref[e] >= 0)
        def _():
            for c in fetch(nxt_ref[e], 1 - slot):
                c.start()

        w1_scr[...] = w1_f32[slot].astype(BF16)
        w2_scr[...] = w2_f32[slot].astype(BF16)

    @pl.when(i < nu_ref[0])
    def _():
        xb = xs_ref[...].astype(BF16)
        dff = w2_scr.shape[0]
        u = jnp.dot(xb, w1_scr[...], preferred_element_type=F32) + b1_ref[0]
        glu = jnp.minimum(u[:, :dff], SWIGLU_LIMIT)
        lin = jnp.clip(u[:, dff:], -SWIGLU_LIMIT, SWIGLU_LIMIT)
        act = glu * jax.nn.sigmoid(SWIGLU_ALPHA * glu) * (lin + 1.0)
        y = jnp.dot(act.astype(BF16), w2_scr[...], preferred_element_type=F32) + b2_ref[0]
        y_ref[...] = y

    @pl.when(i >= nu_ref[0])
    def _():
        y_ref[...] = jnp.zeros_like(y_ref)


def expert_ffn(x_sorted, block_expert, n_used, next_expert, run_parity, w1, b1, w2, b2):
    n_rows, d = x_sorted.shape
    w = d
    nblk = n_rows // MOE_ROWS
    dff2 = w1.shape[2]
    used = lambda i, nu: jnp.minimum(i, nu[0] - 1)
    grid_spec = pltpu.PrefetchScalarGridSpec(
        num_scalar_prefetch=4,
        grid=(nblk,),
        in_specs=[
            pl.BlockSpec((MOE_ROWS, w), lambda i, be, nu, nx, pa: (used(i, nu), 0)),
            pl.BlockSpec(memory_space=pl.ANY),
            pl.BlockSpec((1, 1, dff2), lambda i, be, nu, nx, pa: (be[used(i, nu)], 0, 0)),
            pl.BlockSpec(memory_space=pl.ANY),
            pl.BlockSpec((1, 1, d), lambda i, be, nu, nx, pa: (be[used(i, nu)], 0, 0)),
        ],
        out_specs=pl.BlockSpec((MOE_ROWS, w), lambda i, be, nu, nx, pa: (i, 0)),
        scratch_shapes=[pltpu.VMEM((2, d, dff2), F32), pltpu.VMEM((2, dff2 // 2, d), F32),
                        pltpu.VMEM((d, dff2), BF16), pltpu.VMEM((dff2 // 2, d), BF16),
                        pltpu.SemaphoreType.DMA((2,))],
    )
    return pl.pallas_call(
        _expert_kernel,
        grid_spec=grid_spec,
        out_shape=jax.ShapeDtypeStruct((n_rows, w), F32),
        compiler_params=_cparams(("arbitrary",)),
        name="expert_ffn",
    )(block_expert, n_used, next_expert, run_parity, x_sorted, w1, b1.reshape(N_EXPERTS, 1, dff2), w2,
      b2.reshape(N_EXPERTS, 1, d))


def _gather_assigned_rows(y_hbm, dest_ref, dst_ref, sem, tm):
    def body(g, _):
        base = pl.multiple_of(g * SUBLANES, SUBLANES)
        for u in range(SUBLANES):
            for kk in range(TOP_K):
                row = dest_ref[(base + u) * TOP_K + kk]
                pltpu.make_async_copy(y_hbm.at[pl.ds(row, 1)], dst_ref.at[pl.ds(kk * tm + base + u, 1)],
                                      sem).start(priority=kk % 2)
        return 0
    lax.fori_loop(0, tm // SUBLANES, body, 0)


def _combine_kernel(d0_ref, dn_ref, x1_ref, gate_ref, y_hbm, o_ref, ybuf, sems):
    i = pl.program_id(0)
    nblk = pl.num_programs(0)
    slot = lax.rem(i, 2)
    tm = o_ref.shape[0]

    @pl.when(i == 0)
    def _():
        _gather_assigned_rows(y_hbm, d0_ref.at[0, 0], ybuf.at[0], sems.at[0], tm)

    @pl.when(i + 1 < nblk)
    def _():
        _gather_assigned_rows(y_hbm, dn_ref.at[0, 0], ybuf.at[1 - slot], sems.at[1 - slot], tm)

    pltpu.make_async_copy(y_hbm.at[pl.ds(0, TOP_K * tm)], ybuf.at[slot], sems.at[slot]).wait()
    acc = x1_ref[...]
    for kk in range(TOP_K):
        acc = acc + gate_ref[:, kk:kk + 1] * ybuf[slot, pl.ds(kk * tm, tm), :]
    o_ref[...] = acc


def moe_combine(x1, gate, y_sorted, dest, tm=256):
    n, d = x1.shape
    w = y_sorted.shape[1]
    nblk = n // tm
    dest3 = dest.reshape(nblk, 1, tm * TOP_K)
    return pl.pallas_call(
        _combine_kernel,
        grid=(nblk,),
        in_specs=[pl.BlockSpec((1, 1, TOP_K * tm), lambda i: (0, 0, 0), memory_space=pltpu.SMEM),
                  pl.BlockSpec((1, 1, TOP_K * tm), lambda i: (jnp.minimum(i + 1, nblk - 1), 0, 0),
                               memory_space=pltpu.SMEM),
                  pl.BlockSpec((tm, d), lambda i: (i, 0)),
                  pl.BlockSpec((tm, TOP_K), lambda i: (i, 0)),
                  pl.BlockSpec(memory_space=pl.ANY)],
        out_specs=pl.BlockSpec((tm, d), lambda i: (i, 0)),
        out_shape=jax.ShapeDtypeStruct((n, d), F32),
        scratch_shapes=[pltpu.VMEM((2, TOP_K * tm, w), y_sorted.dtype), pltpu.SemaphoreType.DMA((2,))],
        compiler_params=_cparams(("arbitrary",)),
        name="moe_combine",
    )(dest3, dest3, x1, gate, y_sorted)


def dump_offsets(counts, block_expert, n_used, n_tok):
    counts = counts.reshape(-1)
    padded = (counts + MOE_ROWS - 1) // MOE_ROWS * MOE_ROWS
    pad_end = jnp.cumsum(padded)
    pad_start = pad_end - padded
    pads = padded - counts
    before = jnp.cumsum(pads) - pads
    base = TOP_K * n_tok
    per_expert = base + before - pad_start - counts
    tail = base + jnp.sum(pads) - pad_end[-1]
    nblk = block_expert.shape[0]
    blk = jnp.where(jnp.arange(nblk) < n_used[0], per_expert[block_expert], tail)
    return jnp.concatenate([jnp.full((1,), base, jnp.int32), blk.astype(jnp.int32),
                            jnp.zeros((1,), jnp.int32)])


def _invert_kernel(off_ref, dest_ref, inv_ref):
    i = pl.program_id(0)
    per_step = dest_ref.shape[-1]
    n_tok = pl.num_programs(0) * per_step // TOP_K

    @pl.when(i == 0)
    def _():
        def fill_block(j, _):
            off = off_ref[j]

            def fill(q, _):
                r = j * MOE_ROWS + q
                inv_ref[r] = r + off
                return 0
            lax.fori_loop(0, MOE_ROWS, fill, 0, unroll=8)
            return 0
        lax.fori_loop(0, off_ref.shape[0], fill_block, 0)

    def put(j, _):
        a = i * per_step + j
        tok = lax.shift_right_logical(a, TOP_K.bit_length() - 1)
        inv_ref[dest_ref[0, 0, j] + MOE_ROWS] = jnp.bitwise_and(a, TOP_K - 1) * n_tok + tok
        return 0
    lax.fori_loop(0, per_step, put, 0, unroll=8)


def invert_rows(dest, block_offsets, tm=1024):
    n = dest.shape[0]
    assert TOP_K & (TOP_K - 1) == 0 and n & (n - 1) == 0
    per_step = tm * TOP_K
    grid_spec = pltpu.PrefetchScalarGridSpec(
        num_scalar_prefetch=1,
        grid=(n // tm,),
        in_specs=[pl.BlockSpec((1, 1, per_step), lambda i, off: (i, 0, 0), memory_space=pltpu.SMEM)],
        out_specs=pl.BlockSpec(memory_space=pltpu.SMEM),
    )
    return pl.pallas_call(
        _invert_kernel,
        grid_spec=grid_spec,
        out_shape=jax.ShapeDtypeStruct((block_offsets.shape[0] * MOE_ROWS,), jnp.int32),
        compiler_params=_cparams(("arbitrary",)),
        name="invert_rows",
    )(block_offsets, dest.reshape(n // tm, 1, per_step))


def _start_row_gather(h2_hbm, inv_ref, dst, sem, r, priority):
    tok = jnp.bitwise_and(inv_ref[0, 0, r], h2_hbm.shape[0] - 1)
    pltpu.make_async_copy(h2_hbm.at[pl.ds(tok, 1)], dst.at[pl.ds(r, 1)], sem).start(priority=priority)


def _start_row_scatter(src, inv_ref, ytok_hbm, sem, r, priority):
    pltpu.make_async_copy(src.at[pl.ds(r, 1)], ytok_hbm.at[pl.ds(inv_ref[0, 0, r], 1)], sem).start(priority=priority)


def _fused_expert_kernel(be_ref, nu_ref, nxt_ref, par_ref, inv0_ref, invn_ref, invp_ref, invl_ref,
                         h2_hbm, w1_hbm, b1_ref, w2_hbm, b2_ref, ytok_hbm,
                         w1_f32, w2_f32, w1_scr, w2_scr, xbuf0, xbuf1, ybuf0, ybuf1, wsem, gsem, ssem):
    s = pl.program_id(0)
    nblk = pl.num_programs(0)
    e = be_ref[s]
    prev = be_ref[jnp.maximum(s - 1, 0)]
    used = s < nu_ref[0]
    rows = MOE_ROWS
    xbufs, ybufs = (xbuf0, xbuf1), (ybuf0, ybuf1)

    def whole(buf, hbm, sem_slot):
        return pltpu.make_async_copy(hbm.at[pl.ds(0, rows)], buf, sem_slot)

    def fetch(expert, wslot):
        return (pltpu.make_async_copy(w1_hbm.at[expert], w1_f32.at[wslot], wsem.at[wslot]),
                pltpu.make_async_copy(w2_hbm.at[expert], w2_f32.at[wslot], wsem.at[wslot]))

    @pl.when(s == 0)
    def _():
        ybuf0[...] = jnp.zeros_like(ybuf0)
        ybuf1[...] = jnp.zeros_like(ybuf1)

        def g0(r, _):
            _start_row_gather(h2_hbm, inv0_ref, xbuf0, gsem.at[0], r, 0)
            return 0
        lax.fori_loop(0, rows, g0, 0)

    @pl.when(used & ((s == 0) | (e != prev)))
    def _():
        wslot = par_ref[e]

        @pl.when(s == 0)
        def _():
            for c in fetch(e, wslot):
                c.start()

        for c in fetch(e, wslot):
            c.wait()

        @pl.when(nxt_ref[e] >= 0)
        def _():
            for c in fetch(nxt_ref[e], 1 - wslot):
                c.start()

        w1_scr[...] = w1_f32[wslot].astype(BF16)
        w2_scr[...] = w2_f32[wslot].astype(BF16)

    def step(slot):
        xcur, xnext = xbufs[slot], xbufs[1 - slot]
        ycur, yprev = ybufs[slot], ybufs[1 - slot]
        whole(xcur, h2_hbm, gsem.at[slot]).wait()

        @pl.when(used)
        def _():
            xb = xcur[...].astype(BF16)
            dff = w2_scr.shape[0]
            n1 = w1_scr.shape[1] // MXU_COLS
            n2 = w2_scr.shape[1] // MXU_COLS
            parts = []
            for c in range(n1):
                cs = slice(c * MXU_COLS, (c + 1) * MXU_COLS)
                parts.append(jnp.dot(xb, w1_scr[:, cs], preferred_element_type=F32) + b1_ref[0, :, cs])
                for r in range(c * rows // n1, (c + 1) * rows // n1):
                    _start_row_gather(h2_hbm, invn_ref, xnext, gsem.at[1 - slot], r, r % 2)
            u = jnp.concatenate(parts, axis=1)
            glu = jnp.minimum(u[:, :dff], SWIGLU_LIMIT)
            lin = jnp.clip(u[:, dff:], -SWIGLU_LIMIT, SWIGLU_LIMIT)
            act = (glu * jax.nn.sigmoid(SWIGLU_ALPHA * glu) * (lin + 1.0)).astype(BF16)
            yparts = []
            for c in range(n2):
                cs = slice(c * MXU_COLS, (c + 1) * MXU_COLS)
                yparts.append(jnp.dot(act, w2_scr[:, cs], preferred_element_type=F32) + b2_ref[0, :, cs])
                for r in range(c * rows // n2, (c + 1) * rows // n2):
                    _start_row_scatter(yprev, invp_ref, ytok_hbm, ssem.at[1 - slot], r, (r + 1) % 2)
            y = jnp.concatenate(yparts, axis=1)

            @pl.when(s >= 1)
            def _():
                whole(ycur, ytok_hbm, ssem.at[slot]).wait()
            ycur[...] = y

        @pl.when(jnp.logical_not(used))
        def _():
            def g(r, _):
                _start_row_gather(h2_hbm, invn_ref, xnext, gsem.at[1 - slot], r, 0)
                _start_row_scatter(yprev, invp_ref, ytok_hbm, ssem.at[1 - slot], r, 1)
                return 0
            lax.fori_loop(0, rows, g, 0)

            @pl.when(s >= 1)
            def _():
                whole(ycur, ytok_hbm, ssem.at[slot]).wait()

        @pl.when(s == nblk - 1)
        def _():
            def last(r, _):
                _start_row_scatter(ycur, invl_ref, ytok_hbm, ssem.at[slot], r, 0)
                return 0
            lax.fori_loop(0, rows, last, 0)
            whole(yprev, ytok_hbm, ssem.at[1 - slot]).wait()
            whole(ycur, ytok_hbm, ssem.at[slot]).wait()
            whole(xnext, h2_hbm, gsem.at[1 - slot]).wait()

    for parity in range(2):
        pl.when(lax.rem(s, 2) == parity)(functools.partial(step, parity))


def fused_expert_ffn(h2, inv, block_expert, n_used, next_expert, run_parity, w1, b1, w2, b2):
    n, d = h2.shape
    nblk = block_expert.shape[0]
    dff2 = w1.shape[2]
    inv3 = inv.reshape(nblk + 2, 1, MOE_ROWS)
    used = lambda s, nu: jnp.minimum(s, nu[0] - 1)
    smem_blk = lambda f: pl.BlockSpec((1, 1, MOE_ROWS), f, memory_space=pltpu.SMEM)
    grid_spec = pltpu.PrefetchScalarGridSpec(
        num_scalar_prefetch=4,
        grid=(nblk,),
        in_specs=[
            smem_blk(lambda s, be, nu, nx, pa: (1, 0, 0)),
            smem_blk(lambda s, be, nu, nx, pa: (s + 2, 0, 0)),
            smem_blk(lambda s, be, nu, nx, pa: (s, 0, 0)),
            smem_blk(lambda s, be, nu, nx, pa: (nblk, 0, 0)),
            pl.BlockSpec(memory_space=pl.ANY),
            pl.BlockSpec(memory_space=pl.ANY),
            pl.BlockSpec((1, 1, dff2), lambda s, be, nu, nx, pa: (be[used(s, nu)], 0, 0)),
            pl.BlockSpec(memory_space=pl.ANY),
            pl.BlockSpec((1, 1, d), lambda s, be, nu, nx, pa: (be[used(s, nu)], 0, 0)),
        ],
        out_specs=pl.BlockSpec(memory_space=pl.ANY),
        scratch_shapes=[pltpu.VMEM((2, d, dff2), F32), pltpu.VMEM((2, dff2 // 2, d), F32),
                        pltpu.VMEM((d, dff2), BF16), pltpu.VMEM((dff2 // 2, d), BF16),
                        pltpu.VMEM((MOE_ROWS, d), F32), pltpu.VMEM((MOE_ROWS, d), F32),
                        pltpu.VMEM((MOE_ROWS, d), F32), pltpu.VMEM((MOE_ROWS, d), F32),
                        pltpu.SemaphoreType.DMA((2,)), pltpu.SemaphoreType.DMA((2,)), pltpu.SemaphoreType.DMA((2,))],
    )
    return pl.pallas_call(
        _fused_expert_kernel,
        grid_spec=grid_spec,
        out_shape=jax.ShapeDtypeStruct(((nblk + 1) * MOE_ROWS, d), F32),
        compiler_params=_cparams(("arbitrary",)),
        name="fused_expert_ffn",
    )(block_expert, n_used, next_expert, run_parity, inv3, inv3, inv3, inv3, h2, w1,
      b1.reshape(N_EXPERTS, 1, dff2), w2, b2.reshape(N_EXPERTS, 1, d))


def _weighted_sum_kernel(x1_ref, gate_ref, *refs):
    y_refs, o_ref = refs[:-1], refs[-1]
    acc = x1_ref[...]
    for kk, y_ref in enumerate(y_refs):
        acc = acc + gate_ref[:, kk:kk + 1] * y_ref[...]
    o_ref[...] = acc


def weighted_sum(x1, gate, y_tok, tm=512):
    n, d = x1.shape
    steps = n // tm
    return pl.pallas_call(
        _weighted_sum_kernel,
        grid=(steps,),
        in_specs=[pl.BlockSpec((tm, d), lambda i: (i, 0)), pl.BlockSpec((tm, TOP_K), lambda i: (i, 0))]
                 + [pl.BlockSpec((tm, d), functools.partial(lambda i, kk: (kk * steps + i, 0), kk=kk))
                    for kk in range(TOP_K)],
        out_specs=pl.BlockSpec((tm, d), lambda i: (i, 0)),
        out_shape=jax.ShapeDtypeStruct((n, d), F32),
        compiler_params=_cparams(("arbitrary",)),
        name="weighted_sum",
    )(x1, gate, *([y_tok] * TOP_K))


def kernel(x, mem, positions, norm_mix, w_in, b_gate, a_q_gain, a_k_gain, r_mu, r_w0, r_w2, r_a0, r_a2,
           r_g2, r_k_k, r_k_a, r_r_k, r_ln_w, r_ln_b, mem_norm, w_mem_kv, c_q_gain, c_k_gain, w_branch,
           w_out, norm_ffn, router_w, router_b, exp_w1, exp_b1, exp_w2, exp_b2):
    b, s, d = x.shape
    n = b * s
    depth = norm_mix.shape[0]
    n_groups = len(A_GROUPS)
    qkv_cols = n_groups * A_WIDTH
    off_k, off_v, off_r = qkv_cols, 2 * qkv_cols, 3 * qkv_cols
    off_cq = off_r + R_COLS
    off_gate = off_cq + C_WIDTH

    x2d = x.reshape(n, d)
    cos_t, sin_t = rope_tables(positions.reshape(n, 1).astype(jnp.int32))
    for l in range(depth):
        w_l = w_in[l]
        h = rmsnorm_rows(x2d, norm_mix[l])
        qk_gains = rotary_gains(a_q_gain[l], a_k_gain[l])
        zr = project(h, w_l[:, off_r:off_cq], "plain", out_dtype=F32, tn=R_COLS // 2)
        cq = project(h, w_l[:, off_cq:off_gate], "headnorm", (c_q_gain[l].reshape(1, HEAD_DIM),))
        gates = project(h, w_l[:, off_gate:], "gate", (b_gate[l].reshape(1, -1),))

        shp = lambda t: t.reshape(b, s, -1)
        outs, lses = [], []
        for g, (window, dilation) in enumerate(A_GROUPS):
            assert window // dilation == ATT_BLOCK
            cols = slice(g * A_WIDTH, (g + 1) * A_WIDTH)
            w_g = jnp.concatenate([w_l[:, :off_k][:, cols], w_l[:, off_k:off_v][:, cols],
                                   w_l[:, off_v:off_r][:, cols]], axis=1)
            qkv = project_qkv(h, w_g, qk_gains, cos_t, sin_t, dilation)
            o, lse = band_attention_group(qkv, b, g, dilation)
            outs.append(o)
            lses.append(lse)

        yb = rwkv7_mix(shp(zr), r_mu[l], r_w0[l], r_w2[l], r_a0[l], r_a2[l], r_g2[l], r_k_k[l], r_k_a[l],
                       r_r_k[l].reshape(-1), r_ln_w[l], r_ln_b[l]).reshape(n, R_WIDTH)

        mlen = mem.shape[1]
        mem_n = rmsnorm_rows(mem.reshape(b * mlen, d), mem_norm[l])
        wkv = w_mem_kv[l]
        ck = project(mem_n, wkv[:, :C_WIDTH], "headnorm", (c_k_gain[l].reshape(1, HEAD_DIM),))
        cv = project(mem_n, wkv[:, C_WIDTH:], "plain")
        yc = cross_attention(shp(cq), ck.reshape(b, mlen, C_WIDTH), cv.reshape(b, mlen, C_WIDTH)).reshape(n, C_WIDTH)

        x1, h2p, top_idx, gate, rank, counts = merge_and_route(
            x2d, outs, lses, yb, yc, gates, w_branch[l], w_out[l], norm_ffn[l], router_w[l], router_b[l])
        pad_start, block_expert, zero_flag, n_used, next_expert, run_parity = block_layout(counts, n * TOP_K)
        dest = assignment_rows(top_idx, rank, pad_start)
        inv = invert_rows(dest, dump_offsets(counts, block_expert, n_used, n))
        y_tok = fused_expert_ffn(h2p, inv, block_expert, n_used, next_expert, run_parity,
                                 exp_w1[l], exp_b1[l], exp_w2[l], exp_b2[l])
        x2d = weighted_sum(x1, gate, y_tok)
    return x2d.reshape(b, s, d)
```

```python
import functools

import jax
import jax.numpy as jnp
from jax import lax
from jax.experimental import pallas as pl
from jax.experimental.pallas import tpu as pltpu

F32 = jnp.float32
BF16 = jnp.bfloat16

NORM_EPS = 1e-6
LANES = 128
SUBLANES = 8
HEAD_DIM = 128
A_GROUPS = ((128, 1), (512, 4), (2048, 16))
A_HEADS = 4
A_WIDTH = A_HEADS * HEAD_DIM
ATT_BLOCK = 128
ATT_QB = 4
ROT_DIM = 32
ROPE_THETA = 500000.0
R_HEAD = 64
R_HEADS = 8
R_WIDTH = R_HEADS * R_HEAD
R_DECAY_LORA = 64
R_AAA_LORA = 64
R_GATE_LORA = 128
R_COLS = 3 * R_WIDTH + R_DECAY_LORA + R_AAA_LORA + R_GATE_LORA
R_GN_EPS = 64e-5
R_CHUNK = 64
R_CPI = 4
R_SEG = 256
C_HEADS = 4
C_WIDTH = C_HEADS * HEAD_DIM
N_BRANCH = 3
N_EXPERTS = 32
TOP_K = 4
SWIGLU_ALPHA = 1.702
SWIGLU_LIMIT = 7.0
MOE_ROWS = 256
NEG_BIG = -1e30

VMEM_LIMIT = 56 * 1024 * 1024


def _cparams(sem):
    return pltpu.CompilerParams(dimension_semantics=sem, vmem_limit_bytes=VMEM_LIMIT)


def _rmsnorm_kernel(x_ref, g_ref, o_ref):
    x = x_ref[...]
    ms = jnp.mean(x * x, axis=-1, keepdims=True)
    o_ref[...] = (x * lax.rsqrt(ms + NORM_EPS) * g_ref[...]).astype(o_ref.dtype)


def rmsnorm_rows(x2d, gain, tm=512):
    n, d = x2d.shape
    tm = min(tm, n)
    return pl.pallas_call(
        _rmsnorm_kernel,
        grid=(n // tm,),
        in_specs=[pl.BlockSpec((tm, d), lambda i: (i, 0)),
                  pl.BlockSpec((1, d), lambda i: (0, 0))],
        out_specs=pl.BlockSpec((tm, d), lambda i: (i, 0)),
        out_shape=jax.ShapeDtypeStruct((n, d), BF16),
        compiler_params=_cparams(("arbitrary",)),
        name="rmsnorm_rows",
    )(x2d, gain.reshape(1, d))


def _rope_table_kernel(pos_ref, freq_ref, cos_ref, sin_ref):
    ang = pos_ref[...].astype(F32) * freq_ref[...]
    lane = lax.broadcasted_iota(jnp.int32, ang.shape, 1)
    cos_ref[...] = jnp.cos(ang)
    s = jnp.sin(ang)
    half = ROT_DIM // 2
    sin_ref[...] = jnp.where(lane < half, -s, jnp.where(lane < ROT_DIM, s, 0.0))


def rope_tables(pos_col, tm=1024):
    n = pos_col.shape[0]
    half = ROT_DIM // 2
    inv_freq = ROPE_THETA ** (-jnp.arange(half, dtype=F32) / half)
    freq_row = jnp.concatenate([inv_freq, inv_freq, jnp.zeros((LANES - ROT_DIM,), F32)]).reshape(1, LANES)
    return pl.pallas_call(
        _rope_table_kernel,
        grid=(n // tm,),
        in_specs=[pl.BlockSpec((tm, 1), lambda i: (i, 0)),
                  pl.BlockSpec((1, LANES), lambda i: (0, 0))],
        out_specs=[pl.BlockSpec((tm, LANES), lambda i: (i, 0)),
                   pl.BlockSpec((tm, LANES), lambda i: (i, 0))],
        out_shape=[jax.ShapeDtypeStruct((n, LANES), F32)] * 2,
        compiler_params=_cparams(("arbitrary",)),
        name="rope_tables",
    )(pos_col, freq_row)


def _head_mean_sq(zh):
    avg = jnp.full((HEAD_DIM, HEAD_DIM), 1.0 / HEAD_DIM, BF16)
    return jnp.dot((zh * zh).astype(BF16), avg, preferred_element_type=F32)


def _cast_weight_once(w_ref, w_scr):
    @pl.when(pl.program_id(1) == 0)
    def _():
        w_scr[...] = w_ref[...].astype(w_scr.dtype)


def _proj_kernel(*refs, mode):
    h_ref, w_ref = refs[0], refs[1]
    o_ref, w_scr = refs[-2], refs[-1]
    _cast_weight_once(w_ref, w_scr)
    z = jnp.dot(h_ref[...], w_scr[...], preferred_element_type=F32)
    if mode == "plain":
        o_ref[...] = z.astype(o_ref.dtype)
    elif mode == "gate":
        o_ref[...] = (0.5 * jnp.tanh(0.5 * (z + refs[2][...])) + 0.5).astype(o_ref.dtype)
    else:
        gain = refs[2][...]
        for c in range(z.shape[1] // HEAD_DIM):
            zh = z[:, c * HEAD_DIM:(c + 1) * HEAD_DIM]
            zn = zh * lax.rsqrt(_head_mean_sq(zh) + NORM_EPS) * gain
            o_ref[:, c * HEAD_DIM:(c + 1) * HEAD_DIM] = zn.astype(o_ref.dtype)


def project(h, w, mode, extras=(), out_dtype=BF16, tm=1024, tn=512):
    n, k = h.shape
    m = w.shape[1]
    tm = min(tm, n)
    tn = min(tn, m)
    assert n % tm == 0 and m % tn == 0
    in_specs = [pl.BlockSpec((tm, k), lambda j, i: (i, 0)),
                pl.BlockSpec((k, tn), lambda j, i: (0, j))]
    if mode == "gate":
        in_specs.append(pl.BlockSpec((1, tn), lambda j, i: (0, j)))
    elif mode == "headnorm":
        in_specs.append(pl.BlockSpec((1, HEAD_DIM), lambda j, i: (0, 0)))
    return pl.pallas_call(
        functools.partial(_proj_kernel, mode=mode),
        grid=(m // tn, n // tm),
        in_specs=in_specs,
        out_specs=pl.BlockSpec((tm, tn), lambda j, i: (i, j)),
        out_shape=jax.ShapeDtypeStruct((n, m), out_dtype),
        scratch_shapes=[pltpu.VMEM((k, tn), BF16)],
        compiler_params=_cparams(("arbitrary", "arbitrary")),
        name="proj_" + mode,
    )(h, w, *extras)


def _qkv_proj_kernel(h_ref, w_ref, gain_ref, cos_ref, sin_ref, o_ref, w_scr, z_scr, *, dilation):
    j = pl.program_id(0)
    _cast_weight_once(w_ref, w_scr)
    z = jnp.dot(h_ref[...], w_scr[...], preferred_element_type=F32)
    half = ROT_DIM // 2

    pair = 2 * HEAD_DIM

    def emit(c0, slab):
        if dilation == 1:
            o_ref[:, c0 * HEAD_DIM:c0 * HEAD_DIM + pair] = slab.astype(o_ref.dtype)
        else:
            z_scr[c0] = slab[:, :HEAD_DIM]
            z_scr[c0 + 1] = slab[:, HEAD_DIM:]

    @pl.when(j < 2)
    def _():
        mi = lax.broadcasted_iota(jnp.int32, (pair, pair), 0)
        li = lax.broadcasted_iota(jnp.int32, (pair, pair), 1)
        same_head = (mi // HEAD_DIM) == (li // HEAD_DIM)
        lh, mh = li % HEAD_DIM, mi % HEAD_DIM
        avg = jnp.where(same_head, 1.0 / HEAD_DIM, 0.0).astype(BF16)
        perm = (same_head & (((lh < half) & (mh == lh + half))
                             | ((lh >= half) & (lh < ROT_DIM) & (mh == lh - half)))).astype(BF16)
        gcos = gain_ref[0, 0:1, :] * cos_ref[...]
        gsin = gain_ref[0, 1:2, :] * sin_ref[...]
        gcos = jnp.concatenate([gcos, gcos], axis=1)
        gsin = jnp.concatenate([gsin, gsin], axis=1)
        for c0 in range(0, A_HEADS, 2):
            zz = z[:, c0 * HEAD_DIM:c0 * HEAD_DIM + pair]
            ms = jnp.dot((zz * zz).astype(BF16), avg, preferred_element_type=F32)
            partner = jnp.dot(zz.astype(BF16), perm, preferred_element_type=F32)
            emit(c0, lax.rsqrt(ms + NORM_EPS) * (zz * gcos + partner * gsin))

    @pl.when(j == 2)
    def _():
        for c0 in range(0, A_HEADS, 2):
            emit(c0, z[:, c0 * HEAD_DIM:c0 * HEAD_DIM + pair])

    if dilation > 1:
        rows = z_scr.shape[1] // dilation
        for r in range(dilation):
            for c in range(A_HEADS):
                lo = r * A_WIDTH + c * HEAD_DIM
                o_ref[:, lo:lo + HEAD_DIM] = z_scr[c, pl.ds(r, rows, stride=dilation), :].astype(o_ref.dtype)


def rotary_gains(q_gain, k_gain):
    half = ROT_DIM // 2
    lane = jnp.arange(HEAD_DIM)
    partner = jnp.where(lane < half, lane + half, jnp.where(lane < ROT_DIM, lane - half, lane))
    return jnp.stack([jnp.stack([g, g[partner]]) for g in (q_gain, k_gain)])


def project_qkv(h, w_qkv, gains, cos_t, sin_t, dilation, tm=1024):
    n, k = h.shape
    d = dilation
    return pl.pallas_call(
        functools.partial(_qkv_proj_kernel, dilation=d),
        grid=(3, n // tm),
        in_specs=[pl.BlockSpec((tm, k), lambda j, i: (i, 0)),
                  pl.BlockSpec((k, A_WIDTH), lambda j, i: (0, j)),
                  pl.BlockSpec((1, 2, HEAD_DIM), lambda j, i: (jnp.minimum(j, 1), 0, 0)),
                  pl.BlockSpec((tm, LANES), lambda j, i: (i, 0)),
                  pl.BlockSpec((tm, LANES), lambda j, i: (i, 0))],
        out_specs=pl.BlockSpec((tm // d, d * A_WIDTH), lambda j, i: (i, j)),
        out_shape=jax.ShapeDtypeStruct((n // d, 3 * d * A_WIDTH), BF16),
        scratch_shapes=[pltpu.VMEM((k, A_WIDTH), BF16), pltpu.VMEM((A_HEADS, tm, HEAD_DIM), F32)],
        compiler_params=_cparams(("arbitrary", "arbitrary")),
        name=f"proj_qkv_d{d}",
    )(h, w_qkv, gains, cos_t, sin_t)


def _band_attn_kernel(*refs, qb):
    q_ref = refs[0]
    k_refs = refs[1:qb + 2]
    v_refs = refs[qb + 2:2 * qb + 3]
    o_ref, lse_ref = refs[-2:]
    step = pl.program_id(2)
    scale = HEAD_DIM ** -0.5
    nq = ATT_BLOCK
    qi = lax.broadcasted_iota(jnp.int32, (nq, 2 * nq), 0)
    ki = lax.broadcasted_iota(jnp.int32, (nq, 2 * nq), 1)
    rel = qi + nq - ki
    band = (rel >= 0) & (rel <= nq)
    for a in range(qb):
        blk = step * qb + a
        valid = band & ((blk * nq - nq + ki) >= 0)
        rows = slice(a * nq, (a + 1) * nq)
        lses = []
        for h in range(A_HEADS):
            sl = slice(h * HEAD_DIM, (h + 1) * HEAD_DIM)
            qh = q_ref[0, rows, sl]
            kh = jnp.concatenate([k_refs[a][0, :, sl], k_refs[a + 1][0, :, sl]], axis=0)
            vh = jnp.concatenate([v_refs[a][0, :, sl], v_refs[a + 1][0, :, sl]], axis=0)
            s = lax.dot_general(qh, kh, (((1,), (1,)), ((), ())), preferred_element_type=F32) * scale
            s = jnp.where(valid, s, NEG_BIG)
            m = jnp.max(s, axis=-1, keepdims=True)
            p = jnp.exp(s - m)
            l = jnp.sum(p, axis=-1, keepdims=True)
            o = jnp.dot(p.astype(BF16), vh, preferred_element_type=F32) / l
            o_ref[0, rows, sl] = o.astype(o_ref.dtype)
            lses.append(jnp.broadcast_to(m + jnp.log(l), (nq, LANES // A_HEADS)))
        lse_ref[0, rows, :] = jnp.concatenate(lses, axis=1)


def band_attention_group(qkv, b, g, dilation):
    d = dilation
    sub = qkv.shape[0] // b
    nblk = sub // ATT_BLOCK
    qb = min(ATT_QB, nblk)
    assert nblk % qb == 0
    view = qkv.reshape(b, sub, 3 * d * A_WIDTH)
    qrows = qb * ATT_BLOCK
    key_spec = lambda t, m: pl.BlockSpec(
        (1, ATT_BLOCK, A_WIDTH), lambda bi, r, j: (bi, jnp.maximum(j * qb - 1 + m, 0), t * d + r))
    o, lse = pl.pallas_call(
        functools.partial(_band_attn_kernel, qb=qb),
        grid=(b, d, nblk // qb),
        in_specs=[pl.BlockSpec((1, qrows, A_WIDTH), lambda bi, r, j: (bi, j, r))]
                 + [key_spec(1, m) for m in range(qb + 1)] + [key_spec(2, m) for m in range(qb + 1)],
        out_specs=[pl.BlockSpec((1, qrows, A_WIDTH), lambda bi, r, j: (bi, j, r)),
                   pl.BlockSpec((1, qrows, LANES), lambda bi, r, j: (bi, j, r))],
        out_shape=[jax.ShapeDtypeStruct((b, sub, d * A_WIDTH), BF16),
                   jax.ShapeDtypeStruct((b, sub, d * LANES), F32)],
        compiler_params=_cparams(("arbitrary", "arbitrary", "arbitrary")),
        name=f"band_attn_g{g}",
    )(*([view] * (2 * qb + 3)))
    return o.reshape(b * sub, d * A_WIDTH), lse.reshape(b * sub, d * LANES)


def _cross_attn_kernel(q_ref, k_ref, v_ref, o_ref):
    scale = HEAD_DIM ** -0.5
    for h in range(C_HEADS):
        sl = slice(h * HEAD_DIM, (h + 1) * HEAD_DIM)
        s = lax.dot_general(q_ref[0, :, sl], k_ref[0, :, sl], (((1,), (1,)), ((), ())),
                            preferred_element_type=F32) * scale
        m = jnp.max(s, axis=-1, keepdims=True)
        p = jnp.exp(s - m)
        l = jnp.sum(p, axis=-1, keepdims=True)
        o = jnp.dot(p.astype(BF16), v_ref[0, :, sl], preferred_element_type=F32) / l
        o_ref[0, :, sl] = o.astype(o_ref.dtype)


def cross_attention(qn, kn, v, tm=512):
    b, s, w = qn.shape
    m = kn.shape[1]
    return pl.pallas_call(
        _cross_attn_kernel,
        grid=(b, s // tm),
        in_specs=[pl.BlockSpec((1, tm, w), lambda bi, i: (bi, i, 0)),
                  pl.BlockSpec((1, m, w), lambda bi, i: (bi, 0, 0)),
                  pl.BlockSpec((1, m, w), lambda bi, i: (bi, 0, 0))],
        out_specs=pl.BlockSpec((1, tm, w), lambda bi, i: (bi, i, 0)),
        out_shape=jax.ShapeDtypeStruct((b, s, w), BF16),
        compiler_params=_cparams(("arbitrary", "arbitrary")),
        name="cross_attn",
    )(qn, kn, v)


def _head_sums(x, seg):
    w = seg.shape[0]
    x16 = x.astype(BF16)
    return jnp.concatenate(
        [jnp.dot(x16[:, j:j + w], seg, preferred_element_type=F32) for j in range(0, x.shape[1], w)], axis=1)


def _dot_t(a, b):
    return lax.dot_general(a, b, (((0,), (0,)), ((), ())), preferred_element_type=F32)


def _dot_nt(a, b):
    return lax.dot_general(a, b, (((1,), (1,)), ((), ())), preferred_element_type=F32)


def _rwkv_kernel(zr_ref, mu_ref, w0_ref, wwa_ref, a0_ref, g2_ref, kk_ref, ka_ref, rk_ref,
                 lnw_ref, lnb_ref, seg_ref, y_ref,
                 state_ref, carry_ref, ops_ref, yh_ref):
    t = pl.program_id(1)
    tt = zr_ref.shape[1]
    nch = tt // R_CHUNK
    c = R_CHUNK

    @pl.when(t == 0)
    def _():
        state_ref[...] = jnp.zeros_like(state_ref)
        carry_ref[...] = jnp.zeros_like(carry_ref)

    z = zr_ref[0]
    row = lax.broadcasted_iota(jnp.int32, z.shape, 0)
    prev = jnp.where(row == 0, carry_ref[...], pltpu.roll(z, 1, 0))
    carry_ref[...] = z[tt - 1:tt, :]
    xs = z + (prev - z) * mu_ref[...]

    w3 = 3 * R_WIDTH
    r = xs[:, 0:R_WIDTH]
    k = xs[:, R_WIDTH:2 * R_WIDTH]
    v = xs[:, 2 * R_WIDTH:w3]
    wa_lo = xs[:, w3:w3 + LANES]
    g_lo = xs[:, w3 + LANES:w3 + 2 * LANES]
    lane = lax.broadcasted_iota(jnp.int32, wa_lo.shape, 1)
    wa_in = jnp.where(lane < R_DECAY_LORA, jnp.tanh(wa_lo), wa_lo)
    wa = jnp.dot(wa_in.astype(BF16), wwa_ref[...], preferred_element_type=F32)
    u = -(w0_ref[...] + wa[:, :R_WIDTH])
    softplus = jnp.maximum(u, 0.0) + jnp.log(1.0 + jnp.exp(-jnp.abs(u)))
    w_raw = -softplus - 0.5
    ld = -jnp.exp(w_raw)
    a = jax.nn.sigmoid(a0_ref[...] + wa[:, R_WIDTH:])
    g = jnp.dot(jax.nn.sigmoid(g_lo).astype(BF16), g2_ref[...], preferred_element_type=F32)

    seg = seg_ref[...]
    kk = k * kk_ref[...]
    kk = kk * jnp.minimum(lax.rsqrt(_head_sums(kk * kk, seg)), 1e12)
    k2 = k * (1.0 + (a - 1.0) * ka_ref[...])
    bonus = _head_sums(r * k2 * rk_ref[...], seg) * v

    ri = lax.broadcasted_iota(jnp.int32, (c, c), 0)
    ci = lax.broadcasted_iota(jnp.int32, (c, c), 1)
    tri = (ci <= ri).astype(BF16)
    ld_hi = ld.astype(BF16)
    ld_lo = (ld - ld_hi.astype(F32)).astype(BF16)
    lcs = []
    for ch in range(nch):
        rs = slice(ch * c, (ch + 1) * c)
        lcs.append(jnp.dot(tri, ld_hi[rs], preferred_element_type=F32)
                   + jnp.dot(tri, ld_lo[rs], preferred_element_type=F32))
    lc = jnp.concatenate(lcs, axis=0)
    e_inc = jnp.exp(lc)
    e_exc = jnp.exp(lc - ld)
    e_inv = jnp.exp(-lc)
    a_t = -kk * e_exc
    r_t = r * e_inc
    b_t = kk * a * e_inv
    k_t = k2 * e_inv
    for h in range(R_HEADS):
        hs = slice(h * R_HEAD, (h + 1) * R_HEAD)
        ops_ref[0, h] = a_t[:, hs]
        ops_ref[1, h] = r_t[:, hs]
        ops_ref[2, h] = b_t[:, hs]
        ops_ref[3, h] = k_t[:, hs]
        ops_ref[4, h] = v[:, hs]
        ops_ref[5, h] = e_inc[:, hs]

    strict = ci < ri
    incl = ci <= ri
    eye = (ci == ri)

    ri2 = lax.broadcasted_iota(jnp.int32, (c, 2 * c), 0)
    ci2 = lax.broadcasted_iota(jnp.int32, (c, 2 * c), 1)
    incl2 = jnp.bitwise_and(ci2, c - 1) <= ri2
    eye_f = jnp.where(eye, 1.0, 0.0)
    heads = range(R_HEADS)
    dot = functools.partial(jnp.dot, preferred_element_type=F32)

    def chunk_body(ch, _):
        starts = [pl.multiple_of((ch * R_CPI + sub) * c, c) for sub in range(R_CPI)]
        rows = [pl.ds(r0, c) for r0 in starts]
        items = [(sub, h) for sub in range(R_CPI) for h in heads]
        idx = range(len(items))
        at = [ops_ref[0, h, rows[sub], :] for sub, h in items]
        rt = [ops_ref[1, h, rows[sub], :] for sub, h in items]
        bt = [ops_ref[2, h, rows[sub], :] for sub, h in items]
        kt = [ops_ref[3, h, rows[sub], :] for sub, h in items]
        pc = [ops_ref[5, h, pl.ds(starts[sub] + c - 1, 1), :] for sub, h in items]
        at16 = [x.astype(BF16) for x in at]
        rt16 = [x.astype(BF16) for x in rt]
        bt16 = [x.astype(BF16) for x in bt]
        kt16 = [x.astype(BF16) for x in kt]
        v16 = [ops_ref[4, h, rows[sub], :].astype(BF16) for sub, h in items]
        bk16 = [jnp.concatenate([bt16[i], kt16[i]], axis=0) for i in idx]
        nmat = [jnp.where(strict, _dot_nt(at16[i], bt16[i]), 0.0) for i in idx]
        a_ak = [jnp.where(strict, _dot_nt(at16[i], kt16[i]), 0.0).astype(BF16) for i in idx]
        a_rbk = [jnp.where(incl2, _dot_nt(rt16[i], bk16[i]), 0.0).astype(BF16) for i in idx]
        npow = nmat
        tinv = [eye_f + nmat[i] for i in idx]
        for _i in range(5):
            np16 = [x.astype(BF16) for x in npow]
            npow = [dot(np16[i], np16[i]) for i in idx]
            tinv = [tinv[i] + dot(tinv[i].astype(BF16), npow[i].astype(BF16)) for i in idx]
        akv = [dot(a_ak[i], v16[i]).astype(BF16) for i in idx]
        apw1 = [dot(tinv[i].astype(BF16), jnp.concatenate([at16[i], akv[i]], axis=1)).astype(BF16)
                for i in idx]
        zero = jnp.zeros((c, R_HEAD), BF16)
        rhs2 = [jnp.concatenate([apw1[i], jnp.concatenate([zero, v16[i]], axis=1)], axis=0)
                for i in idx]
        bkh = [jnp.concatenate([bt[i] * pc[i], kt[i] * pc[i]], axis=0).astype(BF16) for i in idx]
        gh = [_dot_t(bkh[i], rhs2[i]) for i in idx]
        qy = [dot(a_rbk[i], rhs2[i]) for i in idx]
        for i, (sub, h) in enumerate(items):
            gm = jnp.where(eye, jnp.broadcast_to(pc[i], (c, c)), 0.0) + gh[i][:, :R_HEAD]
            qp = rt[i] + qy[i][:, :R_HEAD]
            st = state_ref[h]
            res = dot(jnp.concatenate([qp, gm], axis=0).astype(BF16), st.astype(BF16))
            yh_ref[h, rows[sub], :] = res[:c] + qy[i][:, R_HEAD:]
            state_ref[h] = res[c:] + gh[i][:, R_HEAD:]
        return 0

    lax.fori_loop(0, nch // R_CPI, chunk_body, 0)

    y = jnp.concatenate([yh_ref[h] for h in range(R_HEADS)], axis=1)
    mean = _head_sums(y, seg) * (1.0 / R_HEAD)
    dlt = y - mean
    var = _head_sums(dlt * dlt, seg) * (1.0 / R_HEAD)
    yn = dlt * lax.rsqrt(var + R_GN_EPS) * lnw_ref[...] + lnb_ref[...]
    y_ref[0] = ((yn + bonus) * g).astype(y_ref.dtype)


def rwkv7_mix(zr, mu, w0, w2, a0, a2, g2, k_k, k_a, r_k, ln_w, ln_b, tt=256):
    b, s, cols = zr.shape
    row = lambda x: x.reshape(1, -1).astype(F32)
    wwa = jnp.zeros((LANES, 2 * R_WIDTH), F32)
    wwa = wwa.at[:R_DECAY_LORA, :R_WIDTH].set(w2).at[R_DECAY_LORA:, R_WIDTH:].set(a2).astype(BF16)
    hid = jnp.arange(R_SEG) // R_HEAD
    seg = (hid[:, None] == hid[None, :]).astype(BF16)
    full = lambda shape: pl.BlockSpec(shape, lambda bi, t: (0,) * len(shape))
    return pl.pallas_call(
        _rwkv_kernel,
        grid=(b, s // tt),
        in_specs=[pl.BlockSpec((1, tt, cols), lambda bi, t: (bi, t, 0)),
                  full((1, cols)), full((1, R_WIDTH)), full((LANES, 2 * R_WIDTH)), full((1, R_WIDTH)),
                  full((R_GATE_LORA, R_WIDTH)), full((1, R_WIDTH)), full((1, R_WIDTH)), full((1, R_WIDTH)),
                  full((1, R_WIDTH)), full((1, R_WIDTH)), full((R_SEG, R_SEG))],
        out_specs=pl.BlockSpec((1, tt, R_WIDTH), lambda bi, t: (bi, t, 0)),
        out_shape=jax.ShapeDtypeStruct((b, s, R_WIDTH), BF16),
        scratch_shapes=[pltpu.VMEM((R_HEADS, R_HEAD, R_HEAD), F32),
                        pltpu.VMEM((1, cols), F32),
                        pltpu.VMEM((6, R_HEADS, tt, R_HEAD), F32),
                        pltpu.VMEM((R_HEADS, tt, R_HEAD), F32)],
        compiler_params=_cparams(("arbitrary", "arbitrary")),
        name="rwkv7_mix",
    )(zr, row(mu), row(w0), wwa, row(a0), g2.astype(BF16), row(k_k), row(k_a), row(r_k),
      row(ln_w), row(ln_b), seg)


def _merge_kernel(x_ref, o0_ref, o1_ref, o2_ref, l0_ref, l1_ref, l2_ref, yb_ref, yc_ref, gt_ref,
                  wb_ref, wo_ref, gn_ref, rw_ref, rb_ref,
                  x1_ref, h2_ref, idx_ref, gate_ref, rank_ref, cnt_ref, base_ref, o_scr, l_scr):
    @pl.when(pl.program_id(0) == 0)
    def _():
        base_ref[...] = jnp.zeros_like(base_ref)

    tm_rows = x_ref.shape[0]
    for gi, (o_ref, l_ref) in enumerate(((o0_ref, l0_ref), (o1_ref, l1_ref), (o2_ref, l2_ref))):
        dil = A_GROUPS[gi][1]
        for r in range(dil):
            dst = pl.ds(r, tm_rows // dil, stride=dil) if dil > 1 else slice(None)
            for h in range(A_HEADS):
                lo = r * A_WIDTH + h * HEAD_DIM
                o_scr[gi, h, dst, :] = o_ref[:, lo:lo + HEAD_DIM].astype(F32)
            l_scr[gi, dst, :] = l_ref[:, r * LANES:(r + 1) * LANES]
    lses = [l_scr[gi] for gi in range(3)]
    lmax = jnp.maximum(jnp.maximum(lses[0], lses[1]), lses[2])
    es = [jnp.exp(l - lmax) for l in lses]
    inv = 1.0 / (es[0] + es[1] + es[2])
    qw = LANES // A_HEADS
    heads = []
    for h in range(A_HEADS):
        sl = slice(h * HEAD_DIM, (h + 1) * HEAD_DIM)
        acc = None
        for gi in range(3):
            alpha = (es[gi] * inv)[:, h * qw:h * qw + 1]
            term = alpha * o_scr[gi, h]
            acc = term if acc is None else acc + term
        heads.append(acc)
    ya = jnp.concatenate(heads, axis=1).astype(BF16)
    d = x_ref.shape[1]
    merged = None
    for n, yn in enumerate((ya, yb_ref[...], yc_ref[...])):
        proj = jnp.dot(yn, wb_ref[n], preferred_element_type=F32)
        term = gt_ref[:, n * d:(n + 1) * d].astype(F32) * proj
        merged = term if merged is None else merged + term
    x1 = x_ref[...] + jnp.dot(merged.astype(BF16), wo_ref[...], preferred_element_type=F32)
    x1_ref[...] = x1
    ms = jnp.mean(x1 * x1, axis=-1, keepdims=True)
    h2 = x1 * lax.rsqrt(ms + NORM_EPS) * gn_ref[...]
    h2_ref[...] = h2
    rw = rw_ref[...]
    h2_hi, rw_hi = h2.astype(BF16), rw.astype(BF16)
    h2_lo = (h2 - h2_hi.astype(F32)).astype(BF16)
    rw_lo = (rw - rw_hi.astype(F32)).astype(BF16)
    logits = (jnp.dot(h2_hi, rw_hi, preferred_element_type=F32) + jnp.dot(h2_hi, rw_lo, preferred_element_type=F32)
              + jnp.dot(h2_lo, rw_hi, preferred_element_type=F32)) + rb_ref[...]
    tm = logits.shape[0]
    lane = lax.broadcasted_iota(jnp.int32, logits.shape, 1)
    vals, idxs = [], []
    cur = logits
    for _k in range(TOP_K):
        m = jnp.max(cur, axis=-1, keepdims=True)
        ik = jnp.min(jnp.where(cur == m, lane, N_EXPERTS), axis=-1, keepdims=True)
        vals.append(m)
        idxs.append(ik)
        cur = jnp.where(lane == ik, -jnp.inf, cur)
    exps = [jnp.exp(vk - vals[0]) for vk in vals]
    tot = exps[0] + exps[1] + exps[2] + exps[3]
    onehots = [lane == ik for ik in idxs]
    hits = sum(jnp.where(oh, 1.0, 0.0) for oh in onehots)
    ri = lax.broadcasted_iota(jnp.int32, (tm, tm), 0)
    ci = lax.broadcasted_iota(jnp.int32, (tm, tm), 1)
    before = jnp.dot((ci < ri).astype(BF16), hits.astype(BF16), preferred_element_type=F32) + base_ref[...]
    for kk in range(TOP_K):
        idx_ref[:, kk:kk + 1] = idxs[kk]
        gate_ref[:, kk:kk + 1] = exps[kk] / tot
        rank_ref[:, kk:kk + 1] = jnp.sum(jnp.where(onehots[kk], before, 0.0), axis=-1,
                                         keepdims=True).astype(jnp.int32)
    base_ref[...] = base_ref[...] + jnp.sum(hits, axis=0, keepdims=True)
    cnt_ref[...] = base_ref[...].astype(jnp.int32)


def merge_and_route(x2d, outs, lses, yb, yc, gates, w_branch, w_out, norm_ffn, router_w, router_b, tm=512):
    n, d = x2d.shape
    rows = lambda w: pl.BlockSpec((tm, w), lambda i: (i, 0))
    packed = lambda w, dil: pl.BlockSpec((tm // dil, dil * w), lambda i: (i, 0))
    full = lambda shape: pl.BlockSpec(shape, lambda i: (0,) * len(shape))
    return pl.pallas_call(
        _merge_kernel,
        grid=(n // tm,),
        in_specs=[rows(d)] + [packed(A_WIDTH, dil) for _w, dil in A_GROUPS] + [packed(LANES, dil) for _w, dil in A_GROUPS]
                 + [rows(R_WIDTH), rows(C_WIDTH), rows(N_BRANCH * d),
                    full((N_BRANCH, A_WIDTH, d)), full((d, d)), full((1, d)), full((d, N_EXPERTS)), full((1, N_EXPERTS))],
        out_specs=[rows(d), rows(d), rows(TOP_K), rows(TOP_K), rows(TOP_K), full((1, N_EXPERTS))],
        out_shape=[jax.ShapeDtypeStruct((n, d), F32), jax.ShapeDtypeStruct((n, d), F32),
                   jax.ShapeDtypeStruct((n, TOP_K), jnp.int32), jax.ShapeDtypeStruct((n, TOP_K), F32),
                   jax.ShapeDtypeStruct((n, TOP_K), jnp.int32), jax.ShapeDtypeStruct((1, N_EXPERTS), jnp.int32)],
        scratch_shapes=[pltpu.VMEM((1, N_EXPERTS), F32), pltpu.VMEM((len(A_GROUPS), A_HEADS, tm, HEAD_DIM), F32),
                        pltpu.VMEM((len(A_GROUPS), tm, LANES), F32)],
        compiler_params=_cparams(("arbitrary",)),
        name="merge_route",
    )(x2d, *outs, *lses, yb, yc, gates, w_branch.astype(BF16), w_out.astype(BF16),
      norm_ffn.reshape(1, d), router_w, router_b.reshape(1, N_EXPERTS))


def block_layout(counts, n_assign):
    counts = counts.reshape(-1)
    padded = (counts + MOE_ROWS - 1) // MOE_ROWS * MOE_ROWS
    pad_end = jnp.cumsum(padded)
    pad_start = (pad_end - padded).astype(jnp.int32)
    n_blocks = -(-n_assign // MOE_ROWS) + N_EXPERTS
    blk_row = jnp.arange(n_blocks, dtype=jnp.int32) * MOE_ROWS
    owner = jnp.sum((blk_row[:, None] >= pad_end[None, :]).astype(jnp.int32), axis=1)
    block_expert = jnp.minimum(owner, N_EXPERTS - 1).astype(jnp.int32)
    unused = blk_row >= pad_end[-1]
    zero_flag = (unused | (blk_row + MOE_ROWS == pad_end[block_expert])).astype(jnp.int32)
    n_used = (pad_end[-1:] // MOE_ROWS).astype(jnp.int32)
    has_rows = counts > 0
    eid = jnp.arange(N_EXPERTS, dtype=jnp.int32)
    later = jnp.where(has_rows[None, :] & (eid[None, :] > eid[:, None]), eid[None, :], N_EXPERTS)
    next_expert = jnp.min(later, axis=1)
    next_expert = jnp.where(next_expert == N_EXPERTS, -1, next_expert).astype(jnp.int32)
    run_parity = ((jnp.cumsum(has_rows) - has_rows) % 2).astype(jnp.int32)
    return pad_start, block_expert, zero_flag, n_used, next_expert, run_parity


def _dest_kernel(ps_ref, idx_ref, rank_ref, dest_ref):
    idx = idx_ref[...]
    dest = rank_ref[...]
    for e in range(N_EXPERTS):
        dest = dest + jnp.where(idx == e, ps_ref[e], 0)
    dest_ref[...] = dest


def assignment_rows(top_idx, rank, pad_start):
    n = top_idx.shape[0]
    rows = n * TOP_K // LANES
    flat = lambda t: t.reshape(rows, LANES)
    spec = pl.BlockSpec((rows, LANES), lambda i, ps: (0, 0))
    out = pl.pallas_call(
        _dest_kernel,
        grid_spec=pltpu.PrefetchScalarGridSpec(num_scalar_prefetch=1, grid=(1,), in_specs=[spec, spec],
                                               out_specs=spec),
        out_shape=jax.ShapeDtypeStruct((rows, LANES), jnp.int32),
        compiler_params=_cparams(("arbitrary",)),
        name="assignment_rows",
    )(pad_start, flat(top_idx), flat(rank))
    return out.reshape(n, TOP_K)


def _scatter_kernel(zf_ref, dest_ref, h2_ref, xs_hbm, zeros_ref, sem, zsem):
    i = pl.program_id(0)
    tm = h2_ref.shape[0]
    nblk = zf_ref.shape[0]

    def zero_block(j):
        return pltpu.make_async_copy(zeros_ref, xs_hbm.at[pl.ds(j * MOE_ROWS, MOE_ROWS)], zsem)

    @pl.when(i == 0)
    def _():
        zeros_ref[...] = jnp.zeros_like(zeros_ref)

        def start(j, _):
            @pl.when(zf_ref[j] != 0)
            def _():
                zero_block(j).start()
            return 0

        def wait(j, _):
            @pl.when(zf_ref[j] != 0)
            def _():
                zero_block(j).wait()
            return 0

        lax.fori_loop(0, nblk, start, 0)
        lax.fori_loop(0, nblk, wait, 0)

    def body(t, _):
        for kk in range(TOP_K):
            row = dest_ref[0, 0, t * TOP_K + kk]
            pltpu.make_async_copy(h2_ref.at[pl.ds(t, 1)], xs_hbm.at[pl.ds(row, 1)], sem).start(priority=kk % 2)
        return 0

    lax.fori_loop(0, tm, body, 0)
    pltpu.make_async_copy(xs_hbm.at[pl.ds(0, tm * TOP_K)], xs_hbm.at[pl.ds(0, tm * TOP_K)], sem).wait()


def scatter_rows(h2, dest, zero_flag, n_rows, tm=1024):
    n, w = h2.shape
    dest3 = dest.reshape(n // tm, 1, tm * TOP_K)
    grid_spec = pltpu.PrefetchScalarGridSpec(
        num_scalar_prefetch=1,
        grid=(n // tm,),
        in_specs=[pl.BlockSpec((1, 1, tm * TOP_K), lambda i, zf: (i, 0, 0), memory_space=pltpu.SMEM),
                  pl.BlockSpec((tm, w), lambda i, zf: (i, 0))],
        out_specs=pl.BlockSpec(memory_space=pl.ANY),
        scratch_shapes=[pltpu.VMEM((MOE_ROWS, w), h2.dtype), pltpu.SemaphoreType.DMA(()),
                        pltpu.SemaphoreType.DMA(())],
    )
    return pl.pallas_call(
        _scatter_kernel,
        grid_spec=grid_spec,
        out_shape=jax.ShapeDtypeStruct((n_rows, w), h2.dtype),
        compiler_params=_cparams(("arbitrary",)),
        name="scatter_rows",
    )(zero_flag, dest3, h2)


def _expert_kernel(be_ref, nu_ref, nxt_ref, par_ref, xs_ref, w1_hbm, b1_ref, w2_hbm, b2_ref, y_ref,
                   w1_f32, w2_f32, w1_scr, w2_scr, wsem):
    i = pl.program_id(0)
    e = be_ref[i]
    prev = be_ref[jnp.maximum(i - 1, 0)]

    def fetch(expert, slot):
        return (pltpu.make_async_copy(w1_hbm.at[expert], w1_f32.at[slot], wsem.at[slot]),
                pltpu.make_async_copy(w2_hbm.at[expert], w2_f32.at[slot], wsem.at[slot]))

    @pl.when((i < nu_ref[0]) & ((i == 0) | (e != prev)))
    def _():
        slot = par_ref[e]

        @pl.when(i == 0)
        def _():
            for c in fetch(e, slot):
                c.start()

        for c in fetch(e, slot):
            c.wait()

        @pl.when(nxt_ref[e] >= 0)
        def _():
            for c in fetch(nxt_ref[e], 1 - slot):
                c.start()

        w1_scr[...] = w1_f32[slot].astype(BF16)
        w2_scr[...] = w2_f32[slot].astype(BF16)

    @pl.when(i < nu_ref[0])
    def _():
        xb = xs_ref[...].astype(BF16)
        dff = w2_scr.shape[0]
        u = jnp.dot(xb, w1_scr[...], preferred_element_type=F32) + b1_ref[0]
        glu = jnp.minimum(u[:, :dff], SWIGLU_LIMIT)
        lin = jnp.clip(u[:, dff:], -SWIGLU_LIMIT, SWIGLU_LIMIT)
        act = glu * jax.nn.sigmoid(SWIGLU_ALPHA * glu) * (lin + 1.0)
        y = jnp.dot(act.astype(BF16), w2_scr[...], preferred_element_type=F32) + b2_ref[0]
        y_ref[...] = y

    @pl.when(i >= nu_ref[0])
    def _():
        y_ref[...] = jnp.zeros_like(y_ref)


def expert_ffn(x_sorted, block_expert, n_used, next_expert, run_parity, w1, b1, w2, b2):
    n_rows, d = x_sorted.shape
    w = d
    nblk = n_rows // MOE_ROWS
    dff2 = w1.shape[2]
    used = lambda i, nu: jnp.minimum(i, nu[0] - 1)
    grid_spec = pltpu.PrefetchScalarGridSpec(
        num_scalar_prefetch=4,
        grid=(nblk,),
        in_specs=[
            pl.BlockSpec((MOE_ROWS, w), lambda i, be, nu, nx, pa: (used(i, nu), 0)),
            pl.BlockSpec(memory_space=pl.ANY),
            pl.BlockSpec((1, 1, dff2), lambda i, be, nu, nx, pa: (be[used(i, nu)], 0, 0)),
            pl.BlockSpec(memory_space=pl.ANY),
            pl.BlockSpec((1, 1, d), lambda i, be, nu, nx, pa: (be[used(i, nu)], 0, 0)),
        ],
        out_specs=pl.BlockSpec((MOE_ROWS, w), lambda i, be, nu, nx, pa: (i, 0)),
        scratch_shapes=[pltpu.VMEM((2, d, dff2), F32), pltpu.VMEM((2, dff2 // 2, d), F32),
                        pltpu.VMEM((d, dff2), BF16), pltpu.VMEM((dff2 // 2, d), BF16),
                        pltpu.SemaphoreType.DMA((2,))],
    )
    return pl.pallas_call(
        _expert_kernel,
        grid_spec=grid_spec,
        out_shape=jax.ShapeDtypeStruct((n_rows, w), F32),
        compiler_params=_cparams(("arbitrary",)),
        name="expert_ffn",
    )(block_expert, n_used, next_expert, run_parity, x_sorted, w1, b1.reshape(N_EXPERTS, 1, dff2), w2,
      b2.reshape(N_EXPERTS, 1, d))


def _gather_assigned_rows(y_hbm, dest_ref, dst_ref, sem, tm):
    def body(g, _):
        base = pl.multiple_of(g * SUBLANES, SUBLANES)
        for u in range(SUBLANES):
            for kk in range(TOP_K):
                row = dest_ref[(base + u) * TOP_K + kk]
                pltpu.make_async_copy(y_hbm.at[pl.ds(row, 1)], dst_ref.at[pl.ds(kk * tm + base + u, 1)],
                                      sem).start(priority=kk % 2)
        return 0
    lax.fori_loop(0, tm // SUBLANES, body, 0)


def _combine_kernel(d0_ref, dn_ref, x1_ref, gate_ref, y_hbm, o_ref, ybuf, sems):
    i = pl.program_id(0)
    nblk = pl.num_programs(0)
    slot = lax.rem(i, 2)
    tm = o_ref.shape[0]

    @pl.when(i == 0)
    def _():
        _gather_assigned_rows(y_hbm, d0_ref.at[0, 0], ybuf.at[0], sems.at[0], tm)

    @pl.when(i + 1 < nblk)
    def _():
        _gather_assigned_rows(y_hbm, dn_ref.at[0, 0], ybuf.at[1 - slot], sems.at[1 - slot], tm)

    pltpu.make_async_copy(y_hbm.at[pl.ds(0, TOP_K * tm)], ybuf.at[slot], sems.at[slot]).wait()
    acc = x1_ref[...]
    for kk in range(TOP_K):
        acc = acc + gate_ref[:, kk:kk + 1] * ybuf[slot, pl.ds(kk * tm, tm), :]
    o_ref[...] = acc


def moe_combine(x1, gate, y_sorted, dest, tm=256):
    n, d = x1.shape
    w = y_sorted.shape[1]
    nblk = n // tm
    dest3 = dest.reshape(nblk, 1, tm * TOP_K)
    return pl.pallas_call(
        _combine_kernel,
        grid=(nblk,),
        in_specs=[pl.BlockSpec((1, 1, TOP_K * tm), lambda i: (0, 0, 0), memory_space=pltpu.SMEM),
                  pl.BlockSpec((1, 1, TOP_K * tm), lambda i: (jnp.minimum(i + 1, nblk - 1), 0, 0),
                               memory_space=pltpu.SMEM),
                  pl.BlockSpec((tm, d), lambda i: (i, 0)),
                  pl.BlockSpec((tm, TOP_K), lambda i: (i, 0)),
                  pl.BlockSpec(memory_space=pl.ANY)],
        out_specs=pl.BlockSpec((tm, d), lambda i: (i, 0)),
        out_shape=jax.ShapeDtypeStruct((n, d), F32),
        scratch_shapes=[pltpu.VMEM((2, TOP_K * tm, w), y_sorted.dtype), pltpu.SemaphoreType.DMA((2,))],
        compiler_params=_cparams(("arbitrary",)),
        name="moe_combine",
    )(dest3, dest3, x1, gate, y_sorted)


def kernel(x, mem, positions, norm_mix, w_in, b_gate, a_q_gain, a_k_gain, r_mu, r_w0, r_w2, r_a0, r_a2,
           r_g2, r_k_k, r_k_a, r_r_k, r_ln_w, r_ln_b, mem_norm, w_mem_kv, c_q_gain, c_k_gain, w_branch,
           w_out, norm_ffn, router_w, router_b, exp_w1, exp_b1, exp_w2, exp_b2):
    b, s, d = x.shape
    n = b * s
    depth = norm_mix.shape[0]
    n_groups = len(A_GROUPS)
    qkv_cols = n_groups * A_WIDTH
    off_k, off_v, off_r = qkv_cols, 2 * qkv_cols, 3 * qkv_cols
    off_cq = off_r + R_COLS
    off_gate = off_cq + C_WIDTH

    x2d = x.reshape(n, d)
    cos_t, sin_t = rope_tables(positions.reshape(n, 1).astype(jnp.int32))
    for l in range(depth):
        w_l = w_in[l]
        h = rmsnorm_rows(x2d, norm_mix[l])
        qk_gains = rotary_gains(a_q_gain[l], a_k_gain[l])
        zr = project(h, w_l[:, off_r:off_cq], "plain", out_dtype=F32, tn=R_COLS // 2)
        cq = project(h, w_l[:, off_cq:off_gate], "headnorm", (c_q_gain[l].reshape(1, HEAD_DIM),))
        gates = project(h, w_l[:, off_gate:], "gate", (b_gate[l].reshape(1, -1),), tn=2 * A_WIDTH)

        shp = lambda t: t.reshape(b, s, -1)
        outs, lses = [], []
        for g, (window, dilation) in enumerate(A_GROUPS):
            assert window // dilation == ATT_BLOCK
            cols = slice(g * A_WIDTH, (g + 1) * A_WIDTH)
            w_g = jnp.concatenate([w_l[:, :off_k][:, cols], w_l[:, off_k:off_v][:, cols],
                                   w_l[:, off_v:off_r][:, cols]], axis=1)
            qkv = project_qkv(h, w_g, qk_gains, cos_t, sin_t, dilation)
            o, lse = band_attention_group(qkv, b, g, dilation)
            outs.append(o)
            lses.append(lse)

        yb = rwkv7_mix(shp(zr), r_mu[l], r_w0[l], r_w2[l], r_a0[l], r_a2[l], r_g2[l], r_k_k[l], r_k_a[l],
                       r_r_k[l].reshape(-1), r_ln_w[l], r_ln_b[l]).reshape(n, R_WIDTH)

        mlen = mem.shape[1]
        mem_n = rmsnorm_rows(mem.reshape(b * mlen, d), mem_norm[l])
        wkv = w_mem_kv[l]
        ck = project(mem_n, wkv[:, :C_WIDTH], "headnorm", (c_k_gain[l].reshape(1, HEAD_DIM),))
        cv = project(mem_n, wkv[:, C_WIDTH:], "plain")
        yc = cross_attention(shp(cq), ck.reshape(b, mlen, C_WIDTH), cv.reshape(b, mlen, C_WIDTH)).reshape(n, C_WIDTH)

        x1, h2p, top_idx, gate, rank, counts = merge_and_route(
            x2d, outs, lses, yb, yc, gates, w_branch[l], w_out[l], norm_ffn[l], router_w[l], router_b[l])
        pad_start, block_expert, zero_flag, n_used, next_expert, run_parity = block_layout(counts, n * TOP_K)
        dest = assignment_rows(top_idx, rank, pad_start)
        x_sorted = scatter_rows(h2p, dest, zero_flag, block_expert.shape[0] * MOE_ROWS)
        y_sorted = expert_ffn(x_sorted, block_expert, n_used, next_expert, run_parity,
                              exp_w1[l], exp_b1[l], exp_w2[l], exp_b2[l])
        x2d = moe_combine(x1, gate, y_sorted, dest)
    return x2d.reshape(b, s, d)
```

```python
import functools

import jax
import jax.numpy as jnp
from jax import lax
from jax.experimental import pallas as pl
from jax.experimental.pallas import tpu as pltpu

F32 = jnp.float32
BF16 = jnp.bfloat16

NORM_EPS = 1e-6
LANES = 128
SUBLANES = 8
HEAD_DIM = 128
A_GROUPS = ((128, 1), (512, 4), (2048, 16))
A_HEADS = 4
A_WIDTH = A_HEADS * HEAD_DIM
ATT_BLOCK = 128
ATT_QB = 4
ROT_DIM = 32
ROPE_THETA = 500000.0
R_HEAD = 64
R_HEADS = 8
R_WIDTH = R_HEADS * R_HEAD
R_DECAY_LORA = 64
R_AAA_LORA = 64
R_GATE_LORA = 128
R_COLS = 3 * R_WIDTH + R_DECAY_LORA + R_AAA_LORA + R_GATE_LORA
R_GN_EPS = 64e-5
R_CHUNK = 64
R_CPI = 4
R_SEG = 256
C_HEADS = 4
C_WIDTH = C_HEADS * HEAD_DIM
N_BRANCH = 3
N_EXPERTS = 32
TOP_K = 4
SWIGLU_ALPHA = 1.702
SWIGLU_LIMIT = 7.0
MOE_ROWS = 256
NEG_BIG = -1e30

VMEM_LIMIT = 56 * 1024 * 1024


def _cparams(sem):
    return pltpu.CompilerParams(dimension_semantics=sem, vmem_limit_bytes=VMEM_LIMIT)


def _rmsnorm_kernel(x_ref, g_ref, o_ref):
    x = x_ref[...]
    ms = jnp.mean(x * x, axis=-1, keepdims=True)
    o_ref[...] = (x * lax.rsqrt(ms + NORM_EPS) * g_ref[...]).astype(o_ref.dtype)


def rmsnorm_rows(x2d, gain, tm=512):
    n, d = x2d.shape
    tm = min(tm, n)
    return pl.pallas_call(
        _rmsnorm_kernel,
        grid=(n // tm,),
        in_specs=[pl.BlockSpec((tm, d), lambda i: (i, 0)),
                  pl.BlockSpec((1, d), lambda i: (0, 0))],
        out_specs=pl.BlockSpec((tm, d), lambda i: (i, 0)),
        out_shape=jax.ShapeDtypeStruct((n, d), BF16),
        compiler_params=_cparams(("arbitrary",)),
        name="rmsnorm_rows",
    )(x2d, gain.reshape(1, d))


def _rope_table_kernel(pos_ref, freq_ref, cos_ref, sin_ref):
    ang = pos_ref[...].astype(F32) * freq_ref[...]
    lane = lax.broadcasted_iota(jnp.int32, ang.shape, 1)
    cos_ref[...] = jnp.cos(ang)
    s = jnp.sin(ang)
    half = ROT_DIM // 2
    sin_ref[...] = jnp.where(lane < half, -s, jnp.where(lane < ROT_DIM, s, 0.0))


def rope_tables(pos_col, tm=1024):
    n = pos_col.shape[0]
    half = ROT_DIM // 2
    inv_freq = ROPE_THETA ** (-jnp.arange(half, dtype=F32) / half)
    freq_row = jnp.concatenate([inv_freq, inv_freq, jnp.zeros((LANES - ROT_DIM,), F32)]).reshape(1, LANES)
    return pl.pallas_call(
        _rope_table_kernel,
        grid=(n // tm,),
        in_specs=[pl.BlockSpec((tm, 1), lambda i: (i, 0)),
                  pl.BlockSpec((1, LANES), lambda i: (0, 0))],
        out_specs=[pl.BlockSpec((tm, LANES), lambda i: (i, 0)),
                   pl.BlockSpec((tm, LANES), lambda i: (i, 0))],
        out_shape=[jax.ShapeDtypeStruct((n, LANES), F32)] * 2,
        compiler_params=_cparams(("arbitrary",)),
        name="rope_tables",
    )(pos_col, freq_row)


def _head_mean_sq(zh):
    avg = jnp.full((HEAD_DIM, HEAD_DIM), 1.0 / HEAD_DIM, BF16)
    return jnp.dot((zh * zh).astype(BF16), avg, preferred_element_type=F32)


def _cast_weight_once(w_ref, w_scr):
    @pl.when(pl.program_id(1) == 0)
    def _():
        w_scr[...] = w_ref[...].astype(w_scr.dtype)


def _proj_kernel(*refs, mode):
    h_ref, w_ref = refs[0], refs[1]
    o_ref, w_scr = refs[-2], refs[-1]
    _cast_weight_once(w_ref, w_scr)
    z = jnp.dot(h_ref[...], w_scr[...], preferred_element_type=F32)
    if mode == "plain":
        o_ref[...] = z.astype(o_ref.dtype)
    elif mode == "gate":
        o_ref[...] = (0.5 * jnp.tanh(0.5 * (z + refs[2][...])) + 0.5).astype(o_ref.dtype)
    else:
        gain = refs[2][...]
        for c in range(z.shape[1] // HEAD_DIM):
            zh = z[:, c * HEAD_DIM:(c + 1) * HEAD_DIM]
            zn = zh * lax.rsqrt(_head_mean_sq(zh) + NORM_EPS) * gain
            o_ref[:, c * HEAD_DIM:(c + 1) * HEAD_DIM] = zn.astype(o_ref.dtype)


def project(h, w, mode, extras=(), out_dtype=BF16, tm=2048, tn=512):
    n, k = h.shape
    m = w.shape[1]
    tm = min(tm, n)
    tn = min(tn, m)
    assert n % tm == 0 and m % tn == 0
    in_specs = [pl.BlockSpec((tm, k), lambda j, i: (i, 0)),
                pl.BlockSpec((k, tn), lambda j, i: (0, j))]
    if mode == "gate":
        in_specs.append(pl.BlockSpec((1, tn), lambda j, i: (0, j)))
    elif mode == "headnorm":
        in_specs.append(pl.BlockSpec((1, HEAD_DIM), lambda j, i: (0, 0)))
    return pl.pallas_call(
        functools.partial(_proj_kernel, mode=mode),
        grid=(m // tn, n // tm),
        in_specs=in_specs,
        out_specs=pl.BlockSpec((tm, tn), lambda j, i: (i, j)),
        out_shape=jax.ShapeDtypeStruct((n, m), out_dtype),
        scratch_shapes=[pltpu.VMEM((k, tn), BF16)],
        compiler_params=_cparams(("arbitrary", "arbitrary")),
        name="proj_" + mode,
    )(h, w, *extras)


def _qkv_proj_kernel(h_ref, w_ref, gain_ref, cos_ref, sin_ref, o_ref, w_scr, z_scr, *, dilation):
    j = pl.program_id(0)
    _cast_weight_once(w_ref, w_scr)
    z = jnp.dot(h_ref[...], w_scr[...], preferred_element_type=F32)
    half = ROT_DIM // 2

    pair = 2 * HEAD_DIM

    def emit(c0, slab):
        if dilation == 1:
            o_ref[:, c0 * HEAD_DIM:c0 * HEAD_DIM + pair] = slab.astype(o_ref.dtype)
        else:
            z_scr[c0] = slab[:, :HEAD_DIM]
            z_scr[c0 + 1] = slab[:, HEAD_DIM:]

    @pl.when(j < 2)
    def _():
        mi = lax.broadcasted_iota(jnp.int32, (pair, pair), 0)
        li = lax.broadcasted_iota(jnp.int32, (pair, pair), 1)
        same_head = (mi // HEAD_DIM) == (li // HEAD_DIM)
        lh, mh = li % HEAD_DIM, mi % HEAD_DIM
        avg = jnp.where(same_head, 1.0 / HEAD_DIM, 0.0).astype(BF16)
        perm = (same_head & (((lh < half) & (mh == lh + half))
                             | ((lh >= half) & (lh < ROT_DIM) & (mh == lh - half)))).astype(BF16)
        gcos = gain_ref[0, 0:1, :] * cos_ref[...]
        gsin = gain_ref[0, 1:2, :] * sin_ref[...]
        gcos = jnp.concatenate([gcos, gcos], axis=1)
        gsin = jnp.concatenate([gsin, gsin], axis=1)
        for c0 in range(0, A_HEADS, 2):
            zz = z[:, c0 * HEAD_DIM:c0 * HEAD_DIM + pair]
            ms = jnp.dot((zz * zz).astype(BF16), avg, preferred_element_type=F32)
            partner = jnp.dot(zz.astype(BF16), perm, preferred_element_type=F32)
            emit(c0, lax.rsqrt(ms + NORM_EPS) * (zz * gcos + partner * gsin))

    @pl.when(j == 2)
    def _():
        for c0 in range(0, A_HEADS, 2):
            emit(c0, z[:, c0 * HEAD_DIM:c0 * HEAD_DIM + pair])

    if dilation > 1:
        rows = z_scr.shape[1] // dilation
        for r in range(dilation):
            for c in range(A_HEADS):
                lo = r * A_WIDTH + c * HEAD_DIM
                o_ref[:, lo:lo + HEAD_DIM] = z_scr[c, pl.ds(r, rows, stride=dilation), :].astype(o_ref.dtype)


def rotary_gains(q_gain, k_gain):
    half = ROT_DIM // 2
    lane = jnp.arange(HEAD_DIM)
    partner = jnp.where(lane < half, lane + half, jnp.where(lane < ROT_DIM, lane - half, lane))
    return jnp.stack([jnp.stack([g, g[partner]]) for g in (q_gain, k_gain)])


def project_qkv(h, w_qkv, gains, cos_t, sin_t, dilation, tm=2048):
    n, k = h.shape
    d = dilation
    return pl.pallas_call(
        functools.partial(_qkv_proj_kernel, dilation=d),
        grid=(3, n // tm),
        in_specs=[pl.BlockSpec((tm, k), lambda j, i: (i, 0)),
                  pl.BlockSpec((k, A_WIDTH), lambda j, i: (0, j)),
                  pl.BlockSpec((1, 2, HEAD_DIM), lambda j, i: (jnp.minimum(j, 1), 0, 0)),
                  pl.BlockSpec((tm, LANES), lambda j, i: (i, 0)),
                  pl.BlockSpec((tm, LANES), lambda j, i: (i, 0))],
        out_specs=pl.BlockSpec((tm // d, d * A_WIDTH), lambda j, i: (i, j)),
        out_shape=jax.ShapeDtypeStruct((n // d, 3 * d * A_WIDTH), BF16),
        scratch_shapes=[pltpu.VMEM((k, A_WIDTH), BF16), pltpu.VMEM((A_HEADS, tm, HEAD_DIM), F32)],
        compiler_params=_cparams(("arbitrary", "arbitrary")),
        name=f"proj_qkv_d{d}",
    )(h, w_qkv, gains, cos_t, sin_t)


def _band_attn_kernel(*refs, qb):
    q_ref = refs[0]
    k_refs = refs[1:qb + 2]
    v_refs = refs[qb + 2:2 * qb + 3]
    o_ref, lse_ref = refs[-2:]
    step = pl.program_id(2)
    scale = HEAD_DIM ** -0.5
    nq = ATT_BLOCK
    qi = lax.broadcasted_iota(jnp.int32, (nq, 2 * nq), 0)
    ki = lax.broadcasted_iota(jnp.int32, (nq, 2 * nq), 1)
    rel = qi + nq - ki
    band = (rel >= 0) & (rel <= nq)
    for a in range(qb):
        blk = step * qb + a
        valid = band & ((blk * nq - nq + ki) >= 0)
        rows = slice(a * nq, (a + 1) * nq)
        lses = []
        for h in range(A_HEADS):
            sl = slice(h * HEAD_DIM, (h + 1) * HEAD_DIM)
            qh = q_ref[0, rows, sl]
            kh = jnp.concatenate([k_refs[a][0, :, sl], k_refs[a + 1][0, :, sl]], axis=0)
            vh = jnp.concatenate([v_refs[a][0, :, sl], v_refs[a + 1][0, :, sl]], axis=0)
            s = lax.dot_general(qh, kh, (((1,), (1,)), ((), ())), preferred_element_type=F32) * scale
            s = jnp.where(valid, s, NEG_BIG)
            m = jnp.max(s, axis=-1, keepdims=True)
            p = jnp.exp(s - m)
            l = jnp.sum(p, axis=-1, keepdims=True)
            o = jnp.dot(p.astype(BF16), vh, preferred_element_type=F32) / l
            o_ref[0, rows, sl] = o.astype(o_ref.dtype)
            lses.append(jnp.broadcast_to(m + jnp.log(l), (nq, LANES // A_HEADS)))
        lse_ref[0, rows, :] = jnp.concatenate(lses, axis=1)


def band_attention_group(qkv, b, g, dilation):
    d = dilation
    sub = qkv.shape[0] // b
    nblk = sub // ATT_BLOCK
    qb = min(ATT_QB, nblk)
    assert nblk % qb == 0
    view = qkv.reshape(b, sub, 3 * d * A_WIDTH)
    qrows = qb * ATT_BLOCK
    key_spec = lambda t, m: pl.BlockSpec(
        (1, ATT_BLOCK, A_WIDTH), lambda bi, r, j: (bi, jnp.maximum(j * qb - 1 + m, 0), t * d + r))
    o, lse = pl.pallas_call(
        functools.partial(_band_attn_kernel, qb=qb),
        grid=(b, d, nblk // qb),
        in_specs=[pl.BlockSpec((1, qrows, A_WIDTH), lambda bi, r, j: (bi, j, r))]
                 + [key_spec(1, m) for m in range(qb + 1)] + [key_spec(2, m) for m in range(qb + 1)],
        out_specs=[pl.BlockSpec((1, qrows, A_WIDTH), lambda bi, r, j: (bi, j, r)),
                   pl.BlockSpec((1, qrows, LANES), lambda bi, r, j: (bi, j, r))],
        out_shape=[jax.ShapeDtypeStruct((b, sub, d * A_WIDTH), BF16),
                   jax.ShapeDtypeStruct((b, sub, d * LANES), F32)],
        compiler_params=_cparams(("arbitrary", "arbitrary", "arbitrary")),
        name=f"band_attn_g{g}",
    )(*([view] * (2 * qb + 3)))
    return o.reshape(b * sub, d * A_WIDTH), lse.reshape(b * sub, d * LANES)


def _cross_attn_kernel(q_ref, k_ref, v_ref, o_ref):
    scale = HEAD_DIM ** -0.5
    for h in range(C_HEADS):
        sl = slice(h * HEAD_DIM, (h + 1) * HEAD_DIM)
        s = lax.dot_general(q_ref[0, :, sl], k_ref[0, :, sl], (((1,), (1,)), ((), ())),
                            preferred_element_type=F32) * scale
        m = jnp.max(s, axis=-1, keepdims=True)
        p = jnp.exp(s - m)
        l = jnp.sum(p, axis=-1, keepdims=True)
        o = jnp.dot(p.astype(BF16), v_ref[0, :, sl], preferred_element_type=F32) / l
        o_ref[0, :, sl] = o.astype(o_ref.dtype)


def cross_attention(qn, kn, v, tm=512):
    b, s, w = qn.shape
    m = kn.shape[1]
    return pl.pallas_call(
        _cross_attn_kernel,
        grid=(b, s // tm),
        in_specs=[pl.BlockSpec((1, tm, w), lambda bi, i: (bi, i, 0)),
                  pl.BlockSpec((1, m, w), lambda bi, i: (bi, 0, 0)),
                  pl.BlockSpec((1, m, w), lambda bi, i: (bi, 0, 0))],
        out_specs=pl.BlockSpec((1, tm, w), lambda bi, i: (bi, i, 0)),
        out_shape=jax.ShapeDtypeStruct((b, s, w), BF16),
        compiler_params=_cparams(("arbitrary", "arbitrary")),
        name="cross_attn",
    )(qn, kn, v)


def _head_sums(x, seg):
    w = seg.shape[0]
    x16 = x.astype(BF16)
    return jnp.concatenate(
        [jnp.dot(x16[:, j:j + w], seg, preferred_element_type=F32) for j in range(0, x.shape[1], w)], axis=1)


def _dot_t(a, b):
    return lax.dot_general(a, b, (((0,), (0,)), ((), ())), preferred_element_type=F32)


def _dot_nt(a, b):
    return lax.dot_general(a, b, (((1,), (1,)), ((), ())), preferred_element_type=F32)


def _rwkv_kernel(zr_ref, mu_ref, w0_ref, wwa_ref, a0_ref, g2_ref, kk_ref, ka_ref, rk_ref,
                 lnw_ref, lnb_ref, seg_ref, y_ref,
                 state_ref, carry_ref, ops_ref, yh_ref):
    t = pl.program_id(1)
    tt = zr_ref.shape[1]
    nch = tt // R_CHUNK
    c = R_CHUNK

    @pl.when(t == 0)
    def _():
        state_ref[...] = jnp.zeros_like(state_ref)
        carry_ref[...] = jnp.zeros_like(carry_ref)

    z = zr_ref[0]
    row = lax.broadcasted_iota(jnp.int32, z.shape, 0)
    prev = jnp.where(row == 0, carry_ref[...], pltpu.roll(z, 1, 0))
    carry_ref[...] = z[tt - 1:tt, :]
    xs = z + (prev - z) * mu_ref[...]

    w3 = 3 * R_WIDTH
    r = xs[:, 0:R_WIDTH]
    k = xs[:, R_WIDTH:2 * R_WIDTH]
    v = xs[:, 2 * R_WIDTH:w3]
    wa_lo = xs[:, w3:w3 + LANES]
    g_lo = xs[:, w3 + LANES:w3 + 2 * LANES]
    lane = lax.broadcasted_iota(jnp.int32, wa_lo.shape, 1)
    wa_in = jnp.where(lane < R_DECAY_LORA, jnp.tanh(wa_lo), wa_lo)
    wa = jnp.dot(wa_in.astype(BF16), wwa_ref[...], preferred_element_type=F32)
    u = -(w0_ref[...] + wa[:, :R_WIDTH])
    softplus = jnp.maximum(u, 0.0) + jnp.log(1.0 + jnp.exp(-jnp.abs(u)))
    w_raw = -softplus - 0.5
    ld = -jnp.exp(w_raw)
    a = jax.nn.sigmoid(a0_ref[...] + wa[:, R_WIDTH:])
    g = jnp.dot(jax.nn.sigmoid(g_lo).astype(BF16), g2_ref[...], preferred_element_type=F32)

    seg = seg_ref[...]
    kk = k * kk_ref[...]
    kk = kk * jnp.minimum(lax.rsqrt(_head_sums(kk * kk, seg)), 1e12)
    k2 = k * (1.0 + (a - 1.0) * ka_ref[...])
    bonus = _head_sums(r * k2 * rk_ref[...], seg) * v

    ri = lax.broadcasted_iota(jnp.int32, (c, c), 0)
    ci = lax.broadcasted_iota(jnp.int32, (c, c), 1)
    tri = (ci <= ri).astype(BF16)
    ld_hi = ld.astype(BF16)
    ld_lo = (ld - ld_hi.astype(F32)).astype(BF16)
    lcs = []
    for ch in range(nch):
        rs = slice(ch * c, (ch + 1) * c)
        lcs.append(jnp.dot(tri, ld_hi[rs], preferred_element_type=F32)
                   + jnp.dot(tri, ld_lo[rs], preferred_element_type=F32))
    lc = jnp.concatenate(lcs, axis=0)
    e_inc = jnp.exp(lc)
    e_exc = jnp.exp(lc - ld)
    e_inv = jnp.exp(-lc)
    a_t = -kk * e_exc
    r_t = r * e_inc
    b_t = kk * a * e_inv
    k_t = k2 * e_inv
    for h in range(R_HEADS):
        hs = slice(h * R_HEAD, (h + 1) * R_HEAD)
        ops_ref[0, h] = a_t[:, hs]
        ops_ref[1, h] = r_t[:, hs]
        ops_ref[2, h] = b_t[:, hs]
        ops_ref[3, h] = k_t[:, hs]
        ops_ref[4, h] = v[:, hs]
        ops_ref[5, h] = e_inc[:, hs]

    strict = ci < ri
    incl = ci <= ri
    eye = (ci == ri)

    ri2 = lax.broadcasted_iota(jnp.int32, (c, 2 * c), 0)
    ci2 = lax.broadcasted_iota(jnp.int32, (c, 2 * c), 1)
    incl2 = jnp.bitwise_and(ci2, c - 1) <= ri2
    eye_f = jnp.where(eye, 1.0, 0.0)
    heads = range(R_HEADS)
    dot = functools.partial(jnp.dot, preferred_element_type=F32)

    def chunk_body(ch, _):
        starts = [pl.multiple_of((ch * R_CPI + sub) * c, c) for sub in range(R_CPI)]
        rows = [pl.ds(r0, c) for r0 in starts]
        items = [(sub, h) for sub in range(R_CPI) for h in heads]
        idx = range(len(items))
        at = [ops_ref[0, h, rows[sub], :] for sub, h in items]
        rt = [ops_ref[1, h, rows[sub], :] for sub, h in items]
        bt = [ops_ref[2, h, rows[sub], :] for sub, h in items]
        kt = [ops_ref[3, h, rows[sub], :] for sub, h in items]
        pc = [ops_ref[5, h, pl.ds(starts[sub] + c - 1, 1), :] for sub, h in items]
        at16 = [x.astype(BF16) for x in at]
        rt16 = [x.astype(BF16) for x in rt]
        bt16 = [x.astype(BF16) for x in bt]
        kt16 = [x.astype(BF16) for x in kt]
        v16 = [ops_ref[4, h, rows[sub], :].astype(BF16) for sub, h in items]
        bk16 = [jnp.concatenate([bt16[i], kt16[i]], axis=0) for i in idx]
        nmat = [jnp.where(strict, _dot_nt(at16[i], bt16[i]), 0.0) for i in idx]
        a_ak = [jnp.where(strict, _dot_nt(at16[i], kt16[i]), 0.0).astype(BF16) for i in idx]
        a_rbk = [jnp.where(incl2, _dot_nt(rt16[i], bk16[i]), 0.0).astype(BF16) for i in idx]
        npow = nmat
        tinv = [eye_f + nmat[i] for i in idx]
        for _i in range(5):
            np16 = [x.astype(BF16) for x in npow]
            npow = [dot(np16[i], np16[i]) for i in idx]
            tinv = [tinv[i] + dot(tinv[i].astype(BF16), npow[i].astype(BF16)) for i in idx]
        akv = [dot(a_ak[i], v16[i]).astype(BF16) for i in idx]
        apw1 = [dot(tinv[i].astype(BF16), jnp.concatenate([at16[i], akv[i]], axis=1)).astype(BF16)
                for i in idx]
        zero = jnp.zeros((c, R_HEAD), BF16)
        rhs2 = [jnp.concatenate([apw1[i], jnp.concatenate([zero, v16[i]], axis=1)], axis=0)
                for i in idx]
        bkh = [jnp.concatenate([bt[i] * pc[i], kt[i] * pc[i]], axis=0).astype(BF16) for i in idx]
        gh = [_dot_t(bkh[i], rhs2[i]) for i in idx]
        qy = [dot(a_rbk[i], rhs2[i]) for i in idx]
        for i, (sub, h) in enumerate(items):
            gm = jnp.where(eye, jnp.broadcast_to(pc[i], (c, c)), 0.0) + gh[i][:, :R_HEAD]
            qp = rt[i] + qy[i][:, :R_HEAD]
            st = state_ref[h]
            res = dot(jnp.concatenate([qp, gm], axis=0).astype(BF16), st.astype(BF16))
            yh_ref[h, rows[sub], :] = res[:c] + qy[i][:, R_HEAD:]
            state_ref[h] = res[c:] + gh[i][:, R_HEAD:]
        return 0

    lax.fori_loop(0, nch // R_CPI, chunk_body, 0)

    y = jnp.concatenate([yh_ref[h] for h in range(R_HEADS)], axis=1)
    mean = _head_sums(y, seg) * (1.0 / R_HEAD)
    dlt = y - mean
    var = _head_sums(dlt * dlt, seg) * (1.0 / R_HEAD)
    yn = dlt * lax.rsqrt(var + R_GN_EPS) * lnw_ref[...] + lnb_ref[...]
    y_ref[0] = ((yn + bonus) * g).astype(y_ref.dtype)


def rwkv7_mix(zr, mu, w0, w2, a0, a2, g2, k_k, k_a, r_k, ln_w, ln_b, tt=256):
    b, s, cols = zr.shape
    row = lambda x: x.reshape(1, -1).astype(F32)
    wwa = jnp.zeros((LANES, 2 * R_WIDTH), F32)
    wwa = wwa.at[:R_DECAY_LORA, :R_WIDTH].set(w2).at[R_DECAY_LORA:, R_WIDTH:].set(a2).astype(BF16)
    hid = jnp.arange(R_SEG) // R_HEAD
    seg = (hid[:, None] == hid[None, :]).astype(BF16)
    full = lambda shape: pl.BlockSpec(shape, lambda bi, t: (0,) * len(shape))
    return pl.pallas_call(
        _rwkv_kernel,
        grid=(b, s // tt),
        in_specs=[pl.BlockSpec((1, tt, cols), lambda bi, t: (bi, t, 0)),
                  full((1, cols)), full((1, R_WIDTH)), full((LANES, 2 * R_WIDTH)), full((1, R_WIDTH)),
                  full((R_GATE_LORA, R_WIDTH)), full((1, R_WIDTH)), full((1, R_WIDTH)), full((1, R_WIDTH)),
                  full((1, R_WIDTH)), full((1, R_WIDTH)), full((R_SEG, R_SEG))],
        out_specs=pl.BlockSpec((1, tt, R_WIDTH), lambda bi, t: (bi, t, 0)),
        out_shape=jax.ShapeDtypeStruct((b, s, R_WIDTH), BF16),
        scratch_shapes=[pltpu.VMEM((R_HEADS, R_HEAD, R_HEAD), F32),
                        pltpu.VMEM((1, cols), F32),
                        pltpu.VMEM((6, R_HEADS, tt, R_HEAD), F32),
                        pltpu.VMEM((R_HEADS, tt, R_HEAD), F32)],
        compiler_params=_cparams(("arbitrary", "arbitrary")),
        name="rwkv7_mix",
    )(zr, row(mu), row(w0), wwa, row(a0), g2.astype(BF16), row(k_k), row(k_a), row(r_k),
      row(ln_w), row(ln_b), seg)


def _merge_kernel(x_ref, o0_ref, o1_ref, o2_ref, l0_ref, l1_ref, l2_ref, yb_ref, yc_ref, gt_ref,
                  wb_ref, wo_ref, gn_ref, rw_ref, rb_ref,
                  x1_ref, h2_ref, idx_ref, gate_ref, rank_ref, cnt_ref, base_ref, o_scr, l_scr):
    @pl.when(pl.program_id(0) == 0)
    def _():
        base_ref[...] = jnp.zeros_like(base_ref)

    tm_rows = x_ref.shape[0]
    for gi, (o_ref, l_ref) in enumerate(((o0_ref, l0_ref), (o1_ref, l1_ref), (o2_ref, l2_ref))):
        dil = A_GROUPS[gi][1]
        for r in range(dil):
            dst = pl.ds(r, tm_rows // dil, stride=dil) if dil > 1 else slice(None)
            for h in range(A_HEADS):
                lo = r * A_WIDTH + h * HEAD_DIM
                o_scr[gi, h, dst, :] = o_ref[:, lo:lo + HEAD_DIM].astype(F32)
            l_scr[gi, dst, :] = l_ref[:, r * LANES:(r + 1) * LANES]
    lses = [l_scr[gi] for gi in range(3)]
    lmax = jnp.maximum(jnp.maximum(lses[0], lses[1]), lses[2])
    es = [jnp.exp(l - lmax) for l in lses]
    inv = 1.0 / (es[0] + es[1] + es[2])
    qw = LANES // A_HEADS
    heads = []
    for h in range(A_HEADS):
        sl = slice(h * HEAD_DIM, (h + 1) * HEAD_DIM)
        acc = None
        for gi in range(3):
            alpha = (es[gi] * inv)[:, h * qw:h * qw + 1]
            term = alpha * o_scr[gi, h]
            acc = term if acc is None else acc + term
        heads.append(acc)
    ya = jnp.concatenate(heads, axis=1).astype(BF16)
    d = x_ref.shape[1]
    merged = None
    for n, yn in enumerate((ya, yb_ref[...], yc_ref[...])):
        proj = jnp.dot(yn, wb_ref[n], preferred_element_type=F32)
        term = gt_ref[:, n * d:(n + 1) * d].astype(F32) * proj
        merged = term if merged is None else merged + term
    x1 = x_ref[...] + jnp.dot(merged.astype(BF16), wo_ref[...], preferred_element_type=F32)
    x1_ref[...] = x1
    ms = jnp.mean(x1 * x1, axis=-1, keepdims=True)
    h2 = x1 * lax.rsqrt(ms + NORM_EPS) * gn_ref[...]
    h2_ref[...] = h2
    rw = rw_ref[...]
    h2_hi, rw_hi = h2.astype(BF16), rw.astype(BF16)
    h2_lo = (h2 - h2_hi.astype(F32)).astype(BF16)
    rw_lo = (rw - rw_hi.astype(F32)).astype(BF16)
    logits = (jnp.dot(h2_hi, rw_hi, preferred_element_type=F32) + jnp.dot(h2_hi, rw_lo, preferred_element_type=F32)
              + jnp.dot(h2_lo, rw_hi, preferred_element_type=F32)) + rb_ref[...]
    tm = logits.shape[0]
    lane = lax.broadcasted_iota(jnp.int32, logits.shape, 1)
    vals, idxs = [], []
    cur = logits
    for _k in range(TOP_K):
        m = jnp.max(cur, axis=-1, keepdims=True)
        ik = jnp.min(jnp.where(cur == m, lane, N_EXPERTS), axis=-1, keepdims=True)
        vals.append(m)
        idxs.append(ik)
        cur = jnp.where(lane == ik, -jnp.inf, cur)
    exps = [jnp.exp(vk - vals[0]) for vk in vals]
    tot = exps[0] + exps[1] + exps[2] + exps[3]
    onehots = [lane == ik for ik in idxs]
    hits = sum(jnp.where(oh, 1.0, 0.0) for oh in onehots)
    ri = lax.broadcasted_iota(jnp.int32, (tm, tm), 0)
    ci = lax.broadcasted_iota(jnp.int32, (tm, tm), 1)
    before = jnp.dot((ci < ri).astype(BF16), hits.astype(BF16), preferred_element_type=F32) + base_ref[...]
    for kk in range(TOP_K):
        idx_ref[:, kk:kk + 1] = idxs[kk]
        gate_ref[:, kk:kk + 1] = exps[kk] / tot
        rank_ref[:, kk:kk + 1] = jnp.sum(jnp.where(onehots[kk], before, 0.0), axis=-1,
                                         keepdims=True).astype(jnp.int32)
    base_ref[...] = base_ref[...] + jnp.sum(hits, axis=0, keepdims=True)
    cnt_ref[...] = base_ref[...].astype(jnp.int32)


def merge_and_route(x2d, outs, lses, yb, yc, gates, w_branch, w_out, norm_ffn, router_w, router_b, tm=512):
    n, d = x2d.shape
    rows = lambda w: pl.BlockSpec((tm, w), lambda i: (i, 0))
    packed = lambda w, dil: pl.BlockSpec((tm // dil, dil * w), lambda i: (i, 0))
    full = lambda shape: pl.BlockSpec(shape, lambda i: (0,) * len(shape))
    return pl.pallas_call(
        _merge_kernel,
        grid=(n // tm,),
        in_specs=[rows(d)] + [packed(A_WIDTH, dil) for _w, dil in A_GROUPS] + [packed(LANES, dil) for _w, dil in A_GROUPS]
                 + [rows(R_WIDTH), rows(C_WIDTH), rows(N_BRANCH * d),
                    full((N_BRANCH, A_WIDTH, d)), full((d, d)), full((1, d)), full((d, N_EXPERTS)), full((1, N_EXPERTS))],
        out_specs=[rows(d), rows(d), rows(TOP_K), rows(TOP_K), rows(TOP_K), full((1, N_EXPERTS))],
        out_shape=[jax.ShapeDtypeStruct((n, d), F32), jax.ShapeDtypeStruct((n, d), F32),
                   jax.ShapeDtypeStruct((n, TOP_K), jnp.int32), jax.ShapeDtypeStruct((n, TOP_K), F32),
                   jax.ShapeDtypeStruct((n, TOP_K), jnp.int32), jax.ShapeDtypeStruct((1, N_EXPERTS), jnp.int32)],
        scratch_shapes=[pltpu.VMEM((1, N_EXPERTS), F32), pltpu.VMEM((len(A_GROUPS), A_HEADS, tm, HEAD_DIM), F32),
                        pltpu.VMEM((len(A_GROUPS), tm, LANES), F32)],
        compiler_params=_cparams(("arbitrary",)),
        name="merge_route",
    )(x2d, *outs, *lses, yb, yc, gates, w_branch.astype(BF16), w_out.astype(BF16),
      norm_ffn.reshape(1, d), router_w, router_b.reshape(1, N_EXPERTS))


def block_layout(counts, n_assign):
    counts = counts.reshape(-1)
    padded = (counts + MOE_ROWS - 1) // MOE_ROWS * MOE_ROWS
    pad_end = jnp.cumsum(padded)
    pad_start = (pad_end - padded).astype(jnp.int32)
    n_blocks = -(-n_assign // MOE_ROWS) + N_EXPERTS
    blk_row = jnp.arange(n_blocks, dtype=jnp.int32) * MOE_ROWS
    owner = jnp.sum((blk_row[:, None] >= pad_end[None, :]).astype(jnp.int32), axis=1)
    block_expert = jnp.minimum(owner, N_EXPERTS - 1).astype(jnp.int32)
    unused = blk_row >= pad_end[-1]
    zero_flag = (unused | (blk_row + MOE_ROWS == pad_end[block_expert])).astype(jnp.int32)
    n_used = (pad_end[-1:] // MOE_ROWS).astype(jnp.int32)
    has_rows = counts > 0
    eid = jnp.arange(N_EXPERTS, dtype=jnp.int32)
    later = jnp.where(has_rows[None, :] & (eid[None, :] > eid[:, None]), eid[None, :], N_EXPERTS)
    next_expert = jnp.min(later, axis=1)
    next_expert = jnp.where(next_expert == N_EXPERTS, -1, next_expert).astype(jnp.int32)
    run_parity = ((jnp.cumsum(has_rows) - has_rows) % 2).astype(jnp.int32)
    return pad_start, block_expert, zero_flag, n_used, next_expert, run_parity


def _dest_kernel(ps_ref, idx_ref, rank_ref, dest_ref):
    idx = idx_ref[...]
    dest = rank_ref[...]
    for e in range(N_EXPERTS):
        dest = dest + jnp.where(idx == e, ps_ref[e], 0)
    dest_ref[...] = dest


def assignment_rows(top_idx, rank, pad_start):
    n = top_idx.shape[0]
    rows = n * TOP_K // LANES
    flat = lambda t: t.reshape(rows, LANES)
    spec = pl.BlockSpec((rows, LANES), lambda i, ps: (0, 0))
    out = pl.pallas_call(
        _dest_kernel,
        grid_spec=pltpu.PrefetchScalarGridSpec(num_scalar_prefetch=1, grid=(1,), in_specs=[spec, spec],
                                               out_specs=spec),
        out_shape=jax.ShapeDtypeStruct((rows, LANES), jnp.int32),
        compiler_params=_cparams(("arbitrary",)),
        name="assignment_rows",
    )(pad_start, flat(top_idx), flat(rank))
    return out.reshape(n, TOP_K)


def _scatter_kernel(zf_ref, dest_ref, h2_ref, xs_hbm, zeros_ref, sem, zsem):
    i = pl.program_id(0)
    tm = h2_ref.shape[0]
    nblk = zf_ref.shape[0]

    def zero_block(j):
        return pltpu.make_async_copy(zeros_ref, xs_hbm.at[pl.ds(j * MOE_ROWS, MOE_ROWS)], zsem)

    @pl.when(i == 0)
    def _():
        zeros_ref[...] = jnp.zeros_like(zeros_ref)

        def start(j, _):
            @pl.when(zf_ref[j] != 0)
            def _():
                zero_block(j).start()
            return 0

        def wait(j, _):
            @pl.when(zf_ref[j] != 0)
            def _():
                zero_block(j).wait()
            return 0

        lax.fori_loop(0, nblk, start, 0)
        lax.fori_loop(0, nblk, wait, 0)

    def body(t, _):
        for kk in range(TOP_K):
            row = dest_ref[0, 0, t * TOP_K + kk]
            pltpu.make_async_copy(h2_ref.at[pl.ds(t, 1)], xs_hbm.at[pl.ds(row, 1)], sem).start(priority=kk % 2)
        return 0

    lax.fori_loop(0, tm, body, 0)
    pltpu.make_async_copy(xs_hbm.at[pl.ds(0, tm * TOP_K)], xs_hbm.at[pl.ds(0, tm * TOP_K)], sem).wait()


def scatter_rows(h2, dest, zero_flag, n_rows, tm=1024):
    n, w = h2.shape
    dest3 = dest.reshape(n // tm, 1, tm * TOP_K)
    grid_spec = pltpu.PrefetchScalarGridSpec(
        num_scalar_prefetch=1,
        grid=(n // tm,),
        in_specs=[pl.BlockSpec((1, 1, tm * TOP_K), lambda i, zf: (i, 0, 0), memory_space=pltpu.SMEM),
                  pl.BlockSpec((tm, w), lambda i, zf: (i, 0))],
        out_specs=pl.BlockSpec(memory_space=pl.ANY),
        scratch_shapes=[pltpu.VMEM((MOE_ROWS, w), h2.dtype), pltpu.SemaphoreType.DMA(()),
                        pltpu.SemaphoreType.DMA(())],
    )
    return pl.pallas_call(
        _scatter_kernel,
        grid_spec=grid_spec,
        out_shape=jax.ShapeDtypeStruct((n_rows, w), h2.dtype),
        compiler_params=_cparams(("arbitrary",)),
        name="scatter_rows",
    )(zero_flag, dest3, h2)


def _expert_kernel(be_ref, nu_ref, nxt_ref, par_ref, xs_ref, w1_hbm, b1_ref, w2_hbm, b2_ref, y_ref,
                   w1_f32, w2_f32, w1_scr, w2_scr, wsem):
    i = pl.program_id(0)
    e = be_ref[i]
    prev = be_ref[jnp.maximum(i - 1, 0)]

    def fetch(expert, slot):
        return (pltpu.make_async_copy(w1_hbm.at[expert], w1_f32.at[slot], wsem.at[slot]),
                pltpu.make_async_copy(w2_hbm.at[expert], w2_f32.at[slot], wsem.at[slot]))

    @pl.when((i < nu_ref[0]) & ((i == 0) | (e != prev)))
    def _():
        slot = par_ref[e]

        @pl.when(i == 0)
        def _():
            for c in fetch(e, slot):
                c.start()

        for c in fetch(e, slot):
            c.wait()

        @pl.when(nxt_ref[e] >= 0)
        def _():
            for c in fetch(nxt_ref[e], 1 - slot):
                c.start()

        w1_scr[...] = w1_f32[slot].astype(BF16)
        w2_scr[...] = w2_f32[slot].astype(BF16)

    @pl.when(i < nu_ref[0])
    def _():
        xb = xs_ref[...].astype(BF16)
        dff = w2_scr.shape[0]
        u = jnp.dot(xb, w1_scr[...], preferred_element_type=F32) + b1_ref[0]
        glu = jnp.minimum(u[:, :dff], SWIGLU_LIMIT)
        lin = jnp.clip(u[:, dff:], -SWIGLU_LIMIT, SWIGLU_LIMIT)
        act = glu * jax.nn.sigmoid(SWIGLU_ALPHA * glu) * (lin + 1.0)
        y = jnp.dot(act.astype(BF16), w2_scr[...], preferred_element_type=F32) + b2_ref[0]
        y_ref[...] = y

    @pl.when(i >= nu_ref[0])
    def _():
        y_ref[...] = jnp.zeros_like(y_ref)


def expert_ffn(x_sorted, block_expert, n_used, next_expert, run_parity, w1, b1, w2, b2):
    n_rows, d = x_sorted.shape
    w = d
    nblk = n_rows // MOE_ROWS
    dff2 = w1.shape[2]
    used = lambda i, nu: jnp.minimum(i, nu[0] - 1)
    grid_spec = pltpu.PrefetchScalarGridSpec(
        num_scalar_prefetch=4,
        grid=(nblk,),
        in_specs=[
            pl.BlockSpec((MOE_ROWS, w), lambda i, be, nu, nx, pa: (used(i, nu), 0)),
            pl.BlockSpec(memory_space=pl.ANY),
            pl.BlockSpec((1, 1, dff2), lambda i, be, nu, nx, pa: (be[used(i, nu)], 0, 0)),
            pl.BlockSpec(memory_space=pl.ANY),
            pl.BlockSpec((1, 1, d), lambda i, be, nu, nx, pa: (be[used(i, nu)], 0, 0)),
        ],
        out_specs=pl.BlockSpec((MOE_ROWS, w), lambda i, be, nu, nx, pa: (i, 0)),
        scratch_shapes=[pltpu.VMEM((2, d, dff2), F32), pltpu.VMEM((2, dff2 // 2, d), F32),
                        pltpu.VMEM((d, dff2), BF16), pltpu.VMEM((dff2 // 2, d), BF16),
                        pltpu.SemaphoreType.DMA((2,))],
    )
    return pl.pallas_call(
        _expert_kernel,
        grid_spec=grid_spec,
        out_shape=jax.ShapeDtypeStruct((n_rows, w), F32),
        compiler_params=_cparams(("arbitrary",)),
        name="expert_ffn",
    )(block_expert, n_used, next_expert, run_parity, x_sorted, w1, b1.reshape(N_EXPERTS, 1, dff2), w2,
      b2.reshape(N_EXPERTS, 1, d))


def _gather_assigned_rows(y_hbm, dest_ref, dst_ref, sem, tm):
    def body(g, _):
        base = pl.multiple_of(g * SUBLANES, SUBLANES)
        for u in range(SUBLANES):
            for kk in range(TOP_K):
                row = dest_ref[(base + u) * TOP_K + kk]
                pltpu.make_async_copy(y_hbm.at[pl.ds(row, 1)], dst_ref.at[pl.ds(kk * tm + base + u, 1)],
                                      sem).start(priority=kk % 2)
        return 0
    lax.fori_loop(0, tm // SUBLANES, body, 0)


def _combine_kernel(d0_ref, dn_ref, x1_ref, gate_ref, y_hbm, o_ref, ybuf, sems):
    i = pl.program_id(0)
    nblk = pl.num_programs(0)
    slot = lax.rem(i, 2)
    tm = o_ref.shape[0]

    @pl.when(i == 0)
    def _():
        _gather_assigned_rows(y_hbm, d0_ref.at[0, 0], ybuf.at[0], sems.at[0], tm)

    @pl.when(i + 1 < nblk)
    def _():
        _gather_assigned_rows(y_hbm, dn_ref.at[0, 0], ybuf.at[1 - slot], sems.at[1 - slot], tm)

    pltpu.make_async_copy(y_hbm.at[pl.ds(0, TOP_K * tm)], ybuf.at[slot], sems.at[slot]).wait()
    acc = x1_ref[...]
    for kk in range(TOP_K):
        acc = acc + gate_ref[:, kk:kk + 1] * ybuf[slot, pl.ds(kk * tm, tm), :]
    o_ref[...] = acc


def moe_combine(x1, gate, y_sorted, dest, tm=256):
    n, d = x1.shape
    w = y_sorted.shape[1]
    nblk = n // tm
    dest3 = dest.reshape(nblk, 1, tm * TOP_K)
    return pl.pallas_call(
        _combine_kernel,
        grid=(nblk,),
        in_specs=[pl.BlockSpec((1, 1, TOP_K * tm), lambda i: (0, 0, 0), memory_space=pltpu.SMEM),
                  pl.BlockSpec((1, 1, TOP_K * tm), lambda i: (jnp.minimum(i + 1, nblk - 1), 0, 0),
                               memory_space=pltpu.SMEM),
                  pl.BlockSpec((tm, d), lambda i: (i, 0)),
                  pl.BlockSpec((tm, TOP_K), lambda i: (i, 0)),
                  pl.BlockSpec(memory_space=pl.ANY)],
        out_specs=pl.BlockSpec((tm, d), lambda i: (i, 0)),
        out_shape=jax.ShapeDtypeStruct((n, d), F32),
        scratch_shapes=[pltpu.VMEM((2, TOP_K * tm, w), y_sorted.dtype), pltpu.SemaphoreType.DMA((2,))],
        compiler_params=_cparams(("arbitrary",)),
        name="moe_combine",
    )(dest3, dest3, x1, gate, y_sorted)


def kernel(x, mem, positions, norm_mix, w_in, b_gate, a_q_gain, a_k_gain, r_mu, r_w0, r_w2, r_a0, r_a2,
           r_g2, r_k_k, r_k_a, r_r_k, r_ln_w, r_ln_b, mem_norm, w_mem_kv, c_q_gain, c_k_gain, w_branch,
           w_out, norm_ffn, router_w, router_b, exp_w1, exp_b1, exp_w2, exp_b2):
    b, s, d = x.shape
    n = b * s
    depth = norm_mix.shape[0]
    n_groups = len(A_GROUPS)
    qkv_cols = n_groups * A_WIDTH
    off_k, off_v, off_r = qkv_cols, 2 * qkv_cols, 3 * qkv_cols
    off_cq = off_r + R_COLS
    off_gate = off_cq + C_WIDTH

    x2d = x.reshape(n, d)
    cos_t, sin_t = rope_tables(positions.reshape(n, 1).astype(jnp.int32))
    for l in range(depth):
        w_l = w_in[l]
        h = rmsnorm_rows(x2d, norm_mix[l])
        qk_gains = rotary_gains(a_q_gain[l], a_k_gain[l])
        zr = project(h, w_l[:, off_r:off_cq], "plain", out_dtype=F32, tn=R_COLS // 2)
        cq = project(h, w_l[:, off_cq:off_gate], "headnorm", (c_q_gain[l].reshape(1, HEAD_DIM),))
        gates = project(h, w_l[:, off_gate:], "gate", (b_gate[l].reshape(1, -1),), tn=2 * A_WIDTH)

        shp = lambda t: t.reshape(b, s, -1)
        outs, lses = [], []
        for g, (window, dilation) in enumerate(A_GROUPS):
            assert window // dilation == ATT_BLOCK
            cols = slice(g * A_WIDTH, (g + 1) * A_WIDTH)
            w_g = jnp.concatenate([w_l[:, :off_k][:, cols], w_l[:, off_k:off_v][:, cols],
                                   w_l[:, off_v:off_r][:, cols]], axis=1)
            qkv = project_qkv(h, w_g, qk_gains, cos_t, sin_t, dilation)
            o, lse = band_attention_group(qkv, b, g, dilation)
            outs.append(o)
            lses.append(lse)

        yb = rwkv7_mix(shp(zr), r_mu[l], r_w0[l], r_w2[l], r_a0[l], r_a2[l], r_g2[l], r_k_k[l], r_k_a[l],
                       r_r_k[l].reshape(-1), r_ln_w[l], r_ln_b[l]).reshape(n, R_WIDTH)

        mlen = mem.shape[1]
        mem_n = rmsnorm_rows(mem.reshape(b * mlen, d), mem_norm[l])
        wkv = w_mem_kv[l]
        ck = project(mem_n, wkv[:, :C_WIDTH], "headnorm", (c_k_gain[l].reshape(1, HEAD_DIM),))
        cv = project(mem_n, wkv[:, C_WIDTH:], "plain")
        yc = cross_attention(shp(cq), ck.reshape(b, mlen, C_WIDTH), cv.reshape(b, mlen, C_WIDTH)).reshape(n, C_WIDTH)

        x1, h2p, top_idx, gate, rank, counts = merge_and_route(
            x2d, outs, lses, yb, yc, gates, w_branch[l], w_out[l], norm_ffn[l], router_w[l], router_b[l])
        pad_start, block_expert, zero_flag, n_used, next_expert, run_parity = block_layout(counts, n * TOP_K)
        dest = assignment_rows(top_idx, rank, pad_start)
        x_sorted = scatter_rows(h2p, dest, zero_flag, block_expert.shape[0] * MOE_ROWS)
        y_sorted = expert_ffn(x_sorted, block_expert, n_used, next_expert, run_parity,
                              exp_w1[l], exp_b1[l], exp_w2[l], exp_b2[l])
        x2d = moe_combine(x1, gate, y_sorted, dest)
    return x2d.reshape(b, s, d)
```

```python
import functools

import jax
import jax.numpy as jnp
from jax import lax
from jax.experimental import pallas as pl
from jax.experimental.pallas import tpu as pltpu

F32 = jnp.float32
BF16 = jnp.bfloat16

NORM_EPS = 1e-6
LANES = 128
SUBLANES = 8
HEAD_DIM = 128
A_GROUPS = ((128, 1), (512, 4), (2048, 16))
A_HEADS = 4
A_WIDTH = A_HEADS * HEAD_DIM
ATT_BLOCK = 128
ATT_QB = 4
ROT_DIM = 32
ROPE_THETA = 500000.0
R_HEAD = 64
R_HEADS = 8
R_WIDTH = R_HEADS * R_HEAD
R_DECAY_LORA = 64
R_AAA_LORA = 64
R_GATE_LORA = 128
R_COLS = 3 * R_WIDTH + R_DECAY_LORA + R_AAA_LORA + R_GATE_LORA
R_GN_EPS = 64e-5
R_CHUNK = 64
R_CPI = 4
R_SEG = 256
C_HEADS = 4
C_WIDTH = C_HEADS * HEAD_DIM
N_BRANCH = 3
N_EXPERTS = 32
TOP_K = 4
SWIGLU_ALPHA = 1.702
SWIGLU_LIMIT = 7.0
MOE_ROWS = 256
NEG_BIG = -1e30

VMEM_LIMIT = 56 * 1024 * 1024


def _cparams(sem):
    return pltpu.CompilerParams(dimension_semantics=sem, vmem_limit_bytes=VMEM_LIMIT)


def _rmsnorm_kernel(x_ref, g_ref, o_ref):
    x = x_ref[...]
    ms = jnp.mean(x * x, axis=-1, keepdims=True)
    o_ref[...] = (x * lax.rsqrt(ms + NORM_EPS) * g_ref[...]).astype(o_ref.dtype)


def rmsnorm_rows(x2d, gain, tm=512):
    n, d = x2d.shape
    tm = min(tm, n)
    return pl.pallas_call(
        _rmsnorm_kernel,
        grid=(n // tm,),
        in_specs=[pl.BlockSpec((tm, d), lambda i: (i, 0)),
                  pl.BlockSpec((1, d), lambda i: (0, 0))],
        out_specs=pl.BlockSpec((tm, d), lambda i: (i, 0)),
        out_shape=jax.ShapeDtypeStruct((n, d), BF16),
        compiler_params=_cparams(("arbitrary",)),
        name="rmsnorm_rows",
    )(x2d, gain.reshape(1, d))


def _rope_table_kernel(pos_ref, freq_ref, cos_ref, sin_ref):
    ang = pos_ref[...].astype(F32) * freq_ref[...]
    lane = lax.broadcasted_iota(jnp.int32, ang.shape, 1)
    cos_ref[...] = jnp.cos(ang)
    s = jnp.sin(ang)
    half = ROT_DIM // 2
    sin_ref[...] = jnp.where(lane < half, -s, jnp.where(lane < ROT_DIM, s, 0.0))


def rope_tables(pos_col, tm=1024):
    n = pos_col.shape[0]
    half = ROT_DIM // 2
    inv_freq = ROPE_THETA ** (-jnp.arange(half, dtype=F32) / half)
    freq_row = jnp.concatenate([inv_freq, inv_freq, jnp.zeros((LANES - ROT_DIM,), F32)]).reshape(1, LANES)
    return pl.pallas_call(
        _rope_table_kernel,
        grid=(n // tm,),
        in_specs=[pl.BlockSpec((tm, 1), lambda i: (i, 0)),
                  pl.BlockSpec((1, LANES), lambda i: (0, 0))],
        out_specs=[pl.BlockSpec((tm, LANES), lambda i: (i, 0)),
                   pl.BlockSpec((tm, LANES), lambda i: (i, 0))],
        out_shape=[jax.ShapeDtypeStruct((n, LANES), F32)] * 2,
        compiler_params=_cparams(("arbitrary",)),
        name="rope_tables",
    )(pos_col, freq_row)


def _head_mean_sq(zh):
    avg = jnp.full((HEAD_DIM, HEAD_DIM), 1.0 / HEAD_DIM, BF16)
    return jnp.dot((zh * zh).astype(BF16), avg, preferred_element_type=F32)


def _cast_weight_once(w_ref, w_scr):
    @pl.when(pl.program_id(1) == 0)
    def _():
        w_scr[...] = w_ref[...].astype(w_scr.dtype)


def _proj_kernel(*refs, mode):
    h_ref, w_ref = refs[0], refs[1]
    o_ref, w_scr = refs[-2], refs[-1]
    _cast_weight_once(w_ref, w_scr)
    z = jnp.dot(h_ref[...], w_scr[...], preferred_element_type=F32)
    if mode == "plain":
        o_ref[...] = z.astype(o_ref.dtype)
    elif mode == "gate":
        o_ref[...] = (0.5 * jnp.tanh(0.5 * (z + refs[2][...])) + 0.5).astype(o_ref.dtype)
    else:
        gain = refs[2][...]
        for c in range(z.shape[1] // HEAD_DIM):
            zh = z[:, c * HEAD_DIM:(c + 1) * HEAD_DIM]
            zn = zh * lax.rsqrt(_head_mean_sq(zh) + NORM_EPS) * gain
            o_ref[:, c * HEAD_DIM:(c + 1) * HEAD_DIM] = zn.astype(o_ref.dtype)


def project(h, w, mode, extras=(), out_dtype=BF16, tm=2048, tn=512):
    n, k = h.shape
    m = w.shape[1]
    tm = min(tm, n)
    tn = min(tn, m)
    assert n % tm == 0 and m % tn == 0
    in_specs = [pl.BlockSpec((tm, k), lambda j, i: (i, 0)),
                pl.BlockSpec((k, tn), lambda j, i: (0, j))]
    if mode == "gate":
        in_specs.append(pl.BlockSpec((1, tn), lambda j, i: (0, j)))
    elif mode == "headnorm":
        in_specs.append(pl.BlockSpec((1, HEAD_DIM), lambda j, i: (0, 0)))
    return pl.pallas_call(
        functools.partial(_proj_kernel, mode=mode),
        grid=(m // tn, n // tm),
        in_specs=in_specs,
        out_specs=pl.BlockSpec((tm, tn), lambda j, i: (i, j)),
        out_shape=jax.ShapeDtypeStruct((n, m), out_dtype),
        scratch_shapes=[pltpu.VMEM((k, tn), BF16)],
        compiler_params=_cparams(("arbitrary", "arbitrary")),
        name="proj_" + mode,
    )(h, w, *extras)


def _qkv_proj_kernel(h_ref, w_ref, gain_ref, cos_ref, sin_ref, o_ref, w_scr, z_scr, *, dilation):
    j = pl.program_id(0)
    _cast_weight_once(w_ref, w_scr)
    z = jnp.dot(h_ref[...], w_scr[...], preferred_element_type=F32)
    half = ROT_DIM // 2

    pair = 2 * HEAD_DIM

    def emit(c0, slab):
        if dilation == 1:
            o_ref[:, c0 * HEAD_DIM:c0 * HEAD_DIM + pair] = slab.astype(o_ref.dtype)
        else:
            z_scr[c0] = slab[:, :HEAD_DIM]
            z_scr[c0 + 1] = slab[:, HEAD_DIM:]

    @pl.when(j < 2)
    def _():
        mi = lax.broadcasted_iota(jnp.int32, (pair, pair), 0)
        li = lax.broadcasted_iota(jnp.int32, (pair, pair), 1)
        same_head = (mi // HEAD_DIM) == (li // HEAD_DIM)
        lh, mh = li % HEAD_DIM, mi % HEAD_DIM
        avg = jnp.where(same_head, 1.0 / HEAD_DIM, 0.0).astype(BF16)
        perm = (same_head & (((lh < half) & (mh == lh + half))
                             | ((lh >= half) & (lh < ROT_DIM) & (mh == lh - half)))).astype(BF16)
        gcos = gain_ref[0, 0:1, :] * cos_ref[...]
        gsin = gain_ref[0, 1:2, :] * sin_ref[...]
        gcos = jnp.concatenate([gcos, gcos], axis=1)
        gsin = jnp.concatenate([gsin, gsin], axis=1)
        for c0 in range(0, A_HEADS, 2):
            zz = z[:, c0 * HEAD_DIM:c0 * HEAD_DIM + pair]
            ms = jnp.dot((zz * zz).astype(BF16), avg, preferred_element_type=F32)
            partner = jnp.dot(zz.astype(BF16), perm, preferred_element_type=F32)
            emit(c0, lax.rsqrt(ms + NORM_EPS) * (zz * gcos + partner * gsin))

    @pl.when(j == 2)
    def _():
        for c0 in range(0, A_HEADS, 2):
            emit(c0, z[:, c0 * HEAD_DIM:c0 * HEAD_DIM + pair])

    if dilation > 1:
        rows = z_scr.shape[1] // dilation
        for r in range(dilation):
            for c in range(A_HEADS):
                lo = r * A_WIDTH + c * HEAD_DIM
                o_ref[:, lo:lo + HEAD_DIM] = z_scr[c, pl.ds(r, rows, stride=dilation), :].astype(o_ref.dtype)


def rotary_gains(q_gain, k_gain):
    half = ROT_DIM // 2
    lane = jnp.arange(HEAD_DIM)
    partner = jnp.where(lane < half, lane + half, jnp.where(lane < ROT_DIM, lane - half, lane))
    return jnp.stack([jnp.stack([g, g[partner]]) for g in (q_gain, k_gain)])


def project_qkv(h, w_qkv, gains, cos_t, sin_t, dilation, tm=2048):
    n, k = h.shape
    d = dilation
    return pl.pallas_call(
        functools.partial(_qkv_proj_kernel, dilation=d),
        grid=(3, n // tm),
        in_specs=[pl.BlockSpec((tm, k), lambda j, i: (i, 0)),
                  pl.BlockSpec((k, A_WIDTH), lambda j, i: (0, j)),
                  pl.BlockSpec((1, 2, HEAD_DIM), lambda j, i: (jnp.minimum(j, 1), 0, 0)),
                  pl.BlockSpec((tm, LANES), lambda j, i: (i, 0)),
                  pl.BlockSpec((tm, LANES), lambda j, i: (i, 0))],
        out_specs=pl.BlockSpec((tm // d, d * A_WIDTH), lambda j, i: (i, j)),
        out_shape=jax.ShapeDtypeStruct((n // d, 3 * d * A_WIDTH), BF16),
        scratch_shapes=[pltpu.VMEM((k, A_WIDTH), BF16), pltpu.VMEM((A_HEADS, tm, HEAD_DIM), F32)],
        compiler_params=_cparams(("arbitrary", "arbitrary")),
        name=f"proj_qkv_d{d}",
    )(h, w_qkv, gains, cos_t, sin_t)


def _band_attn_kernel(*refs, qb):
    q_ref = refs[0]
    k_refs = refs[1:qb + 2]
    v_refs = refs[qb + 2:2 * qb + 3]
    o_ref, lse_ref = refs[-2:]
    step = pl.program_id(2)
    scale = HEAD_DIM ** -0.5
    nq = ATT_BLOCK
    qi = lax.broadcasted_iota(jnp.int32, (nq, 2 * nq), 0)
    ki = lax.broadcasted_iota(jnp.int32, (nq, 2 * nq), 1)
    rel = qi + nq - ki
    band = (rel >= 0) & (rel <= nq)
    for a in range(qb):
        blk = step * qb + a
        valid = band & ((blk * nq - nq + ki) >= 0)
        rows = slice(a * nq, (a + 1) * nq)
        lses = []
        for h in range(A_HEADS):
            sl = slice(h * HEAD_DIM, (h + 1) * HEAD_DIM)
            qh = q_ref[0, rows, sl]
            kh = jnp.concatenate([k_refs[a][0, :, sl], k_refs[a + 1][0, :, sl]], axis=0)
            vh = jnp.concatenate([v_refs[a][0, :, sl], v_refs[a + 1][0, :, sl]], axis=0)
            s = lax.dot_general(qh, kh, (((1,), (1,)), ((), ())), preferred_element_type=F32) * scale
            s = jnp.where(valid, s, NEG_BIG)
            m = jnp.max(s, axis=-1, keepdims=True)
            p = jnp.exp(s - m)
            l = jnp.sum(p, axis=-1, keepdims=True)
            o = jnp.dot(p.astype(BF16), vh, preferred_element_type=F32) / l
            o_ref[0, rows, sl] = o.astype(o_ref.dtype)
            lses.append(jnp.broadcast_to(m + jnp.log(l), (nq, LANES // A_HEADS)))
        lse_ref[0, rows, :] = jnp.concatenate(lses, axis=1)


def band_attention_group(qkv, b, g, dilation):
    d = dilation
    sub = qkv.shape[0] // b
    nblk = sub // ATT_BLOCK
    qb = min(ATT_QB, nblk)
    assert nblk % qb == 0
    view = qkv.reshape(b, sub, 3 * d * A_WIDTH)
    qrows = qb * ATT_BLOCK
    key_spec = lambda t, m: pl.BlockSpec(
        (1, ATT_BLOCK, A_WIDTH), lambda bi, r, j: (bi, jnp.maximum(j * qb - 1 + m, 0), t * d + r))
    o, lse = pl.pallas_call(
        functools.partial(_band_attn_kernel, qb=qb),
        grid=(b, d, nblk // qb),
        in_specs=[pl.BlockSpec((1, qrows, A_WIDTH), lambda bi, r, j: (bi, j, r))]
                 + [key_spec(1, m) for m in range(qb + 1)] + [key_spec(2, m) for m in range(qb + 1)],
        out_specs=[pl.BlockSpec((1, qrows, A_WIDTH), lambda bi, r, j: (bi, j, r)),
                   pl.BlockSpec((1, qrows, LANES), lambda bi, r, j: (bi, j, r))],
        out_shape=[jax.ShapeDtypeStruct((b, sub, d * A_WIDTH), BF16),
                   jax.ShapeDtypeStruct((b, sub, d * LANES), F32)],
        compiler_params=_cparams(("arbitrary", "arbitrary", "arbitrary")),
        name=f"band_attn_g{g}",
    )(*([view] * (2 * qb + 3)))
    return o.reshape(b * sub, d * A_WIDTH), lse.reshape(b * sub, d * LANES)


def _cross_attn_kernel(q_ref, k_ref, v_ref, o_ref):
    scale = HEAD_DIM ** -0.5
    for h in range(C_HEADS):
        sl = slice(h * HEAD_DIM, (h + 1) * HEAD_DIM)
        s = lax.dot_general(q_ref[0, :, sl], k_ref[0, :, sl], (((1,), (1,)), ((), ())),
                            preferred_element_type=F32) * scale
        m = jnp.max(s, axis=-1, keepdims=True)
        p = jnp.exp(s - m)
        l = jnp.sum(p, axis=-1, keepdims=True)
        o = jnp.dot(p.astype(BF16), v_ref[0, :, sl], preferred_element_type=F32) / l
        o_ref[0, :, sl] = o.astype(o_ref.dtype)


def cross_attention(qn, kn, v, tm=512):
    b, s, w = qn.shape
    m = kn.shape[1]
    return pl.pallas_call(
        _cross_attn_kernel,
        grid=(b, s // tm),
        in_specs=[pl.BlockSpec((1, tm, w), lambda bi, i: (bi, i, 0)),
                  pl.BlockSpec((1, m, w), lambda bi, i: (bi, 0, 0)),
                  pl.BlockSpec((1, m, w), lambda bi, i: (bi, 0, 0))],
        out_specs=pl.BlockSpec((1, tm, w), lambda bi, i: (bi, i, 0)),
        out_shape=jax.ShapeDtypeStruct((b, s, w), BF16),
        compiler_params=_cparams(("arbitrary", "arbitrary")),
        name="cross_attn",
    )(qn, kn, v)


def _head_sums(x, seg):
    w = seg.shape[0]
    x16 = x.astype(BF16)
    return jnp.concatenate(
        [jnp.dot(x16[:, j:j + w], seg, preferred_element_type=F32) for j in range(0, x.shape[1], w)], axis=1)


def _dot_t(a, b):
    return lax.dot_general(a, b, (((0,), (0,)), ((), ())), preferred_element_type=F32)


def _dot_nt(a, b):
    return lax.dot_general(a, b, (((1,), (1,)), ((), ())), preferred_element_type=F32)


def _rwkv_kernel(zr_ref, mu_ref, w0_ref, wwa_ref, a0_ref, g2_ref, kk_ref, ka_ref, rk_ref,
                 lnw_ref, lnb_ref, seg_ref, y_ref,
                 state_ref, carry_ref, ops_ref, yh_ref):
    t = pl.program_id(1)
    tt = zr_ref.shape[1]
    nch = tt // R_CHUNK
    c = R_CHUNK

    @pl.when(t == 0)
    def _():
        state_ref[...] = jnp.zeros_like(state_ref)
        carry_ref[...] = jnp.zeros_like(carry_ref)

    z = zr_ref[0]
    row = lax.broadcasted_iota(jnp.int32, z.shape, 0)
    prev = jnp.where(row == 0, carry_ref[...], pltpu.roll(z, 1, 0))
    carry_ref[...] = z[tt - 1:tt, :]
    xs = z + (prev - z) * mu_ref[...]

    w3 = 3 * R_WIDTH
    r = xs[:, 0:R_WIDTH]
    k = xs[:, R_WIDTH:2 * R_WIDTH]
    v = xs[:, 2 * R_WIDTH:w3]
    wa_lo = xs[:, w3:w3 + LANES]
    g_lo = xs[:, w3 + LANES:w3 + 2 * LANES]
    lane = lax.broadcasted_iota(jnp.int32, wa_lo.shape, 1)
    wa_in = jnp.where(lane < R_DECAY_LORA, jnp.tanh(wa_lo), wa_lo)
    wa = jnp.dot(wa_in.astype(BF16), wwa_ref[...], preferred_element_type=F32)
    u = -(w0_ref[...] + wa[:, :R_WIDTH])
    softplus = jnp.maximum(u, 0.0) + jnp.log(1.0 + jnp.exp(-jnp.abs(u)))
    w_raw = -softplus - 0.5
    ld = -jnp.exp(w_raw)
    a = jax.nn.sigmoid(a0_ref[...] + wa[:, R_WIDTH:])
    g = jnp.dot(jax.nn.sigmoid(g_lo).astype(BF16), g2_ref[...], preferred_element_type=F32)

    seg = seg_ref[...]
    kk = k * kk_ref[...]
    kk = kk * jnp.minimum(lax.rsqrt(_head_sums(kk * kk, seg)), 1e12)
    k2 = k * (1.0 + (a - 1.0) * ka_ref[...])
    bonus = _head_sums(r * k2 * rk_ref[...], seg) * v

    ri = lax.broadcasted_iota(jnp.int32, (c, c), 0)
    ci = lax.broadcasted_iota(jnp.int32, (c, c), 1)
    tri = (ci <= ri).astype(BF16)
    ld_hi = ld.astype(BF16)
    ld_lo = (ld - ld_hi.astype(F32)).astype(BF16)
    lcs = []
    for ch in range(nch):
        rs = slice(ch * c, (ch + 1) * c)
        lcs.append(jnp.dot(tri, ld_hi[rs], preferred_element_type=F32)
                   + jnp.dot(tri, ld_lo[rs], preferred_element_type=F32))
    lc = jnp.concatenate(lcs, axis=0)
    e_inc = jnp.exp(lc)
    e_exc = jnp.exp(lc - ld)
    e_inv = jnp.exp(-lc)
    a_t = -kk * e_exc
    r_t = r * e_inc
    b_t = kk * a * e_inv
    k_t = k2 * e_inv
    for h in range(R_HEADS):
        hs = slice(h * R_HEAD, (h + 1) * R_HEAD)
        ops_ref[0, h] = a_t[:, hs]
        ops_ref[1, h] = r_t[:, hs]
        ops_ref[2, h] = b_t[:, hs]
        ops_ref[3, h] = k_t[:, hs]
        ops_ref[4, h] = v[:, hs]
        ops_ref[5, h] = e_inc[:, hs]

    strict = ci < ri
    incl = ci <= ri
    eye = (ci == ri)

    ri2 = lax.broadcasted_iota(jnp.int32, (c, 2 * c), 0)
    ci2 = lax.broadcasted_iota(jnp.int32, (c, 2 * c), 1)
    incl2 = jnp.bitwise_and(ci2, c - 1) <= ri2
    eye_f = jnp.where(eye, 1.0, 0.0)
    heads = range(R_HEADS)
    dot = functools.partial(jnp.dot, preferred_element_type=F32)

    def chunk_body(ch, _):
        starts = [pl.multiple_of((ch * R_CPI + sub) * c, c) for sub in range(R_CPI)]
        rows = [pl.ds(r0, c) for r0 in starts]
        items = [(sub, h) for sub in range(R_CPI) for h in heads]
        idx = range(len(items))
        at = [ops_ref[0, h, rows[sub], :] for sub, h in items]
        rt = [ops_ref[1, h, rows[sub], :] for sub, h in items]
        bt = [ops_ref[2, h, rows[sub], :] for sub, h in items]
        kt = [ops_ref[3, h, rows[sub], :] for sub, h in items]
        pc = [ops_ref[5, h, pl.ds(starts[sub] + c - 1, 1), :] for sub, h in items]
        at16 = [x.astype(BF16) for x in at]
        rt16 = [x.astype(BF16) for x in rt]
        bt16 = [x.astype(BF16) for x in bt]
        kt16 = [x.astype(BF16) for x in kt]
        v16 = [ops_ref[4, h, rows[sub], :].astype(BF16) for sub, h in items]
        bk16 = [jnp.concatenate([bt16[i], kt16[i]], axis=0) for i in idx]
        nmat = [jnp.where(strict, _dot_nt(at16[i], bt16[i]), 0.0) for i in idx]
        a_ak = [jnp.where(strict, _dot_nt(at16[i], kt16[i]), 0.0).astype(BF16) for i in idx]
        a_rbk = [jnp.where(incl2, _dot_nt(rt16[i], bk16[i]), 0.0).astype(BF16) for i in idx]
        npow = nmat
        tinv = [eye_f + nmat[i] for i in idx]
        for _i in range(5):
            np16 = [x.astype(BF16) for x in npow]
            npow = [dot(np16[i], np16[i]) for i in idx]
            tinv = [tinv[i] + dot(tinv[i].astype(BF16), npow[i].astype(BF16)) for i in idx]
        akv = [dot(a_ak[i], v16[i]).astype(BF16) for i in idx]
        apw1 = [dot(tinv[i].astype(BF16), jnp.concatenate([at16[i], akv[i]], axis=1)).astype(BF16)
                for i in idx]
        zero = jnp.zeros((c, R_HEAD), BF16)
        rhs2 = [jnp.concatenate([apw1[i], jnp.concatenate([zero, v16[i]], axis=1)], axis=0)
                for i in idx]
        bkh = [jnp.concatenate([bt[i] * pc[i], kt[i] * pc[i]], axis=0).astype(BF16) for i in idx]
        gh = [_dot_t(bkh[i], rhs2[i]) for i in idx]
        qy = [dot(a_rbk[i], rhs2[i]) for i in idx]
        for i, (sub, h) in enumerate(items):
            gm = jnp.where(eye, jnp.broadcast_to(pc[i], (c, c)), 0.0) + gh[i][:, :R_HEAD]
            qp = rt[i] + qy[i][:, :R_HEAD]
            st = state_ref[h]
            res = dot(jnp.concatenate([qp, gm], axis=0).astype(BF16), st.astype(BF16))
            yh_ref[h, rows[sub], :] = res[:c] + qy[i][:, R_HEAD:]
            state_ref[h] = res[c:] + gh[i][:, R_HEAD:]
        return 0

    lax.fori_loop(0, nch // R_CPI, chunk_body, 0)

    y = jnp.concatenate([yh_ref[h] for h in range(R_HEADS)], axis=1)
    mean = _head_sums(y, seg) * (1.0 / R_HEAD)
    dlt = y - mean
    var = _head_sums(dlt * dlt, seg) * (1.0 / R_HEAD)
    yn = dlt * lax.rsqrt(var + R_GN_EPS) * lnw_ref[...] + lnb_ref[...]
    y_ref[0] = ((yn + bonus) * g).astype(y_ref.dtype)


def rwkv7_mix(zr, mu, w0, w2, a0, a2, g2, k_k, k_a, r_k, ln_w, ln_b, tt=256):
    b, s, cols = zr.shape
    row = lambda x: x.reshape(1, -1).astype(F32)
    wwa = jnp.zeros((LANES, 2 * R_WIDTH), F32)
    wwa = wwa.at[:R_DECAY_LORA, :R_WIDTH].set(w2).at[R_DECAY_LORA:, R_WIDTH:].set(a2).astype(BF16)
    hid = jnp.arange(R_SEG) // R_HEAD
    seg = (hid[:, None] == hid[None, :]).astype(BF16)
    full = lambda shape: pl.BlockSpec(shape, lambda bi, t: (0,) * len(shape))
    return pl.pallas_call(
        _rwkv_kernel,
        grid=(b, s // tt),
        in_specs=[pl.BlockSpec((1, tt, cols), lambda bi, t: (bi, t, 0)),
                  full((1, cols)), full((1, R_WIDTH)), full((LANES, 2 * R_WIDTH)), full((1, R_WIDTH)),
                  full((R_GATE_LORA, R_WIDTH)), full((1, R_WIDTH)), full((1, R_WIDTH)), full((1, R_WIDTH)),
                  full((1, R_WIDTH)), full((1, R_WIDTH)), full((R_SEG, R_SEG))],
        out_specs=pl.BlockSpec((1, tt, R_WIDTH), lambda bi, t: (bi, t, 0)),
        out_shape=jax.ShapeDtypeStruct((b, s, R_WIDTH), BF16),
        scratch_shapes=[pltpu.VMEM((R_HEADS, R_HEAD, R_HEAD), F32),
                        pltpu.VMEM((1, cols), F32),
                        pltpu.VMEM((6, R_HEADS, tt, R_HEAD), F32),
                        pltpu.VMEM((R_HEADS, tt, R_HEAD), F32)],
        compiler_params=_cparams(("arbitrary", "arbitrary")),
        name="rwkv7_mix",
    )(zr, row(mu), row(w0), wwa, row(a0), g2.astype(BF16), row(k_k), row(k_a), row(r_k),
      row(ln_w), row(ln_b), seg)


def _merge_kernel(x_ref, o0_ref, o1_ref, o2_ref, l0_ref, l1_ref, l2_ref, yb_ref, yc_ref, gt_ref,
                  wb_ref, wo_ref, gn_ref, rw_ref, rb_ref,
                  x1_ref, h2_ref, idx_ref, gate_ref, rank_ref, cnt_ref, base_ref, o_scr, l_scr):
    @pl.when(pl.program_id(0) == 0)
    def _():
        base_ref[...] = jnp.zeros_like(base_ref)

    tm_rows = x_ref.shape[0]
    for gi, (o_ref, l_ref) in enumerate(((o0_ref, l0_ref), (o1_ref, l1_ref), (o2_ref, l2_ref))):
        dil = A_GROUPS[gi][1]
        for r in range(dil):
            dst = pl.ds(r, tm_rows // dil, stride=dil) if dil > 1 else slice(None)
            for h in range(A_HEADS):
                lo = r * A_WIDTH + h * HEAD_DIM
                o_scr[gi, h, dst, :] = o_ref[:, lo:lo + HEAD_DIM].astype(F32)
            l_scr[gi, dst, :] = l_ref[:, r * LANES:(r + 1) * LANES]
    lses = [l_scr[gi] for gi in range(3)]
    lmax = jnp.maximum(jnp.maximum(lses[0], lses[1]), lses[2])
    es = [jnp.exp(l - lmax) for l in lses]
    inv = 1.0 / (es[0] + es[1] + es[2])
    qw = LANES // A_HEADS
    heads = []
    for h in range(A_HEADS):
        sl = slice(h * HEAD_DIM, (h + 1) * HEAD_DIM)
        acc = None
        for gi in range(3):
            alpha = (es[gi] * inv)[:, h * qw:h * qw + 1]
            term = alpha * o_scr[gi, h]
            acc = term if acc is None else acc + term
        heads.append(acc)
    ya = jnp.concatenate(heads, axis=1).astype(BF16)
    d = x_ref.shape[1]
    merged = None
    for n, yn in enumerate((ya, yb_ref[...], yc_ref[...])):
        proj = jnp.dot(yn, wb_ref[n], preferred_element_type=F32)
        term = gt_ref[:, n * d:(n + 1) * d].astype(F32) * proj
        merged = term if merged is None else merged + term
    x1 = x_ref[...] + jnp.dot(merged.astype(BF16), wo_ref[...], preferred_element_type=F32)
    x1_ref[...] = x1
    ms = jnp.mean(x1 * x1, axis=-1, keepdims=True)
    h2 = x1 * lax.rsqrt(ms + NORM_EPS) * gn_ref[...]
    h2_ref[...] = h2
    rw = rw_ref[...]
    h2_hi, rw_hi = h2.astype(BF16), rw.astype(BF16)
    h2_lo = (h2 - h2_hi.astype(F32)).astype(BF16)
    rw_lo = (rw - rw_hi.astype(F32)).astype(BF16)
    logits = (jnp.dot(h2_hi, rw_hi, preferred_element_type=F32) + jnp.dot(h2_hi, rw_lo, preferred_element_type=F32)
              + jnp.dot(h2_lo, rw_hi, preferred_element_type=F32)) + rb_ref[...]
    tm = logits.shape[0]
    lane = lax.broadcasted_iota(jnp.int32, logits.shape, 1)
    vals, idxs = [], []
    cur = logits
    for _k in range(TOP_K):
        m = jnp.max(cur, axis=-1, keepdims=True)
        ik = jnp.min(jnp.where(cur == m, lane, N_EXPERTS), axis=-1, keepdims=True)
        vals.append(m)
        idxs.append(ik)
        cur = jnp.where(lane == ik, -jnp.inf, cur)
    exps = [jnp.exp(vk - vals[0]) for vk in vals]
    tot = exps[0] + exps[1] + exps[2] + exps[3]
    onehots = [lane == ik for ik in idxs]
    hits = sum(jnp.where(oh, 1.0, 0.0) for oh in onehots)
    ri = lax.broadcasted_iota(jnp.int32, (tm, tm), 0)
    ci = lax.broadcasted_iota(jnp.int32, (tm, tm), 1)
    before = jnp.dot((ci < ri).astype(BF16), hits.astype(BF16), preferred_element_type=F32) + base_ref[...]
    for kk in range(TOP_K):
        idx_ref[:, kk:kk + 1] = idxs[kk]
        gate_ref[:, kk:kk + 1] = exps[kk] / tot
        rank_ref[:, kk:kk + 1] = jnp.sum(jnp.where(onehots[kk], before, 0.0), axis=-1,
                                         keepdims=True).astype(jnp.int32)
    base_ref[...] = base_ref[...] + jnp.sum(hits, axis=0, keepdims=True)
    cnt_ref[...] = base_ref[...].astype(jnp.int32)


def merge_and_route(x2d, outs, lses, yb, yc, gates, w_branch, w_out, norm_ffn, router_w, router_b, tm=512):
    n, d = x2d.shape
    rows = lambda w: pl.BlockSpec((tm, w), lambda i: (i, 0))
    packed = lambda w, dil: pl.BlockSpec((tm // dil, dil * w), lambda i: (i, 0))
    full = lambda shape: pl.BlockSpec(shape, lambda i: (0,) * len(shape))
    return pl.pallas_call(
        _merge_kernel,
        grid=(n // tm,),
        in_specs=[rows(d)] + [packed(A_WIDTH, dil) for _w, dil in A_GROUPS] + [packed(LANES, dil) for _w, dil in A_GROUPS]
                 + [rows(R_WIDTH), rows(C_WIDTH), rows(N_BRANCH * d),
                    full((N_BRANCH, A_WIDTH, d)), full((d, d)), full((1, d)), full((d, N_EXPERTS)), full((1, N_EXPERTS))],
        out_specs=[rows(d), rows(d), rows(TOP_K), rows(TOP_K), rows(TOP_K), full((1, N_EXPERTS))],
        out_shape=[jax.ShapeDtypeStruct((n, d), F32), jax.ShapeDtypeStruct((n, d), F32),
                   jax.ShapeDtypeStruct((n, TOP_K), jnp.int32), jax.ShapeDtypeStruct((n, TOP_K), F32),
                   jax.ShapeDtypeStruct((n, TOP_K), jnp.int32), jax.ShapeDtypeStruct((1, N_EXPERTS), jnp.int32)],
        scratch_shapes=[pltpu.VMEM((1, N_EXPERTS), F32), pltpu.VMEM((len(A_GROUPS), A_HEADS, tm, HEAD_DIM), F32),
                        pltpu.VMEM((len(A_GROUPS), tm, LANES), F32)],
        compiler_params=_cparams(("arbitrary",)),
        name="merge_route",
    )(x2d, *outs, *lses, yb, yc, gates, w_branch.astype(BF16), w_out.astype(BF16),
      norm_ffn.reshape(1, d), router_w, router_b.reshape(1, N_EXPERTS))


def block_layout(counts, n_assign):
    counts = counts.reshape(-1)
    padded = (counts + MOE_ROWS - 1) // MOE_ROWS * MOE_ROWS
    pad_end = jnp.cumsum(padded)
    pad_start = (pad_end - padded).astype(jnp.int32)
    n_blocks = -(-n_assign // MOE_ROWS) + N_EXPERTS
    blk_row = jnp.arange(n_blocks, dtype=jnp.int32) * MOE_ROWS
    owner = jnp.sum((blk_row[:, None] >= pad_end[None, :]).astype(jnp.int32), axis=1)
    block_expert = jnp.minimum(owner, N_EXPERTS - 1).astype(jnp.int32)
    unused = blk_row >= pad_end[-1]
    zero_flag = (unused | (blk_row + MOE_ROWS == pad_end[block_expert])).astype(jnp.int32)
    n_used = (pad_end[-1:] // MOE_ROWS).astype(jnp.int32)
    has_rows = counts > 0
    eid = jnp.arange(N_EXPERTS, dtype=jnp.int32)
    later = jnp.where(has_rows[None, :] & (eid[None, :] > eid[:, None]), eid[None, :], N_EXPERTS)
    next_expert = jnp.min(later, axis=1)
    next_expert = jnp.where(next_expert == N_EXPERTS, -1, next_expert).astype(jnp.int32)
    run_parity = ((jnp.cumsum(has_rows) - has_rows) % 2).astype(jnp.int32)
    return pad_start, block_expert, zero_flag, n_used, next_expert, run_parity


def _dest_kernel(ps_ref, idx_ref, rank_ref, dest_ref):
    idx = idx_ref[...]
    dest = rank_ref[...]
    for e in range(N_EXPERTS):
        dest = dest + jnp.where(idx == e, ps_ref[e], 0)
    dest_ref[...] = dest


def assignment_rows(top_idx, rank, pad_start):
    n = top_idx.shape[0]
    rows = n * TOP_K // LANES
    flat = lambda t: t.reshape(rows, LANES)
    spec = pl.BlockSpec((rows, LANES), lambda i, ps: (0, 0))
    out = pl.pallas_call(
        _dest_kernel,
        grid_spec=pltpu.PrefetchScalarGridSpec(num_scalar_prefetch=1, grid=(1,), in_specs=[spec, spec],
                                               out_specs=spec),
        out_shape=jax.ShapeDtypeStruct((rows, LANES), jnp.int32),
        compiler_params=_cparams(("arbitrary",)),
        name="assignment_rows",
    )(pad_start, flat(top_idx), flat(rank))
    return out.reshape(n, TOP_K)


def _scatter_kernel(zf_ref, dest_ref, h2_ref, xs_hbm, zeros_ref, sem, zsem):
    i = pl.program_id(0)
    tiles = h2_ref.shape[0]
    tm = tiles * SUBLANES
    nblk = zf_ref.shape[0]

    def zero_block(j):
        return pltpu.make_async_copy(zeros_ref, xs_hbm.at[pl.ds(j * MOE_ROWS, MOE_ROWS)], zsem)

    @pl.when(i == 0)
    def _():
        zeros_ref[...] = jnp.zeros_like(zeros_ref)

        def start(j, _):
            @pl.when(zf_ref[j] != 0)
            def _():
                zero_block(j).start()
            return 0

        def wait(j, _):
            @pl.when(zf_ref[j] != 0)
            def _():
                zero_block(j).wait()
            return 0

        lax.fori_loop(0, nblk, start, 0)
        lax.fori_loop(0, nblk, wait, 0)

    def body(g, _):
        for u in range(SUBLANES):
            for kk in range(TOP_K):
                row = dest_ref[0, 0, (g * SUBLANES + u) * TOP_K + kk]
                pltpu.make_async_copy(h2_ref.at[g, pl.ds(u, 1)], xs_hbm.at[pl.ds(row, 1)], sem).start(
                    priority=kk % 2)
        return 0

    lax.fori_loop(0, tiles, body, 0)
    pltpu.make_async_copy(xs_hbm.at[pl.ds(0, tm * TOP_K)], xs_hbm.at[pl.ds(0, tm * TOP_K)], sem).wait()


def scatter_rows(h2, dest, zero_flag, n_rows, tm=1024):
    n, w = h2.shape
    dest3 = dest.reshape(n // tm, 1, tm * TOP_K)
    grid_spec = pltpu.PrefetchScalarGridSpec(
        num_scalar_prefetch=1,
        grid=(n // tm,),
        in_specs=[pl.BlockSpec((1, 1, tm * TOP_K), lambda i, zf: (i, 0, 0), memory_space=pltpu.SMEM),
                  pl.BlockSpec((tm // SUBLANES, SUBLANES, w), lambda i, zf: (i, 0, 0))],
        out_specs=pl.BlockSpec(memory_space=pl.ANY),
        scratch_shapes=[pltpu.VMEM((MOE_ROWS, w), h2.dtype), pltpu.SemaphoreType.DMA(()),
                        pltpu.SemaphoreType.DMA(())],
    )
    return pl.pallas_call(
        _scatter_kernel,
        grid_spec=grid_spec,
        out_shape=jax.ShapeDtypeStruct((n_rows, w), h2.dtype),
        compiler_params=_cparams(("arbitrary",)),
        name="scatter_rows",
    )(zero_flag, dest3, h2.reshape(n // SUBLANES, SUBLANES, w))


def _expert_kernel(be_ref, nu_ref, nxt_ref, par_ref, xs_ref, w1_hbm, b1_ref, w2_hbm, b2_ref, y_ref,
                   w1_f32, w2_f32, w1_scr, w2_scr, wsem):
    i = pl.program_id(0)
    e = be_ref[i]
    prev = be_ref[jnp.maximum(i - 1, 0)]

    def fetch(expert, slot):
        return (pltpu.make_async_copy(w1_hbm.at[expert], w1_f32.at[slot], wsem.at[slot]),
                pltpu.make_async_copy(w2_hbm.at[expert], w2_f32.at[slot], wsem.at[slot]))

    @pl.when((i < nu_ref[0]) & ((i == 0) | (e != prev)))
    def _():
        slot = par_ref[e]

        @pl.when(i == 0)
        def _():
            for c in fetch(e, slot):
                c.start()

        for c in fetch(e, slot):
            c.wait()

        @pl.when(nxt_ref[e] >= 0)
        def _():
            for c in fetch(nxt_ref[e], 1 - slot):
                c.start()

        w1_scr[...] = w1_f32[slot].astype(BF16)
        w2_scr[...] = w2_f32[slot].astype(BF16)

    @pl.when(i < nu_ref[0])
    def _():
        xb = xs_ref[...].astype(BF16)
        dff = w2_scr.shape[0]
        u = jnp.dot(xb, w1_scr[...], preferred_element_type=F32) + b1_ref[0]
        glu = jnp.minimum(u[:, :dff], SWIGLU_LIMIT)
        lin = jnp.clip(u[:, dff:], -SWIGLU_LIMIT, SWIGLU_LIMIT)
        act = glu * jax.nn.sigmoid(SWIGLU_ALPHA * glu) * (lin + 1.0)
        y = jnp.dot(act.astype(BF16), w2_scr[...], preferred_element_type=F32) + b2_ref[0]
        y_ref[...] = y

    @pl.when(i >= nu_ref[0])
    def _():
        y_ref[...] = jnp.zeros_like(y_ref)


def expert_ffn(x_sorted, block_expert, n_used, next_expert, run_parity, w1, b1, w2, b2):
    n_rows, d = x_sorted.shape
    w = d
    nblk = n_rows // MOE_ROWS
    dff2 = w1.shape[2]
    used = lambda i, nu: jnp.minimum(i, nu[0] - 1)
    grid_spec = pltpu.PrefetchScalarGridSpec(
        num_scalar_prefetch=4,
        grid=(nblk,),
        in_specs=[
            pl.BlockSpec((MOE_ROWS, w), lambda i, be, nu, nx, pa: (used(i, nu), 0)),
            pl.BlockSpec(memory_space=pl.ANY),
            pl.BlockSpec((1, 1, dff2), lambda i, be, nu, nx, pa: (be[used(i, nu)], 0, 0)),
            pl.BlockSpec(memory_space=pl.ANY),
            pl.BlockSpec((1, 1, d), lambda i, be, nu, nx, pa: (be[used(i, nu)], 0, 0)),
        ],
        out_specs=pl.BlockSpec((MOE_ROWS, w), lambda i, be, nu, nx, pa: (i, 0)),
        scratch_shapes=[pltpu.VMEM((2, d, dff2), F32), pltpu.VMEM((2, dff2 // 2, d), F32),
                        pltpu.VMEM((d, dff2), BF16), pltpu.VMEM((dff2 // 2, d), BF16),
                        pltpu.SemaphoreType.DMA((2,))],
    )
    return pl.pallas_call(
        _expert_kernel,
        grid_spec=grid_spec,
        out_shape=jax.ShapeDtypeStruct((n_rows, w), F32),
        compiler_params=_cparams(("arbitrary",)),
        name="expert_ffn",
    )(block_expert, n_used, next_expert, run_parity, x_sorted, w1, b1.reshape(N_EXPERTS, 1, dff2), w2,
      b2.reshape(N_EXPERTS, 1, d))


def _gather_assigned_rows(y_hbm, dest_ref, dst_ref, sem, tm):
    tiles = tm // SUBLANES

    def body(g, _):
        for u in range(SUBLANES):
            for kk in range(TOP_K):
                row = dest_ref[(g * SUBLANES + u) * TOP_K + kk]
                pltpu.make_async_copy(y_hbm.at[pl.ds(row, 1)], dst_ref.at[kk * tiles + g, pl.ds(u, 1)],
                                      sem).start(priority=kk % 2)
        return 0
    lax.fori_loop(0, tiles, body, 0)


def _combine_kernel(d0_ref, dn_ref, x1_ref, gate_ref, y_hbm, y_tiles_hbm, o_ref, ybuf, sems):
    i = pl.program_id(0)
    nblk = pl.num_programs(0)
    slot = lax.rem(i, 2)
    tm = o_ref.shape[0]
    tiles = tm // SUBLANES

    @pl.when(i == 0)
    def _():
        _gather_assigned_rows(y_hbm, d0_ref.at[0, 0], ybuf.at[0], sems.at[0], tm)

    @pl.when(i + 1 < nblk)
    def _():
        _gather_assigned_rows(y_hbm, dn_ref.at[0, 0], ybuf.at[1 - slot], sems.at[1 - slot], tm)

    pltpu.make_async_copy(y_tiles_hbm.at[pl.ds(0, TOP_K * tiles)], ybuf.at[slot], sems.at[slot]).wait()
    acc = x1_ref[...]
    for kk in range(TOP_K):
        rows = ybuf[slot, pl.ds(kk * tiles, tiles)].reshape(tm, ybuf.shape[-1])
        acc = acc + gate_ref[:, kk:kk + 1] * rows
    o_ref[...] = acc


def moe_combine(x1, gate, y_sorted, dest, tm=256):
    n, d = x1.shape
    n_rows, w = y_sorted.shape
    nblk = n // tm
    dest3 = dest.reshape(nblk, 1, tm * TOP_K)
    return pl.pallas_call(
        _combine_kernel,
        grid=(nblk,),
        in_specs=[pl.BlockSpec((1, 1, TOP_K * tm), lambda i: (0, 0, 0), memory_space=pltpu.SMEM),
                  pl.BlockSpec((1, 1, TOP_K * tm), lambda i: (jnp.minimum(i + 1, nblk - 1), 0, 0),
                               memory_space=pltpu.SMEM),
                  pl.BlockSpec((tm, d), lambda i: (i, 0)),
                  pl.BlockSpec((tm, TOP_K), lambda i: (i, 0)),
                  pl.BlockSpec(memory_space=pl.ANY),
                  pl.BlockSpec(memory_space=pl.ANY)],
        out_specs=pl.BlockSpec((tm, d), lambda i: (i, 0)),
        out_shape=jax.ShapeDtypeStruct((n, d), F32),
        scratch_shapes=[pltpu.VMEM((2, TOP_K * tm // SUBLANES, SUBLANES, w), y_sorted.dtype),
                        pltpu.SemaphoreType.DMA((2,))],
        compiler_params=_cparams(("arbitrary",)),
        name="moe_combine",
    )(dest3, dest3, x1, gate, y_sorted, y_sorted.reshape(n_rows // SUBLANES, SUBLANES, w))


def kernel(x, mem, positions, norm_mix, w_in, b_gate, a_q_gain, a_k_gain, r_mu, r_w0, r_w2, r_a0, r_a2,
           r_g2, r_k_k, r_k_a, r_r_k, r_ln_w, r_ln_b, mem_norm, w_mem_kv, c_q_gain, c_k_gain, w_branch,
           w_out, norm_ffn, router_w, router_b, exp_w1, exp_b1, exp_w2, exp_b2):
    b, s, d = x.shape
    n = b * s
    depth = norm_mix.shape[0]
    n_groups = len(A_GROUPS)
    qkv_cols = n_groups * A_WIDTH
    off_k, off_v, off_r = qkv_cols, 2 * qkv_cols, 3 * qkv_cols
    off_cq = off_r + R_COLS
    off_gate = off_cq + C_WIDTH

    x2d = x.reshape(n, d)
    cos_t, sin_t = rope_tables(positions.reshape(n, 1).astype(jnp.int32))
    for l in range(depth):
        w_l = w_in[l]
        h = rmsnorm_rows(x2d, norm_mix[l])
        qk_gains = rotary_gains(a_q_gain[l], a_k_gain[l])
        zr = project(h, w_l[:, off_r:off_cq], "plain", out_dtype=F32, tn=R_COLS // 2)
        cq = project(h, w_l[:, off_cq:off_gate], "headnorm", (c_q_gain[l].reshape(1, HEAD_DIM),))
        gates = project(h, w_l[:, off_gate:], "gate", (b_gate[l].reshape(1, -1),), tn=2 * A_WIDTH)

        shp = lambda t: t.reshape(b, s, -1)
        outs, lses = [], []
        for g, (window, dilation) in enumerate(A_GROUPS):
            assert window // dilation == ATT_BLOCK
            cols = slice(g * A_WIDTH, (g + 1) * A_WIDTH)
            w_g = jnp.concatenate([w_l[:, :off_k][:, cols], w_l[:, off_k:off_v][:, cols],
                                   w_l[:, off_v:off_r][:, cols]], axis=1)
            qkv = project_qkv(h, w_g, qk_gains, cos_t, sin_t, dilation)
            o, lse = band_attention_group(qkv, b, g, dilation)
            outs.append(o)
            lses.append(lse)

        yb = rwkv7_mix(shp(zr), r_mu[l], r_w0[l], r_w2[l], r_a0[l], r_a2[l], r_g2[l], r_k_k[l], r_k_a[l],
                       r_r_k[l].reshape(-1), r_ln_w[l], r_ln_b[l]).reshape(n, R_WIDTH)

        mlen = mem.shape[1]
        mem_n = rmsnorm_rows(mem.reshape(b * mlen, d), mem_norm[l])
        wkv = w_mem_kv[l]
        ck = project(mem_n, wkv[:, :C_WIDTH], "headnorm", (c_k_gain[l].reshape(1, HEAD_DIM),))
        cv = project(mem_n, wkv[:, C_WIDTH:], "plain")
        yc = cross_attention(shp(cq), ck.reshape(b, mlen, C_WIDTH), cv.reshape(b, mlen, C_WIDTH)).reshape(n, C_WIDTH)

        x1, h2p, top_idx, gate, rank, counts = merge_and_route(
            x2d, outs, lses, yb, yc, gates, w_branch[l], w_out[l], norm_ffn[l], router_w[l], router_b[l])
        pad_start, block_expert, zero_flag, n_used, next_expert, run_parity = block_layout(counts, n * TOP_K)
        dest = assignment_rows(top_idx, rank, pad_start)
        x_sorted = scatter_rows(h2p, dest, zero_flag, block_expert.shape[0] * MOE_ROWS)
        y_sorted = expert_ffn(x_sorted, block_expert, n_used, next_expert, run_parity,
                              exp_w1[l], exp_b1[l], exp_w2[l], exp_b2[l])
        x2d = moe_combine(x1, gate, y_sorted, dest)
    return x2d.reshape(b, s, d)
```

```python
import functools

import jax
import jax.numpy as jnp
from jax import lax
from jax.experimental import pallas as pl
from jax.experimental.pallas import tpu as pltpu

F32 = jnp.float32
BF16 = jnp.bfloat16

NORM_EPS = 1e-6
LANES = 128
SUBLANES = 8
HEAD_DIM = 128
A_GROUPS = ((128, 1), (512, 4), (2048, 16))
A_HEADS = 4
A_WIDTH = A_HEADS * HEAD_DIM
ATT_BLOCK = 128
ATT_QB = 4
ROT_DIM = 32
ROPE_THETA = 500000.0
R_HEAD = 64
R_HEADS = 8
R_WIDTH = R_HEADS * R_HEAD
R_DECAY_LORA = 64
R_AAA_LORA = 64
R_GATE_LORA = 128
R_COLS = 3 * R_WIDTH + R_DECAY_LORA + R_AAA_LORA + R_GATE_LORA
R_GN_EPS = 64e-5
R_CHUNK = 64
R_CPI = 4
R_SEG = 256
C_HEADS = 4
C_WIDTH = C_HEADS * HEAD_DIM
N_BRANCH = 3
N_EXPERTS = 32
TOP_K = 4
SWIGLU_ALPHA = 1.702
SWIGLU_LIMIT = 7.0
MOE_ROWS = 256
NEG_BIG = -1e30

VMEM_LIMIT = 56 * 1024 * 1024


def _cparams(sem):
    return pltpu.CompilerParams(dimension_semantics=sem, vmem_limit_bytes=VMEM_LIMIT)


def _rmsnorm_kernel(x_ref, g_ref, *refs):
    o_ref = refs[-3] if len(refs) > 1 else refs[0]
    x = x_ref[...]
    ms = jnp.mean(x * x, axis=-1, keepdims=True)
    o_ref[...] = (x * lax.rsqrt(ms + NORM_EPS) * g_ref[...]).astype(o_ref.dtype)
    if len(refs) > 1:
        pos_ref, freq_ref, _, cos_ref, sin_ref = refs
        ang = pos_ref[...].astype(F32) * freq_ref[...]
        lane = lax.broadcasted_iota(jnp.int32, ang.shape, 1)
        cos_ref[...] = jnp.cos(ang)
        s = jnp.sin(ang)
        half = ROT_DIM // 2
        sin_ref[...] = jnp.where(lane < half, -s, jnp.where(lane < ROT_DIM, s, 0.0))


def rmsnorm_rows(x2d, gain, pos_col=None, tm=512):
    n, d = x2d.shape
    tm = min(tm, n)
    rows = lambda w: pl.BlockSpec((tm, w), lambda i: (i, 0))
    in_specs = [rows(d), pl.BlockSpec((1, d), lambda i: (0, 0))]
    out_specs, out_shape, extra = rows(d), jax.ShapeDtypeStruct((n, d), BF16), ()
    if pos_col is not None:
        half = ROT_DIM // 2
        inv_freq = ROPE_THETA ** (-jnp.arange(half, dtype=F32) / half)
        freq_row = jnp.concatenate([inv_freq, inv_freq, jnp.zeros((LANES - ROT_DIM,), F32)]).reshape(1, LANES)
        in_specs += [rows(1), pl.BlockSpec((1, LANES), lambda i: (0, 0))]
        out_specs = [out_specs, rows(LANES), rows(LANES)]
        out_shape = [out_shape] + [jax.ShapeDtypeStruct((n, LANES), F32)] * 2
        extra = (pos_col, freq_row)
    return pl.pallas_call(
        _rmsnorm_kernel,
        grid=(n // tm,),
        in_specs=in_specs,
        out_specs=out_specs,
        out_shape=out_shape,
        compiler_params=_cparams(("arbitrary",)),
        name="rmsnorm_rows",
    )(x2d, gain.reshape(1, d), *extra)


def _head_mean_sq(zh):
    avg = jnp.full((HEAD_DIM, HEAD_DIM), 1.0 / HEAD_DIM, BF16)
    return jnp.dot((zh * zh).astype(BF16), avg, preferred_element_type=F32)


def _cast_weight_once(w_ref, w_scr):
    @pl.when(pl.program_id(1) == 0)
    def _():
        w_scr[...] = w_ref[...].astype(w_scr.dtype)


def _proj_kernel(*refs, mode):
    h_ref, w_ref = refs[0], refs[1]
    o_ref, w_scr = refs[-2], refs[-1]
    _cast_weight_once(w_ref, w_scr)
    z = jnp.dot(h_ref[...], w_scr[...], preferred_element_type=F32)
    if mode == "plain":
        o_ref[...] = z.astype(o_ref.dtype)
    elif mode == "gate":
        o_ref[...] = (0.5 * jnp.tanh(0.5 * (z + refs[2][...])) + 0.5).astype(o_ref.dtype)
    else:
        gain = refs[2][...]
        for c in range(z.shape[1] // HEAD_DIM):
            zh = z[:, c * HEAD_DIM:(c + 1) * HEAD_DIM]
            zn = zh * lax.rsqrt(_head_mean_sq(zh) + NORM_EPS) * gain
            o_ref[:, c * HEAD_DIM:(c + 1) * HEAD_DIM] = zn.astype(o_ref.dtype)


def project(h, w, mode, extras=(), out_dtype=BF16, tm=2048, tn=512):
    n, k = h.shape
    m = w.shape[1]
    tm = min(tm, n)
    tn = min(tn, m)
    assert n % tm == 0 and m % tn == 0
    in_specs = [pl.BlockSpec((tm, k), lambda j, i: (i, 0)),
                pl.BlockSpec((k, tn), lambda j, i: (0, j))]
    if mode == "gate":
        in_specs.append(pl.BlockSpec((1, tn), lambda j, i: (0, j)))
    elif mode == "headnorm":
        in_specs.append(pl.BlockSpec((1, HEAD_DIM), lambda j, i: (0, 0)))
    return pl.pallas_call(
        functools.partial(_proj_kernel, mode=mode),
        grid=(m // tn, n // tm),
        in_specs=in_specs,
        out_specs=pl.BlockSpec((tm, tn), lambda j, i: (i, j)),
        out_shape=jax.ShapeDtypeStruct((n, m), out_dtype),
        scratch_shapes=[pltpu.VMEM((k, tn), BF16)],
        compiler_params=_cparams(("arbitrary", "arbitrary")),
        name="proj_" + mode,
    )(h, w, *extras)


def _qkv_proj_kernel(h_ref, w_ref, gain_ref, cos_ref, sin_ref, o_ref, w_scr, z_scr, *, dilation):
    j = pl.program_id(0)
    _cast_weight_once(w_ref, w_scr)
    z = jnp.dot(h_ref[...], w_scr[...], preferred_element_type=F32)
    half = ROT_DIM // 2

    pair = 2 * HEAD_DIM

    def emit(c0, slab):
        if dilation == 1:
            o_ref[:, c0 * HEAD_DIM:c0 * HEAD_DIM + pair] = slab.astype(o_ref.dtype)
        else:
            z_scr[c0] = slab[:, :HEAD_DIM]
            z_scr[c0 + 1] = slab[:, HEAD_DIM:]

    @pl.when(j < 2)
    def _():
        mi = lax.broadcasted_iota(jnp.int32, (pair, pair), 0)
        li = lax.broadcasted_iota(jnp.int32, (pair, pair), 1)
        same_head = (mi // HEAD_DIM) == (li // HEAD_DIM)
        lh, mh = li % HEAD_DIM, mi % HEAD_DIM
        avg = jnp.where(same_head, 1.0 / HEAD_DIM, 0.0).astype(BF16)
        perm = (same_head & (((lh < half) & (mh == lh + half))
                             | ((lh >= half) & (lh < ROT_DIM) & (mh == lh - half)))).astype(BF16)
        gcos = gain_ref[0, 0:1, :] * cos_ref[...]
        gsin = gain_ref[0, 1:2, :] * sin_ref[...]
        gcos = jnp.concatenate([gcos, gcos], axis=1)
        gsin = jnp.concatenate([gsin, gsin], axis=1)
        for c0 in range(0, A_HEADS, 2):
            zz = z[:, c0 * HEAD_DIM:c0 * HEAD_DIM + pair]
            ms = jnp.dot((zz * zz).astype(BF16), avg, preferred_element_type=F32)
            partner = jnp.dot(zz.astype(BF16), perm, preferred_element_type=F32)
            emit(c0, lax.rsqrt(ms + NORM_EPS) * (zz * gcos + partner * gsin))

    @pl.when(j == 2)
    def _():
        for c0 in range(0, A_HEADS, 2):
            emit(c0, z[:, c0 * HEAD_DIM:c0 * HEAD_DIM + pair])

    if dilation > 1:
        rows = z_scr.shape[1] // dilation
        for r in range(dilation):
            for c in range(A_HEADS):
                lo = r * A_WIDTH + c * HEAD_DIM
                o_ref[:, lo:lo + HEAD_DIM] = z_scr[c, pl.ds(r, rows, stride=dilation), :].astype(o_ref.dtype)


def rotary_gains(q_gain, k_gain):
    half = ROT_DIM // 2
    lane = jnp.arange(HEAD_DIM)
    partner = jnp.where(lane < half, lane + half, jnp.where(lane < ROT_DIM, lane - half, lane))
    return jnp.stack([jnp.stack([g, g[partner]]) for g in (q_gain, k_gain)])


def project_qkv(h, w_qkv, gains, cos_t, sin_t, dilation, tm=2048):
    n, k = h.shape
    d = dilation
    return pl.pallas_call(
        functools.partial(_qkv_proj_kernel, dilation=d),
        grid=(3, n // tm),
        in_specs=[pl.BlockSpec((tm, k), lambda j, i: (i, 0)),
                  pl.BlockSpec((k, A_WIDTH), lambda j, i: (0, j)),
                  pl.BlockSpec((1, 2, HEAD_DIM), lambda j, i: (jnp.minimum(j, 1), 0, 0)),
                  pl.BlockSpec((tm, LANES), lambda j, i: (i, 0)),
                  pl.BlockSpec((tm, LANES), lambda j, i: (i, 0))],
        out_specs=pl.BlockSpec((tm // d, d * A_WIDTH), lambda j, i: (i, j)),
        out_shape=jax.ShapeDtypeStruct((n // d, 3 * d * A_WIDTH), BF16),
        scratch_shapes=[pltpu.VMEM((k, A_WIDTH), BF16), pltpu.VMEM((A_HEADS, tm, HEAD_DIM), F32)],
        compiler_params=_cparams(("arbitrary", "arbitrary")),
        name=f"proj_qkv_d{d}",
    )(h, w_qkv, gains, cos_t, sin_t)


def _band_attn_kernel(*refs, qb):
    q_ref = refs[0]
    k_refs = refs[1:qb + 2]
    v_refs = refs[qb + 2:2 * qb + 3]
    o_ref, lse_ref = refs[-2:]
    step = pl.program_id(2)
    scale = HEAD_DIM ** -0.5
    nq = ATT_BLOCK
    qi = lax.broadcasted_iota(jnp.int32, (nq, 2 * nq), 0)
    ki = lax.broadcasted_iota(jnp.int32, (nq, 2 * nq), 1)
    rel = qi + nq - ki
    band = (rel >= 0) & (rel <= nq)
    for a in range(qb):
        blk = step * qb + a
        valid = band & ((blk * nq - nq + ki) >= 0)
        rows = slice(a * nq, (a + 1) * nq)
        lses = []
        for h in range(A_HEADS):
            sl = slice(h * HEAD_DIM, (h + 1) * HEAD_DIM)
            qh = q_ref[0, rows, sl]
            kh = jnp.concatenate([k_refs[a][0, :, sl], k_refs[a + 1][0, :, sl]], axis=0)
            vh = jnp.concatenate([v_refs[a][0, :, sl], v_refs[a + 1][0, :, sl]], axis=0)
            s = lax.dot_general(qh, kh, (((1,), (1,)), ((), ())), preferred_element_type=F32) * scale
            s = jnp.where(valid, s, NEG_BIG)
            m = jnp.max(s, axis=-1, keepdims=True)
            p = jnp.exp(s - m)
            l = jnp.sum(p, axis=-1, keepdims=True)
            o = jnp.dot(p.astype(BF16), vh, preferred_element_type=F32) / l
            o_ref[0, rows, sl] = o.astype(o_ref.dtype)
            lses.append(jnp.broadcast_to(m + jnp.log(l), (nq, LANES // A_HEADS)))
        lse_ref[0, rows, :] = jnp.concatenate(lses, axis=1)


def band_attention_group(qkv, b, g, dilation):
    d = dilation
    sub = qkv.shape[0] // b
    nblk = sub // ATT_BLOCK
    qb = min(ATT_QB, nblk)
    assert nblk % qb == 0
    view = qkv.reshape(b, sub, 3 * d * A_WIDTH)
    qrows = qb * ATT_BLOCK
    key_spec = lambda t, m: pl.BlockSpec(
        (1, ATT_BLOCK, A_WIDTH), lambda bi, r, j: (bi, jnp.maximum(j * qb - 1 + m, 0), t * d + r))
    o, lse = pl.pallas_call(
        functools.partial(_band_attn_kernel, qb=qb),
        grid=(b, d, nblk // qb),
        in_specs=[pl.BlockSpec((1, qrows, A_WIDTH), lambda bi, r, j: (bi, j, r))]
                 + [key_spec(1, m) for m in range(qb + 1)] + [key_spec(2, m) for m in range(qb + 1)],
        out_specs=[pl.BlockSpec((1, qrows, A_WIDTH), lambda bi, r, j: (bi, j, r)),
                   pl.BlockSpec((1, qrows, LANES), lambda bi, r, j: (bi, j, r))],
        out_shape=[jax.ShapeDtypeStruct((b, sub, d * A_WIDTH), BF16),
                   jax.ShapeDtypeStruct((b, sub, d * LANES), F32)],
        compiler_params=_cparams(("arbitrary", "arbitrary", "arbitrary")),
        name=f"band_attn_g{g}",
    )(*([view] * (2 * qb + 3)))
    return o.reshape(b * sub, d * A_WIDTH), lse.reshape(b * sub, d * LANES)


def _cross_attn_kernel(q_ref, k_ref, v_ref, o_ref):
    scale = HEAD_DIM ** -0.5
    for h in range(C_HEADS):
        sl = slice(h * HEAD_DIM, (h + 1) * HEAD_DIM)
        s = lax.dot_general(q_ref[0, :, sl], k_ref[0, :, sl], (((1,), (1,)), ((), ())),
                            preferred_element_type=F32) * scale
        m = jnp.max(s, axis=-1, keepdims=True)
        p = jnp.exp(s - m)
        l = jnp.sum(p, axis=-1, keepdims=True)
        o = jnp.dot(p.astype(BF16), v_ref[0, :, sl], preferred_element_type=F32) / l
        o_ref[0, :, sl] = o.astype(o_ref.dtype)


def cross_attention(qn, kn, v, tm=512):
    b, s, w = qn.shape
    m = kn.shape[1]
    return pl.pallas_call(
        _cross_attn_kernel,
        grid=(b, s // tm),
        in_specs=[pl.BlockSpec((1, tm, w), lambda bi, i: (bi, i, 0)),
                  pl.BlockSpec((1, m, w), lambda bi, i: (bi, 0, 0)),
                  pl.BlockSpec((1, m, w), lambda bi, i: (bi, 0, 0))],
        out_specs=pl.BlockSpec((1, tm, w), lambda bi, i: (bi, i, 0)),
        out_shape=jax.ShapeDtypeStruct((b, s, w), BF16),
        compiler_params=_cparams(("arbitrary", "arbitrary")),
        name="cross_attn",
    )(qn, kn, v)


def _head_sums(x, seg):
    w = seg.shape[0]
    x16 = x.astype(BF16)
    return jnp.concatenate(
        [jnp.dot(x16[:, j:j + w], seg, preferred_element_type=F32) for j in range(0, x.shape[1], w)], axis=1)


def _dot_t(a, b):
    return lax.dot_general(a, b, (((0,), (0,)), ((), ())), preferred_element_type=F32)


def _dot_nt(a, b):
    return lax.dot_general(a, b, (((1,), (1,)), ((), ())), preferred_element_type=F32)


def _rwkv_kernel(zr_ref, mu_ref, w0_ref, wwa_ref, a0_ref, g2_ref, kk_ref, ka_ref, rk_ref,
                 lnw_ref, lnb_ref, seg_ref, y_ref,
                 state_ref, carry_ref, ops_ref, yh_ref):
    t = pl.program_id(1)
    tt = zr_ref.shape[1]
    nch = tt // R_CHUNK
    c = R_CHUNK

    @pl.when(t == 0)
    def _():
        state_ref[...] = jnp.zeros_like(state_ref)
        carry_ref[...] = jnp.zeros_like(carry_ref)

    z = zr_ref[0]
    row = lax.broadcasted_iota(jnp.int32, z.shape, 0)
    prev = jnp.where(row == 0, carry_ref[...], pltpu.roll(z, 1, 0))
    carry_ref[...] = z[tt - 1:tt, :]
    xs = z + (prev - z) * mu_ref[...]

    w3 = 3 * R_WIDTH
    r = xs[:, 0:R_WIDTH]
    k = xs[:, R_WIDTH:2 * R_WIDTH]
    v = xs[:, 2 * R_WIDTH:w3]
    wa_lo = xs[:, w3:w3 + LANES]
    g_lo = xs[:, w3 + LANES:w3 + 2 * LANES]
    lane = lax.broadcasted_iota(jnp.int32, wa_lo.shape, 1)
    wa_in = jnp.where(lane < R_DECAY_LORA, jnp.tanh(wa_lo), wa_lo)
    wa = jnp.dot(wa_in.astype(BF16), wwa_ref[...], preferred_element_type=F32)
    u = -(w0_ref[...] + wa[:, :R_WIDTH])
    softplus = jnp.maximum(u, 0.0) + jnp.log(1.0 + jnp.exp(-jnp.abs(u)))
    w_raw = -softplus - 0.5
    ld = -jnp.exp(w_raw)
    a = jax.nn.sigmoid(a0_ref[...] + wa[:, R_WIDTH:])
    g = jnp.dot(jax.nn.sigmoid(g_lo).astype(BF16), g2_ref[...], preferred_element_type=F32)

    seg = seg_ref[...]
    kk = k * kk_ref[...]
    kk = kk * jnp.minimum(lax.rsqrt(_head_sums(kk * kk, seg)), 1e12)
    k2 = k * (1.0 + (a - 1.0) * ka_ref[...])
    bonus = _head_sums(r * k2 * rk_ref[...], seg) * v

    ri = lax.broadcasted_iota(jnp.int32, (c, c), 0)
    ci = lax.broadcasted_iota(jnp.int32, (c, c), 1)
    tri = (ci <= ri).astype(BF16)
    ld_hi = ld.astype(BF16)
    ld_lo = (ld - ld_hi.astype(F32)).astype(BF16)
    lcs = []
    for ch in range(nch):
        rs = slice(ch * c, (ch + 1) * c)
        lcs.append(jnp.dot(tri, ld_hi[rs], preferred_element_type=F32)
                   + jnp.dot(tri, ld_lo[rs], preferred_element_type=F32))
    lc = jnp.concatenate(lcs, axis=0)
    e_inc = jnp.exp(lc)
    e_exc = jnp.exp(lc - ld)
    e_inv = jnp.exp(-lc)
    a_t = -kk * e_exc
    r_t = r * e_inc
    b_t = kk * a * e_inv
    k_t = k2 * e_inv
    for h in range(R_HEADS):
        hs = slice(h * R_HEAD, (h + 1) * R_HEAD)
        ops_ref[0, h] = a_t[:, hs]
        ops_ref[1, h] = r_t[:, hs]
        ops_ref[2, h] = b_t[:, hs]
        ops_ref[3, h] = k_t[:, hs]
        ops_ref[4, h] = v[:, hs]
        ops_ref[5, h] = e_inc[:, hs]

    strict = ci < ri
    incl = ci <= ri
    eye = (ci == ri)

    ri2 = lax.broadcasted_iota(jnp.int32, (c, 2 * c), 0)
    ci2 = lax.broadcasted_iota(jnp.int32, (c, 2 * c), 1)
    incl2 = jnp.bitwise_and(ci2, c - 1) <= ri2
    eye_f = jnp.where(eye, 1.0, 0.0)
    heads = range(R_HEADS)
    dot = functools.partial(jnp.dot, preferred_element_type=F32)

    def chunk_body(ch, _):
        starts = [pl.multiple_of((ch * R_CPI + sub) * c, c) for sub in range(R_CPI)]
        rows = [pl.ds(r0, c) for r0 in starts]
        items = [(sub, h) for sub in range(R_CPI) for h in heads]
        idx = range(len(items))
        at = [ops_ref[0, h, rows[sub], :] for sub, h in items]
        rt = [ops_ref[1, h, rows[sub], :] for sub, h in items]
        bt = [ops_ref[2, h, rows[sub], :] for sub, h in items]
        kt = [ops_ref[3, h, rows[sub], :] for sub, h in items]
        pc = [ops_ref[5, h, pl.ds(starts[sub] + c - 1, 1), :] for sub, h in items]
        at16 = [x.astype(BF16) for x in at]
        rt16 = [x.astype(BF16) for x in rt]
        bt16 = [x.astype(BF16) for x in bt]
        kt16 = [x.astype(BF16) for x in kt]
        v16 = [ops_ref[4, h, rows[sub], :].astype(BF16) for sub, h in items]
        bk16 = [jnp.concatenate([bt16[i], kt16[i]], axis=0) for i in idx]
        nmat = [jnp.where(strict, _dot_nt(at16[i], bt16[i]), 0.0) for i in idx]
        a_ak = [jnp.where(strict, _dot_nt(at16[i], kt16[i]), 0.0).astype(BF16) for i in idx]
        a_rbk = [jnp.where(incl2, _dot_nt(rt16[i], bk16[i]), 0.0).astype(BF16) for i in idx]
        npow = nmat
        tinv = [eye_f + nmat[i] for i in idx]
        for _i in range(5):
            np16 = [x.astype(BF16) for x in npow]
            npow = [dot(np16[i], np16[i]) for i in idx]
            tinv = [tinv[i] + dot(tinv[i].astype(BF16), npow[i].astype(BF16)) for i in idx]
        akv = [dot(a_ak[i], v16[i]).astype(BF16) for i in idx]
        apw1 = [dot(tinv[i].astype(BF16), jnp.concatenate([at16[i], akv[i]], axis=1)).astype(BF16)
                for i in idx]
        zero = jnp.zeros((c, R_HEAD), BF16)
        rhs2 = [jnp.concatenate([apw1[i], jnp.concatenate([zero, v16[i]], axis=1)], axis=0)
                for i in idx]
        bkh = [jnp.concatenate([bt[i] * pc[i], kt[i] * pc[i]], axis=0).astype(BF16) for i in idx]
        gh = [_dot_t(bkh[i], rhs2[i]) for i in idx]
        qy = [dot(a_rbk[i], rhs2[i]) for i in idx]
        for i, (sub, h) in enumerate(items):
            gm = jnp.where(eye, jnp.broadcast_to(pc[i], (c, c)), 0.0) + gh[i][:, :R_HEAD]
            qp = rt[i] + qy[i][:, :R_HEAD]
            st = state_ref[h]
            res = dot(jnp.concatenate([qp, gm], axis=0).astype(BF16), st.astype(BF16))
            yh_ref[h, rows[sub], :] = res[:c] + qy[i][:, R_HEAD:]
            state_ref[h] = res[c:] + gh[i][:, R_HEAD:]
        return 0

    lax.fori_loop(0, nch // R_CPI, chunk_body, 0)

    y = jnp.concatenate([yh_ref[h] for h in range(R_HEADS)], axis=1)
    mean = _head_sums(y, seg) * (1.0 / R_HEAD)
    dlt = y - mean
    var = _head_sums(dlt * dlt, seg) * (1.0 / R_HEAD)
    yn = dlt * lax.rsqrt(var + R_GN_EPS) * lnw_ref[...] + lnb_ref[...]
    y_ref[0] = ((yn + bonus) * g).astype(y_ref.dtype)


def rwkv7_mix(zr, mu, w0, w2, a0, a2, g2, k_k, k_a, r_k, ln_w, ln_b, tt=256):
    b, s, cols = zr.shape
    row = lambda x: x.reshape(1, -1).astype(F32)
    wwa = jnp.zeros((LANES, 2 * R_WIDTH), F32)
    wwa = wwa.at[:R_DECAY_LORA, :R_WIDTH].set(w2).at[R_DECAY_LORA:, R_WIDTH:].set(a2).astype(BF16)
    hid = jnp.arange(R_SEG) // R_HEAD
    seg = (hid[:, None] == hid[None, :]).astype(BF16)
    full = lambda shape: pl.BlockSpec(shape, lambda bi, t: (0,) * len(shape))
    return pl.pallas_call(
        _rwkv_kernel,
        grid=(b, s // tt),
        in_specs=[pl.BlockSpec((1, tt, cols), lambda bi, t: (bi, t, 0)),
                  full((1, cols)), full((1, R_WIDTH)), full((LANES, 2 * R_WIDTH)), full((1, R_WIDTH)),
                  full((R_GATE_LORA, R_WIDTH)), full((1, R_WIDTH)), full((1, R_WIDTH)), full((1, R_WIDTH)),
                  full((1, R_WIDTH)), full((1, R_WIDTH)), full((R_SEG, R_SEG))],
        out_specs=pl.BlockSpec((1, tt, R_WIDTH), lambda bi, t: (bi, t, 0)),
        out_shape=jax.ShapeDtypeStruct((b, s, R_WIDTH), BF16),
        scratch_shapes=[pltpu.VMEM((R_HEADS, R_HEAD, R_HEAD), F32),
                        pltpu.VMEM((1, cols), F32),
                        pltpu.VMEM((6, R_HEADS, tt, R_HEAD), F32),
                        pltpu.VMEM((R_HEADS, tt, R_HEAD), F32)],
        compiler_params=_cparams(("arbitrary", "arbitrary")),
        name="rwkv7_mix",
    )(zr, row(mu), row(w0), wwa, row(a0), g2.astype(BF16), row(k_k), row(k_a), row(r_k),
      row(ln_w), row(ln_b), seg)


def _merge_kernel(x_ref, o0_ref, o1_ref, o2_ref, l0_ref, l1_ref, l2_ref, yb_ref, yc_ref, gt_ref,
                  wb_ref, wo_ref, gn_ref, rw_ref, rb_ref,
                  x1_ref, h2_ref, idx_ref, gate_ref, rank_ref, cnt_ref, base_ref, o_scr, l_scr):
    @pl.when(pl.program_id(0) == 0)
    def _():
        base_ref[...] = jnp.zeros_like(base_ref)

    tm_rows = x_ref.shape[0]
    for gi, (o_ref, l_ref) in enumerate(((o0_ref, l0_ref), (o1_ref, l1_ref), (o2_ref, l2_ref))):
        dil = A_GROUPS[gi][1]
        for r in range(dil):
            dst = pl.ds(r, tm_rows // dil, stride=dil) if dil > 1 else slice(None)
            for h in range(A_HEADS):
                lo = r * A_WIDTH + h * HEAD_DIM
                o_scr[gi, h, dst, :] = o_ref[:, lo:lo + HEAD_DIM].astype(F32)
            l_scr[gi, dst, :] = l_ref[:, r * LANES:(r + 1) * LANES]
    lses = [l_scr[gi] for gi in range(3)]
    lmax = jnp.maximum(jnp.maximum(lses[0], lses[1]), lses[2])
    es = [jnp.exp(l - lmax) for l in lses]
    inv = 1.0 / (es[0] + es[1] + es[2])
    qw = LANES // A_HEADS
    heads = []
    for h in range(A_HEADS):
        sl = slice(h * HEAD_DIM, (h + 1) * HEAD_DIM)
        acc = None
        for gi in range(3):
            alpha = (es[gi] * inv)[:, h * qw:h * qw + 1]
            term = alpha * o_scr[gi, h]
            acc = term if acc is None else acc + term
        heads.append(acc)
    ya = jnp.concatenate(heads, axis=1).astype(BF16)
    d = x_ref.shape[1]
    merged = None
    for n, yn in enumerate((ya, yb_ref[...], yc_ref[...])):
        proj = jnp.dot(yn, wb_ref[n], preferred_element_type=F32)
        term = gt_ref[:, n * d:(n + 1) * d].astype(F32) * proj
        merged = term if merged is None else merged + term
    x1 = x_ref[...] + jnp.dot(merged.astype(BF16), wo_ref[...], preferred_element_type=F32)
    x1_ref[...] = x1
    ms = jnp.mean(x1 * x1, axis=-1, keepdims=True)
    h2 = x1 * lax.rsqrt(ms + NORM_EPS) * gn_ref[...]
    h2_ref[...] = h2
    rw = rw_ref[...]
    h2_hi, rw_hi = h2.astype(BF16), rw.astype(BF16)
    h2_lo = (h2 - h2_hi.astype(F32)).astype(BF16)
    rw_lo = (rw - rw_hi.astype(F32)).astype(BF16)
    logits = (jnp.dot(h2_hi, rw_hi, preferred_element_type=F32) + jnp.dot(h2_hi, rw_lo, preferred_element_type=F32)
              + jnp.dot(h2_lo, rw_hi, preferred_element_type=F32)) + rb_ref[...]
    tm = logits.shape[0]
    lane = lax.broadcasted_iota(jnp.int32, logits.shape, 1)
    vals, idxs = [], []
    cur = logits
    for _k in range(TOP_K):
        m = jnp.max(cur, axis=-1, keepdims=True)
        ik = jnp.min(jnp.where(cur == m, lane, N_EXPERTS), axis=-1, keepdims=True)
        vals.append(m)
        idxs.append(ik)
        cur = jnp.where(lane == ik, -jnp.inf, cur)
    exps = [jnp.exp(vk - vals[0]) for vk in vals]
    tot = exps[0] + exps[1] + exps[2] + exps[3]
    onehots = [lane == ik for ik in idxs]
    hits = sum(jnp.where(oh, 1.0, 0.0) for oh in onehots)
    ri = lax.broadcasted_iota(jnp.int32, (tm, tm), 0)
    ci = lax.broadcasted_iota(jnp.int32, (tm, tm), 1)
    before = jnp.dot((ci < ri).astype(BF16), hits.astype(BF16), preferred_element_type=F32) + base_ref[...]
    for kk in range(TOP_K):
        idx_ref[:, kk:kk + 1] = idxs[kk]
        gate_ref[:, kk:kk + 1] = exps[kk] / tot
        rank_ref[:, kk:kk + 1] = jnp.sum(jnp.where(onehots[kk], before, 0.0), axis=-1,
                                         keepdims=True).astype(jnp.int32)
    base_ref[...] = base_ref[...] + jnp.sum(hits, axis=0, keepdims=True)
    cnt_ref[...] = base_ref[...].astype(jnp.int32)


def merge_and_route(x2d, outs, lses, yb, yc, gates, w_branch, w_out, norm_ffn, router_w, router_b, tm=512):
    n, d = x2d.shape
    rows = lambda w: pl.BlockSpec((tm, w), lambda i: (i, 0))
    packed = lambda w, dil: pl.BlockSpec((tm // dil, dil * w), lambda i: (i, 0))
    full = lambda shape: pl.BlockSpec(shape, lambda i: (0,) * len(shape))
    return pl.pallas_call(
        _merge_kernel,
        grid=(n // tm,),
        in_specs=[rows(d)] + [packed(A_WIDTH, dil) for _w, dil in A_GROUPS] + [packed(LANES, dil) for _w, dil in A_GROUPS]
                 + [rows(R_WIDTH), rows(C_WIDTH), rows(N_BRANCH * d),
                    full((N_BRANCH, A_WIDTH, d)), full((d, d)), full((1, d)), full((d, N_EXPERTS)), full((1, N_EXPERTS))],
        out_specs=[rows(d), rows(d), rows(TOP_K), rows(TOP_K), rows(TOP_K), full((1, N_EXPERTS))],
        out_shape=[jax.ShapeDtypeStruct((n, d), F32), jax.ShapeDtypeStruct((n, d), F32),
                   jax.ShapeDtypeStruct((n, TOP_K), jnp.int32), jax.ShapeDtypeStruct((n, TOP_K), F32),
                   jax.ShapeDtypeStruct((n, TOP_K), jnp.int32), jax.ShapeDtypeStruct((1, N_EXPERTS), jnp.int32)],
        scratch_shapes=[pltpu.VMEM((1, N_EXPERTS), F32), pltpu.VMEM((len(A_GROUPS), A_HEADS, tm, HEAD_DIM), F32),
                        pltpu.VMEM((len(A_GROUPS), tm, LANES), F32)],
        compiler_params=_cparams(("arbitrary",)),
        name="merge_route",
    )(x2d, *outs, *lses, yb, yc, gates, w_branch.astype(BF16), w_out.astype(BF16),
      norm_ffn.reshape(1, d), router_w, router_b.reshape(1, N_EXPERTS))


def block_layout(counts, n_assign):
    counts = counts.reshape(-1)
    padded = (counts + MOE_ROWS - 1) // MOE_ROWS * MOE_ROWS
    pad_end = jnp.cumsum(padded)
    pad_start = (pad_end - padded).astype(jnp.int32)
    n_blocks = -(-n_assign // MOE_ROWS) + N_EXPERTS
    blk_row = jnp.arange(n_blocks, dtype=jnp.int32) * MOE_ROWS
    owner = jnp.sum((blk_row[:, None] >= pad_end[None, :]).astype(jnp.int32), axis=1)
    block_expert = jnp.minimum(owner, N_EXPERTS - 1).astype(jnp.int32)
    unused = blk_row >= pad_end[-1]
    zero_flag = (unused | (blk_row + MOE_ROWS == pad_end[block_expert])).astype(jnp.int32)
    n_used = (pad_end[-1:] // MOE_ROWS).astype(jnp.int32)
    has_rows = counts > 0
    eid = jnp.arange(N_EXPERTS, dtype=jnp.int32)
    later = jnp.where(has_rows[None, :] & (eid[None, :] > eid[:, None]), eid[None, :], N_EXPERTS)
    next_expert = jnp.min(later, axis=1)
    next_expert = jnp.where(next_expert == N_EXPERTS, -1, next_expert).astype(jnp.int32)
    run_parity = ((jnp.cumsum(has_rows) - has_rows) % 2).astype(jnp.int32)
    return pad_start, block_expert, zero_flag, n_used, next_expert, run_parity


def _dest_kernel(ps_ref, idx_ref, rank_ref, dest_ref):
    idx = idx_ref[...]
    dest = rank_ref[...]
    for e in range(N_EXPERTS):
        dest = dest + jnp.where(idx == e, ps_ref[e], 0)
    dest_ref[...] = dest


def assignment_rows(top_idx, rank, pad_start):
    n = top_idx.shape[0]
    rows = n * TOP_K // LANES
    flat = lambda t: t.reshape(rows, LANES)
    spec = pl.BlockSpec((rows, LANES), lambda i, ps: (0, 0))
    out = pl.pallas_call(
        _dest_kernel,
        grid_spec=pltpu.PrefetchScalarGridSpec(num_scalar_prefetch=1, grid=(1,), in_specs=[spec, spec],
                                               out_specs=spec),
        out_shape=jax.ShapeDtypeStruct((rows, LANES), jnp.int32),
        compiler_params=_cparams(("arbitrary",)),
        name="assignment_rows",
    )(pad_start, flat(top_idx), flat(rank))
    return out.reshape(n, TOP_K)


def _scatter_kernel(zf_ref, dest_ref, h2_ref, xs_hbm, zeros_ref, sem, zsem):
    i = pl.program_id(0)
    tiles = h2_ref.shape[0]
    tm = tiles * SUBLANES
    nblk = zf_ref.shape[0]

    def zero_block(j):
        return pltpu.make_async_copy(zeros_ref, xs_hbm.at[pl.ds(j * MOE_ROWS, MOE_ROWS)], zsem)

    @pl.when(i == 0)
    def _():
        zeros_ref[...] = jnp.zeros_like(zeros_ref)

        def start(j, _):
            @pl.when(zf_ref[j] != 0)
            def _():
                zero_block(j).start()
            return 0

        def wait(j, _):
            @pl.when(zf_ref[j] != 0)
            def _():
                zero_block(j).wait()
            return 0

        lax.fori_loop(0, nblk, start, 0)
        lax.fori_loop(0, nblk, wait, 0)

    def body(g, _):
        for u in range(SUBLANES):
            for kk in range(TOP_K):
                row = dest_ref[0, 0, (g * SUBLANES + u) * TOP_K + kk]
                pltpu.make_async_copy(h2_ref.at[g, pl.ds(u, 1)], xs_hbm.at[pl.ds(row, 1)], sem).start(
                    priority=kk % 2)
        return 0

    lax.fori_loop(0, tiles, body, 0)
    pltpu.make_async_copy(xs_hbm.at[pl.ds(0, tm * TOP_K)], xs_hbm.at[pl.ds(0, tm * TOP_K)], sem).wait()


def scatter_rows(h2, dest, zero_flag, n_rows, tm=1024):
    n, w = h2.shape
    dest3 = dest.reshape(n // tm, 1, tm * TOP_K)
    grid_spec = pltpu.PrefetchScalarGridSpec(
        num_scalar_prefetch=1,
        grid=(n // tm,),
        in_specs=[pl.BlockSpec((1, 1, tm * TOP_K), lambda i, zf: (i, 0, 0), memory_space=pltpu.SMEM),
                  pl.BlockSpec((tm // SUBLANES, SUBLANES, w), lambda i, zf: (i, 0, 0))],
        out_specs=pl.BlockSpec(memory_space=pl.ANY),
        scratch_shapes=[pltpu.VMEM((MOE_ROWS, w), h2.dtype), pltpu.SemaphoreType.DMA(()),
                        pltpu.SemaphoreType.DMA(())],
    )
    return pl.pallas_call(
        _scatter_kernel,
        grid_spec=grid_spec,
        out_shape=jax.ShapeDtypeStruct((n_rows, w), h2.dtype),
        compiler_params=_cparams(("arbitrary",)),
        name="scatter_rows",
    )(zero_flag, dest3, h2.reshape(n // SUBLANES, SUBLANES, w))


def _expert_kernel(be_ref, nu_ref, nxt_ref, par_ref, xs_ref, w1_hbm, b1_ref, w2_hbm, b2_ref, y_ref,
                   w1_f32, w2_f32, w1_scr, w2_scr, wsem):
    i = pl.program_id(0)
    e = be_ref[i]
    prev = be_ref[jnp.maximum(i - 1, 0)]

    def fetch(expert, slot):
        return (pltpu.make_async_copy(w1_hbm.at[expert], w1_f32.at[slot], wsem.at[slot]),
                pltpu.make_async_copy(w2_hbm.at[expert], w2_f32.at[slot], wsem.at[slot]))

    @pl.when((i < nu_ref[0]) & ((i == 0) | (e != prev)))
    def _():
        slot = par_ref[e]

        @pl.when(i == 0)
        def _():
            for c in fetch(e, slot):
                c.start()

        for c in fetch(e, slot):
            c.wait()

        @pl.when(nxt_ref[e] >= 0)
        def _():
            for c in fetch(nxt_ref[e], 1 - slot):
                c.start()

        w1_scr[...] = w1_f32[slot].astype(BF16)
        w2_scr[...] = w2_f32[slot].astype(BF16)

    @pl.when(i < nu_ref[0])
    def _():
        xb = xs_ref[...].astype(BF16)
        dff = w2_scr.shape[0]
        u = jnp.dot(xb, w1_scr[...], preferred_element_type=F32) + b1_ref[0]
        glu = jnp.minimum(u[:, :dff], SWIGLU_LIMIT)
        lin = jnp.clip(u[:, dff:], -SWIGLU_LIMIT, SWIGLU_LIMIT)
        act = glu * jax.nn.sigmoid(SWIGLU_ALPHA * glu) * (lin + 1.0)
        y = jnp.dot(act.astype(BF16), w2_scr[...], preferred_element_type=F32) + b2_ref[0]
        y_ref[...] = y

    @pl.when(i >= nu_ref[0])
    def _():
        y_ref[...] = jnp.zeros_like(y_ref)


def expert_ffn(x_sorted, block_expert, n_used, next_expert, run_parity, w1, b1, w2, b2):
    n_rows, d = x_sorted.shape
    w = d
    nblk = n_rows // MOE_ROWS
    dff2 = w1.shape[2]
    used = lambda i, nu: jnp.minimum(i, nu[0] - 1)
    grid_spec = pltpu.PrefetchScalarGridSpec(
        num_scalar_prefetch=4,
        grid=(nblk,),
        in_specs=[
            pl.BlockSpec((MOE_ROWS, w), lambda i, be, nu, nx, pa: (used(i, nu), 0)),
            pl.BlockSpec(memory_space=pl.ANY),
            pl.BlockSpec((1, 1, dff2), lambda i, be, nu, nx, pa: (be[used(i, nu)], 0, 0)),
            pl.BlockSpec(memory_space=pl.ANY),
            pl.BlockSpec((1, 1, d), lambda i, be, nu, nx, pa: (be[used(i, nu)], 0, 0)),
        ],
        out_specs=pl.BlockSpec((MOE_ROWS, w), lambda i, be, nu, nx, pa: (i, 0)),
        scratch_shapes=[pltpu.VMEM((2, d, dff2), F32), pltpu.VMEM((2, dff2 // 2, d), F32),
                        pltpu.VMEM((d, dff2), BF16), pltpu.VMEM((dff2 // 2, d), BF16),
                        pltpu.SemaphoreType.DMA((2,))],
    )
    return pl.pallas_call(
        _expert_kernel,
        grid_spec=grid_spec,
        out_shape=jax.ShapeDtypeStruct((n_rows, w), F32),
        compiler_params=_cparams(("arbitrary",)),
        name="expert_ffn",
    )(block_expert, n_used, next_expert, run_parity, x_sorted, w1, b1.reshape(N_EXPERTS, 1, dff2), w2,
      b2.reshape(N_EXPERTS, 1, d))


def _gather_assigned_rows(y_hbm, dest_ref, dst_ref, sem, tm):
    tiles = tm // SUBLANES

    def body(g, _):
        for u in range(SUBLANES):
            for kk in range(TOP_K):
                row = dest_ref[(g * SUBLANES + u) * TOP_K + kk]
                pltpu.make_async_copy(y_hbm.at[pl.ds(row, 1)], dst_ref.at[kk * tiles + g, pl.ds(u, 1)],
                                      sem).start(priority=kk % 2)
        return 0
    lax.fori_loop(0, tiles, body, 0)


def _combine_kernel(d0_ref, dn_ref, x1_ref, gate_ref, y_hbm, y_tiles_hbm, o_ref, ybuf, sems):
    i = pl.program_id(0)
    nblk = pl.num_programs(0)
    slot = lax.rem(i, 2)
    tm = o_ref.shape[0]
    tiles = tm // SUBLANES

    @pl.when(i == 0)
    def _():
        _gather_assigned_rows(y_hbm, d0_ref.at[0, 0], ybuf.at[0], sems.at[0], tm)

    @pl.when(i + 1 < nblk)
    def _():
        _gather_assigned_rows(y_hbm, dn_ref.at[0, 0], ybuf.at[1 - slot], sems.at[1 - slot], tm)

    pltpu.make_async_copy(y_tiles_hbm.at[pl.ds(0, TOP_K * tiles)], ybuf.at[slot], sems.at[slot]).wait()
    acc = x1_ref[...]
    for kk in range(TOP_K):
        rows = ybuf[slot, pl.ds(kk * tiles, tiles)].reshape(tm, ybuf.shape[-1])
        acc = acc + gate_ref[:, kk:kk + 1] * rows
    o_ref[...] = acc


def moe_combine(x1, gate, y_sorted, dest, tm=256):
    n, d = x1.shape
    n_rows, w = y_sorted.shape
    nblk = n // tm
    dest3 = dest.reshape(nblk, 1, tm * TOP_K)
    return pl.pallas_call(
        _combine_kernel,
        grid=(nblk,),
        in_specs=[pl.BlockSpec((1, 1, TOP_K * tm), lambda i: (0, 0, 0), memory_space=pltpu.SMEM),
                  pl.BlockSpec((1, 1, TOP_K * tm), lambda i: (jnp.minimum(i + 1, nblk - 1), 0, 0),
                               memory_space=pltpu.SMEM),
                  pl.BlockSpec((tm, d), lambda i: (i, 0)),
                  pl.BlockSpec((tm, TOP_K), lambda i: (i, 0)),
                  pl.BlockSpec(memory_space=pl.ANY),
                  pl.BlockSpec(memory_space=pl.ANY)],
        out_specs=pl.BlockSpec((tm, d), lambda i: (i, 0)),
        out_shape=jax.ShapeDtypeStruct((n, d), F32),
        scratch_shapes=[pltpu.VMEM((2, TOP_K * tm // SUBLANES, SUBLANES, w), y_sorted.dtype),
                        pltpu.SemaphoreType.DMA((2,))],
        compiler_params=_cparams(("arbitrary",)),
        name="moe_combine",
    )(dest3, dest3, x1, gate, y_sorted, y_sorted.reshape(n_rows // SUBLANES, SUBLANES, w))


def kernel(x, mem, positions, norm_mix, w_in, b_gate, a_q_gain, a_k_gain, r_mu, r_w0, r_w2, r_a0, r_a2,
           r_g2, r_k_k, r_k_a, r_r_k, r_ln_w, r_ln_b, mem_norm, w_mem_kv, c_q_gain, c_k_gain, w_branch,
           w_out, norm_ffn, router_w, router_b, exp_w1, exp_b1, exp_w2, exp_b2):
    b, s, d = x.shape
    n = b * s
    depth = norm_mix.shape[0]
    n_groups = len(A_GROUPS)
    qkv_cols = n_groups * A_WIDTH
    off_k, off_v, off_r = qkv_cols, 2 * qkv_cols, 3 * qkv_cols
    off_cq = off_r + R_COLS
    off_gate = off_cq + C_WIDTH

    x2d = x.reshape(n, d)
    pos_col = positions.reshape(n, 1).astype(jnp.int32)
    for l in range(depth):
        w_l = w_in[l]
        if l == 0:
            h, cos_t, sin_t = rmsnorm_rows(x2d, norm_mix[l], pos_col)
        else:
            h = rmsnorm_rows(x2d, norm_mix[l])
        qk_gains = rotary_gains(a_q_gain[l], a_k_gain[l])
        zr = project(h, w_l[:, off_r:off_cq], "plain", out_dtype=F32, tn=R_COLS // 2)
        cq = project(h, w_l[:, off_cq:off_gate], "headnorm", (c_q_gain[l].reshape(1, HEAD_DIM),))
        gates = project(h, w_l[:, off_gate:], "gate", (b_gate[l].reshape(1, -1),), tn=2 * A_WIDTH)

        shp = lambda t: t.reshape(b, s, -1)
        outs, lses = [], []
        for g, (window, dilation) in enumerate(A_GROUPS):
            assert window // dilation == ATT_BLOCK
            cols = slice(g * A_WIDTH, (g + 1) * A_WIDTH)
            w_g = jnp.concatenate([w_l[:, :off_k][:, cols], w_l[:, off_k:off_v][:, cols],
                                   w_l[:, off_v:off_r][:, cols]], axis=1)
            qkv = project_qkv(h, w_g, qk_gains, cos_t, sin_t, dilation)
            o, lse = band_attention_group(qkv, b, g, dilation)
            outs.append(o)
            lses.append(lse)

        yb = rwkv7_mix(shp(zr), r_mu[l], r_w0[l], r_w2[l], r_a0[l], r_a2[l], r_g2[l], r_k_k[l], r_k_a[l],
                       r_r_k[l].reshape(-1), r_ln_w[l], r_ln_b[l]).reshape(n, R_WIDTH)

        mlen = mem.shape[1]
        mem_n = rmsnorm_rows(mem.reshape(b * mlen, d), mem_norm[l])
        wkv = w_mem_kv[l]
        ck = project(mem_n, wkv[:, :C_WIDTH], "headnorm", (c_k_gain[l].reshape(1, HEAD_DIM),))
        cv = project(mem_n, wkv[:, C_WIDTH:], "plain")
        yc = cross_attention(shp(cq), ck.reshape(b, mlen, C_WIDTH), cv.reshape(b, mlen, C_WIDTH)).reshape(n, C_WIDTH)

        x1, h2p, top_idx, gate, rank, counts = merge_and_route(
            x2d, outs, lses, yb, yc, gates, w_branch[l], w_out[l], norm_ffn[l], router_w[l], router_b[l])
        pad_start, block_expert, zero_flag, n_used, next_expert, run_parity = block_layout(counts, n * TOP_K)
        dest = assignment_rows(top_idx, rank, pad_start)
        x_sorted = scatter_rows(h2p, dest, zero_flag, block_expert.shape[0] * MOE_ROWS)
        y_sorted = expert_ffn(x_sorted, block_expert, n_used, next_expert, run_parity,
                              exp_w1[l], exp_b1[l], exp_w2[l], exp_b2[l])
        x2d = moe_combine(x1, gate, y_sorted, dest)
    return x2d.reshape(b, s, d)
```

```python
import functools

import jax
import jax.numpy as jnp
from jax import lax
from jax.experimental import pallas as pl
from jax.experimental.pallas import tpu as pltpu

F32 = jnp.float32
BF16 = jnp.bfloat16

NORM_EPS = 1e-6
LANES = 128
SUBLANES = 8
HEAD_DIM = 128
A_GROUPS = ((128, 1), (512, 4), (2048, 16))
A_HEADS = 4
A_WIDTH = A_HEADS * HEAD_DIM
ATT_BLOCK = 128
ATT_QB = 4
ROT_DIM = 32
ROPE_THETA = 500000.0
R_HEAD = 64
R_HEADS = 8
R_WIDTH = R_HEADS * R_HEAD
R_DECAY_LORA = 64
R_AAA_LORA = 64
R_GATE_LORA = 128
R_COLS = 3 * R_WIDTH + R_DECAY_LORA + R_AAA_LORA + R_GATE_LORA
R_GN_EPS = 64e-5
R_CHUNK = 64
R_CPI = 4
MXU_COLS = 256
R_SEG = MXU_COLS
C_HEADS = 4
C_WIDTH = C_HEADS * HEAD_DIM
N_BRANCH = 3
N_EXPERTS = 32
TOP_K = 4
SWIGLU_ALPHA = 1.702
SWIGLU_LIMIT = 7.0
MOE_ROWS = 256
NEG_BIG = -1e30

VMEM_LIMIT = 56 * 1024 * 1024


def _cparams(sem):
    return pltpu.CompilerParams(dimension_semantics=sem, vmem_limit_bytes=VMEM_LIMIT)


def _rmsnorm_kernel(x_ref, g_ref, *refs):
    o_ref = refs[-3] if len(refs) > 1 else refs[0]
    x = x_ref[...]
    ms = jnp.mean(x * x, axis=-1, keepdims=True)
    o_ref[...] = (x * lax.rsqrt(ms + NORM_EPS) * g_ref[...]).astype(o_ref.dtype)
    if len(refs) > 1:
        pos_ref, freq_ref, _, cos_ref, sin_ref = refs
        ang = pos_ref[...].astype(F32) * freq_ref[...]
        lane = lax.broadcasted_iota(jnp.int32, ang.shape, 1)
        cos_ref[...] = jnp.cos(ang)
        s = jnp.sin(ang)
        half = ROT_DIM // 2
        sin_ref[...] = jnp.where(lane < half, -s, jnp.where(lane < ROT_DIM, s, 0.0))


def rmsnorm_rows(x2d, gain, pos_col=None, tm=512):
    n, d = x2d.shape
    tm = min(tm, n)
    rows = lambda w: pl.BlockSpec((tm, w), lambda i: (i, 0))
    in_specs = [rows(d), pl.BlockSpec((1, d), lambda i: (0, 0))]
    out_specs, out_shape, extra = rows(d), jax.ShapeDtypeStruct((n, d), BF16), ()
    if pos_col is not None:
        half = ROT_DIM // 2
        inv_freq = ROPE_THETA ** (-jnp.arange(half, dtype=F32) / half)
        freq_row = jnp.concatenate([inv_freq, inv_freq, jnp.zeros((LANES - ROT_DIM,), F32)]).reshape(1, LANES)
        in_specs += [rows(1), pl.BlockSpec((1, LANES), lambda i: (0, 0))]
        out_specs = [out_specs, rows(LANES), rows(LANES)]
        out_shape = [out_shape] + [jax.ShapeDtypeStruct((n, LANES), F32)] * 2
        extra = (pos_col, freq_row)
    return pl.pallas_call(
        _rmsnorm_kernel,
        grid=(n // tm,),
        in_specs=in_specs,
        out_specs=out_specs,
        out_shape=out_shape,
        compiler_params=_cparams(("arbitrary",)),
        name="rmsnorm_rows",
    )(x2d, gain.reshape(1, d), *extra)


def _head_mean_sq(zh):
    avg = jnp.full((HEAD_DIM, HEAD_DIM), 1.0 / HEAD_DIM, BF16)
    return jnp.dot((zh * zh).astype(BF16), avg, preferred_element_type=F32)


def _cast_weight_once(w_ref, w_scr):
    @pl.when(pl.program_id(1) == 0)
    def _():
        w_scr[...] = w_ref[...].astype(w_scr.dtype)


def _proj_kernel(*refs, mode):
    h_ref, w_ref = refs[0], refs[1]
    o_ref, w_scr = refs[-2], refs[-1]
    _cast_weight_once(w_ref, w_scr)
    z = jnp.dot(h_ref[...], w_scr[...], preferred_element_type=F32)
    if mode == "plain":
        o_ref[...] = z.astype(o_ref.dtype)
    elif mode == "gate":
        o_ref[...] = (0.5 * jnp.tanh(0.5 * (z + refs[2][...])) + 0.5).astype(o_ref.dtype)
    else:
        gain = refs[2][...]
        for c in range(z.shape[1] // HEAD_DIM):
            zh = z[:, c * HEAD_DIM:(c + 1) * HEAD_DIM]
            zn = zh * lax.rsqrt(_head_mean_sq(zh) + NORM_EPS) * gain
            o_ref[:, c * HEAD_DIM:(c + 1) * HEAD_DIM] = zn.astype(o_ref.dtype)


def project(h, w, mode, extras=(), out_dtype=BF16, tm=2048, tn=512, col0=0, ncols=None):
    n, k = h.shape
    m = w.shape[1] if ncols is None else ncols
    tm = min(tm, n)
    tn = min(tn, m)
    assert n % tm == 0 and m % tn == 0 and col0 % tn == 0
    cblk = col0 // tn
    in_specs = [pl.BlockSpec((tm, k), lambda j, i: (i, 0)),
                pl.BlockSpec((k, tn), lambda j, i: (0, j + cblk))]
    if mode == "gate":
        in_specs.append(pl.BlockSpec((1, tn), lambda j, i: (0, j)))
    elif mode == "headnorm":
        in_specs.append(pl.BlockSpec((1, HEAD_DIM), lambda j, i: (0, 0)))
    return pl.pallas_call(
        functools.partial(_proj_kernel, mode=mode),
        grid=(m // tn, n // tm),
        in_specs=in_specs,
        out_specs=pl.BlockSpec((tm, tn), lambda j, i: (i, j)),
        out_shape=jax.ShapeDtypeStruct((n, m), out_dtype),
        scratch_shapes=[pltpu.VMEM((k, tn), BF16)],
        compiler_params=_cparams(("arbitrary", "arbitrary")),
        name="proj_" + mode,
    )(h, w, *extras)


def _qkv_proj_kernel(h_ref, w_ref, gain_ref, cos_ref, sin_ref, o_ref, w_scr, z_scr, *, dilation):
    j = pl.program_id(0)
    _cast_weight_once(w_ref, w_scr)
    z = jnp.dot(h_ref[...], w_scr[...], preferred_element_type=F32)
    half = ROT_DIM // 2

    pair = 2 * HEAD_DIM

    def emit(c0, slab):
        if dilation == 1:
            o_ref[:, c0 * HEAD_DIM:c0 * HEAD_DIM + pair] = slab.astype(o_ref.dtype)
        else:
            z_scr[c0] = slab[:, :HEAD_DIM]
            z_scr[c0 + 1] = slab[:, HEAD_DIM:]

    @pl.when(j < 2)
    def _():
        mi = lax.broadcasted_iota(jnp.int32, (pair, pair), 0)
        li = lax.broadcasted_iota(jnp.int32, (pair, pair), 1)
        same_head = (mi // HEAD_DIM) == (li // HEAD_DIM)
        lh, mh = li % HEAD_DIM, mi % HEAD_DIM
        avg = jnp.where(same_head, 1.0 / HEAD_DIM, 0.0).astype(BF16)
        perm = (same_head & (((lh < half) & (mh == lh + half))
                             | ((lh >= half) & (lh < ROT_DIM) & (mh == lh - half)))).astype(BF16)
        gcos = gain_ref[0, 0:1, :] * cos_ref[...]
        gsin = gain_ref[0, 1:2, :] * sin_ref[...]
        gcos = jnp.concatenate([gcos, gcos], axis=1)
        gsin = jnp.concatenate([gsin, gsin], axis=1)
        for c0 in range(0, A_HEADS, 2):
            zz = z[:, c0 * HEAD_DIM:c0 * HEAD_DIM + pair]
            ms = jnp.dot((zz * zz).astype(BF16), avg, preferred_element_type=F32)
            partner = jnp.dot(zz.astype(BF16), perm, preferred_element_type=F32)
            emit(c0, lax.rsqrt(ms + NORM_EPS) * (zz * gcos + partner * gsin))

    @pl.when(j == 2)
    def _():
        for c0 in range(0, A_HEADS, 2):
            emit(c0, z[:, c0 * HEAD_DIM:c0 * HEAD_DIM + pair])

    if dilation > 1:
        rows = z_scr.shape[1] // dilation
        for r in range(dilation):
            for c in range(A_HEADS):
                lo = r * A_WIDTH + c * HEAD_DIM
                o_ref[:, lo:lo + HEAD_DIM] = z_scr[c, pl.ds(r, rows, stride=dilation), :].astype(o_ref.dtype)


def rotary_gains(q_gain, k_gain):
    half = ROT_DIM // 2
    lane = jnp.arange(HEAD_DIM)
    partner = jnp.where(lane < half, lane + half, jnp.where(lane < ROT_DIM, lane - half, lane))
    return jnp.stack([jnp.stack([g, g[partner]]) for g in (q_gain, k_gain)])


def project_qkv(h, w_all, group, n_groups, gains, cos_t, sin_t, dilation, tm=2048):
    n, k = h.shape
    d = dilation
    return pl.pallas_call(
        functools.partial(_qkv_proj_kernel, dilation=d),
        grid=(3, n // tm),
        in_specs=[pl.BlockSpec((tm, k), lambda j, i: (i, 0)),
                  pl.BlockSpec((k, A_WIDTH), lambda j, i: (0, j * n_groups + group)),
                  pl.BlockSpec((1, 2, HEAD_DIM), lambda j, i: (jnp.minimum(j, 1), 0, 0)),
                  pl.BlockSpec((tm, LANES), lambda j, i: (i, 0)),
                  pl.BlockSpec((tm, LANES), lambda j, i: (i, 0))],
        out_specs=pl.BlockSpec((tm // d, d * A_WIDTH), lambda j, i: (i, j)),
        out_shape=jax.ShapeDtypeStruct((n // d, 3 * d * A_WIDTH), BF16),
        scratch_shapes=[pltpu.VMEM((k, A_WIDTH), BF16), pltpu.VMEM((A_HEADS, tm, HEAD_DIM), F32)],
        compiler_params=_cparams(("arbitrary", "arbitrary")),
        name=f"proj_qkv_d{d}",
    )(h, w_all, gains, cos_t, sin_t)


def _band_attn_kernel(*refs, qb):
    q_ref = refs[0]
    k_refs = refs[1:qb + 2]
    v_refs = refs[qb + 2:2 * qb + 3]
    o_ref, lse_ref = refs[-2:]
    step = pl.program_id(2)
    scale = HEAD_DIM ** -0.5
    nq = ATT_BLOCK
    qi = lax.broadcasted_iota(jnp.int32, (nq, 2 * nq), 0)
    ki = lax.broadcasted_iota(jnp.int32, (nq, 2 * nq), 1)
    rel = qi + nq - ki
    band = (rel >= 0) & (rel <= nq)
    for a in range(qb):
        blk = step * qb + a
        valid = band & ((blk * nq - nq + ki) >= 0)
        rows = slice(a * nq, (a + 1) * nq)
        lses = []
        for h in range(A_HEADS):
            sl = slice(h * HEAD_DIM, (h + 1) * HEAD_DIM)
            qh = q_ref[0, rows, sl]
            kh = jnp.concatenate([k_refs[a][0, :, sl], k_refs[a + 1][0, :, sl]], axis=0)
            vh = jnp.concatenate([v_refs[a][0, :, sl], v_refs[a + 1][0, :, sl]], axis=0)
            s = lax.dot_general(qh, kh, (((1,), (1,)), ((), ())), preferred_element_type=F32) * scale
            s = jnp.where(valid, s, NEG_BIG)
            m = jnp.max(s, axis=-1, keepdims=True)
            p = jnp.exp(s - m)
            l = jnp.sum(p, axis=-1, keepdims=True)
            o = jnp.dot(p.astype(BF16), vh, preferred_element_type=F32) / l
            o_ref[0, rows, sl] = o.astype(o_ref.dtype)
            lses.append(jnp.broadcast_to(m + jnp.log(l), (nq, LANES // A_HEADS)))
        lse_ref[0, rows, :] = jnp.concatenate(lses, axis=1)


def band_attention_group(qkv, b, g, dilation):
    d = dilation
    sub = qkv.shape[0] // b
    nblk = sub // ATT_BLOCK
    qb = min(ATT_QB, nblk)
    assert nblk % qb == 0
    view = qkv.reshape(b, sub, 3 * d * A_WIDTH)
    qrows = qb * ATT_BLOCK
    key_spec = lambda t, m: pl.BlockSpec(
        (1, ATT_BLOCK, A_WIDTH), lambda bi, r, j: (bi, jnp.maximum(j * qb - 1 + m, 0), t * d + r))
    o, lse = pl.pallas_call(
        functools.partial(_band_attn_kernel, qb=qb),
        grid=(b, d, nblk // qb),
        in_specs=[pl.BlockSpec((1, qrows, A_WIDTH), lambda bi, r, j: (bi, j, r))]
                 + [key_spec(1, m) for m in range(qb + 1)] + [key_spec(2, m) for m in range(qb + 1)],
        out_specs=[pl.BlockSpec((1, qrows, A_WIDTH), lambda bi, r, j: (bi, j, r)),
                   pl.BlockSpec((1, qrows, LANES), lambda bi, r, j: (bi, j, r))],
        out_shape=[jax.ShapeDtypeStruct((b, sub, d * A_WIDTH), BF16),
                   jax.ShapeDtypeStruct((b, sub, d * LANES), F32)],
        compiler_params=_cparams(("arbitrary", "arbitrary", "arbitrary")),
        name=f"band_attn_g{g}",
    )(*([view] * (2 * qb + 3)))
    return o.reshape(b * sub, d * A_WIDTH), lse.reshape(b * sub, d * LANES)


def _cross_attn_kernel(q_ref, k_ref, v_ref, o_ref):
    scale = HEAD_DIM ** -0.5
    for h in range(C_HEADS):
        sl = slice(h * HEAD_DIM, (h + 1) * HEAD_DIM)
        s = lax.dot_general(q_ref[0, :, sl], k_ref[0, :, sl], (((1,), (1,)), ((), ())),
                            preferred_element_type=F32) * scale
        m = jnp.max(s, axis=-1, keepdims=True)
        p = jnp.exp(s - m)
        l = jnp.sum(p, axis=-1, keepdims=True)
        o = jnp.dot(p.astype(BF16), v_ref[0, :, sl], preferred_element_type=F32) / l
        o_ref[0, :, sl] = o.astype(o_ref.dtype)


def cross_attention(qn, kn, v, tm=512):
    b, s, w = qn.shape
    m = kn.shape[1]
    return pl.pallas_call(
        _cross_attn_kernel,
        grid=(b, s // tm),
        in_specs=[pl.BlockSpec((1, tm, w), lambda bi, i: (bi, i, 0)),
                  pl.BlockSpec((1, m, w), lambda bi, i: (bi, 0, 0)),
                  pl.BlockSpec((1, m, w), lambda bi, i: (bi, 0, 0))],
        out_specs=pl.BlockSpec((1, tm, w), lambda bi, i: (bi, i, 0)),
        out_shape=jax.ShapeDtypeStruct((b, s, w), BF16),
        compiler_params=_cparams(("arbitrary", "arbitrary")),
        name="cross_attn",
    )(qn, kn, v)


def _head_sums(x, seg):
    w = seg.shape[0]
    x16 = x.astype(BF16)
    return jnp.concatenate(
        [jnp.dot(x16[:, j:j + w], seg, preferred_element_type=F32) for j in range(0, x.shape[1], w)], axis=1)


def _dot_t(a, b):
    return lax.dot_general(a, b, (((0,), (0,)), ((), ())), preferred_element_type=F32)


def _dot_nt(a, b):
    return lax.dot_general(a, b, (((1,), (1,)), ((), ())), preferred_element_type=F32)


def _rwkv_kernel(zr_ref, mu_ref, w0_ref, wwa_ref, a0_ref, g2_ref, kk_ref, ka_ref, rk_ref,
                 lnw_ref, lnb_ref, seg_ref, y_ref,
                 state_ref, carry_ref, ops_ref, yh_ref):
    t = pl.program_id(1)
    tt = zr_ref.shape[1]
    nch = tt // R_CHUNK
    c = R_CHUNK

    @pl.when(t == 0)
    def _():
        state_ref[...] = jnp.zeros_like(state_ref)
        carry_ref[...] = jnp.zeros_like(carry_ref)

    z = zr_ref[0]
    row = lax.broadcasted_iota(jnp.int32, z.shape, 0)
    prev = jnp.where(row == 0, carry_ref[...], pltpu.roll(z, 1, 0))
    carry_ref[...] = z[tt - 1:tt, :]
    xs = z + (prev - z) * mu_ref[...]

    w3 = 3 * R_WIDTH
    r = xs[:, 0:R_WIDTH]
    k = xs[:, R_WIDTH:2 * R_WIDTH]
    v = xs[:, 2 * R_WIDTH:w3]
    wa_lo = xs[:, w3:w3 + LANES]
    g_lo = xs[:, w3 + LANES:w3 + 2 * LANES]
    lane = lax.broadcasted_iota(jnp.int32, wa_lo.shape, 1)
    wa_in = jnp.where(lane < R_DECAY_LORA, jnp.tanh(wa_lo), wa_lo)
    wa = jnp.dot(wa_in.astype(BF16), wwa_ref[...], preferred_element_type=F32)
    u = -(w0_ref[...] + wa[:, :R_WIDTH])
    softplus = jnp.maximum(u, 0.0) + jnp.log(1.0 + jnp.exp(-jnp.abs(u)))
    w_raw = -softplus - 0.5
    ld = -jnp.exp(w_raw)
    a = jax.nn.sigmoid(a0_ref[...] + wa[:, R_WIDTH:])
    g = jnp.dot(jax.nn.sigmoid(g_lo).astype(BF16), g2_ref[...], preferred_element_type=F32)

    seg = seg_ref[...]
    kk = k * kk_ref[...]
    kk = kk * jnp.minimum(lax.rsqrt(_head_sums(kk * kk, seg)), 1e12)
    k2 = k * (1.0 + (a - 1.0) * ka_ref[...])
    bonus = _head_sums(r * k2 * rk_ref[...], seg) * v

    ri = lax.broadcasted_iota(jnp.int32, (c, c), 0)
    ci = lax.broadcasted_iota(jnp.int32, (c, c), 1)
    tri = (ci <= ri).astype(BF16)
    ld_hi = ld.astype(BF16)
    ld_lo = (ld - ld_hi.astype(F32)).astype(BF16)
    lcs = []
    for ch in range(nch):
        rs = slice(ch * c, (ch + 1) * c)
        lcs.append(jnp.dot(tri, ld_hi[rs], preferred_element_type=F32)
                   + jnp.dot(tri, ld_lo[rs], preferred_element_type=F32))
    lc = jnp.concatenate(lcs, axis=0)
    e_inc = jnp.exp(lc)
    e_exc = jnp.exp(lc - ld)
    e_inv = jnp.exp(-lc)
    a_t = -kk * e_exc
    r_t = r * e_inc
    b_t = kk * a * e_inv
    k_t = k2 * e_inv
    for h in range(R_HEADS):
        hs = slice(h * R_HEAD, (h + 1) * R_HEAD)
        ops_ref[0, h] = a_t[:, hs]
        ops_ref[1, h] = r_t[:, hs]
        ops_ref[2, h] = b_t[:, hs]
        ops_ref[3, h] = k_t[:, hs]
        ops_ref[4, h] = v[:, hs]
        ops_ref[5, h] = e_inc[:, hs]

    strict = ci < ri
    incl = ci <= ri
    eye = (ci == ri)

    ri2 = lax.broadcasted_iota(jnp.int32, (c, 2 * c), 0)
    ci2 = lax.broadcasted_iota(jnp.int32, (c, 2 * c), 1)
    incl2 = jnp.bitwise_and(ci2, c - 1) <= ri2
    eye_f = jnp.where(eye, 1.0, 0.0)
    heads = range(R_HEADS)
    dot = functools.partial(jnp.dot, preferred_element_type=F32)

    def chunk_body(ch, _):
        starts = [pl.multiple_of((ch * R_CPI + sub) * c, c) for sub in range(R_CPI)]
        rows = [pl.ds(r0, c) for r0 in starts]
        items = [(sub, h) for sub in range(R_CPI) for h in heads]
        idx = range(len(items))
        at = [ops_ref[0, h, rows[sub], :] for sub, h in items]
        rt = [ops_ref[1, h, rows[sub], :] for sub, h in items]
        bt = [ops_ref[2, h, rows[sub], :] for sub, h in items]
        kt = [ops_ref[3, h, rows[sub], :] for sub, h in items]
        pc = [ops_ref[5, h, pl.ds(starts[sub] + c - 1, 1), :] for sub, h in items]
        at16 = [x.astype(BF16) for x in at]
        rt16 = [x.astype(BF16) for x in rt]
        bt16 = [x.astype(BF16) for x in bt]
        kt16 = [x.astype(BF16) for x in kt]
        v16 = [ops_ref[4, h, rows[sub], :].astype(BF16) for sub, h in items]
        bk16 = [jnp.concatenate([bt16[i], kt16[i]], axis=0) for i in idx]
        nmat = [jnp.where(strict, _dot_nt(at16[i], bt16[i]), 0.0) for i in idx]
        a_ak = [jnp.where(strict, _dot_nt(at16[i], kt16[i]), 0.0).astype(BF16) for i in idx]
        a_rbk = [jnp.where(incl2, _dot_nt(rt16[i], bk16[i]), 0.0).astype(BF16) for i in idx]
        npow = nmat
        tinv = [eye_f + nmat[i] for i in idx]
        for _i in range(5):
            np16 = [x.astype(BF16) for x in npow]
            npow = [dot(np16[i], np16[i]) for i in idx]
            tinv = [tinv[i] + dot(tinv[i].astype(BF16), npow[i].astype(BF16)) for i in idx]
        akv = [dot(a_ak[i], v16[i]).astype(BF16) for i in idx]
        apw1 = [dot(tinv[i].astype(BF16), jnp.concatenate([at16[i], akv[i]], axis=1)).astype(BF16)
                for i in idx]
        zero = jnp.zeros((c, R_HEAD), BF16)
        rhs2 = [jnp.concatenate([apw1[i], jnp.concatenate([zero, v16[i]], axis=1)], axis=0)
                for i in idx]
        bkh = [jnp.concatenate([bt[i] * pc[i], kt[i] * pc[i]], axis=0).astype(BF16) for i in idx]
        gh = [_dot_t(bkh[i], rhs2[i]) for i in idx]
        qy = [dot(a_rbk[i], rhs2[i]) for i in idx]
        for i, (sub, h) in enumerate(items):
            gm = jnp.where(eye, jnp.broadcast_to(pc[i], (c, c)), 0.0) + gh[i][:, :R_HEAD]
            qp = rt[i] + qy[i][:, :R_HEAD]
            st = state_ref[h]
            res = dot(jnp.concatenate([qp, gm], axis=0).astype(BF16), st.astype(BF16))
            yh_ref[h, rows[sub], :] = res[:c] + qy[i][:, R_HEAD:]
            state_ref[h] = res[c:] + gh[i][:, R_HEAD:]
        return 0

    lax.fori_loop(0, nch // R_CPI, chunk_body, 0)

    y = jnp.concatenate([yh_ref[h] for h in range(R_HEADS)], axis=1)
    mean = _head_sums(y, seg) * (1.0 / R_HEAD)
    dlt = y - mean
    var = _head_sums(dlt * dlt, seg) * (1.0 / R_HEAD)
    yn = dlt * lax.rsqrt(var + R_GN_EPS) * lnw_ref[...] + lnb_ref[...]
    y_ref[0] = ((yn + bonus) * g).astype(y_ref.dtype)


def rwkv7_mix(zr, mu, w0, w2, a0, a2, g2, k_k, k_a, r_k, ln_w, ln_b, tt=256):
    b, s, cols = zr.shape
    row = lambda x: x.reshape(1, -1).astype(F32)
    wwa = jnp.zeros((LANES, 2 * R_WIDTH), F32)
    wwa = wwa.at[:R_DECAY_LORA, :R_WIDTH].set(w2).at[R_DECAY_LORA:, R_WIDTH:].set(a2).astype(BF16)
    hid = jnp.arange(R_SEG) // R_HEAD
    seg = (hid[:, None] == hid[None, :]).astype(BF16)
    full = lambda shape: pl.BlockSpec(shape, lambda bi, t: (0,) * len(shape))
    return pl.pallas_call(
        _rwkv_kernel,
        grid=(b, s // tt),
        in_specs=[pl.BlockSpec((1, tt, cols), lambda bi, t: (bi, t, 0)),
                  full((1, cols)), full((1, R_WIDTH)), full((LANES, 2 * R_WIDTH)), full((1, R_WIDTH)),
                  full((R_GATE_LORA, R_WIDTH)), full((1, R_WIDTH)), full((1, R_WIDTH)), full((1, R_WIDTH)),
                  full((1, R_WIDTH)), full((1, R_WIDTH)), full((R_SEG, R_SEG))],
        out_specs=pl.BlockSpec((1, tt, R_WIDTH), lambda bi, t: (bi, t, 0)),
        out_shape=jax.ShapeDtypeStruct((b, s, R_WIDTH), BF16),
        scratch_shapes=[pltpu.VMEM((R_HEADS, R_HEAD, R_HEAD), F32),
                        pltpu.VMEM((1, cols), F32),
                        pltpu.VMEM((6, R_HEADS, tt, R_HEAD), F32),
                        pltpu.VMEM((R_HEADS, tt, R_HEAD), F32)],
        compiler_params=_cparams(("arbitrary", "arbitrary")),
        name="rwkv7_mix",
    )(zr, row(mu), row(w0), wwa, row(a0), g2.astype(BF16), row(k_k), row(k_a), row(r_k),
      row(ln_w), row(ln_b), seg)


def _merge_kernel(x_ref, o0_ref, o1_ref, o2_ref, l0_ref, l1_ref, l2_ref, yb_ref, yc_ref, gt_ref,
                  wb_ref, wo_ref, gn_ref, rw_ref, rb_ref,
                  x1_ref, h2_ref, idx_ref, gate_ref, rank_ref, cnt_ref, base_ref, o_scr, l_scr):
    @pl.when(pl.program_id(0) == 0)
    def _():
        base_ref[...] = jnp.zeros_like(base_ref)

    tm_rows = x_ref.shape[0]
    for gi, (o_ref, l_ref) in enumerate(((o0_ref, l0_ref), (o1_ref, l1_ref), (o2_ref, l2_ref))):
        dil = A_GROUPS[gi][1]
        for r in range(dil):
            dst = pl.ds(r, tm_rows // dil, stride=dil) if dil > 1 else slice(None)
            for h in range(A_HEADS):
                lo = r * A_WIDTH + h * HEAD_DIM
                o_scr[gi, h, dst, :] = o_ref[:, lo:lo + HEAD_DIM].astype(F32)
            l_scr[gi, dst, :] = l_ref[:, r * LANES:(r + 1) * LANES]
    lses = [l_scr[gi] for gi in range(3)]
    lmax = jnp.maximum(jnp.maximum(lses[0], lses[1]), lses[2])
    es = [jnp.exp(l - lmax) for l in lses]
    inv = 1.0 / (es[0] + es[1] + es[2])
    qw = LANES // A_HEADS
    heads = []
    for h in range(A_HEADS):
        sl = slice(h * HEAD_DIM, (h + 1) * HEAD_DIM)
        acc = None
        for gi in range(3):
            alpha = (es[gi] * inv)[:, h * qw:h * qw + 1]
            term = alpha * o_scr[gi, h]
            acc = term if acc is None else acc + term
        heads.append(acc)
    ya = jnp.concatenate(heads, axis=1).astype(BF16)
    d = x_ref.shape[1]
    merged = None
    for n, yn in enumerate((ya, yb_ref[...], yc_ref[...])):
        proj = jnp.dot(yn, wb_ref[n], preferred_element_type=F32)
        term = gt_ref[:, n * d:(n + 1) * d].astype(F32) * proj
        merged = term if merged is None else merged + term
    x1 = x_ref[...] + jnp.dot(merged.astype(BF16), wo_ref[...], preferred_element_type=F32)
    x1_ref[...] = x1
    ms = jnp.mean(x1 * x1, axis=-1, keepdims=True)
    h2 = x1 * lax.rsqrt(ms + NORM_EPS) * gn_ref[...]
    h2_ref[...] = h2
    rw = rw_ref[...]
    h2_hi, rw_hi = h2.astype(BF16), rw.astype(BF16)
    h2_lo = (h2 - h2_hi.astype(F32)).astype(BF16)
    rw_lo = (rw - rw_hi.astype(F32)).astype(BF16)
    logits = (jnp.dot(h2_hi, rw_hi, preferred_element_type=F32) + jnp.dot(h2_hi, rw_lo, preferred_element_type=F32)
              + jnp.dot(h2_lo, rw_hi, preferred_element_type=F32)) + rb_ref[...]
    tm = logits.shape[0]
    lane = lax.broadcasted_iota(jnp.int32, logits.shape, 1)
    vals, idxs = [], []
    cur = logits
    for _k in range(TOP_K):
        m = jnp.max(cur, axis=-1, keepdims=True)
        ik = jnp.min(jnp.where(cur == m, lane, N_EXPERTS), axis=-1, keepdims=True)
        vals.append(m)
        idxs.append(ik)
        cur = jnp.where(lane == ik, -jnp.inf, cur)
    exps = [jnp.exp(vk - vals[0]) for vk in vals]
    tot = exps[0] + exps[1] + exps[2] + exps[3]
    onehots = [lane == ik for ik in idxs]
    hits = sum(jnp.where(oh, 1.0, 0.0) for oh in onehots)
    ri = lax.broadcasted_iota(jnp.int32, (tm, tm), 0)
    ci = lax.broadcasted_iota(jnp.int32, (tm, tm), 1)
    before = jnp.dot((ci < ri).astype(BF16), hits.astype(BF16), preferred_element_type=F32) + base_ref[...]
    for kk in range(TOP_K):
        idx_ref[:, kk:kk + 1] = idxs[kk]
        gate_ref[:, kk:kk + 1] = exps[kk] / tot
        rank_ref[:, kk:kk + 1] = jnp.sum(jnp.where(onehots[kk], before, 0.0), axis=-1,
                                         keepdims=True).astype(jnp.int32)
    base_ref[...] = base_ref[...] + jnp.sum(hits, axis=0, keepdims=True)
    cnt_ref[...] = base_ref[...].astype(jnp.int32)


def merge_and_route(x2d, outs, lses, yb, yc, gates, w_branch, w_out, norm_ffn, router_w, router_b, tm=512):
    n, d = x2d.shape
    rows = lambda w: pl.BlockSpec((tm, w), lambda i: (i, 0))
    packed = lambda w, dil: pl.BlockSpec((tm // dil, dil * w), lambda i: (i, 0))
    full = lambda shape: pl.BlockSpec(shape, lambda i: (0,) * len(shape))
    return pl.pallas_call(
        _merge_kernel,
        grid=(n // tm,),
        in_specs=[rows(d)] + [packed(A_WIDTH, dil) for _w, dil in A_GROUPS] + [packed(LANES, dil) for _w, dil in A_GROUPS]
                 + [rows(R_WIDTH), rows(C_WIDTH), rows(N_BRANCH * d),
                    full((N_BRANCH, A_WIDTH, d)), full((d, d)), full((1, d)), full((d, N_EXPERTS)), full((1, N_EXPERTS))],
        out_specs=[rows(d), rows(d), rows(TOP_K), rows(TOP_K), rows(TOP_K), full((1, N_EXPERTS))],
        out_shape=[jax.ShapeDtypeStruct((n, d), F32), jax.ShapeDtypeStruct((n, d), F32),
                   jax.ShapeDtypeStruct((n, TOP_K), jnp.int32), jax.ShapeDtypeStruct((n, TOP_K), F32),
                   jax.ShapeDtypeStruct((n, TOP_K), jnp.int32), jax.ShapeDtypeStruct((1, N_EXPERTS), jnp.int32)],
        scratch_shapes=[pltpu.VMEM((1, N_EXPERTS), F32), pltpu.VMEM((len(A_GROUPS), A_HEADS, tm, HEAD_DIM), F32),
                        pltpu.VMEM((len(A_GROUPS), tm, LANES), F32)],
        compiler_params=_cparams(("arbitrary",)),
        name="merge_route",
    )(x2d, *outs, *lses, yb, yc, gates, w_branch.astype(BF16), w_out.astype(BF16),
      norm_ffn.reshape(1, d), router_w, router_b.reshape(1, N_EXPERTS))


def block_layout(counts, n_assign):
    counts = counts.reshape(-1)
    padded = (counts + MOE_ROWS - 1) // MOE_ROWS * MOE_ROWS
    pad_end = jnp.cumsum(padded)
    pad_start = (pad_end - padded).astype(jnp.int32)
    n_blocks = -(-n_assign // MOE_ROWS) + N_EXPERTS
    blk_row = jnp.arange(n_blocks, dtype=jnp.int32) * MOE_ROWS
    owner = jnp.sum((blk_row[:, None] >= pad_end[None, :]).astype(jnp.int32), axis=1)
    block_expert = jnp.minimum(owner, N_EXPERTS - 1).astype(jnp.int32)
    unused = blk_row >= pad_end[-1]
    zero_flag = (unused | (blk_row + MOE_ROWS == pad_end[block_expert])).astype(jnp.int32)
    n_used = (pad_end[-1:] // MOE_ROWS).astype(jnp.int32)
    has_rows = counts > 0
    eid = jnp.arange(N_EXPERTS, dtype=jnp.int32)
    later = jnp.where(has_rows[None, :] & (eid[None, :] > eid[:, None]), eid[None, :], N_EXPERTS)
    next_expert = jnp.min(later, axis=1)
    next_expert = jnp.where(next_expert == N_EXPERTS, -1, next_expert).astype(jnp.int32)
    run_parity = ((jnp.cumsum(has_rows) - has_rows) % 2).astype(jnp.int32)
    return pad_start, block_expert, zero_flag, n_used, next_expert, run_parity


def _dest_kernel(ps_ref, idx_ref, rank_ref, dest_ref):
    idx = idx_ref[...]
    dest = rank_ref[...]
    for e in range(N_EXPERTS):
        dest = dest + jnp.where(idx == e, ps_ref[e], 0)
    dest_ref[...] = dest


def assignment_rows(top_idx, rank, pad_start):
    n = top_idx.shape[0]
    rows = n * TOP_K // LANES
    flat = lambda t: t.reshape(rows, LANES)
    spec = pl.BlockSpec((rows, LANES), lambda i, ps: (0, 0))
    out = pl.pallas_call(
        _dest_kernel,
        grid_spec=pltpu.PrefetchScalarGridSpec(num_scalar_prefetch=1, grid=(1,), in_specs=[spec, spec],
                                               out_specs=spec),
        out_shape=jax.ShapeDtypeStruct((rows, LANES), jnp.int32),
        compiler_params=_cparams(("arbitrary",)),
        name="assignment_rows",
    )(pad_start, flat(top_idx), flat(rank))
    return out.reshape(n, TOP_K)


def _scatter_kernel(zf_ref, dest_ref, h2_ref, xs_hbm, zeros_ref, sem, zsem):
    i = pl.program_id(0)
    tiles = h2_ref.shape[0]
    tm = tiles * SUBLANES
    nblk = zf_ref.shape[0]

    def zero_block(j):
        return pltpu.make_async_copy(zeros_ref, xs_hbm.at[pl.ds(j * MOE_ROWS, MOE_ROWS)], zsem)

    @pl.when(i == 0)
    def _():
        zeros_ref[...] = jnp.zeros_like(zeros_ref)

        def start(j, _):
            @pl.when(zf_ref[j] != 0)
            def _():
                zero_block(j).start()
            return 0

        def wait(j, _):
            @pl.when(zf_ref[j] != 0)
            def _():
                zero_block(j).wait()
            return 0

        lax.fori_loop(0, nblk, start, 0)
        lax.fori_loop(0, nblk, wait, 0)

    def body(g, _):
        for u in range(SUBLANES):
            for kk in range(TOP_K):
                row = dest_ref[0, 0, (g * SUBLANES + u) * TOP_K + kk]
                pltpu.make_async_copy(h2_ref.at[g, pl.ds(u, 1)], xs_hbm.at[pl.ds(row, 1)], sem).start(
                    priority=kk % 2)
        return 0

    lax.fori_loop(0, tiles, body, 0)
    pltpu.make_async_copy(xs_hbm.at[pl.ds(0, tm * TOP_K)], xs_hbm.at[pl.ds(0, tm * TOP_K)], sem).wait()


def scatter_rows(h2, dest, zero_flag, n_rows, tm=1024):
    n, w = h2.shape
    dest3 = dest.reshape(n // tm, 1, tm * TOP_K)
    grid_spec = pltpu.PrefetchScalarGridSpec(
        num_scalar_prefetch=1,
        grid=(n // tm,),
        in_specs=[pl.BlockSpec((1, 1, tm * TOP_K), lambda i, zf: (i, 0, 0), memory_space=pltpu.SMEM),
                  pl.BlockSpec((tm // SUBLANES, SUBLANES, w), lambda i, zf: (i, 0, 0))],
        out_specs=pl.BlockSpec(memory_space=pl.ANY),
        scratch_shapes=[pltpu.VMEM((MOE_ROWS, w), h2.dtype), pltpu.SemaphoreType.DMA(()),
                        pltpu.SemaphoreType.DMA(())],
    )
    return pl.pallas_call(
        _scatter_kernel,
        grid_spec=grid_spec,
        out_shape=jax.ShapeDtypeStruct((n_rows, w), h2.dtype),
        compiler_params=_cparams(("arbitrary",)),
        name="scatter_rows",
    )(zero_flag, dest3, h2.reshape(n // SUBLANES, SUBLANES, w))


def _expert_kernel(be_ref, nu_ref, nxt_ref, par_ref, xs_ref, w1_hbm, b1_ref, w2_hbm, b2_ref, y_ref,
                   w1_f32, w2_f32, w1_scr, w2_scr, wsem):
    i = pl.program_id(0)
    e = be_ref[i]
    prev = be_ref[jnp.maximum(i - 1, 0)]

    def fetch(expert, slot):
        return (pltpu.make_async_copy(w1_hbm.at[expert], w1_f32.at[slot], wsem.at[slot]),
                pltpu.make_async_copy(w2_hbm.at[expert], w2_f32.at[slot], wsem.at[slot]))

    @pl.when((i < nu_ref[0]) & ((i == 0) | (e != prev)))
    def _():
        slot = par_ref[e]

        @pl.when(i == 0)
        def _():
            for c in fetch(e, slot):
                c.start()

        for c in fetch(e, slot):
            c.wait()

        @pl.when(nxt_ref[e] >= 0)
        def _():
            for c in fetch(nxt_ref[e], 1 - slot):
                c.start()

        w1_scr[...] = w1_f32[slot].astype(BF16)
        w2_scr[...] = w2_f32[slot].astype(BF16)

    @pl.when(i < nu_ref[0])
    def _():
        xb = xs_ref[...].astype(BF16)
        dff = w2_scr.shape[0]
        u = jnp.dot(xb, w1_scr[...], preferred_element_type=F32) + b1_ref[0]
        glu = jnp.minimum(u[:, :dff], SWIGLU_LIMIT)
        lin = jnp.clip(u[:, dff:], -SWIGLU_LIMIT, SWIGLU_LIMIT)
        act = glu * jax.nn.sigmoid(SWIGLU_ALPHA * glu) * (lin + 1.0)
        y = jnp.dot(act.astype(BF16), w2_scr[...], preferred_element_type=F32) + b2_ref[0]
        y_ref[...] = y

    @pl.when(i >= nu_ref[0])
    def _():
        y_ref[...] = jnp.zeros_like(y_ref)


def expert_ffn(x_sorted, block_expert, n_used, next_expert, run_parity, w1, b1, w2, b2):
    n_rows, d = x_sorted.shape
    w = d
    nblk = n_rows // MOE_ROWS
    dff2 = w1.shape[2]
    used = lambda i, nu: jnp.minimum(i, nu[0] - 1)
    grid_spec = pltpu.PrefetchScalarGridSpec(
        num_scalar_prefetch=4,
        grid=(nblk,),
        in_specs=[
            pl.BlockSpec((MOE_ROWS, w), lambda i, be, nu, nx, pa: (used(i, nu), 0)),
            pl.BlockSpec(memory_space=pl.ANY),
            pl.BlockSpec((1, 1, dff2), lambda i, be, nu, nx, pa: (be[used(i, nu)], 0, 0)),
            pl.BlockSpec(memory_space=pl.ANY),
            pl.BlockSpec((1, 1, d), lambda i, be, nu, nx, pa: (be[used(i, nu)], 0, 0)),
        ],
        out_specs=pl.BlockSpec((MOE_ROWS, w), lambda i, be, nu, nx, pa: (i, 0)),
        scratch_shapes=[pltpu.VMEM((2, d, dff2), F32), pltpu.VMEM((2, dff2 // 2, d), F32),
                        pltpu.VMEM((d, dff2), BF16), pltpu.VMEM((dff2 // 2, d), BF16),
                        pltpu.SemaphoreType.DMA((2,))],
    )
    return pl.pallas_call(
        _expert_kernel,
        grid_spec=grid_spec,
        out_shape=jax.ShapeDtypeStruct((n_rows, w), F32),
        compiler_params=_cparams(("arbitrary",)),
        name="expert_ffn",
    )(block_expert, n_used, next_expert, run_parity, x_sorted, w1, b1.reshape(N_EXPERTS, 1, dff2), w2,
      b2.reshape(N_EXPERTS, 1, d))


def _gather_assigned_rows(y_hbm, dest_ref, dst_ref, sem, tm):
    tiles = tm // SUBLANES

    def body(g, _):
        for u in range(SUBLANES):
            for kk in range(TOP_K):
                row = dest_ref[(g * SUBLANES + u) * TOP_K + kk]
                pltpu.make_async_copy(y_hbm.at[pl.ds(row, 1)], dst_ref.at[kk * tiles + g, pl.ds(u, 1)],
                                      sem).start(priority=kk % 2)
        return 0
    lax.fori_loop(0, tiles, body, 0)


def _combine_kernel(d0_ref, dn_ref, x1_ref, gate_ref, y_hbm, y_tiles_hbm, o_ref, ybuf, sems):
    i = pl.program_id(0)
    nblk = pl.num_programs(0)
    slot = lax.rem(i, 2)
    tm = o_ref.shape[0]
    tiles = tm // SUBLANES

    @pl.when(i == 0)
    def _():
        _gather_assigned_rows(y_hbm, d0_ref.at[0, 0], ybuf.at[0], sems.at[0], tm)

    @pl.when(i + 1 < nblk)
    def _():
        _gather_assigned_rows(y_hbm, dn_ref.at[0, 0], ybuf.at[1 - slot], sems.at[1 - slot], tm)

    pltpu.make_async_copy(y_tiles_hbm.at[pl.ds(0, TOP_K * tiles)], ybuf.at[slot], sems.at[slot]).wait()
    acc = x1_ref[...]
    for kk in range(TOP_K):
        rows = ybuf[slot, pl.ds(kk * tiles, tiles)].reshape(tm, ybuf.shape[-1])
        acc = acc + gate_ref[:, kk:kk + 1] * rows
    o_ref[...] = acc


def moe_combine(x1, gate, y_sorted, dest, tm=256):
    n, d = x1.shape
    n_rows, w = y_sorted.shape
    nblk = n // tm
    dest3 = dest.reshape(nblk, 1, tm * TOP_K)
    return pl.pallas_call(
        _combine_kernel,
        grid=(nblk,),
        in_specs=[pl.BlockSpec((1, 1, TOP_K * tm), lambda i: (0, 0, 0), memory_space=pltpu.SMEM),
                  pl.BlockSpec((1, 1, TOP_K * tm), lambda i: (jnp.minimum(i + 1, nblk - 1), 0, 0),
                               memory_space=pltpu.SMEM),
                  pl.BlockSpec((tm, d), lambda i: (i, 0)),
                  pl.BlockSpec((tm, TOP_K), lambda i: (i, 0)),
                  pl.BlockSpec(memory_space=pl.ANY),
                  pl.BlockSpec(memory_space=pl.ANY)],
        out_specs=pl.BlockSpec((tm, d), lambda i: (i, 0)),
        out_shape=jax.ShapeDtypeStruct((n, d), F32),
        scratch_shapes=[pltpu.VMEM((2, TOP_K * tm // SUBLANES, SUBLANES, w), y_sorted.dtype),
                        pltpu.SemaphoreType.DMA((2,))],
        compiler_params=_cparams(("arbitrary",)),
        name="moe_combine",
    )(dest3, dest3, x1, gate, y_sorted, y_sorted.reshape(n_rows // SUBLANES, SUBLANES, w))


def kernel(x, mem, positions, norm_mix, w_in, b_gate, a_q_gain, a_k_gain, r_mu, r_w0, r_w2, r_a0, r_a2,
           r_g2, r_k_k, r_k_a, r_r_k, r_ln_w, r_ln_b, mem_norm, w_mem_kv, c_q_gain, c_k_gain, w_branch,
           w_out, norm_ffn, router_w, router_b, exp_w1, exp_b1, exp_w2, exp_b2):
    b, s, d = x.shape
    n = b * s
    depth = norm_mix.shape[0]
    n_groups = len(A_GROUPS)
    qkv_cols = n_groups * A_WIDTH
    off_k, off_v, off_r = qkv_cols, 2 * qkv_cols, 3 * qkv_cols
    off_cq = off_r + R_COLS
    off_gate = off_cq + C_WIDTH

    x2d = x.reshape(n, d)
    pos_col = positions.reshape(n, 1).astype(jnp.int32)
    for l in range(depth):
        w_l = w_in[l]
        if l == 0:
            h, cos_t, sin_t = rmsnorm_rows(x2d, norm_mix[l], pos_col)
        else:
            h = rmsnorm_rows(x2d, norm_mix[l])
        qk_gains = rotary_gains(a_q_gain[l], a_k_gain[l])
        zr = project(h, w_l[:, off_r:off_cq], "plain", out_dtype=F32, tn=R_COLS // 2)
        cq = project(h, w_l[:, off_cq:off_gate], "headnorm", (c_q_gain[l].reshape(1, HEAD_DIM),))
        gates = project(h, w_l, "gate", (b_gate[l].reshape(1, -1),), tn=3 * MXU_COLS,
                        col0=off_gate, ncols=N_BRANCH * d)

        shp = lambda t: t.reshape(b, s, -1)
        outs, lses = [], []
        for g, (window, dilation) in enumerate(A_GROUPS):
            assert window // dilation == ATT_BLOCK
            qkv = project_qkv(h, w_l, g, n_groups, qk_gains, cos_t, sin_t, dilation)
            o, lse = band_attention_group(qkv, b, g, dilation)
            outs.append(o)
            lses.append(lse)

        yb = rwkv7_mix(shp(zr), r_mu[l], r_w0[l], r_w2[l], r_a0[l], r_a2[l], r_g2[l], r_k_k[l], r_k_a[l],
                       r_r_k[l].reshape(-1), r_ln_w[l], r_ln_b[l]).reshape(n, R_WIDTH)

        mlen = mem.shape[1]
        mem_n = rmsnorm_rows(mem.reshape(b * mlen, d), mem_norm[l])
        wkv = w_mem_kv[l]
        ck = project(mem_n, wkv[:, :C_WIDTH], "headnorm", (c_k_gain[l].reshape(1, HEAD_DIM),))
        cv = project(mem_n, wkv[:, C_WIDTH:], "plain")
        yc = cross_attention(shp(cq), ck.reshape(b, mlen, C_WIDTH), cv.reshape(b, mlen, C_WIDTH)).reshape(n, C_WIDTH)

        x1, h2p, top_idx, gate, rank, counts = merge_and_route(
            x2d, outs, lses, yb, yc, gates, w_branch[l], w_out[l], norm_ffn[l], router_w[l], router_b[l])
        pad_start, block_expert, zero_flag, n_used, next_expert, run_parity = block_layout(counts, n * TOP_K)
        dest = assignment_rows(top_idx, rank, pad_start)
        x_sorted = scatter_rows(h2p, dest, zero_flag, block_expert.shape[0] * MOE_ROWS)
        y_sorted = expert_ffn(x_sorted, block_expert, n_used, next_expert, run_parity,
                              exp_w1[l], exp_b1[l], exp_w2[l], exp_b2[l])
        x2d = moe_combine(x1, gate, y_sorted, dest)
    return x2d.reshape(b, s, d)
```

```python
import functools

import jax
import jax.numpy as jnp
from jax import lax
from jax.experimental import pallas as pl
from jax.experimental.pallas import tpu as pltpu

F32 = jnp.float32
BF16 = jnp.bfloat16

NORM_EPS = 1e-6
LANES = 128
SUBLANES = 8
HEAD_DIM = 128
A_GROUPS = ((128, 1), (512, 4), (2048, 16))
A_HEADS = 4
A_WIDTH = A_HEADS * HEAD_DIM
ATT_BLOCK = 128
ATT_QB = 4
ROT_DIM = 32
ROPE_THETA = 500000.0
R_HEAD = 64
R_HEADS = 8
R_WIDTH = R_HEADS * R_HEAD
R_DECAY_LORA = 64
R_AAA_LORA = 64
R_GATE_LORA = 128
R_COLS = 3 * R_WIDTH + R_DECAY_LORA + R_AAA_LORA + R_GATE_LORA
R_GN_EPS = 64e-5
R_CHUNK = 64
R_CPI = 4
MXU_COLS = 256
R_SEG = MXU_COLS
C_HEADS = 4
C_WIDTH = C_HEADS * HEAD_DIM
N_BRANCH = 3
N_EXPERTS = 32
TOP_K = 4
SWIGLU_ALPHA = 1.702
SWIGLU_LIMIT = 7.0
MOE_ROWS = 256
NEG_BIG = -1e30

VMEM_LIMIT = 56 * 1024 * 1024


def _cparams(sem):
    return pltpu.CompilerParams(dimension_semantics=sem, vmem_limit_bytes=VMEM_LIMIT)


def _rmsnorm_kernel(x_ref, g_ref, *refs):
    o_ref = refs[-3] if len(refs) > 1 else refs[0]
    x = x_ref[...]
    ms = jnp.mean(x * x, axis=-1, keepdims=True)
    o_ref[...] = (x * lax.rsqrt(ms + NORM_EPS) * g_ref[...]).astype(o_ref.dtype)
    if len(refs) > 1:
        pos_ref, freq_ref, _, cos_ref, sin_ref = refs
        ang = pos_ref[...].astype(F32) * freq_ref[...]
        lane = lax.broadcasted_iota(jnp.int32, ang.shape, 1)
        cos_ref[...] = jnp.cos(ang)
        s = jnp.sin(ang)
        half = ROT_DIM // 2
        sin_ref[...] = jnp.where(lane < half, -s, jnp.where(lane < ROT_DIM, s, 0.0))


def rmsnorm_rows(x2d, gain, pos_col=None, tm=512):
    n, d = x2d.shape
    tm = min(tm, n)
    rows = lambda w: pl.BlockSpec((tm, w), lambda i: (i, 0))
    in_specs = [rows(d), pl.BlockSpec((1, d), lambda i: (0, 0))]
    out_specs, out_shape, extra = rows(d), jax.ShapeDtypeStruct((n, d), BF16), ()
    if pos_col is not None:
        half = ROT_DIM // 2
        inv_freq = ROPE_THETA ** (-jnp.arange(half, dtype=F32) / half)
        freq_row = jnp.concatenate([inv_freq, inv_freq, jnp.zeros((LANES - ROT_DIM,), F32)]).reshape(1, LANES)
        in_specs += [rows(1), pl.BlockSpec((1, LANES), lambda i: (0, 0))]
        out_specs = [out_specs, rows(LANES), rows(LANES)]
        out_shape = [out_shape] + [jax.ShapeDtypeStruct((n, LANES), F32)] * 2
        extra = (pos_col, freq_row)
    return pl.pallas_call(
        _rmsnorm_kernel,
        grid=(n // tm,),
        in_specs=in_specs,
        out_specs=out_specs,
        out_shape=out_shape,
        compiler_params=_cparams(("arbitrary",)),
        name="rmsnorm_rows",
    )(x2d, gain.reshape(1, d), *extra)


def _head_mean_sq(zh):
    avg = jnp.full((HEAD_DIM, HEAD_DIM), 1.0 / HEAD_DIM, BF16)
    return jnp.dot((zh * zh).astype(BF16), avg, preferred_element_type=F32)


def _cast_weight_once(w_ref, w_scr):
    @pl.when(pl.program_id(1) == 0)
    def _():
        w_scr[...] = w_ref[...].astype(w_scr.dtype)


def _proj_kernel(*refs, mode):
    h_ref, w_ref = refs[0], refs[1]
    o_ref, w_scr = refs[-2], refs[-1]
    _cast_weight_once(w_ref, w_scr)
    z = jnp.dot(h_ref[...], w_scr[...], preferred_element_type=F32)
    if mode == "plain":
        o_ref[...] = z.astype(o_ref.dtype)
    elif mode == "gate":
        o_ref[...] = (0.5 * jnp.tanh(0.5 * (z + refs[2][...])) + 0.5).astype(o_ref.dtype)
    else:
        gain = refs[2][...]
        for c in range(z.shape[1] // HEAD_DIM):
            zh = z[:, c * HEAD_DIM:(c + 1) * HEAD_DIM]
            zn = zh * lax.rsqrt(_head_mean_sq(zh) + NORM_EPS) * gain
            o_ref[:, c * HEAD_DIM:(c + 1) * HEAD_DIM] = zn.astype(o_ref.dtype)


def project(h, w, mode, extras=(), out_dtype=BF16, tm=2048, tn=512, col0=0, ncols=None):
    n, k = h.shape
    m = w.shape[1] if ncols is None else ncols
    tm = min(tm, n)
    tn = min(tn, m)
    assert n % tm == 0 and m % tn == 0 and col0 % tn == 0
    cblk = col0 // tn
    in_specs = [pl.BlockSpec((tm, k), lambda j, i: (i, 0)),
                pl.BlockSpec((k, tn), lambda j, i: (0, j + cblk))]
    if mode == "gate":
        in_specs.append(pl.BlockSpec((1, tn), lambda j, i: (0, j)))
    elif mode == "headnorm":
        in_specs.append(pl.BlockSpec((1, HEAD_DIM), lambda j, i: (0, 0)))
    return pl.pallas_call(
        functools.partial(_proj_kernel, mode=mode),
        grid=(m // tn, n // tm),
        in_specs=in_specs,
        out_specs=pl.BlockSpec((tm, tn), lambda j, i: (i, j)),
        out_shape=jax.ShapeDtypeStruct((n, m), out_dtype),
        scratch_shapes=[pltpu.VMEM((k, tn), BF16)],
        compiler_params=_cparams(("arbitrary", "arbitrary")),
        name="proj_" + mode,
    )(h, w, *extras)


def _qkv_proj_kernel(h_ref, w_ref, gain_ref, cos_ref, sin_ref, o_ref, w_scr, z_scr, *, dilation):
    j = pl.program_id(0)
    _cast_weight_once(w_ref, w_scr)
    z = jnp.dot(h_ref[...], w_scr[...], preferred_element_type=F32)
    half = ROT_DIM // 2

    pair = 2 * HEAD_DIM

    def emit(c0, slab):
        if dilation == 1:
            o_ref[:, c0 * HEAD_DIM:c0 * HEAD_DIM + pair] = slab.astype(o_ref.dtype)
        else:
            z_scr[c0] = slab[:, :HEAD_DIM]
            z_scr[c0 + 1] = slab[:, HEAD_DIM:]

    @pl.when(j < 2)
    def _():
        mi = lax.broadcasted_iota(jnp.int32, (pair, pair), 0)
        li = lax.broadcasted_iota(jnp.int32, (pair, pair), 1)
        same_head = (mi // HEAD_DIM) == (li // HEAD_DIM)
        lh, mh = li % HEAD_DIM, mi % HEAD_DIM
        avg = jnp.where(same_head, 1.0 / HEAD_DIM, 0.0).astype(BF16)
        perm = (same_head & (((lh < half) & (mh == lh + half))
                             | ((lh >= half) & (lh < ROT_DIM) & (mh == lh - half)))).astype(BF16)
        gcos = gain_ref[0, 0:1, :] * cos_ref[...]
        gsin = gain_ref[0, 1:2, :] * sin_ref[...]
        gcos = jnp.concatenate([gcos, gcos], axis=1)
        gsin = jnp.concatenate([gsin, gsin], axis=1)
        for c0 in range(0, A_HEADS, 2):
            zz = z[:, c0 * HEAD_DIM:c0 * HEAD_DIM + pair]
            ms = jnp.dot((zz * zz).astype(BF16), avg, preferred_element_type=F32)
            partner = jnp.dot(zz.astype(BF16), perm, preferred_element_type=F32)
            emit(c0, lax.rsqrt(ms + NORM_EPS) * (zz * gcos + partner * gsin))

    @pl.when(j == 2)
    def _():
        for c0 in range(0, A_HEADS, 2):
            emit(c0, z[:, c0 * HEAD_DIM:c0 * HEAD_DIM + pair])

    if dilation > 1:
        rows = z_scr.shape[1] // dilation
        for r in range(dilation):
            for c in range(A_HEADS):
                lo = r * A_WIDTH + c * HEAD_DIM
                o_ref[:, lo:lo + HEAD_DIM] = z_scr[c, pl.ds(r, rows, stride=dilation), :].astype(o_ref.dtype)


def rotary_gains(q_gain, k_gain):
    half = ROT_DIM // 2
    lane = jnp.arange(HEAD_DIM)
    partner = jnp.where(lane < half, lane + half, jnp.where(lane < ROT_DIM, lane - half, lane))
    return jnp.stack([jnp.stack([g, g[partner]]) for g in (q_gain, k_gain)])


def project_qkv(h, w_all, group, n_groups, gains, cos_t, sin_t, dilation, tm=2048):
    n, k = h.shape
    d = dilation
    return pl.pallas_call(
        functools.partial(_qkv_proj_kernel, dilation=d),
        grid=(3, n // tm),
        in_specs=[pl.BlockSpec((tm, k), lambda j, i: (i, 0)),
                  pl.BlockSpec((k, A_WIDTH), lambda j, i: (0, j * n_groups + group)),
                  pl.BlockSpec((1, 2, HEAD_DIM), lambda j, i: (jnp.minimum(j, 1), 0, 0)),
                  pl.BlockSpec((tm, LANES), lambda j, i: (i, 0)),
                  pl.BlockSpec((tm, LANES), lambda j, i: (i, 0))],
        out_specs=pl.BlockSpec((tm // d, d * A_WIDTH), lambda j, i: (i, j)),
        out_shape=jax.ShapeDtypeStruct((n // d, 3 * d * A_WIDTH), BF16),
        scratch_shapes=[pltpu.VMEM((k, A_WIDTH), BF16), pltpu.VMEM((A_HEADS, tm, HEAD_DIM), F32)],
        compiler_params=_cparams(("arbitrary", "arbitrary")),
        name=f"proj_qkv_d{d}",
    )(h, w_all, gains, cos_t, sin_t)


def _band_attn_kernel(*refs, qb):
    q_ref = refs[0]
    k_refs = refs[1:qb + 2]
    v_refs = refs[qb + 2:2 * qb + 3]
    o_ref, lse_ref = refs[-2:]
    step = pl.program_id(2)
    scale = HEAD_DIM ** -0.5
    nq = ATT_BLOCK
    qi = lax.broadcasted_iota(jnp.int32, (nq, 2 * nq), 0)
    ki = lax.broadcasted_iota(jnp.int32, (nq, 2 * nq), 1)
    rel = qi + nq - ki
    band = (rel >= 0) & (rel <= nq)
    for a in range(qb):
        blk = step * qb + a
        valid = band & ((blk * nq - nq + ki) >= 0)
        rows = slice(a * nq, (a + 1) * nq)
        lses = []
        for h in range(A_HEADS):
            sl = slice(h * HEAD_DIM, (h + 1) * HEAD_DIM)
            qh = q_ref[0, rows, sl]
            kh = jnp.concatenate([k_refs[a][0, :, sl], k_refs[a + 1][0, :, sl]], axis=0)
            vh = jnp.concatenate([v_refs[a][0, :, sl], v_refs[a + 1][0, :, sl]], axis=0)
            s = lax.dot_general(qh, kh, (((1,), (1,)), ((), ())), preferred_element_type=F32) * scale
            s = jnp.where(valid, s, NEG_BIG)
            m = jnp.max(s, axis=-1, keepdims=True)
            p = jnp.exp(s - m)
            l = jnp.sum(p, axis=-1, keepdims=True)
            o = jnp.dot(p.astype(BF16), vh, preferred_element_type=F32) / l
            o_ref[0, rows, sl] = o.astype(o_ref.dtype)
            lses.append(jnp.broadcast_to(m + jnp.log(l), (nq, LANES // A_HEADS)))
        lse_ref[0, rows, :] = jnp.concatenate(lses, axis=1)


def band_attention_group(qkv, b, g, dilation):
    d = dilation
    sub = qkv.shape[0] // b
    nblk = sub // ATT_BLOCK
    qb = min(ATT_QB, nblk)
    assert nblk % qb == 0
    view = qkv.reshape(b, sub, 3 * d * A_WIDTH)
    qrows = qb * ATT_BLOCK
    key_spec = lambda t, m: pl.BlockSpec(
        (1, ATT_BLOCK, A_WIDTH), lambda bi, r, j: (bi, jnp.maximum(j * qb - 1 + m, 0), t * d + r))
    o, lse = pl.pallas_call(
        functools.partial(_band_attn_kernel, qb=qb),
        grid=(b, d, nblk // qb),
        in_specs=[pl.BlockSpec((1, qrows, A_WIDTH), lambda bi, r, j: (bi, j, r))]
                 + [key_spec(1, m) for m in range(qb + 1)] + [key_spec(2, m) for m in range(qb + 1)],
        out_specs=[pl.BlockSpec((1, qrows, A_WIDTH), lambda bi, r, j: (bi, j, r)),
                   pl.BlockSpec((1, qrows, LANES), lambda bi, r, j: (bi, j, r))],
        out_shape=[jax.ShapeDtypeStruct((b, sub, d * A_WIDTH), BF16),
                   jax.ShapeDtypeStruct((b, sub, d * LANES), F32)],
        compiler_params=_cparams(("arbitrary", "arbitrary", "arbitrary")),
        name=f"band_attn_g{g}",
    )(*([view] * (2 * qb + 3)))
    return o.reshape(b * sub, d * A_WIDTH), lse.reshape(b * sub, d * LANES)


def _cross_attn_kernel(q_ref, k_ref, v_ref, o_ref):
    scale = HEAD_DIM ** -0.5
    for h in range(C_HEADS):
        sl = slice(h * HEAD_DIM, (h + 1) * HEAD_DIM)
        s = lax.dot_general(q_ref[0, :, sl], k_ref[0, :, sl], (((1,), (1,)), ((), ())),
                            preferred_element_type=F32) * scale
        m = jnp.max(s, axis=-1, keepdims=True)
        p = jnp.exp(s - m)
        l = jnp.sum(p, axis=-1, keepdims=True)
        o = jnp.dot(p.astype(BF16), v_ref[0, :, sl], preferred_element_type=F32) / l
        o_ref[0, :, sl] = o.astype(o_ref.dtype)


def cross_attention(qn, kn, v, tm=512):
    b, s, w = qn.shape
    m = kn.shape[1]
    return pl.pallas_call(
        _cross_attn_kernel,
        grid=(b, s // tm),
        in_specs=[pl.BlockSpec((1, tm, w), lambda bi, i: (bi, i, 0)),
                  pl.BlockSpec((1, m, w), lambda bi, i: (bi, 0, 0)),
                  pl.BlockSpec((1, m, w), lambda bi, i: (bi, 0, 0))],
        out_specs=pl.BlockSpec((1, tm, w), lambda bi, i: (bi, i, 0)),
        out_shape=jax.ShapeDtypeStruct((b, s, w), BF16),
        compiler_params=_cparams(("arbitrary", "arbitrary")),
        name="cross_attn",
    )(qn, kn, v)


def _head_sums(x, seg):
    w = seg.shape[0]
    x16 = x.astype(BF16)
    return jnp.concatenate(
        [jnp.dot(x16[:, j:j + w], seg, preferred_element_type=F32) for j in range(0, x.shape[1], w)], axis=1)


def _dot_t(a, b):
    return lax.dot_general(a, b, (((0,), (0,)), ((), ())), preferred_element_type=F32)


def _dot_nt(a, b):
    return lax.dot_general(a, b, (((1,), (1,)), ((), ())), preferred_element_type=F32)


def _rwkv_kernel(zr_ref, mu_ref, w0_ref, wwa_ref, a0_ref, g2_ref, kk_ref, ka_ref, rk_ref,
                 lnw_ref, lnb_ref, seg_ref, y_ref,
                 state_ref, carry_ref, ops_ref, yh_ref):
    t = pl.program_id(1)
    tt = zr_ref.shape[1]
    nch = tt // R_CHUNK
    c = R_CHUNK

    @pl.when(t == 0)
    def _():
        state_ref[...] = jnp.zeros_like(state_ref)
        carry_ref[...] = jnp.zeros_like(carry_ref)

    z = zr_ref[0]
    row = lax.broadcasted_iota(jnp.int32, z.shape, 0)
    prev = jnp.where(row == 0, carry_ref[...], pltpu.roll(z, 1, 0))
    carry_ref[...] = z[tt - 1:tt, :]
    xs = z + (prev - z) * mu_ref[...]

    w3 = 3 * R_WIDTH
    r = xs[:, 0:R_WIDTH]
    k = xs[:, R_WIDTH:2 * R_WIDTH]
    v = xs[:, 2 * R_WIDTH:w3]
    wa_lo = xs[:, w3:w3 + LANES]
    g_lo = xs[:, w3 + LANES:w3 + 2 * LANES]
    lane = lax.broadcasted_iota(jnp.int32, wa_lo.shape, 1)
    wa_in = jnp.where(lane < R_DECAY_LORA, jnp.tanh(wa_lo), wa_lo)
    wa = jnp.dot(wa_in.astype(BF16), wwa_ref[...], preferred_element_type=F32)
    u = -(w0_ref[...] + wa[:, :R_WIDTH])
    softplus = jnp.maximum(u, 0.0) + jnp.log(1.0 + jnp.exp(-jnp.abs(u)))
    w_raw = -softplus - 0.5
    ld = -jnp.exp(w_raw)
    a = jax.nn.sigmoid(a0_ref[...] + wa[:, R_WIDTH:])
    g = jnp.dot(jax.nn.sigmoid(g_lo).astype(BF16), g2_ref[...], preferred_element_type=F32)

    seg = seg_ref[...]
    kk = k * kk_ref[...]
    kk = kk * jnp.minimum(lax.rsqrt(_head_sums(kk * kk, seg)), 1e12)
    k2 = k * (1.0 + (a - 1.0) * ka_ref[...])
    bonus = _head_sums(r * k2 * rk_ref[...], seg) * v

    ri = lax.broadcasted_iota(jnp.int32, (c, c), 0)
    ci = lax.broadcasted_iota(jnp.int32, (c, c), 1)
    tri = (ci <= ri).astype(BF16)
    ld_hi = ld.astype(BF16)
    ld_lo = (ld - ld_hi.astype(F32)).astype(BF16)
    lcs = []
    for ch in range(nch):
        rs = slice(ch * c, (ch + 1) * c)
        lcs.append(jnp.dot(tri, ld_hi[rs], preferred_element_type=F32)
                   + jnp.dot(tri, ld_lo[rs], preferred_element_type=F32))
    lc = jnp.concatenate(lcs, axis=0)
    e_inc = jnp.exp(lc)
    e_exc = jnp.exp(lc - ld)
    e_inv = jnp.exp(-lc)
    a_t = -kk * e_exc
    r_t = r * e_inc
    b_t = kk * a * e_inv
    k_t = k2 * e_inv
    for h in range(R_HEADS):
        hs = slice(h * R_HEAD, (h + 1) * R_HEAD)
        ops_ref[0, h] = a_t[:, hs]
        ops_ref[1, h] = r_t[:, hs]
        ops_ref[2, h] = b_t[:, hs]
        ops_ref[3, h] = k_t[:, hs]
        ops_ref[4, h] = v[:, hs]
        ops_ref[5, h] = e_inc[:, hs]

    strict = ci < ri
    incl = ci <= ri
    eye = (ci == ri)

    ri2 = lax.broadcasted_iota(jnp.int32, (c, 2 * c), 0)
    ci2 = lax.broadcasted_iota(jnp.int32, (c, 2 * c), 1)
    incl2 = jnp.bitwise_and(ci2, c - 1) <= ri2
    eye_f = jnp.where(eye, 1.0, 0.0)
    heads = range(R_HEADS)
    dot = functools.partial(jnp.dot, preferred_element_type=F32)

    def chunk_body(ch, _):
        starts = [pl.multiple_of((ch * R_CPI + sub) * c, c) for sub in range(R_CPI)]
        rows = [pl.ds(r0, c) for r0 in starts]
        items = [(sub, h) for sub in range(R_CPI) for h in heads]
        idx = range(len(items))
        at = [ops_ref[0, h, rows[sub], :] for sub, h in items]
        rt = [ops_ref[1, h, rows[sub], :] for sub, h in items]
        bt = [ops_ref[2, h, rows[sub], :] for sub, h in items]
        kt = [ops_ref[3, h, rows[sub], :] for sub, h in items]
        pc = [ops_ref[5, h, pl.ds(starts[sub] + c - 1, 1), :] for sub, h in items]
        at16 = [x.astype(BF16) for x in at]
        rt16 = [x.astype(BF16) for x in rt]
        bt16 = [x.astype(BF16) for x in bt]
        kt16 = [x.astype(BF16) for x in kt]
        v16 = [ops_ref[4, h, rows[sub], :].astype(BF16) for sub, h in items]
        bk16 = [jnp.concatenate([bt16[i], kt16[i]], axis=0) for i in idx]
        nmat = [jnp.where(strict, _dot_nt(at16[i], bt16[i]), 0.0) for i in idx]
        a_ak = [jnp.where(strict, _dot_nt(at16[i], kt16[i]), 0.0).astype(BF16) for i in idx]
        a_rbk = [jnp.where(incl2, _dot_nt(rt16[i], bk16[i]), 0.0).astype(BF16) for i in idx]
        npow = nmat
        tinv = [eye_f + nmat[i] for i in idx]
        for _i in range(5):
            np16 = [x.astype(BF16) for x in npow]
            npow = [dot(np16[i], np16[i]) for i in idx]
            tinv = [tinv[i] + dot(tinv[i].astype(BF16), npow[i].astype(BF16)) for i in idx]
        akv = [dot(a_ak[i], v16[i]).astype(BF16) for i in idx]
        apw1 = [dot(tinv[i].astype(BF16), jnp.concatenate([at16[i], akv[i]], axis=1)).astype(BF16)
                for i in idx]
        zero = jnp.zeros((c, R_HEAD), BF16)
        rhs2 = [jnp.concatenate([apw1[i], jnp.concatenate([zero, v16[i]], axis=1)], axis=0)
                for i in idx]
        bkh = [jnp.concatenate([bt[i] * pc[i], kt[i] * pc[i]], axis=0).astype(BF16) for i in idx]
        gh = [_dot_t(bkh[i], rhs2[i]) for i in idx]
        qy = [dot(a_rbk[i], rhs2[i]) for i in idx]
        for i, (sub, h) in enumerate(items):
            gm = jnp.where(eye, jnp.broadcast_to(pc[i], (c, c)), 0.0) + gh[i][:, :R_HEAD]
            qp = rt[i] + qy[i][:, :R_HEAD]
            st = state_ref[h]
            res = dot(jnp.concatenate([qp, gm], axis=0).astype(BF16), st.astype(BF16))
            yh_ref[h, rows[sub], :] = res[:c] + qy[i][:, R_HEAD:]
            state_ref[h] = res[c:] + gh[i][:, R_HEAD:]
        return 0

    lax.fori_loop(0, nch // R_CPI, chunk_body, 0)

    y = jnp.concatenate([yh_ref[h] for h in range(R_HEADS)], axis=1)
    mean = _head_sums(y, seg) * (1.0 / R_HEAD)
    dlt = y - mean
    var = _head_sums(dlt * dlt, seg) * (1.0 / R_HEAD)
    yn = dlt * lax.rsqrt(var + R_GN_EPS) * lnw_ref[...] + lnb_ref[...]
    y_ref[0] = ((yn + bonus) * g).astype(y_ref.dtype)


def rwkv7_mix(zr, mu, w0, w2, a0, a2, g2, k_k, k_a, r_k, ln_w, ln_b, tt=256):
    b, s, cols = zr.shape
    row = lambda x: x.reshape(1, -1).astype(F32)
    wwa = jnp.zeros((LANES, 2 * R_WIDTH), F32)
    wwa = wwa.at[:R_DECAY_LORA, :R_WIDTH].set(w2).at[R_DECAY_LORA:, R_WIDTH:].set(a2).astype(BF16)
    hid = jnp.arange(R_SEG) // R_HEAD
    seg = (hid[:, None] == hid[None, :]).astype(BF16)
    full = lambda shape: pl.BlockSpec(shape, lambda bi, t: (0,) * len(shape))
    return pl.pallas_call(
        _rwkv_kernel,
        grid=(b, s // tt),
        in_specs=[pl.BlockSpec((1, tt, cols), lambda bi, t: (bi, t, 0)),
                  full((1, cols)), full((1, R_WIDTH)), full((LANES, 2 * R_WIDTH)), full((1, R_WIDTH)),
                  full((R_GATE_LORA, R_WIDTH)), full((1, R_WIDTH)), full((1, R_WIDTH)), full((1, R_WIDTH)),
                  full((1, R_WIDTH)), full((1, R_WIDTH)), full((R_SEG, R_SEG))],
        out_specs=pl.BlockSpec((1, tt, R_WIDTH), lambda bi, t: (bi, t, 0)),
        out_shape=jax.ShapeDtypeStruct((b, s, R_WIDTH), BF16),
        scratch_shapes=[pltpu.VMEM((R_HEADS, R_HEAD, R_HEAD), F32),
                        pltpu.VMEM((1, cols), F32),
                        pltpu.VMEM((6, R_HEADS, tt, R_HEAD), F32),
                        pltpu.VMEM((R_HEADS, tt, R_HEAD), F32)],
        compiler_params=_cparams(("arbitrary", "arbitrary")),
        name="rwkv7_mix",
    )(zr, row(mu), row(w0), wwa, row(a0), g2.astype(BF16), row(k_k), row(k_a), row(r_k),
      row(ln_w), row(ln_b), seg)


def _merge_kernel(x_ref, o0_ref, o1_ref, o2_ref, l0_ref, l1_ref, l2_ref, yb_ref, yc_ref, gt_ref,
                  wb_ref, wo_ref, gn_ref, rw_ref, rb_ref,
                  x1_ref, h2_ref, gate_ref, route_ref, cnt_ref, base_ref, o_scr, l_scr):
    @pl.when(pl.program_id(0) == 0)
    def _():
        base_ref[...] = jnp.zeros_like(base_ref)

    tm_rows = x_ref.shape[0]
    for gi, (o_ref, l_ref) in enumerate(((o0_ref, l0_ref), (o1_ref, l1_ref), (o2_ref, l2_ref))):
        dil = A_GROUPS[gi][1]
        for r in range(dil):
            dst = pl.ds(r, tm_rows // dil, stride=dil) if dil > 1 else slice(None)
            for h in range(A_HEADS):
                lo = r * A_WIDTH + h * HEAD_DIM
                o_scr[gi, h, dst, :] = o_ref[:, lo:lo + HEAD_DIM].astype(F32)
            l_scr[gi, dst, :] = l_ref[:, r * LANES:(r + 1) * LANES]
    lses = [l_scr[gi] for gi in range(3)]
    lmax = jnp.maximum(jnp.maximum(lses[0], lses[1]), lses[2])
    es = [jnp.exp(l - lmax) for l in lses]
    inv = 1.0 / (es[0] + es[1] + es[2])
    qw = LANES // A_HEADS
    heads = []
    for h in range(A_HEADS):
        sl = slice(h * HEAD_DIM, (h + 1) * HEAD_DIM)
        acc = None
        for gi in range(3):
            alpha = (es[gi] * inv)[:, h * qw:h * qw + 1]
            term = alpha * o_scr[gi, h]
            acc = term if acc is None else acc + term
        heads.append(acc)
    ya = jnp.concatenate(heads, axis=1).astype(BF16)
    d = x_ref.shape[1]
    merged = None
    for n, yn in enumerate((ya, yb_ref[...], yc_ref[...])):
        proj = jnp.dot(yn, wb_ref[n], preferred_element_type=F32)
        term = gt_ref[:, n * d:(n + 1) * d].astype(F32) * proj
        merged = term if merged is None else merged + term
    x1 = x_ref[...] + jnp.dot(merged.astype(BF16), wo_ref[...], preferred_element_type=F32)
    x1_ref[...] = x1
    ms = jnp.mean(x1 * x1, axis=-1, keepdims=True)
    h2 = x1 * lax.rsqrt(ms + NORM_EPS) * gn_ref[...]
    h2_ref[...] = h2
    rw = rw_ref[...]
    h2_hi, rw_hi = h2.astype(BF16), rw.astype(BF16)
    h2_lo = (h2 - h2_hi.astype(F32)).astype(BF16)
    rw_lo = (rw - rw_hi.astype(F32)).astype(BF16)
    logits = (jnp.dot(h2_hi, rw_hi, preferred_element_type=F32) + jnp.dot(h2_hi, rw_lo, preferred_element_type=F32)
              + jnp.dot(h2_lo, rw_hi, preferred_element_type=F32)) + rb_ref[...]
    tm = logits.shape[0]
    lane = lax.broadcasted_iota(jnp.int32, logits.shape, 1)
    vals, idxs = [], []
    cur = logits
    for _k in range(TOP_K):
        m = jnp.max(cur, axis=-1, keepdims=True)
        ik = jnp.min(jnp.where(cur == m, lane, N_EXPERTS), axis=-1, keepdims=True)
        vals.append(m)
        idxs.append(ik)
        cur = jnp.where(lane == ik, -jnp.inf, cur)
    exps = [jnp.exp(vk - vals[0]) for vk in vals]
    tot = exps[0] + exps[1] + exps[2] + exps[3]
    onehots = [lane == ik for ik in idxs]
    hits = sum(jnp.where(oh, 1.0, 0.0) for oh in onehots)
    ri = lax.broadcasted_iota(jnp.int32, (tm, tm), 0)
    ci = lax.broadcasted_iota(jnp.int32, (tm, tm), 1)
    before = jnp.dot((ci < ri).astype(BF16), hits.astype(BF16), preferred_element_type=F32) + base_ref[...]
    lane_w = lax.broadcasted_iota(jnp.int32, (tm, LANES), 1)
    slab = jnp.zeros((tm, LANES), F32)
    for kk in range(TOP_K):
        gate_ref[:, kk:kk + 1] = exps[kk] / tot
        rank_k = jnp.sum(jnp.where(onehots[kk], before, 0.0), axis=-1, keepdims=True)
        slab = jnp.where(lane_w == kk, idxs[kk].astype(F32), slab)
        slab = jnp.where(lane_w == TOP_K + kk, rank_k, slab)
    route_ref[...] = slab.T[:2 * TOP_K, :].astype(jnp.int32)
    base_ref[...] = base_ref[...] + jnp.sum(hits, axis=0, keepdims=True)
    cnt_ref[...] = base_ref[...].astype(jnp.int32)


def merge_and_route(x2d, outs, lses, yb, yc, gates, w_branch, w_out, norm_ffn, router_w, router_b, tm=512):
    n, d = x2d.shape
    rows = lambda w: pl.BlockSpec((tm, w), lambda i: (i, 0))
    packed = lambda w, dil: pl.BlockSpec((tm // dil, dil * w), lambda i: (i, 0))
    full = lambda shape: pl.BlockSpec(shape, lambda i: (0,) * len(shape))
    return pl.pallas_call(
        _merge_kernel,
        grid=(n // tm,),
        in_specs=[rows(d)] + [packed(A_WIDTH, dil) for _w, dil in A_GROUPS] + [packed(LANES, dil) for _w, dil in A_GROUPS]
                 + [rows(R_WIDTH), rows(C_WIDTH), rows(N_BRANCH * d),
                    full((N_BRANCH, A_WIDTH, d)), full((d, d)), full((1, d)), full((d, N_EXPERTS)), full((1, N_EXPERTS))],
        out_specs=[rows(d), rows(d), rows(TOP_K), pl.BlockSpec((2 * TOP_K, tm), lambda i: (0, i)),
                   full((1, N_EXPERTS))],
        out_shape=[jax.ShapeDtypeStruct((n, d), F32), jax.ShapeDtypeStruct((n, d), F32),
                   jax.ShapeDtypeStruct((n, TOP_K), F32), jax.ShapeDtypeStruct((2 * TOP_K, n), jnp.int32),
                   jax.ShapeDtypeStruct((1, N_EXPERTS), jnp.int32)],
        scratch_shapes=[pltpu.VMEM((1, N_EXPERTS), F32), pltpu.VMEM((len(A_GROUPS), A_HEADS, tm, HEAD_DIM), F32),
                        pltpu.VMEM((len(A_GROUPS), tm, LANES), F32)],
        compiler_params=_cparams(("arbitrary",)),
        name="merge_route",
    )(x2d, *outs, *lses, yb, yc, gates, w_branch.astype(BF16), w_out.astype(BF16),
      norm_ffn.reshape(1, d), router_w, router_b.reshape(1, N_EXPERTS))


def block_layout(counts, n_assign):
    counts = counts.reshape(-1)
    padded = (counts + MOE_ROWS - 1) // MOE_ROWS * MOE_ROWS
    pad_end = jnp.cumsum(padded)
    pad_start = (pad_end - padded).astype(jnp.int32)
    n_blocks = -(-n_assign // MOE_ROWS) + N_EXPERTS
    blk_row = jnp.arange(n_blocks, dtype=jnp.int32) * MOE_ROWS
    owner = jnp.sum((blk_row[:, None] >= pad_end[None, :]).astype(jnp.int32), axis=1)
    block_expert = jnp.minimum(owner, N_EXPERTS - 1).astype(jnp.int32)
    unused = blk_row >= pad_end[-1]
    zero_flag = (unused | (blk_row + MOE_ROWS == pad_end[block_expert])).astype(jnp.int32)
    n_used = (pad_end[-1:] // MOE_ROWS).astype(jnp.int32)
    has_rows = counts > 0
    eid = jnp.arange(N_EXPERTS, dtype=jnp.int32)
    later = jnp.where(has_rows[None, :] & (eid[None, :] > eid[:, None]), eid[None, :], N_EXPERTS)
    next_expert = jnp.min(later, axis=1)
    next_expert = jnp.where(next_expert == N_EXPERTS, -1, next_expert).astype(jnp.int32)
    run_parity = ((jnp.cumsum(has_rows) - has_rows) % 2).astype(jnp.int32)
    return pad_start, block_expert, zero_flag, n_used, next_expert, run_parity


def _dest_kernel(ps_ref, route_ref, dest_ref):
    idx = route_ref[:TOP_K, :]
    dest = route_ref[TOP_K:, :]
    for e in range(N_EXPERTS):
        dest = dest + jnp.where(idx == e, ps_ref[e], 0)
    dest_ref[...] = dest


def assignment_rows(route, pad_start):
    n = route.shape[1]
    return pl.pallas_call(
        _dest_kernel,
        grid_spec=pltpu.PrefetchScalarGridSpec(
            num_scalar_prefetch=1, grid=(1,),
            in_specs=[pl.BlockSpec((2 * TOP_K, n), lambda i, ps: (0, 0))],
            out_specs=pl.BlockSpec((TOP_K, n), lambda i, ps: (0, 0))),
        out_shape=jax.ShapeDtypeStruct((TOP_K, n), jnp.int32),
        compiler_params=_cparams(("arbitrary",)),
        name="assignment_rows",
    )(pad_start, route)


def _scatter_kernel(zf_ref, dest_ref, h2_ref, xs_hbm, zeros_ref, sem, zsem):
    i = pl.program_id(0)
    tiles = h2_ref.shape[0]
    tm = tiles * SUBLANES
    nblk = zf_ref.shape[0]

    def zero_block(j):
        return pltpu.make_async_copy(zeros_ref, xs_hbm.at[pl.ds(j * MOE_ROWS, MOE_ROWS)], zsem)

    @pl.when(i == 0)
    def _():
        zeros_ref[...] = jnp.zeros_like(zeros_ref)

        def start(j, _):
            @pl.when(zf_ref[j] != 0)
            def _():
                zero_block(j).start()
            return 0

        def wait(j, _):
            @pl.when(zf_ref[j] != 0)
            def _():
                zero_block(j).wait()
            return 0

        lax.fori_loop(0, nblk, start, 0)
        lax.fori_loop(0, nblk, wait, 0)

    def body(g, _):
        for u in range(SUBLANES):
            for kk in range(TOP_K):
                row = dest_ref[kk, g * SUBLANES + u]
                pltpu.make_async_copy(h2_ref.at[g, pl.ds(u, 1)], xs_hbm.at[pl.ds(row, 1)], sem).start(
                    priority=kk % 2)
        return 0

    lax.fori_loop(0, tiles, body, 0)
    pltpu.make_async_copy(xs_hbm.at[pl.ds(0, tm * TOP_K)], xs_hbm.at[pl.ds(0, tm * TOP_K)], sem).wait()


def scatter_rows(h2, dest, zero_flag, n_rows, tm=1024):
    n, w = h2.shape
    grid_spec = pltpu.PrefetchScalarGridSpec(
        num_scalar_prefetch=1,
        grid=(n // tm,),
        in_specs=[pl.BlockSpec((TOP_K, tm), lambda i, zf: (0, i), memory_space=pltpu.SMEM),
                  pl.BlockSpec((tm // SUBLANES, SUBLANES, w), lambda i, zf: (i, 0, 0))],
        out_specs=pl.BlockSpec(memory_space=pl.ANY),
        scratch_shapes=[pltpu.VMEM((MOE_ROWS, w), h2.dtype), pltpu.SemaphoreType.DMA(()),
                        pltpu.SemaphoreType.DMA(())],
    )
    return pl.pallas_call(
        _scatter_kernel,
        grid_spec=grid_spec,
        out_shape=jax.ShapeDtypeStruct((n_rows, w), h2.dtype),
        compiler_params=_cparams(("arbitrary",)),
        name="scatter_rows",
    )(zero_flag, dest, h2.reshape(n // SUBLANES, SUBLANES, w))


def _expert_kernel(be_ref, nu_ref, nxt_ref, par_ref, xs_ref, w1_hbm, b1_ref, w2_hbm, b2_ref, y_ref,
                   w1_f32, w2_f32, w1_scr, w2_scr, wsem):
    i = pl.program_id(0)
    e = be_ref[i]
    prev = be_ref[jnp.maximum(i - 1, 0)]

    def fetch(expert, slot):
        return (pltpu.make_async_copy(w1_hbm.at[expert], w1_f32.at[slot], wsem.at[slot]),
                pltpu.make_async_copy(w2_hbm.at[expert], w2_f32.at[slot], wsem.at[slot]))

    @pl.when((i < nu_ref[0]) & ((i == 0) | (e != prev)))
    def _():
        slot = par_ref[e]

        @pl.when(i == 0)
        def _():
            for c in fetch(e, slot):
                c.start()

        for c in fetch(e, slot):
            c.wait()

        @pl.when(nxt_ref[e] >= 0)
        def _():
            for c in fetch(nxt_ref[e], 1 - slot):
                c.start()

        w1_scr[...] = w1_f32[slot].astype(BF16)
        w2_scr[...] = w2_f32[slot].astype(BF16)

    @pl.when(i < nu_ref[0])
    def _():
        xb = xs_ref[...].astype(BF16)
        dff = w2_scr.shape[0]
        u = jnp.dot(xb, w1_scr[...], preferred_element_type=F32) + b1_ref[0]
        glu = jnp.minimum(u[:, :dff], SWIGLU_LIMIT)
        lin = jnp.clip(u[:, dff:], -SWIGLU_LIMIT, SWIGLU_LIMIT)
        act = glu * jax.nn.sigmoid(SWIGLU_ALPHA * glu) * (lin + 1.0)
        y = jnp.dot(act.astype(BF16), w2_scr[...], preferred_element_type=F32) + b2_ref[0]
        y_ref[...] = y

    @pl.when(i >= nu_ref[0])
    def _():
        y_ref[...] = jnp.zeros_like(y_ref)


def expert_ffn(x_sorted, block_expert, n_used, next_expert, run_parity, w1, b1, w2, b2):
    n_rows, d = x_sorted.shape
    w = d
    nblk = n_rows // MOE_ROWS
    dff2 = w1.shape[2]
    used = lambda i, nu: jnp.minimum(i, nu[0] - 1)
    grid_spec = pltpu.PrefetchScalarGridSpec(
        num_scalar_prefetch=4,
        grid=(nblk,),
        in_specs=[
            pl.BlockSpec((MOE_ROWS, w), lambda i, be, nu, nx, pa: (used(i, nu), 0)),
            pl.BlockSpec(memory_space=pl.ANY),
            pl.BlockSpec((1, 1, dff2), lambda i, be, nu, nx, pa: (be[used(i, nu)], 0, 0)),
            pl.BlockSpec(memory_space=pl.ANY),
            pl.BlockSpec((1, 1, d), lambda i, be, nu, nx, pa: (be[used(i, nu)], 0, 0)),
        ],
        out_specs=pl.BlockSpec((MOE_ROWS, w), lambda i, be, nu, nx, pa: (i, 0)),
        scratch_shapes=[pltpu.VMEM((2, d, dff2), F32), pltpu.VMEM((2, dff2 // 2, d), F32),
                        pltpu.VMEM((d, dff2), BF16), pltpu.VMEM((dff2 // 2, d), BF16),
                        pltpu.SemaphoreType.DMA((2,))],
    )
    return pl.pallas_call(
        _expert_kernel,
        grid_spec=grid_spec,
        out_shape=jax.ShapeDtypeStruct((n_rows, w), F32),
        compiler_params=_cparams(("arbitrary",)),
        name="expert_ffn",
    )(block_expert, n_used, next_expert, run_parity, x_sorted, w1, b1.reshape(N_EXPERTS, 1, dff2), w2,
      b2.reshape(N_EXPERTS, 1, d))


def _gather_assigned_rows(y_hbm, dest_ref, dst_ref, sem, tm):
    tiles = tm // SUBLANES

    def body(g, _):
        for u in range(SUBLANES):
            for kk in range(TOP_K):
                row = dest_ref[kk, g * SUBLANES + u]
                pltpu.make_async_copy(y_hbm.at[pl.ds(row, 1)], dst_ref.at[kk * tiles + g, pl.ds(u, 1)],
                                      sem).start(priority=kk % 2)
        return 0
    lax.fori_loop(0, tiles, body, 0)


def _combine_kernel(d0_ref, dn_ref, x1_ref, gate_ref, y_hbm, y_tiles_hbm, o_ref, ybuf, sems):
    i = pl.program_id(0)
    nblk = pl.num_programs(0)
    slot = lax.rem(i, 2)
    tm = o_ref.shape[0]
    tiles = tm // SUBLANES

    @pl.when(i == 0)
    def _():
        _gather_assigned_rows(y_hbm, d0_ref, ybuf.at[0], sems.at[0], tm)

    @pl.when(i + 1 < nblk)
    def _():
        _gather_assigned_rows(y_hbm, dn_ref, ybuf.at[1 - slot], sems.at[1 - slot], tm)

    pltpu.make_async_copy(y_tiles_hbm.at[pl.ds(0, TOP_K * tiles)], ybuf.at[slot], sems.at[slot]).wait()
    acc = x1_ref[...]
    for kk in range(TOP_K):
        rows = ybuf[slot, pl.ds(kk * tiles, tiles)].reshape(tm, ybuf.shape[-1])
        acc = acc + gate_ref[:, kk:kk + 1] * rows
    o_ref[...] = acc


def moe_combine(x1, gate, y_sorted, dest, tm=256):
    n, d = x1.shape
    n_rows, w = y_sorted.shape
    nblk = n // tm
    return pl.pallas_call(
        _combine_kernel,
        grid=(nblk,),
        in_specs=[pl.BlockSpec((TOP_K, tm), lambda i: (0, 0), memory_space=pltpu.SMEM),
                  pl.BlockSpec((TOP_K, tm), lambda i: (0, jnp.minimum(i + 1, nblk - 1)),
                               memory_space=pltpu.SMEM),
                  pl.BlockSpec((tm, d), lambda i: (i, 0)),
                  pl.BlockSpec((tm, TOP_K), lambda i: (i, 0)),
                  pl.BlockSpec(memory_space=pl.ANY),
                  pl.BlockSpec(memory_space=pl.ANY)],
        out_specs=pl.BlockSpec((tm, d), lambda i: (i, 0)),
        out_shape=jax.ShapeDtypeStruct((n, d), F32),
        scratch_shapes=[pltpu.VMEM((2, TOP_K * tm // SUBLANES, SUBLANES, w), y_sorted.dtype),
                        pltpu.SemaphoreType.DMA((2,))],
        compiler_params=_cparams(("arbitrary",)),
        name="moe_combine",
    )(dest, dest, x1, gate, y_sorted, y_sorted.reshape(n_rows // SUBLANES, SUBLANES, w))


def kernel(x, mem, positions, norm_mix, w_in, b_gate, a_q_gain, a_k_gain, r_mu, r_w0, r_w2, r_a0, r_a2,
           r_g2, r_k_k, r_k_a, r_r_k, r_ln_w, r_ln_b, mem_norm, w_mem_kv, c_q_gain, c_k_gain, w_branch,
           w_out, norm_ffn, router_w, router_b, exp_w1, exp_b1, exp_w2, exp_b2):
    b, s, d = x.shape
    n = b * s
    depth = norm_mix.shape[0]
    n_groups = len(A_GROUPS)
    qkv_cols = n_groups * A_WIDTH
    off_k, off_v, off_r = qkv_cols, 2 * qkv_cols, 3 * qkv_cols
    off_cq = off_r + R_COLS
    off_gate = off_cq + C_WIDTH

    x2d = x.reshape(n, d)
    pos_col = positions.reshape(n, 1).astype(jnp.int32)
    for l in range(depth):
        w_l = w_in[l]
        if l == 0:
            h, cos_t, sin_t = rmsnorm_rows(x2d, norm_mix[l], pos_col)
        else:
            h = rmsnorm_rows(x2d, norm_mix[l])
        qk_gains = rotary_gains(a_q_gain[l], a_k_gain[l])
        zr = project(h, w_l[:, off_r:off_cq], "plain", out_dtype=F32, tn=R_COLS // 2)
        cq = project(h, w_l[:, off_cq:off_gate], "headnorm", (c_q_gain[l].reshape(1, HEAD_DIM),))
        gates = project(h, w_l, "gate", (b_gate[l].reshape(1, -1),), tn=3 * MXU_COLS,
                        col0=off_gate, ncols=N_BRANCH * d)

        shp = lambda t: t.reshape(b, s, -1)
        outs, lses = [], []
        for g, (window, dilation) in enumerate(A_GROUPS):
            assert window // dilation == ATT_BLOCK
            qkv = project_qkv(h, w_l, g, n_groups, qk_gains, cos_t, sin_t, dilation)
            o, lse = band_attention_group(qkv, b, g, dilation)
            outs.append(o)
            lses.append(lse)

        yb = rwkv7_mix(shp(zr), r_mu[l], r_w0[l], r_w2[l], r_a0[l], r_a2[l], r_g2[l], r_k_k[l], r_k_a[l],
                       r_r_k[l].reshape(-1), r_ln_w[l], r_ln_b[l]).reshape(n, R_WIDTH)

        mlen = mem.shape[1]
        mem_n = rmsnorm_rows(mem.reshape(b * mlen, d), mem_norm[l])
        wkv = w_mem_kv[l]
        ck = project(mem_n, wkv[:, :C_WIDTH], "headnorm", (c_k_gain[l].reshape(1, HEAD_DIM),))
        cv = project(mem_n, wkv[:, C_WIDTH:], "plain")
        yc = cross_attention(shp(cq), ck.reshape(b, mlen, C_WIDTH), cv.reshape(b, mlen, C_WIDTH)).reshape(n, C_WIDTH)

        x1, h2p, gate, route, counts = merge_and_route(
            x2d, outs, lses, yb, yc, gates, w_branch[l], w_out[l], norm_ffn[l], router_w[l], router_b[l])
        pad_start, block_expert, zero_flag, n_used, next_expert, run_parity = block_layout(counts, n * TOP_K)
        dest = assignment_rows(route, pad_start)
        x_sorted = scatter_rows(h2p, dest, zero_flag, block_expert.shape[0] * MOE_ROWS)
        y_sorted = expert_ffn(x_sorted, block_expert, n_used, next_expert, run_parity,
                              exp_w1[l], exp_b1[l], exp_w2[l], exp_b2[l])
        x2d = moe_combine(x1, gate, y_sorted, dest)
    return x2d.reshape(b, s, d)
```

```python
import functools

import jax
import jax.numpy as jnp
from jax import lax
from jax.experimental import pallas as pl
from jax.experimental.pallas import tpu as pltpu

F32 = jnp.float32
BF16 = jnp.bfloat16

NORM_EPS = 1e-6
LANES = 128
SUBLANES = 8
HEAD_DIM = 128
A_GROUPS = ((128, 1), (512, 4), (2048, 16))
A_HEADS = 4
A_WIDTH = A_HEADS * HEAD_DIM
ATT_BLOCK = 128
ATT_QB = 4
ROT_DIM = 32
ROPE_THETA = 500000.0
R_HEAD = 64
R_HEADS = 8
R_WIDTH = R_HEADS * R_HEAD
R_DECAY_LORA = 64
R_AAA_LORA = 64
R_GATE_LORA = 128
R_COLS = 3 * R_WIDTH + R_DECAY_LORA + R_AAA_LORA + R_GATE_LORA
R_GN_EPS = 64e-5
R_CHUNK = 64
R_CPI = 4
MXU_COLS = 256
R_SEG = MXU_COLS
C_HEADS = 4
C_WIDTH = C_HEADS * HEAD_DIM
N_BRANCH = 3
N_EXPERTS = 32
TOP_K = 4
SWIGLU_ALPHA = 1.702
SWIGLU_LIMIT = 7.0
MOE_ROWS = 256
NEG_BIG = -1e30

VMEM_LIMIT = 56 * 1024 * 1024


def _cparams(sem):
    return pltpu.CompilerParams(dimension_semantics=sem, vmem_limit_bytes=VMEM_LIMIT)


def _rmsnorm_kernel(x_ref, g_ref, *refs):
    o_ref = refs[-3] if len(refs) > 1 else refs[0]
    x = x_ref[...]
    ms = jnp.mean(x * x, axis=-1, keepdims=True)
    o_ref[...] = (x * lax.rsqrt(ms + NORM_EPS) * g_ref[...]).astype(o_ref.dtype)
    if len(refs) > 1:
        pos_ref, freq_ref, _, cos_ref, sin_ref = refs
        ang = pos_ref[...].astype(F32) * freq_ref[...]
        lane = lax.broadcasted_iota(jnp.int32, ang.shape, 1)
        cos_ref[...] = jnp.cos(ang)
        s = jnp.sin(ang)
        half = ROT_DIM // 2
        sin_ref[...] = jnp.where(lane < half, -s, jnp.where(lane < ROT_DIM, s, 0.0))


def rmsnorm_rows(x2d, gain, pos_col=None, tm=512):
    n, d = x2d.shape
    tm = min(tm, n)
    rows = lambda w: pl.BlockSpec((tm, w), lambda i: (i, 0))
    in_specs = [rows(d), pl.BlockSpec((1, d), lambda i: (0, 0))]
    out_specs, out_shape, extra = rows(d), jax.ShapeDtypeStruct((n, d), BF16), ()
    if pos_col is not None:
        half = ROT_DIM // 2
        inv_freq = ROPE_THETA ** (-jnp.arange(half, dtype=F32) / half)
        freq_row = jnp.concatenate([inv_freq, inv_freq, jnp.zeros((LANES - ROT_DIM,), F32)]).reshape(1, LANES)
        in_specs += [rows(1), pl.BlockSpec((1, LANES), lambda i: (0, 0))]
        out_specs = [out_specs, rows(LANES), rows(LANES)]
        out_shape = [out_shape] + [jax.ShapeDtypeStruct((n, LANES), F32)] * 2
        extra = (pos_col, freq_row)
    return pl.pallas_call(
        _rmsnorm_kernel,
        grid=(n // tm,),
        in_specs=in_specs,
        out_specs=out_specs,
        out_shape=out_shape,
        compiler_params=_cparams(("arbitrary",)),
        name="rmsnorm_rows",
    )(x2d, gain.reshape(1, d), *extra)


def _head_mean_sq(zh):
    avg = jnp.full((HEAD_DIM, HEAD_DIM), 1.0 / HEAD_DIM, BF16)
    return jnp.dot((zh * zh).astype(BF16), avg, preferred_element_type=F32)


def _cast_weight_once(w_ref, w_scr):
    @pl.when(pl.program_id(1) == 0)
    def _():
        w_scr[...] = w_ref[...].astype(w_scr.dtype)


def _proj_kernel(*refs, mode):
    h_ref, w_ref = refs[0], refs[1]
    o_ref, w_scr = refs[-2], refs[-1]
    _cast_weight_once(w_ref, w_scr)
    z = jnp.dot(h_ref[...], w_scr[...], preferred_element_type=F32)
    if mode == "plain":
        o_ref[...] = z.astype(o_ref.dtype)
    elif mode == "gate":
        o_ref[...] = (0.5 * jnp.tanh(0.5 * (z + refs[2][...])) + 0.5).astype(o_ref.dtype)
    else:
        gain = refs[2][...]
        for c in range(z.shape[1] // HEAD_DIM):
            zh = z[:, c * HEAD_DIM:(c + 1) * HEAD_DIM]
            zn = zh * lax.rsqrt(_head_mean_sq(zh) + NORM_EPS) * gain
            o_ref[:, c * HEAD_DIM:(c + 1) * HEAD_DIM] = zn.astype(o_ref.dtype)


def project(h, w, mode, extras=(), out_dtype=BF16, tm=2048, tn=512, col0=0, ncols=None):
    n, k = h.shape
    m = w.shape[1] if ncols is None else ncols
    tm = min(tm, n)
    tn = min(tn, m)
    assert n % tm == 0 and m % tn == 0 and col0 % tn == 0
    cblk = col0 // tn
    in_specs = [pl.BlockSpec((tm, k), lambda j, i: (i, 0)),
                pl.BlockSpec((k, tn), lambda j, i: (0, j + cblk))]
    if mode == "gate":
        in_specs.append(pl.BlockSpec((1, tn), lambda j, i: (0, j)))
    elif mode == "headnorm":
        in_specs.append(pl.BlockSpec((1, HEAD_DIM), lambda j, i: (0, 0)))
    return pl.pallas_call(
        functools.partial(_proj_kernel, mode=mode),
        grid=(m // tn, n // tm),
        in_specs=in_specs,
        out_specs=pl.BlockSpec((tm, tn), lambda j, i: (i, j)),
        out_shape=jax.ShapeDtypeStruct((n, m), out_dtype),
        scratch_shapes=[pltpu.VMEM((k, tn), BF16)],
        compiler_params=_cparams(("arbitrary", "arbitrary")),
        name="proj_" + mode,
    )(h, w, *extras)


def _qkv_proj_kernel(h_ref, w_ref, gain_ref, cos_ref, sin_ref, o_ref, w_scr, z_scr, *, dilation):
    j = pl.program_id(0)
    _cast_weight_once(w_ref, w_scr)
    z = jnp.dot(h_ref[...], w_scr[...], preferred_element_type=F32)
    half = ROT_DIM // 2

    pair = 2 * HEAD_DIM

    def emit(c0, slab):
        if dilation == 1:
            o_ref[:, c0 * HEAD_DIM:c0 * HEAD_DIM + pair] = slab.astype(o_ref.dtype)
        else:
            z_scr[c0] = slab[:, :HEAD_DIM]
            z_scr[c0 + 1] = slab[:, HEAD_DIM:]

    @pl.when(j < 2)
    def _():
        mi = lax.broadcasted_iota(jnp.int32, (pair, pair), 0)
        li = lax.broadcasted_iota(jnp.int32, (pair, pair), 1)
        same_head = (mi // HEAD_DIM) == (li // HEAD_DIM)
        lh, mh = li % HEAD_DIM, mi % HEAD_DIM
        avg = jnp.where(same_head, 1.0 / HEAD_DIM, 0.0).astype(BF16)
        perm = (same_head & (((lh < half) & (mh == lh + half))
                             | ((lh >= half) & (lh < ROT_DIM) & (mh == lh - half)))).astype(BF16)
        gcos = gain_ref[0, 0:1, :] * cos_ref[...]
        gsin = gain_ref[0, 1:2, :] * sin_ref[...]
        gcos = jnp.concatenate([gcos, gcos], axis=1)
        gsin = jnp.concatenate([gsin, gsin], axis=1)
        for c0 in range(0, A_HEADS, 2):
            zz = z[:, c0 * HEAD_DIM:c0 * HEAD_DIM + pair]
            ms = jnp.dot((zz * zz).astype(BF16), avg, preferred_element_type=F32)
            partner = jnp.dot(zz.astype(BF16), perm, preferred_element_type=F32)
            emit(c0, lax.rsqrt(ms + NORM_EPS) * (zz * gcos + partner * gsin))

    @pl.when(j == 2)
    def _():
        for c0 in range(0, A_HEADS, 2):
            emit(c0, z[:, c0 * HEAD_DIM:c0 * HEAD_DIM + pair])

    if dilation > 1:
        rows = z_scr.shape[1] // dilation
        for r in range(dilation):
            for c in range(A_HEADS):
                lo = r * A_WIDTH + c * HEAD_DIM
                o_ref[:, lo:lo + HEAD_DIM] = z_scr[c, pl.ds(r, rows, stride=dilation), :].astype(o_ref.dtype)


def rotary_gains(q_gain, k_gain):
    half = ROT_DIM // 2
    lane = jnp.arange(HEAD_DIM)
    partner = jnp.where(lane < half, lane + half, jnp.where(lane < ROT_DIM, lane - half, lane))
    return jnp.stack([jnp.stack([g, g[partner]]) for g in (q_gain, k_gain)])


def project_qkv(h, w_all, group, n_groups, gains, cos_t, sin_t, dilation, tm=2048):
    n, k = h.shape
    d = dilation
    return pl.pallas_call(
        functools.partial(_qkv_proj_kernel, dilation=d),
        grid=(3, n // tm),
        in_specs=[pl.BlockSpec((tm, k), lambda j, i: (i, 0)),
                  pl.BlockSpec((k, A_WIDTH), lambda j, i: (0, j * n_groups + group)),
                  pl.BlockSpec((1, 2, HEAD_DIM), lambda j, i: (jnp.minimum(j, 1), 0, 0)),
                  pl.BlockSpec((tm, LANES), lambda j, i: (i, 0)),
                  pl.BlockSpec((tm, LANES), lambda j, i: (i, 0))],
        out_specs=pl.BlockSpec((tm // d, d * A_WIDTH), lambda j, i: (i, j)),
        out_shape=jax.ShapeDtypeStruct((n // d, 3 * d * A_WIDTH), BF16),
        scratch_shapes=[pltpu.VMEM((k, A_WIDTH), BF16), pltpu.VMEM((A_HEADS, tm, HEAD_DIM), F32)],
        compiler_params=_cparams(("arbitrary", "arbitrary")),
        name=f"proj_qkv_d{d}",
    )(h, w_all, gains, cos_t, sin_t)


def _band_attn_kernel(*refs, qb):
    q_ref = refs[0]
    k_refs = refs[1:qb + 2]
    v_refs = refs[qb + 2:2 * qb + 3]
    o_ref, lse_ref = refs[-2:]
    step = pl.program_id(2)
    scale = HEAD_DIM ** -0.5
    nq = ATT_BLOCK
    qi = lax.broadcasted_iota(jnp.int32, (nq, 2 * nq), 0)
    ki = lax.broadcasted_iota(jnp.int32, (nq, 2 * nq), 1)
    rel = qi + nq - ki
    band = (rel >= 0) & (rel <= nq)
    for a in range(qb):
        blk = step * qb + a
        valid = band & ((blk * nq - nq + ki) >= 0)
        rows = slice(a * nq, (a + 1) * nq)
        lses = []
        for h in range(A_HEADS):
            sl = slice(h * HEAD_DIM, (h + 1) * HEAD_DIM)
            qh = q_ref[0, rows, sl]
            kh = jnp.concatenate([k_refs[a][0, :, sl], k_refs[a + 1][0, :, sl]], axis=0)
            vh = jnp.concatenate([v_refs[a][0, :, sl], v_refs[a + 1][0, :, sl]], axis=0)
            s = lax.dot_general(qh, kh, (((1,), (1,)), ((), ())), preferred_element_type=F32) * scale
            s = jnp.where(valid, s, NEG_BIG)
            m = jnp.max(s, axis=-1, keepdims=True)
            p = jnp.exp(s - m)
            l = jnp.sum(p, axis=-1, keepdims=True)
            o = jnp.dot(p.astype(BF16), vh, preferred_element_type=F32) / l
            o_ref[0, rows, sl] = o.astype(o_ref.dtype)
            lses.append(jnp.broadcast_to(m + jnp.log(l), (nq, LANES // A_HEADS)))
        lse_ref[0, rows, :] = jnp.concatenate(lses, axis=1)


def band_attention_group(qkv, b, g, dilation):
    d = dilation
    sub = qkv.shape[0] // b
    nblk = sub // ATT_BLOCK
    qb = min(ATT_QB, nblk)
    assert nblk % qb == 0
    view = qkv.reshape(b, sub, 3 * d * A_WIDTH)
    qrows = qb * ATT_BLOCK
    key_spec = lambda t, m: pl.BlockSpec(
        (1, ATT_BLOCK, A_WIDTH), lambda bi, r, j: (bi, jnp.maximum(j * qb - 1 + m, 0), t * d + r))
    o, lse = pl.pallas_call(
        functools.partial(_band_attn_kernel, qb=qb),
        grid=(b, d, nblk // qb),
        in_specs=[pl.BlockSpec((1, qrows, A_WIDTH), lambda bi, r, j: (bi, j, r))]
                 + [key_spec(1, m) for m in range(qb + 1)] + [key_spec(2, m) for m in range(qb + 1)],
        out_specs=[pl.BlockSpec((1, qrows, A_WIDTH), lambda bi, r, j: (bi, j, r)),
                   pl.BlockSpec((1, qrows, LANES), lambda bi, r, j: (bi, j, r))],
        out_shape=[jax.ShapeDtypeStruct((b, sub, d * A_WIDTH), BF16),
                   jax.ShapeDtypeStruct((b, sub, d * LANES), F32)],
        compiler_params=_cparams(("arbitrary", "arbitrary", "arbitrary")),
        name=f"band_attn_g{g}",
    )(*([view] * (2 * qb + 3)))
    return o.reshape(b * sub, d * A_WIDTH), lse.reshape(b * sub, d * LANES)


def _cross_attn_kernel(q_ref, k_ref, v_ref, o_ref):
    scale = HEAD_DIM ** -0.5
    for h in range(C_HEADS):
        sl = slice(h * HEAD_DIM, (h + 1) * HEAD_DIM)
        s = lax.dot_general(q_ref[0, :, sl], k_ref[0, :, sl], (((1,), (1,)), ((), ())),
                            preferred_element_type=F32) * scale
        m = jnp.max(s, axis=-1, keepdims=True)
        p = jnp.exp(s - m)
        l = jnp.sum(p, axis=-1, keepdims=True)
        o = jnp.dot(p.astype(BF16), v_ref[0, :, sl], preferred_element_type=F32) / l
        o_ref[0, :, sl] = o.astype(o_ref.dtype)


def cross_attention(qn, kn, v, tm=512):
    b, s, w = qn.shape
    m = kn.shape[1]
    return pl.pallas_call(
        _cross_attn_kernel,
        grid=(b, s // tm),
        in_specs=[pl.BlockSpec((1, tm, w), lambda bi, i: (bi, i, 0)),
                  pl.BlockSpec((1, m, w), lambda bi, i: (bi, 0, 0)),
                  pl.BlockSpec((1, m, w), lambda bi, i: (bi, 0, 0))],
        out_specs=pl.BlockSpec((1, tm, w), lambda bi, i: (bi, i, 0)),
        out_shape=jax.ShapeDtypeStruct((b, s, w), BF16),
        compiler_params=_cparams(("arbitrary", "arbitrary")),
        name="cross_attn",
    )(qn, kn, v)


def _head_sums(x, seg):
    w = seg.shape[0]
    x16 = x.astype(BF16)
    return jnp.concatenate(
        [jnp.dot(x16[:, j:j + w], seg, preferred_element_type=F32) for j in range(0, x.shape[1], w)], axis=1)


def _dot_t(a, b):
    return lax.dot_general(a, b, (((0,), (0,)), ((), ())), preferred_element_type=F32)


def _dot_nt(a, b):
    return lax.dot_general(a, b, (((1,), (1,)), ((), ())), preferred_element_type=F32)


def _rwkv_kernel(zr_ref, mu_ref, w0_ref, wwa_ref, a0_ref, g2_ref, kk_ref, ka_ref, rk_ref,
                 lnw_ref, lnb_ref, seg_ref, y_ref,
                 state_ref, carry_ref, ops_ref, yh_ref):
    t = pl.program_id(1)
    tt = zr_ref.shape[1]
    nch = tt // R_CHUNK
    c = R_CHUNK

    @pl.when(t == 0)
    def _():
        state_ref[...] = jnp.zeros_like(state_ref)
        carry_ref[...] = jnp.zeros_like(carry_ref)

    z = zr_ref[0]
    row = lax.broadcasted_iota(jnp.int32, z.shape, 0)
    prev = jnp.where(row == 0, carry_ref[...], pltpu.roll(z, 1, 0))
    carry_ref[...] = z[tt - 1:tt, :]
    xs = z + (prev - z) * mu_ref[...]

    w3 = 3 * R_WIDTH
    r = xs[:, 0:R_WIDTH]
    k = xs[:, R_WIDTH:2 * R_WIDTH]
    v = xs[:, 2 * R_WIDTH:w3]
    wa_lo = xs[:, w3:w3 + LANES]
    g_lo = xs[:, w3 + LANES:w3 + 2 * LANES]
    lane = lax.broadcasted_iota(jnp.int32, wa_lo.shape, 1)
    wa_in = jnp.where(lane < R_DECAY_LORA, jnp.tanh(wa_lo), wa_lo)
    wa = jnp.dot(wa_in.astype(BF16), wwa_ref[...], preferred_element_type=F32)
    u = -(w0_ref[...] + wa[:, :R_WIDTH])
    softplus = jnp.maximum(u, 0.0) + jnp.log(1.0 + jnp.exp(-jnp.abs(u)))
    w_raw = -softplus - 0.5
    ld = -jnp.exp(w_raw)
    a = jax.nn.sigmoid(a0_ref[...] + wa[:, R_WIDTH:])
    g = jnp.dot(jax.nn.sigmoid(g_lo).astype(BF16), g2_ref[...], preferred_element_type=F32)

    seg = seg_ref[...]
    kk = k * kk_ref[...]
    kk = kk * jnp.minimum(lax.rsqrt(_head_sums(kk * kk, seg)), 1e12)
    k2 = k * (1.0 + (a - 1.0) * ka_ref[...])
    bonus = _head_sums(r * k2 * rk_ref[...], seg) * v

    ri = lax.broadcasted_iota(jnp.int32, (c, c), 0)
    ci = lax.broadcasted_iota(jnp.int32, (c, c), 1)
    tri = (ci <= ri).astype(BF16)
    ld_hi = ld.astype(BF16)
    ld_lo = (ld - ld_hi.astype(F32)).astype(BF16)
    lcs = []
    for ch in range(nch):
        rs = slice(ch * c, (ch + 1) * c)
        lcs.append(jnp.dot(tri, ld_hi[rs], preferred_element_type=F32)
                   + jnp.dot(tri, ld_lo[rs], preferred_element_type=F32))
    lc = jnp.concatenate(lcs, axis=0)
    e_inc = jnp.exp(lc)
    e_exc = jnp.exp(lc - ld)
    e_inv = jnp.exp(-lc)
    a_t = -kk * e_exc
    r_t = r * e_inc
    b_t = kk * a * e_inv
    k_t = k2 * e_inv
    for h in range(R_HEADS):
        hs = slice(h * R_HEAD, (h + 1) * R_HEAD)
        ops_ref[0, h] = a_t[:, hs]
        ops_ref[1, h] = r_t[:, hs]
        ops_ref[2, h] = b_t[:, hs]
        ops_ref[3, h] = k_t[:, hs]
        ops_ref[4, h] = v[:, hs]
        ops_ref[5, h] = e_inc[:, hs]

    strict = ci < ri
    incl = ci <= ri
    eye = (ci == ri)

    ri2 = lax.broadcasted_iota(jnp.int32, (c, 2 * c), 0)
    ci2 = lax.broadcasted_iota(jnp.int32, (c, 2 * c), 1)
    incl2 = jnp.bitwise_and(ci2, c - 1) <= ri2
    eye_f = jnp.where(eye, 1.0, 0.0)
    heads = range(R_HEADS)
    dot = functools.partial(jnp.dot, preferred_element_type=F32)

    def chunk_body(ch, _):
        starts = [pl.multiple_of((ch * R_CPI + sub) * c, c) for sub in range(R_CPI)]
        rows = [pl.ds(r0, c) for r0 in starts]
        items = [(sub, h) for sub in range(R_CPI) for h in heads]
        idx = range(len(items))
        at = [ops_ref[0, h, rows[sub], :] for sub, h in items]
        rt = [ops_ref[1, h, rows[sub], :] for sub, h in items]
        bt = [ops_ref[2, h, rows[sub], :] for sub, h in items]
        kt = [ops_ref[3, h, rows[sub], :] for sub, h in items]
        pc = [ops_ref[5, h, pl.ds(starts[sub] + c - 1, 1), :] for sub, h in items]
        at16 = [x.astype(BF16) for x in at]
        rt16 = [x.astype(BF16) for x in rt]
        bt16 = [x.astype(BF16) for x in bt]
        kt16 = [x.astype(BF16) for x in kt]
        v16 = [ops_ref[4, h, rows[sub], :].astype(BF16) for sub, h in items]
        bk16 = [jnp.concatenate([bt16[i], kt16[i]], axis=0) for i in idx]
        nmat = [jnp.where(strict, _dot_nt(at16[i], bt16[i]), 0.0) for i in idx]
        a_ak = [jnp.where(strict, _dot_nt(at16[i], kt16[i]), 0.0).astype(BF16) for i in idx]
        a_rbk = [jnp.where(incl2, _dot_nt(rt16[i], bk16[i]), 0.0).astype(BF16) for i in idx]
        npow = nmat
        tinv = [eye_f + nmat[i] for i in idx]
        for _i in range(5):
            np16 = [x.astype(BF16) for x in npow]
            npow = [dot(np16[i], np16[i]) for i in idx]
            tinv = [tinv[i] + dot(tinv[i].astype(BF16), npow[i].astype(BF16)) for i in idx]
        akv = [dot(a_ak[i], v16[i]).astype(BF16) for i in idx]
        apw1 = [dot(tinv[i].astype(BF16), jnp.concatenate([at16[i], akv[i]], axis=1)).astype(BF16)
                for i in idx]
        zero = jnp.zeros((c, R_HEAD), BF16)
        rhs2 = [jnp.concatenate([apw1[i], jnp.concatenate([zero, v16[i]], axis=1)], axis=0)
                for i in idx]
        bkh = [jnp.concatenate([bt[i] * pc[i], kt[i] * pc[i]], axis=0).astype(BF16) for i in idx]
        gh = [_dot_t(bkh[i], rhs2[i]) for i in idx]
        qy = [dot(a_rbk[i], rhs2[i]) for i in idx]
        for i, (sub, h) in enumerate(items):
            gm = jnp.where(eye, jnp.broadcast_to(pc[i], (c, c)), 0.0) + gh[i][:, :R_HEAD]
            qp = rt[i] + qy[i][:, :R_HEAD]
            st = state_ref[h]
            res = dot(jnp.concatenate([qp, gm], axis=0).astype(BF16), st.astype(BF16))
            yh_ref[h, rows[sub], :] = res[:c] + qy[i][:, R_HEAD:]
            state_ref[h] = res[c:] + gh[i][:, R_HEAD:]
        return 0

    lax.fori_loop(0, nch // R_CPI, chunk_body, 0)

    y = jnp.concatenate([yh_ref[h] for h in range(R_HEADS)], axis=1)
    mean = _head_sums(y, seg) * (1.0 / R_HEAD)
    dlt = y - mean
    var = _head_sums(dlt * dlt, seg) * (1.0 / R_HEAD)
    yn = dlt * lax.rsqrt(var + R_GN_EPS) * lnw_ref[...] + lnb_ref[...]
    y_ref[0] = ((yn + bonus) * g).astype(y_ref.dtype)


def rwkv7_mix(zr, mu, w0, w2, a0, a2, g2, k_k, k_a, r_k, ln_w, ln_b, tt=256):
    b, s, cols = zr.shape
    row = lambda x: x.reshape(1, -1).astype(F32)
    wwa = jnp.zeros((LANES, 2 * R_WIDTH), F32)
    wwa = wwa.at[:R_DECAY_LORA, :R_WIDTH].set(w2).at[R_DECAY_LORA:, R_WIDTH:].set(a2).astype(BF16)
    hid = jnp.arange(R_SEG) // R_HEAD
    seg = (hid[:, None] == hid[None, :]).astype(BF16)
    full = lambda shape: pl.BlockSpec(shape, lambda bi, t: (0,) * len(shape))
    return pl.pallas_call(
        _rwkv_kernel,
        grid=(b, s // tt),
        in_specs=[pl.BlockSpec((1, tt, cols), lambda bi, t: (bi, t, 0)),
                  full((1, cols)), full((1, R_WIDTH)), full((LANES, 2 * R_WIDTH)), full((1, R_WIDTH)),
                  full((R_GATE_LORA, R_WIDTH)), full((1, R_WIDTH)), full((1, R_WIDTH)), full((1, R_WIDTH)),
                  full((1, R_WIDTH)), full((1, R_WIDTH)), full((R_SEG, R_SEG))],
        out_specs=pl.BlockSpec((1, tt, R_WIDTH), lambda bi, t: (bi, t, 0)),
        out_shape=jax.ShapeDtypeStruct((b, s, R_WIDTH), BF16),
        scratch_shapes=[pltpu.VMEM((R_HEADS, R_HEAD, R_HEAD), F32),
                        pltpu.VMEM((1, cols), F32),
                        pltpu.VMEM((6, R_HEADS, tt, R_HEAD), F32),
                        pltpu.VMEM((R_HEADS, tt, R_HEAD), F32)],
        compiler_params=_cparams(("arbitrary", "arbitrary")),
        name="rwkv7_mix",
    )(zr, row(mu), row(w0), wwa, row(a0), g2.astype(BF16), row(k_k), row(k_a), row(r_k),
      row(ln_w), row(ln_b), seg)


def _merge_kernel(x_ref, o0_ref, o1_ref, o2_ref, l0_ref, l1_ref, l2_ref, yb_ref, yc_ref, gt_ref,
                  wb_ref, wo_ref, gn_ref, rw_ref, rb_ref,
                  x1_ref, h2_ref, idx_ref, gate_ref, rank_ref, cnt_ref, base_ref, o_scr, l_scr):
    @pl.when(pl.program_id(0) == 0)
    def _():
        base_ref[...] = jnp.zeros_like(base_ref)

    tm_rows = x_ref.shape[0]
    for gi, (o_ref, l_ref) in enumerate(((o0_ref, l0_ref), (o1_ref, l1_ref), (o2_ref, l2_ref))):
        dil = A_GROUPS[gi][1]
        for r in range(dil):
            dst = pl.ds(r, tm_rows // dil, stride=dil) if dil > 1 else slice(None)
            for h in range(A_HEADS):
                lo = r * A_WIDTH + h * HEAD_DIM
                o_scr[gi, h, dst, :] = o_ref[:, lo:lo + HEAD_DIM].astype(F32)
            l_scr[gi, dst, :] = l_ref[:, r * LANES:(r + 1) * LANES]
    lses = [l_scr[gi] for gi in range(3)]
    lmax = jnp.maximum(jnp.maximum(lses[0], lses[1]), lses[2])
    es = [jnp.exp(l - lmax) for l in lses]
    inv = 1.0 / (es[0] + es[1] + es[2])
    qw = LANES // A_HEADS
    heads = []
    for h in range(A_HEADS):
        acc = None
        for gi in range(3):
            alpha = (es[gi] * inv)[:, h * qw:h * qw + 1]
            term = alpha * o_scr[gi, h]
            acc = term if acc is None else acc + term
        heads.append(acc)
    ya = jnp.concatenate(heads, axis=1).astype(BF16)
    d = x_ref.shape[1]
    merged = None
    for n, yn in enumerate((ya, yb_ref[...], yc_ref[...])):
        proj = jnp.dot(yn, wb_ref[n], preferred_element_type=F32)
        term = gt_ref[:, n * d:(n + 1) * d].astype(F32) * proj
        merged = term if merged is None else merged + term
    x1 = x_ref[...] + jnp.dot(merged.astype(BF16), wo_ref[...], preferred_element_type=F32)
    x1_ref[...] = x1
    ms = jnp.mean(x1 * x1, axis=-1, keepdims=True)
    h2 = x1 * lax.rsqrt(ms + NORM_EPS) * gn_ref[...]
    h2_ref[...] = h2
    rw = rw_ref[...]
    h2_hi, rw_hi = h2.astype(BF16), rw.astype(BF16)
    h2_lo = (h2 - h2_hi.astype(F32)).astype(BF16)
    rw_lo = (rw - rw_hi.astype(F32)).astype(BF16)
    logits = (jnp.dot(h2_hi, rw_hi, preferred_element_type=F32) + jnp.dot(h2_hi, rw_lo, preferred_element_type=F32)
              + jnp.dot(h2_lo, rw_hi, preferred_element_type=F32)) + rb_ref[...]
    tm = logits.shape[0]
    lane = lax.broadcasted_iota(jnp.int32, logits.shape, 1)
    vals, idxs = [], []
    cur = logits
    for _k in range(TOP_K):
        m = jnp.max(cur, axis=-1, keepdims=True)
        ik = jnp.min(jnp.where(cur == m, lane, N_EXPERTS), axis=-1, keepdims=True)
        vals.append(m)
        idxs.append(ik)
        cur = jnp.where(lane == ik, -jnp.inf, cur)
    exps = [jnp.exp(vk - vals[0]) for vk in vals]
    tot = exps[0] + exps[1] + exps[2] + exps[3]
    onehots = [lane == ik for ik in idxs]
    hits = sum(jnp.where(oh, 1.0, 0.0) for oh in onehots)
    ri = lax.broadcasted_iota(jnp.int32, (tm, tm), 0)
    ci = lax.broadcasted_iota(jnp.int32, (tm, tm), 1)
    before = jnp.dot((ci < ri).astype(BF16), hits.astype(BF16), preferred_element_type=F32) + base_ref[...]
    for kk in range(TOP_K):
        idx_ref[:, kk:kk + 1] = idxs[kk]
        gate_ref[:, kk:kk + 1] = exps[kk] / tot
        rank_ref[:, kk:kk + 1] = jnp.sum(jnp.where(onehots[kk], before, 0.0), axis=-1,
                                         keepdims=True).astype(jnp.int32)
    base_ref[...] = base_ref[...] + jnp.sum(hits, axis=0, keepdims=True)
    cnt_ref[...] = base_ref[...].astype(jnp.int32)


def merge_and_route(x2d, outs, lses, yb, yc, gates, w_branch, w_out, norm_ffn, router_w, router_b, tm=512):
    n, d = x2d.shape
    rows = lambda w: pl.BlockSpec((tm, w), lambda i: (i, 0))
    packed = lambda w, dil: pl.BlockSpec((tm // dil, dil * w), lambda i: (i, 0))
    full = lambda shape: pl.BlockSpec(shape, lambda i: (0,) * len(shape))
    return pl.pallas_call(
        _merge_kernel,
        grid=(n // tm,),
        in_specs=[rows(d)] + [packed(A_WIDTH, dil) for _w, dil in A_GROUPS] + [packed(LANES, dil) for _w, dil in A_GROUPS]
                 + [rows(R_WIDTH), rows(C_WIDTH), rows(N_BRANCH * d),
                    full((N_BRANCH, A_WIDTH, d)), full((d, d)), full((1, d)), full((d, N_EXPERTS)), full((1, N_EXPERTS))],
        out_specs=[rows(d), rows(d), rows(TOP_K), rows(TOP_K), rows(TOP_K), full((1, N_EXPERTS))],
        out_shape=[jax.ShapeDtypeStruct((n, d), F32), jax.ShapeDtypeStruct((n, d), F32),
                   jax.ShapeDtypeStruct((n, TOP_K), jnp.int32), jax.ShapeDtypeStruct((n, TOP_K), F32),
                   jax.ShapeDtypeStruct((n, TOP_K), jnp.int32), jax.ShapeDtypeStruct((1, N_EXPERTS), jnp.int32)],
        scratch_shapes=[pltpu.VMEM((1, N_EXPERTS), F32), pltpu.VMEM((len(A_GROUPS), A_HEADS, tm, HEAD_DIM), F32),
                        pltpu.VMEM((len(A_GROUPS), tm, LANES), F32)],
        compiler_params=_cparams(("arbitrary",)),
        name="merge_route",
    )(x2d, *outs, *lses, yb, yc, gates, w_branch.astype(BF16), w_out.astype(BF16),
      norm_ffn.reshape(1, d), router_w, router_b.reshape(1, N_EXPERTS))


def block_layout(counts, n_assign):
    counts = counts.reshape(-1)
    padded = (counts + MOE_ROWS - 1) // MOE_ROWS * MOE_ROWS
    pad_end = jnp.cumsum(padded)
    pad_start = (pad_end - padded).astype(jnp.int32)
    n_blocks = -(-n_assign // MOE_ROWS) + N_EXPERTS
    blk_row = jnp.arange(n_blocks, dtype=jnp.int32) * MOE_ROWS
    owner = jnp.sum((blk_row[:, None] >= pad_end[None, :]).astype(jnp.int32), axis=1)
    block_expert = jnp.minimum(owner, N_EXPERTS - 1).astype(jnp.int32)
    unused = blk_row >= pad_end[-1]
    zero_flag = (unused | (blk_row + MOE_ROWS == pad_end[block_expert])).astype(jnp.int32)
    n_used = (pad_end[-1:] // MOE_ROWS).astype(jnp.int32)
    has_rows = counts > 0
    eid = jnp.arange(N_EXPERTS, dtype=jnp.int32)
    later = jnp.where(has_rows[None, :] & (eid[None, :] > eid[:, None]), eid[None, :], N_EXPERTS)
    next_expert = jnp.min(later, axis=1)
    next_expert = jnp.where(next_expert == N_EXPERTS, -1, next_expert).astype(jnp.int32)
    run_parity = ((jnp.cumsum(has_rows) - has_rows) % 2).astype(jnp.int32)
    return pad_start, block_expert, zero_flag, n_used, next_expert, run_parity


def _dest_kernel(ps_ref, idx_ref, rank_ref, dest_ref):
    idx = idx_ref[...]
    dest = rank_ref[...]
    for e in range(N_EXPERTS):
        dest = dest + jnp.where(idx == e, ps_ref[e], 0)
    dest_ref[...] = dest


def assignment_rows(top_idx, rank, pad_start):
    n = top_idx.shape[0]
    rows = n * TOP_K // LANES
    flat = lambda t: t.reshape(rows, LANES)
    spec = pl.BlockSpec((rows, LANES), lambda i, ps: (0, 0))
    out = pl.pallas_call(
        _dest_kernel,
        grid_spec=pltpu.PrefetchScalarGridSpec(num_scalar_prefetch=1, grid=(1,), in_specs=[spec, spec],
                                               out_specs=spec),
        out_shape=jax.ShapeDtypeStruct((rows, LANES), jnp.int32),
        compiler_params=_cparams(("arbitrary",)),
        name="assignment_rows",
    )(pad_start, flat(top_idx), flat(rank))
    return out.reshape(n, TOP_K)


def _scatter_kernel(zf_ref, dest_ref, h2_ref, xs_hbm, zeros_ref, sem, zsem):
    i = pl.program_id(0)
    tiles = h2_ref.shape[0]
    tm = tiles * SUBLANES
    nblk = zf_ref.shape[0]

    def zero_block(j):
        return pltpu.make_async_copy(zeros_ref, xs_hbm.at[pl.ds(j * MOE_ROWS, MOE_ROWS)], zsem)

    @pl.when(i == 0)
    def _():
        zeros_ref[...] = jnp.zeros_like(zeros_ref)

        def start(j, _):
            @pl.when(zf_ref[j] != 0)
            def _():
                zero_block(j).start()
            return 0

        def wait(j, _):
            @pl.when(zf_ref[j] != 0)
            def _():
                zero_block(j).wait()
            return 0

        lax.fori_loop(0, nblk, start, 0)
        lax.fori_loop(0, nblk, wait, 0)

    def body(g, _):
        for u in range(SUBLANES):
            for kk in range(TOP_K):
                row = dest_ref[0, 0, (g * SUBLANES + u) * TOP_K + kk]
                pltpu.make_async_copy(h2_ref.at[g, pl.ds(u, 1)], xs_hbm.at[pl.ds(row, 1)], sem).start(
                    priority=kk % 2)
        return 0

    lax.fori_loop(0, tiles, body, 0)
    pltpu.make_async_copy(xs_hbm.at[pl.ds(0, tm * TOP_K)], xs_hbm.at[pl.ds(0, tm * TOP_K)], sem).wait()


def scatter_rows(h2, dest, zero_flag, n_rows, tm=1024):
    n, w = h2.shape
    dest3 = dest.reshape(n // tm, 1, tm * TOP_K)
    grid_spec = pltpu.PrefetchScalarGridSpec(
        num_scalar_prefetch=1,
        grid=(n // tm,),
        in_specs=[pl.BlockSpec((1, 1, tm * TOP_K), lambda i, zf: (i, 0, 0), memory_space=pltpu.SMEM),
                  pl.BlockSpec((tm // SUBLANES, SUBLANES, w), lambda i, zf: (i, 0, 0))],
        out_specs=pl.BlockSpec(memory_space=pl.ANY),
        scratch_shapes=[pltpu.VMEM((MOE_ROWS, w), h2.dtype), pltpu.SemaphoreType.DMA(()),
                        pltpu.SemaphoreType.DMA(())],
    )
    return pl.pallas_call(
        _scatter_kernel,
        grid_spec=grid_spec,
        out_shape=jax.ShapeDtypeStruct((n_rows, w), h2.dtype),
        compiler_params=_cparams(("arbitrary",)),
        name="scatter_rows",
    )(zero_flag, dest3, h2.reshape(n // SUBLANES, SUBLANES, w))


def _expert_kernel(be_ref, nu_ref, nxt_ref, par_ref, xs_ref, w1_hbm, b1_ref, w2_hbm, b2_ref, y_ref,
                   w1_f32, w2_f32, w1_scr, w2_scr, wsem):
    i = pl.program_id(0)
    e = be_ref[i]
    prev = be_ref[jnp.maximum(i - 1, 0)]

    def fetch(expert, slot):
        return (pltpu.make_async_copy(w1_hbm.at[expert], w1_f32.at[slot], wsem.at[slot]),
                pltpu.make_async_copy(w2_hbm.at[expert], w2_f32.at[slot], wsem.at[slot]))

    @pl.when((i < nu_ref[0]) & ((i == 0) | (e != prev)))
    def _():
        slot = par_ref[e]

        @pl.when(i == 0)
        def _():
            for c in fetch(e, slot):
                c.start()

        for c in fetch(e, slot):
            c.wait()

        @pl.when(nxt_ref[e] >= 0)
        def _():
            for c in fetch(nxt_ref[e], 1 - slot):
                c.start()

        w1_scr[...] = w1_f32[slot].astype(BF16)
        w2_scr[...] = w2_f32[slot].astype(BF16)

    @pl.when(i < nu_ref[0])
    def _():
        xb = xs_ref[...].astype(BF16)
        dff = w2_scr.shape[0]
        u = jnp.dot(xb, w1_scr[...], preferred_element_type=F32) + b1_ref[0]
        glu = jnp.minimum(u[:, :dff], SWIGLU_LIMIT)
        lin = jnp.clip(u[:, dff:], -SWIGLU_LIMIT, SWIGLU_LIMIT)
        act = glu * jax.nn.sigmoid(SWIGLU_ALPHA * glu) * (lin + 1.0)
        y = jnp.dot(act.astype(BF16), w2_scr[...], preferred_element_type=F32) + b2_ref[0]
        y_ref[...] = y

    @pl.when(i >= nu_ref[0])
    def _():
        y_ref[...] = jnp.zeros_like(y_ref)


def expert_ffn(x_sorted, block_expert, n_used, next_expert, run_parity, w1, b1, w2, b2):
    n_rows, d = x_sorted.shape
    w = d
    nblk = n_rows // MOE_ROWS
    dff2 = w1.shape[2]
    used = lambda i, nu: jnp.minimum(i, nu[0] - 1)
    grid_spec = pltpu.PrefetchScalarGridSpec(
        num_scalar_prefetch=4,
        grid=(nblk,),
        in_specs=[
            pl.BlockSpec((MOE_ROWS, w), lambda i, be, nu, nx, pa: (used(i, nu), 0)),
            pl.BlockSpec(memory_space=pl.ANY),
            pl.BlockSpec((1, 1, dff2), lambda i, be, nu, nx, pa: (be[used(i, nu)], 0, 0)),
            pl.BlockSpec(memory_space=pl.ANY),
            pl.BlockSpec((1, 1, d), lambda i, be, nu, nx, pa: (be[used(i, nu)], 0, 0)),
        ],
        out_specs=pl.BlockSpec((MOE_ROWS, w), lambda i, be, nu, nx, pa: (i, 0)),
        scratch_shapes=[pltpu.VMEM((2, d, dff2), F32), pltpu.VMEM((2, dff2 // 2, d), F32),
                        pltpu.VMEM((d, dff2), BF16), pltpu.VMEM((dff2 // 2, d), BF16),
                        pltpu.SemaphoreType.DMA((2,))],
    )
    return pl.pallas_call(
        _expert_kernel,
        grid_spec=grid_spec,
        out_shape=jax.ShapeDtypeStruct((n_rows, w), F32),
        compiler_params=_cparams(("arbitrary",)),
        name="expert_ffn",
    )(block_expert, n_used, next_expert, run_parity, x_sorted, w1, b1.reshape(N_EXPERTS, 1, dff2), w2,
      b2.reshape(N_EXPERTS, 1, d))


def _gather_assigned_rows(y_hbm, dest_ref, dst_ref, sem, tm):
    tiles = tm // SUBLANES

    def body(g, _):
        for u in range(SUBLANES):
            for kk in range(TOP_K):
                row = dest_ref[(g * SUBLANES + u) * TOP_K + kk]
                pltpu.make_async_copy(y_hbm.at[pl.ds(row, 1)], dst_ref.at[kk * tiles + g, pl.ds(u, 1)],
                                      sem).start(priority=kk % 2)
        return 0
    lax.fori_loop(0, tiles, body, 0)


def _combine_kernel(d0_ref, dn_ref, x1_ref, gate_ref, y_hbm, y_tiles_hbm, o_ref, ybuf, sems):
    i = pl.program_id(0)
    nblk = pl.num_programs(0)
    slot = lax.rem(i, 2)
    tm = o_ref.shape[0]
    tiles = tm // SUBLANES

    @pl.when(i == 0)
    def _():
        _gather_assigned_rows(y_hbm, d0_ref.at[0, 0], ybuf.at[0], sems.at[0], tm)

    @pl.when(i + 1 < nblk)
    def _():
        _gather_assigned_rows(y_hbm, dn_ref.at[0, 0], ybuf.at[1 - slot], sems.at[1 - slot], tm)

    pltpu.make_async_copy(y_tiles_hbm.at[pl.ds(0, TOP_K * tiles)], ybuf.at[slot], sems.at[slot]).wait()
    acc = x1_ref[...]
    for kk in range(TOP_K):
        rows = ybuf[slot, pl.ds(kk * tiles, tiles)].reshape(tm, ybuf.shape[-1])
        acc = acc + gate_ref[:, kk:kk + 1] * rows
    o_ref[...] = acc


def moe_combine(x1, gate, y_sorted, dest, tm=512):
    n, d = x1.shape
    n_rows, w = y_sorted.shape
    nblk = n // tm
    dest3 = dest.reshape(nblk, 1, tm * TOP_K)
    return pl.pallas_call(
        _combine_kernel,
        grid=(nblk,),
        in_specs=[pl.BlockSpec((1, 1, TOP_K * tm), lambda i: (0, 0, 0), memory_space=pltpu.SMEM),
                  pl.BlockSpec((1, 1, TOP_K * tm), lambda i: (jnp.minimum(i + 1, nblk - 1), 0, 0),
                               memory_space=pltpu.SMEM),
                  pl.BlockSpec((tm, d), lambda i: (i, 0)),
                  pl.BlockSpec((tm, TOP_K), lambda i: (i, 0)),
                  pl.BlockSpec(memory_space=pl.ANY),
                  pl.BlockSpec(memory_space=pl.ANY)],
        out_specs=pl.BlockSpec((tm, d), lambda i: (i, 0)),
        out_shape=jax.ShapeDtypeStruct((n, d), F32),
        scratch_shapes=[pltpu.VMEM((2, TOP_K * tm // SUBLANES, SUBLANES, w), y_sorted.dtype),
                        pltpu.SemaphoreType.DMA((2,))],
        compiler_params=_cparams(("arbitrary",)),
        name="moe_combine",
    )(dest3, dest3, x1, gate, y_sorted, y_sorted.reshape(n_rows // SUBLANES, SUBLANES, w))


def kernel(x, mem, positions, norm_mix, w_in, b_gate, a_q_gain, a_k_gain, r_mu, r_w0, r_w2, r_a0, r_a2,
           r_g2, r_k_k, r_k_a, r_r_k, r_ln_w, r_ln_b, mem_norm, w_mem_kv, c_q_gain, c_k_gain, w_branch,
           w_out, norm_ffn, router_w, router_b, exp_w1, exp_b1, exp_w2, exp_b2):
    b, s, d = x.shape
    n = b * s
    depth = norm_mix.shape[0]
    n_groups = len(A_GROUPS)
    qkv_cols = n_groups * A_WIDTH
    off_k, off_v, off_r = qkv_cols, 2 * qkv_cols, 3 * qkv_cols
    off_cq = off_r + R_COLS
    off_gate = off_cq + C_WIDTH

    x2d = x.reshape(n, d)
    pos_col = positions.reshape(n, 1).astype(jnp.int32)
    for l in range(depth):
        w_l = w_in[l]
        if l == 0:
            h, cos_t, sin_t = rmsnorm_rows(x2d, norm_mix[l], pos_col)
        else:
            h = rmsnorm_rows(x2d, norm_mix[l])
        qk_gains = rotary_gains(a_q_gain[l], a_k_gain[l])
        zr = project(h, w_l[:, off_r:off_cq], "plain", out_dtype=F32, tn=R_COLS // 2)
        cq = project(h, w_l[:, off_cq:off_gate], "headnorm", (c_q_gain[l].reshape(1, HEAD_DIM),))
        gates = project(h, w_l, "gate", (b_gate[l].reshape(1, -1),), tn=3 * MXU_COLS,
                        col0=off_gate, ncols=N_BRANCH * d)

        shp = lambda t: t.reshape(b, s, -1)
        outs, lses = [], []
        for g, (window, dilation) in enumerate(A_GROUPS):
            assert window // dilation == ATT_BLOCK
            qkv = project_qkv(h, w_l, g, n_groups, qk_gains, cos_t, sin_t, dilation)
            o, lse = band_attention_group(qkv, b, g, dilation)
            outs.append(o)
            lses.append(lse)

        yb = rwkv7_mix(shp(zr), r_mu[l], r_w0[l], r_w2[l], r_a0[l], r_a2[l], r_g2[l], r_k_k[l], r_k_a[l],
                       r_r_k[l].reshape(-1), r_ln_w[l], r_ln_b[l]).reshape(n, R_WIDTH)

        mlen = mem.shape[1]
        mem_n = rmsnorm_rows(mem.reshape(b * mlen, d), mem_norm[l])
        wkv = w_mem_kv[l]
        ck = project(mem_n, wkv[:, :C_WIDTH], "headnorm", (c_k_gain[l].reshape(1, HEAD_DIM),))
        cv = project(mem_n, wkv[:, C_WIDTH:], "plain")
        yc = cross_attention(shp(cq), ck.reshape(b, mlen, C_WIDTH), cv.reshape(b, mlen, C_WIDTH)).reshape(n, C_WIDTH)

        x1, h2p, top_idx, gate, rank, counts = merge_and_route(
            x2d, outs, lses, yb, yc, gates, w_branch[l], w_out[l], norm_ffn[l], router_w[l], router_b[l])
        pad_start, block_expert, zero_flag, n_used, next_expert, run_parity = block_layout(counts, n * TOP_K)
        dest = assignment_rows(top_idx, rank, pad_start)
        x_sorted = scatter_rows(h2p, dest, zero_flag, block_expert.shape[0] * MOE_ROWS)
        y_sorted = expert_ffn(x_sorted, block_expert, n_used, next_expert, run_parity,
                              exp_w1[l], exp_b1[l], exp_w2[l], exp_b2[l])
        x2d = moe_combine(x1, gate, y_sorted, dest)
    return x2d.reshape(b, s, d)
```

```python
import functools

import jax
import jax.numpy as jnp
from jax import lax
from jax.experimental import pallas as pl
from jax.experimental.pallas import tpu as pltpu

F32 = jnp.float32
BF16 = jnp.bfloat16

NORM_EPS = 1e-6
LANES = 128
SUBLANES = 8
HEAD_DIM = 128
A_GROUPS = ((128, 1), (512, 4), (2048, 16))
A_HEADS = 4
A_WIDTH = A_HEADS * HEAD_DIM
ATT_BLOCK = 128
ATT_QB = 8
ROT_DIM = 32
ROPE_THETA = 500000.0
R_HEAD = 64
R_HEADS = 8
R_WIDTH = R_HEADS * R_HEAD
R_DECAY_LORA = 64
R_AAA_LORA = 64
R_GATE_LORA = 128
R_COLS = 3 * R_WIDTH + R_DECAY_LORA + R_AAA_LORA + R_GATE_LORA
R_GN_EPS = 64e-5
R_CHUNK = 64
R_CPI = 4
MXU_COLS = 256
R_SEG = MXU_COLS
C_HEADS = 4
C_WIDTH = C_HEADS * HEAD_DIM
N_BRANCH = 3
N_EXPERTS = 32
TOP_K = 4
SWIGLU_ALPHA = 1.702
SWIGLU_LIMIT = 7.0
MOE_ROWS = 256
NEG_BIG = -1e30

VMEM_LIMIT = 56 * 1024 * 1024


def _cparams(sem):
    return pltpu.CompilerParams(dimension_semantics=sem, vmem_limit_bytes=VMEM_LIMIT)


def _rmsnorm_kernel(x_ref, g_ref, *refs):
    o_ref = refs[-3] if len(refs) > 1 else refs[0]
    x = x_ref[...]
    ms = jnp.mean(x * x, axis=-1, keepdims=True)
    o_ref[...] = (x * lax.rsqrt(ms + NORM_EPS) * g_ref[...]).astype(o_ref.dtype)
    if len(refs) > 1:
        pos_ref, freq_ref, _, cos_ref, sin_ref = refs
        ang = pos_ref[...].astype(F32) * freq_ref[...]
        lane = lax.broadcasted_iota(jnp.int32, ang.shape, 1)
        cos_ref[...] = jnp.cos(ang)
        s = jnp.sin(ang)
        half = ROT_DIM // 2
        sin_ref[...] = jnp.where(lane < half, -s, jnp.where(lane < ROT_DIM, s, 0.0))


def rmsnorm_rows(x2d, gain, pos_col=None, tm=512):
    n, d = x2d.shape
    tm = min(tm, n)
    rows = lambda w: pl.BlockSpec((tm, w), lambda i: (i, 0))
    in_specs = [rows(d), pl.BlockSpec((1, d), lambda i: (0, 0))]
    out_specs, out_shape, extra = rows(d), jax.ShapeDtypeStruct((n, d), BF16), ()
    if pos_col is not None:
        half = ROT_DIM // 2
        inv_freq = ROPE_THETA ** (-jnp.arange(half, dtype=F32) / half)
        freq_row = jnp.concatenate([inv_freq, inv_freq, jnp.zeros((LANES - ROT_DIM,), F32)]).reshape(1, LANES)
        in_specs += [rows(1), pl.BlockSpec((1, LANES), lambda i: (0, 0))]
        out_specs = [out_specs, rows(LANES), rows(LANES)]
        out_shape = [out_shape] + [jax.ShapeDtypeStruct((n, LANES), F32)] * 2
        extra = (pos_col, freq_row)
    return pl.pallas_call(
        _rmsnorm_kernel,
        grid=(n // tm,),
        in_specs=in_specs,
        out_specs=out_specs,
        out_shape=out_shape,
        compiler_params=_cparams(("arbitrary",)),
        name="rmsnorm_rows",
    )(x2d, gain.reshape(1, d), *extra)


def _head_mean_sq(zh):
    avg = jnp.full((HEAD_DIM, HEAD_DIM), 1.0 / HEAD_DIM, BF16)
    return jnp.dot((zh * zh).astype(BF16), avg, preferred_element_type=F32)


def _cast_weight_once(w_ref, w_scr):
    @pl.when(pl.program_id(1) == 0)
    def _():
        w_scr[...] = w_ref[...].astype(w_scr.dtype)


def _proj_kernel(*refs, mode):
    h_ref, w_ref = refs[0], refs[1]
    o_ref, w_scr = refs[-2], refs[-1]
    _cast_weight_once(w_ref, w_scr)
    z = jnp.dot(h_ref[...], w_scr[...], preferred_element_type=F32)
    if mode == "plain":
        o_ref[...] = z.astype(o_ref.dtype)
    elif mode == "gate":
        o_ref[...] = (0.5 * jnp.tanh(0.5 * (z + refs[2][...])) + 0.5).astype(o_ref.dtype)
    else:
        gain = refs[2][...]
        for c in range(z.shape[1] // HEAD_DIM):
            zh = z[:, c * HEAD_DIM:(c + 1) * HEAD_DIM]
            zn = zh * lax.rsqrt(_head_mean_sq(zh) + NORM_EPS) * gain
            o_ref[:, c * HEAD_DIM:(c + 1) * HEAD_DIM] = zn.astype(o_ref.dtype)


def project(h, w, mode, extras=(), out_dtype=BF16, tm=2048, tn=512, col0=0, ncols=None):
    n, k = h.shape
    m = w.shape[1] if ncols is None else ncols
    tm = min(tm, n)
    tn = min(tn, m)
    assert n % tm == 0 and m % tn == 0 and col0 % tn == 0
    cblk = col0 // tn
    in_specs = [pl.BlockSpec((tm, k), lambda j, i: (i, 0)),
                pl.BlockSpec((k, tn), lambda j, i: (0, j + cblk))]
    if mode == "gate":
        in_specs.append(pl.BlockSpec((1, tn), lambda j, i: (0, j)))
    elif mode == "headnorm":
        in_specs.append(pl.BlockSpec((1, HEAD_DIM), lambda j, i: (0, 0)))
    return pl.pallas_call(
        functools.partial(_proj_kernel, mode=mode),
        grid=(m // tn, n // tm),
        in_specs=in_specs,
        out_specs=pl.BlockSpec((tm, tn), lambda j, i: (i, j)),
        out_shape=jax.ShapeDtypeStruct((n, m), out_dtype),
        scratch_shapes=[pltpu.VMEM((k, tn), BF16)],
        compiler_params=_cparams(("arbitrary", "arbitrary")),
        name="proj_" + mode,
    )(h, w, *extras)


def _qkv_proj_kernel(h_ref, w_ref, gain_ref, cos_ref, sin_ref, o_ref, w_scr, z_scr, *, dilation):
    j = pl.program_id(0)
    _cast_weight_once(w_ref, w_scr)
    z = jnp.dot(h_ref[...], w_scr[...], preferred_element_type=F32)
    half = ROT_DIM // 2

    pair = 2 * HEAD_DIM

    def emit(c0, slab):
        if dilation == 1:
            o_ref[:, c0 * HEAD_DIM:c0 * HEAD_DIM + pair] = slab.astype(o_ref.dtype)
        else:
            z_scr[c0] = slab[:, :HEAD_DIM]
            z_scr[c0 + 1] = slab[:, HEAD_DIM:]

    @pl.when(j < 2)
    def _():
        mi = lax.broadcasted_iota(jnp.int32, (pair, pair), 0)
        li = lax.broadcasted_iota(jnp.int32, (pair, pair), 1)
        same_head = (mi // HEAD_DIM) == (li // HEAD_DIM)
        lh, mh = li % HEAD_DIM, mi % HEAD_DIM
        avg = jnp.where(same_head, 1.0 / HEAD_DIM, 0.0).astype(BF16)
        perm = (same_head & (((lh < half) & (mh == lh + half))
                             | ((lh >= half) & (lh < ROT_DIM) & (mh == lh - half)))).astype(BF16)
        gcos = gain_ref[0, 0:1, :] * cos_ref[...]
        gsin = gain_ref[0, 1:2, :] * sin_ref[...]
        gcos = jnp.concatenate([gcos, gcos], axis=1)
        gsin = jnp.concatenate([gsin, gsin], axis=1)
        for c0 in range(0, A_HEADS, 2):
            zz = z[:, c0 * HEAD_DIM:c0 * HEAD_DIM + pair]
            ms = jnp.dot((zz * zz).astype(BF16), avg, preferred_element_type=F32)
            partner = jnp.dot(zz.astype(BF16), perm, preferred_element_type=F32)
            emit(c0, lax.rsqrt(ms + NORM_EPS) * (zz * gcos + partner * gsin))

    @pl.when(j == 2)
    def _():
        for c0 in range(0, A_HEADS, 2):
            emit(c0, z[:, c0 * HEAD_DIM:c0 * HEAD_DIM + pair])

    if dilation > 1:
        rows = z_scr.shape[1] // dilation
        for r in range(dilation):
            for c in range(A_HEADS):
                lo = r * A_WIDTH + c * HEAD_DIM
                o_ref[:, lo:lo + HEAD_DIM] = z_scr[c, pl.ds(r, rows, stride=dilation), :].astype(o_ref.dtype)


def rotary_gains(q_gain, k_gain):
    half = ROT_DIM // 2
    lane = jnp.arange(HEAD_DIM)
    partner = jnp.where(lane < half, lane + half, jnp.where(lane < ROT_DIM, lane - half, lane))
    return jnp.stack([jnp.stack([g, g[partner]]) for g in (q_gain, k_gain)])


def project_qkv(h, w_all, group, n_groups, gains, cos_t, sin_t, dilation, tm=2048):
    n, k = h.shape
    d = dilation
    return pl.pallas_call(
        functools.partial(_qkv_proj_kernel, dilation=d),
        grid=(3, n // tm),
        in_specs=[pl.BlockSpec((tm, k), lambda j, i: (i, 0)),
                  pl.BlockSpec((k, A_WIDTH), lambda j, i: (0, j * n_groups + group)),
                  pl.BlockSpec((1, 2, HEAD_DIM), lambda j, i: (jnp.minimum(j, 1), 0, 0)),
                  pl.BlockSpec((tm, LANES), lambda j, i: (i, 0)),
                  pl.BlockSpec((tm, LANES), lambda j, i: (i, 0))],
        out_specs=pl.BlockSpec((tm // d, d * A_WIDTH), lambda j, i: (i, j)),
        out_shape=jax.ShapeDtypeStruct((n // d, 3 * d * A_WIDTH), BF16),
        scratch_shapes=[pltpu.VMEM((k, A_WIDTH), BF16), pltpu.VMEM((A_HEADS, tm, HEAD_DIM), F32)],
        compiler_params=_cparams(("arbitrary", "arbitrary")),
        name=f"proj_qkv_d{d}",
    )(h, w_all, gains, cos_t, sin_t)


def _band_attn_kernel(*refs, qb):
    q_ref = refs[0]
    k_refs = refs[1:qb + 2]
    v_refs = refs[qb + 2:2 * qb + 3]
    o_ref, lse_ref = refs[-2:]
    step = pl.program_id(2)
    scale = HEAD_DIM ** -0.5
    nq = ATT_BLOCK
    qi = lax.broadcasted_iota(jnp.int32, (nq, 2 * nq), 0)
    ki = lax.broadcasted_iota(jnp.int32, (nq, 2 * nq), 1)
    rel = qi + nq - ki
    band = (rel >= 0) & (rel <= nq)
    for a in range(qb):
        blk = step * qb + a
        valid = band & ((blk * nq - nq + ki) >= 0)
        rows = slice(a * nq, (a + 1) * nq)
        lses = []
        for h in range(A_HEADS):
            sl = slice(h * HEAD_DIM, (h + 1) * HEAD_DIM)
            qh = q_ref[0, rows, sl]
            kh = jnp.concatenate([k_refs[a][0, :, sl], k_refs[a + 1][0, :, sl]], axis=0)
            vh = jnp.concatenate([v_refs[a][0, :, sl], v_refs[a + 1][0, :, sl]], axis=0)
            s = lax.dot_general(qh, kh, (((1,), (1,)), ((), ())), preferred_element_type=F32) * scale
            s = jnp.where(valid, s, NEG_BIG)
            m = jnp.max(s, axis=-1, keepdims=True)
            p = jnp.exp(s - m)
            l = jnp.sum(p, axis=-1, keepdims=True)
            o = jnp.dot(p.astype(BF16), vh, preferred_element_type=F32) / l
            o_ref[0, rows, sl] = o.astype(o_ref.dtype)
            lses.append(jnp.broadcast_to(m + jnp.log(l), (nq, LANES // A_HEADS)))
        lse_ref[0, rows, :] = jnp.concatenate(lses, axis=1)


def band_attention_group(qkv, b, g, dilation):
    d = dilation
    sub = qkv.shape[0] // b
    nblk = sub // ATT_BLOCK
    qb = min(ATT_QB, nblk)
    assert nblk % qb == 0
    view = qkv.reshape(b, sub, 3 * d * A_WIDTH)
    qrows = qb * ATT_BLOCK
    key_spec = lambda t, m: pl.BlockSpec(
        (1, ATT_BLOCK, A_WIDTH), lambda bi, r, j: (bi, jnp.maximum(j * qb - 1 + m, 0), t * d + r))
    o, lse = pl.pallas_call(
        functools.partial(_band_attn_kernel, qb=qb),
        grid=(b, d, nblk // qb),
        in_specs=[pl.BlockSpec((1, qrows, A_WIDTH), lambda bi, r, j: (bi, j, r))]
                 + [key_spec(1, m) for m in range(qb + 1)] + [key_spec(2, m) for m in range(qb + 1)],
        out_specs=[pl.BlockSpec((1, qrows, A_WIDTH), lambda bi, r, j: (bi, j, r)),
                   pl.BlockSpec((1, qrows, LANES), lambda bi, r, j: (bi, j, r))],
        out_shape=[jax.ShapeDtypeStruct((b, sub, d * A_WIDTH), BF16),
                   jax.ShapeDtypeStruct((b, sub, d * LANES), F32)],
        compiler_params=_cparams(("arbitrary", "arbitrary", "arbitrary")),
        name=f"band_attn_g{g}",
    )(*([view] * (2 * qb + 3)))
    return o.reshape(b * sub, d * A_WIDTH), lse.reshape(b * sub, d * LANES)


def _cross_attn_kernel(q_ref, k_ref, v_ref, o_ref):
    scale = HEAD_DIM ** -0.5
    for h in range(C_HEADS):
        sl = slice(h * HEAD_DIM, (h + 1) * HEAD_DIM)
        s = lax.dot_general(q_ref[0, :, sl], k_ref[0, :, sl], (((1,), (1,)), ((), ())),
                            preferred_element_type=F32) * scale
        m = jnp.max(s, axis=-1, keepdims=True)
        p = jnp.exp(s - m)
        l = jnp.sum(p, axis=-1, keepdims=True)
        o = jnp.dot(p.astype(BF16), v_ref[0, :, sl], preferred_element_type=F32) / l
        o_ref[0, :, sl] = o.astype(o_ref.dtype)


def cross_attention(qn, kn, v, tm=512):
    b, s, w = qn.shape
    m = kn.shape[1]
    return pl.pallas_call(
        _cross_attn_kernel,
        grid=(b, s // tm),
        in_specs=[pl.BlockSpec((1, tm, w), lambda bi, i: (bi, i, 0)),
                  pl.BlockSpec((1, m, w), lambda bi, i: (bi, 0, 0)),
                  pl.BlockSpec((1, m, w), lambda bi, i: (bi, 0, 0))],
        out_specs=pl.BlockSpec((1, tm, w), lambda bi, i: (bi, i, 0)),
        out_shape=jax.ShapeDtypeStruct((b, s, w), BF16),
        compiler_params=_cparams(("arbitrary", "arbitrary")),
        name="cross_attn",
    )(qn, kn, v)


def _head_sums(x, seg):
    w = seg.shape[0]
    x16 = x.astype(BF16)
    return jnp.concatenate(
        [jnp.dot(x16[:, j:j + w], seg, preferred_element_type=F32) for j in range(0, x.shape[1], w)], axis=1)


def _dot_t(a, b):
    return lax.dot_general(a, b, (((0,), (0,)), ((), ())), preferred_element_type=F32)


def _dot_nt(a, b):
    return lax.dot_general(a, b, (((1,), (1,)), ((), ())), preferred_element_type=F32)


def _rwkv_kernel(zr_ref, mu_ref, w0_ref, wwa_ref, a0_ref, g2_ref, kk_ref, ka_ref, rk_ref,
                 lnw_ref, lnb_ref, seg_ref, y_ref,
                 state_ref, carry_ref, ops_ref, yh_ref):
    t = pl.program_id(1)
    tt = zr_ref.shape[1]
    nch = tt // R_CHUNK
    c = R_CHUNK

    @pl.when(t == 0)
    def _():
        state_ref[...] = jnp.zeros_like(state_ref)
        carry_ref[...] = jnp.zeros_like(carry_ref)

    z = zr_ref[0]
    row = lax.broadcasted_iota(jnp.int32, z.shape, 0)
    prev = jnp.where(row == 0, carry_ref[...], pltpu.roll(z, 1, 0))
    carry_ref[...] = z[tt - 1:tt, :]
    xs = z + (prev - z) * mu_ref[...]

    w3 = 3 * R_WIDTH
    r = xs[:, 0:R_WIDTH]
    k = xs[:, R_WIDTH:2 * R_WIDTH]
    v = xs[:, 2 * R_WIDTH:w3]
    wa_lo = xs[:, w3:w3 + LANES]
    g_lo = xs[:, w3 + LANES:w3 + 2 * LANES]
    lane = lax.broadcasted_iota(jnp.int32, wa_lo.shape, 1)
    wa_in = jnp.where(lane < R_DECAY_LORA, jnp.tanh(wa_lo), wa_lo)
    wa = jnp.dot(wa_in.astype(BF16), wwa_ref[...], preferred_element_type=F32)
    u = -(w0_ref[...] + wa[:, :R_WIDTH])
    softplus = jnp.maximum(u, 0.0) + jnp.log(1.0 + jnp.exp(-jnp.abs(u)))
    w_raw = -softplus - 0.5
    ld = -jnp.exp(w_raw)
    a = jax.nn.sigmoid(a0_ref[...] + wa[:, R_WIDTH:])
    g = jnp.dot(jax.nn.sigmoid(g_lo).astype(BF16), g2_ref[...], preferred_element_type=F32)

    seg = seg_ref[...]
    kk = k * kk_ref[...]
    kk = kk * jnp.minimum(lax.rsqrt(_head_sums(kk * kk, seg)), 1e12)
    k2 = k * (1.0 + (a - 1.0) * ka_ref[...])
    bonus = _head_sums(r * k2 * rk_ref[...], seg) * v

    ri = lax.broadcasted_iota(jnp.int32, (c, c), 0)
    ci = lax.broadcasted_iota(jnp.int32, (c, c), 1)
    tri = (ci <= ri).astype(BF16)
    ld_hi = ld.astype(BF16)
    ld_lo = (ld - ld_hi.astype(F32)).astype(BF16)
    lcs = []
    for ch in range(nch):
        rs = slice(ch * c, (ch + 1) * c)
        lcs.append(jnp.dot(tri, ld_hi[rs], preferred_element_type=F32)
                   + jnp.dot(tri, ld_lo[rs], preferred_element_type=F32))
    lc = jnp.concatenate(lcs, axis=0)
    e_inc = jnp.exp(lc)
    e_exc = jnp.exp(lc - ld)
    e_inv = jnp.exp(-lc)
    a_t = -kk * e_exc
    r_t = r * e_inc
    b_t = kk * a * e_inv
    k_t = k2 * e_inv
    for h in range(R_HEADS):
        hs = slice(h * R_HEAD, (h + 1) * R_HEAD)
        ops_ref[0, h] = a_t[:, hs]
        ops_ref[1, h] = r_t[:, hs]
        ops_ref[2, h] = b_t[:, hs]
        ops_ref[3, h] = k_t[:, hs]
        ops_ref[4, h] = v[:, hs]
        ops_ref[5, h] = e_inc[:, hs]

    strict = ci < ri
    incl = ci <= ri
    eye = (ci == ri)

    ri2 = lax.broadcasted_iota(jnp.int32, (c, 2 * c), 0)
    ci2 = lax.broadcasted_iota(jnp.int32, (c, 2 * c), 1)
    incl2 = jnp.bitwise_and(ci2, c - 1) <= ri2
    eye_f = jnp.where(eye, 1.0, 0.0)
    heads = range(R_HEADS)
    dot = functools.partial(jnp.dot, preferred_element_type=F32)

    def chunk_body(ch, _):
        starts = [pl.multiple_of((ch * R_CPI + sub) * c, c) for sub in range(R_CPI)]
        rows = [pl.ds(r0, c) for r0 in starts]
        items = [(sub, h) for sub in range(R_CPI) for h in heads]
        idx = range(len(items))
        at = [ops_ref[0, h, rows[sub], :] for sub, h in items]
        rt = [ops_ref[1, h, rows[sub], :] for sub, h in items]
        bt = [ops_ref[2, h, rows[sub], :] for sub, h in items]
        kt = [ops_ref[3, h, rows[sub], :] for sub, h in items]
        pc = [ops_ref[5, h, pl.ds(starts[sub] + c - 1, 1), :] for sub, h in items]
        at16 = [x.astype(BF16) for x in at]
        rt16 = [x.astype(BF16) for x in rt]
        bt16 = [x.astype(BF16) for x in bt]
        kt16 = [x.astype(BF16) for x in kt]
        v16 = [ops_ref[4, h, rows[sub], :].astype(BF16) for sub, h in items]
        bk16 = [jnp.concatenate([bt16[i], kt16[i]], axis=0) for i in idx]
        nmat = [jnp.where(strict, _dot_nt(at16[i], bt16[i]), 0.0) for i in idx]
        a_ak = [jnp.where(strict, _dot_nt(at16[i], kt16[i]), 0.0).astype(BF16) for i in idx]
        a_rbk = [jnp.where(incl2, _dot_nt(rt16[i], bk16[i]), 0.0).astype(BF16) for i in idx]
        npow = nmat
        tinv = [eye_f + nmat[i] for i in idx]
        for _i in range(5):
            np16 = [x.astype(BF16) for x in npow]
            npow = [dot(np16[i], np16[i]) for i in idx]
            tinv = [tinv[i] + dot(tinv[i].astype(BF16), npow[i].astype(BF16)) for i in idx]
        akv = [dot(a_ak[i], v16[i]).astype(BF16) for i in idx]
        apw1 = [dot(tinv[i].astype(BF16), jnp.concatenate([at16[i], akv[i]], axis=1)).astype(BF16)
                for i in idx]
        zero = jnp.zeros((c, R_HEAD), BF16)
        rhs2 = [jnp.concatenate([apw1[i], jnp.concatenate([zero, v16[i]], axis=1)], axis=0)
                for i in idx]
        bkh = [jnp.concatenate([bt[i] * pc[i], kt[i] * pc[i]], axis=0).astype(BF16) for i in idx]
        gh = [_dot_t(bkh[i], rhs2[i]) for i in idx]
        qy = [dot(a_rbk[i], rhs2[i]) for i in idx]
        for i, (sub, h) in enumerate(items):
            gm = jnp.where(eye, jnp.broadcast_to(pc[i], (c, c)), 0.0) + gh[i][:, :R_HEAD]
            qp = rt[i] + qy[i][:, :R_HEAD]
            st = state_ref[h]
            res = dot(jnp.concatenate([qp, gm], axis=0).astype(BF16), st.astype(BF16))
            yh_ref[h, rows[sub], :] = res[:c] + qy[i][:, R_HEAD:]
            state_ref[h] = res[c:] + gh[i][:, R_HEAD:]
        return 0

    lax.fori_loop(0, nch // R_CPI, chunk_body, 0)

    y = jnp.concatenate([yh_ref[h] for h in range(R_HEADS)], axis=1)
    mean = _head_sums(y, seg) * (1.0 / R_HEAD)
    dlt = y - mean
    var = _head_sums(dlt * dlt, seg) * (1.0 / R_HEAD)
    yn = dlt * lax.rsqrt(var + R_GN_EPS) * lnw_ref[...] + lnb_ref[...]
    y_ref[0] = ((yn + bonus) * g).astype(y_ref.dtype)


def rwkv7_mix(zr, mu, w0, w2, a0, a2, g2, k_k, k_a, r_k, ln_w, ln_b, tt=256):
    b, s, cols = zr.shape
    row = lambda x: x.reshape(1, -1).astype(F32)
    wwa = jnp.zeros((LANES, 2 * R_WIDTH), F32)
    wwa = wwa.at[:R_DECAY_LORA, :R_WIDTH].set(w2).at[R_DECAY_LORA:, R_WIDTH:].set(a2).astype(BF16)
    hid = jnp.arange(R_SEG) // R_HEAD
    seg = (hid[:, None] == hid[None, :]).astype(BF16)
    full = lambda shape: pl.BlockSpec(shape, lambda bi, t: (0,) * len(shape))
    return pl.pallas_call(
        _rwkv_kernel,
        grid=(b, s // tt),
        in_specs=[pl.BlockSpec((1, tt, cols), lambda bi, t: (bi, t, 0)),
                  full((1, cols)), full((1, R_WIDTH)), full((LANES, 2 * R_WIDTH)), full((1, R_WIDTH)),
                  full((R_GATE_LORA, R_WIDTH)), full((1, R_WIDTH)), full((1, R_WIDTH)), full((1, R_WIDTH)),
                  full((1, R_WIDTH)), full((1, R_WIDTH)), full((R_SEG, R_SEG))],
        out_specs=pl.BlockSpec((1, tt, R_WIDTH), lambda bi, t: (bi, t, 0)),
        out_shape=jax.ShapeDtypeStruct((b, s, R_WIDTH), BF16),
        scratch_shapes=[pltpu.VMEM((R_HEADS, R_HEAD, R_HEAD), F32),
                        pltpu.VMEM((1, cols), F32),
                        pltpu.VMEM((6, R_HEADS, tt, R_HEAD), F32),
                        pltpu.VMEM((R_HEADS, tt, R_HEAD), F32)],
        compiler_params=_cparams(("arbitrary", "arbitrary")),
        name="rwkv7_mix",
    )(zr, row(mu), row(w0), wwa, row(a0), g2.astype(BF16), row(k_k), row(k_a), row(r_k),
      row(ln_w), row(ln_b), seg)


def _merge_kernel(x_ref, o0_ref, o1_ref, o2_ref, l0_ref, l1_ref, l2_ref, yb_ref, yc_ref, gt_ref,
                  wb_ref, wo_ref, gn_ref, rw_ref, rb_ref,
                  x1_ref, h2_ref, idx_ref, gate_ref, rank_ref, cnt_ref, base_ref, o_scr, l_scr):
    @pl.when(pl.program_id(0) == 0)
    def _():
        base_ref[...] = jnp.zeros_like(base_ref)

    tm_rows = x_ref.shape[0]
    for gi, (o_ref, l_ref) in enumerate(((o0_ref, l0_ref), (o1_ref, l1_ref), (o2_ref, l2_ref))):
        dil = A_GROUPS[gi][1]
        for r in range(dil):
            dst = pl.ds(r, tm_rows // dil, stride=dil) if dil > 1 else slice(None)
            for h in range(A_HEADS):
                lo = r * A_WIDTH + h * HEAD_DIM
                o_scr[gi, h, dst, :] = o_ref[:, lo:lo + HEAD_DIM].astype(F32)
            l_scr[gi, dst, :] = l_ref[:, r * LANES:(r + 1) * LANES]
    lses = [l_scr[gi] for gi in range(3)]
    lmax = jnp.maximum(jnp.maximum(lses[0], lses[1]), lses[2])
    es = [jnp.exp(l - lmax) for l in lses]
    inv = 1.0 / (es[0] + es[1] + es[2])
    qw = LANES // A_HEADS
    heads = []
    for h in range(A_HEADS):
        acc = None
        for gi in range(3):
            alpha = (es[gi] * inv)[:, h * qw:h * qw + 1]
            term = alpha * o_scr[gi, h]
            acc = term if acc is None else acc + term
        heads.append(acc)
    ya = jnp.concatenate(heads, axis=1).astype(BF16)
    d = x_ref.shape[1]
    merged = None
    for n, yn in enumerate((ya, yb_ref[...], yc_ref[...])):
        proj = jnp.dot(yn, wb_ref[n], preferred_element_type=F32)
        term = gt_ref[:, n * d:(n + 1) * d].astype(F32) * proj
        merged = term if merged is None else merged + term
    x1 = x_ref[...] + jnp.dot(merged.astype(BF16), wo_ref[...], preferred_element_type=F32)
    x1_ref[...] = x1
    ms = jnp.mean(x1 * x1, axis=-1, keepdims=True)
    h2 = x1 * lax.rsqrt(ms + NORM_EPS) * gn_ref[...]
    h2_ref[...] = h2
    rw = rw_ref[...]
    h2_hi, rw_hi = h2.astype(BF16), rw.astype(BF16)
    h2_lo = (h2 - h2_hi.astype(F32)).astype(BF16)
    rw_lo = (rw - rw_hi.astype(F32)).astype(BF16)
    logits = (jnp.dot(h2_hi, rw_hi, preferred_element_type=F32) + jnp.dot(h2_hi, rw_lo, preferred_element_type=F32)
              + jnp.dot(h2_lo, rw_hi, preferred_element_type=F32)) + rb_ref[...]
    tm = logits.shape[0]
    lane = lax.broadcasted_iota(jnp.int32, logits.shape, 1)
    vals, idxs = [], []
    cur = logits
    for _k in range(TOP_K):
        m = jnp.max(cur, axis=-1, keepdims=True)
        ik = jnp.min(jnp.where(cur == m, lane, N_EXPERTS), axis=-1, keepdims=True)
        vals.append(m)
        idxs.append(ik)
        cur = jnp.where(lane == ik, -jnp.inf, cur)
    exps = [jnp.exp(vk - vals[0]) for vk in vals]
    tot = exps[0] + exps[1] + exps[2] + exps[3]
    onehots = [lane == ik for ik in idxs]
    hits = sum(jnp.where(oh, 1.0, 0.0) for oh in onehots)
    ri = lax.broadcasted_iota(jnp.int32, (tm, tm), 0)
    ci = lax.broadcasted_iota(jnp.int32, (tm, tm), 1)
    before = jnp.dot((ci < ri).astype(BF16), hits.astype(BF16), preferred_element_type=F32) + base_ref[...]
    for kk in range(TOP_K):
        idx_ref[:, kk:kk + 1] = idxs[kk]
        gate_ref[:, kk:kk + 1] = exps[kk] / tot
        rank_ref[:, kk:kk + 1] = jnp.sum(jnp.where(onehots[kk], before, 0.0), axis=-1,
                                         keepdims=True).astype(jnp.int32)
    base_ref[...] = base_ref[...] + jnp.sum(hits, axis=0, keepdims=True)
    cnt_ref[...] = base_ref[...].astype(jnp.int32)


def merge_and_route(x2d, outs, lses, yb, yc, gates, w_branch, w_out, norm_ffn, router_w, router_b, tm=512):
    n, d = x2d.shape
    rows = lambda w: pl.BlockSpec((tm, w), lambda i: (i, 0))
    packed = lambda w, dil: pl.BlockSpec((tm // dil, dil * w), lambda i: (i, 0))
    full = lambda shape: pl.BlockSpec(shape, lambda i: (0,) * len(shape))
    return pl.pallas_call(
        _merge_kernel,
        grid=(n // tm,),
        in_specs=[rows(d)] + [packed(A_WIDTH, dil) for _w, dil in A_GROUPS] + [packed(LANES, dil) for _w, dil in A_GROUPS]
                 + [rows(R_WIDTH), rows(C_WIDTH), rows(N_BRANCH * d),
                    full((N_BRANCH, A_WIDTH, d)), full((d, d)), full((1, d)), full((d, N_EXPERTS)), full((1, N_EXPERTS))],
        out_specs=[rows(d), rows(d), rows(TOP_K), rows(TOP_K), rows(TOP_K), full((1, N_EXPERTS))],
        out_shape=[jax.ShapeDtypeStruct((n, d), F32), jax.ShapeDtypeStruct((n, d), F32),
                   jax.ShapeDtypeStruct((n, TOP_K), jnp.int32), jax.ShapeDtypeStruct((n, TOP_K), F32),
                   jax.ShapeDtypeStruct((n, TOP_K), jnp.int32), jax.ShapeDtypeStruct((1, N_EXPERTS), jnp.int32)],
        scratch_shapes=[pltpu.VMEM((1, N_EXPERTS), F32), pltpu.VMEM((len(A_GROUPS), A_HEADS, tm, HEAD_DIM), F32),
                        pltpu.VMEM((len(A_GROUPS), tm, LANES), F32)],
        compiler_params=_cparams(("arbitrary",)),
        name="merge_route",
    )(x2d, *outs, *lses, yb, yc, gates, w_branch.astype(BF16), w_out.astype(BF16),
      norm_ffn.reshape(1, d), router_w, router_b.reshape(1, N_EXPERTS))


def block_layout(counts, n_assign):
    counts = counts.reshape(-1)
    padded = (counts + MOE_ROWS - 1) // MOE_ROWS * MOE_ROWS
    pad_end = jnp.cumsum(padded)
    pad_start = (pad_end - padded).astype(jnp.int32)
    n_blocks = -(-n_assign // MOE_ROWS) + N_EXPERTS
    blk_row = jnp.arange(n_blocks, dtype=jnp.int32) * MOE_ROWS
    owner = jnp.sum((blk_row[:, None] >= pad_end[None, :]).astype(jnp.int32), axis=1)
    block_expert = jnp.minimum(owner, N_EXPERTS - 1).astype(jnp.int32)
    unused = blk_row >= pad_end[-1]
    zero_flag = (unused | (blk_row + MOE_ROWS == pad_end[block_expert])).astype(jnp.int32)
    n_used = (pad_end[-1:] // MOE_ROWS).astype(jnp.int32)
    has_rows = counts > 0
    eid = jnp.arange(N_EXPERTS, dtype=jnp.int32)
    later = jnp.where(has_rows[None, :] & (eid[None, :] > eid[:, None]), eid[None, :], N_EXPERTS)
    next_expert = jnp.min(later, axis=1)
    next_expert = jnp.where(next_expert == N_EXPERTS, -1, next_expert).astype(jnp.int32)
    run_parity = ((jnp.cumsum(has_rows) - has_rows) % 2).astype(jnp.int32)
    return pad_start, block_expert, zero_flag, n_used, next_expert, run_parity


def _dest_kernel(ps_ref, idx_ref, rank_ref, dest_ref):
    idx = idx_ref[...]
    dest = rank_ref[...]
    for e in range(N_EXPERTS):
        dest = dest + jnp.where(idx == e, ps_ref[e], 0)
    dest_ref[...] = dest


def assignment_rows(top_idx, rank, pad_start):
    n = top_idx.shape[0]
    rows = n * TOP_K // LANES
    flat = lambda t: t.reshape(rows, LANES)
    spec = pl.BlockSpec((rows, LANES), lambda i, ps: (0, 0))
    out = pl.pallas_call(
        _dest_kernel,
        grid_spec=pltpu.PrefetchScalarGridSpec(num_scalar_prefetch=1, grid=(1,), in_specs=[spec, spec],
                                               out_specs=spec),
        out_shape=jax.ShapeDtypeStruct((rows, LANES), jnp.int32),
        compiler_params=_cparams(("arbitrary",)),
        name="assignment_rows",
    )(pad_start, flat(top_idx), flat(rank))
    return out.reshape(n, TOP_K)


def _scatter_kernel(zf_ref, dest_ref, h2_ref, xs_hbm, zeros_ref, sem, zsem):
    i = pl.program_id(0)
    tiles = h2_ref.shape[0]
    tm = tiles * SUBLANES
    nblk = zf_ref.shape[0]

    def zero_block(j):
        return pltpu.make_async_copy(zeros_ref, xs_hbm.at[pl.ds(j * MOE_ROWS, MOE_ROWS)], zsem)

    @pl.when(i == 0)
    def _():
        zeros_ref[...] = jnp.zeros_like(zeros_ref)

        def start(j, _):
            @pl.when(zf_ref[j] != 0)
            def _():
                zero_block(j).start()
            return 0

        def wait(j, _):
            @pl.when(zf_ref[j] != 0)
            def _():
                zero_block(j).wait()
            return 0

        lax.fori_loop(0, nblk, start, 0)
        lax.fori_loop(0, nblk, wait, 0)

    def body(g, _):
        for u in range(SUBLANES):
            for kk in range(TOP_K):
                row = dest_ref[0, 0, (g * SUBLANES + u) * TOP_K + kk]
                pltpu.make_async_copy(h2_ref.at[g, pl.ds(u, 1)], xs_hbm.at[pl.ds(row, 1)], sem).start(
                    priority=kk % 2)
        return 0

    lax.fori_loop(0, tiles, body, 0)
    pltpu.make_async_copy(xs_hbm.at[pl.ds(0, tm * TOP_K)], xs_hbm.at[pl.ds(0, tm * TOP_K)], sem).wait()


def scatter_rows(h2, dest, zero_flag, n_rows, tm=1024):
    n, w = h2.shape
    dest3 = dest.reshape(n // tm, 1, tm * TOP_K)
    grid_spec = pltpu.PrefetchScalarGridSpec(
        num_scalar_prefetch=1,
        grid=(n // tm,),
        in_specs=[pl.BlockSpec((1, 1, tm * TOP_K), lambda i, zf: (i, 0, 0), memory_space=pltpu.SMEM),
                  pl.BlockSpec((tm // SUBLANES, SUBLANES, w), lambda i, zf: (i, 0, 0))],
        out_specs=pl.BlockSpec(memory_space=pl.ANY),
        scratch_shapes=[pltpu.VMEM((MOE_ROWS, w), h2.dtype), pltpu.SemaphoreType.DMA(()),
                        pltpu.SemaphoreType.DMA(())],
    )
    return pl.pallas_call(
        _scatter_kernel,
        grid_spec=grid_spec,
        out_shape=jax.ShapeDtypeStruct((n_rows, w), h2.dtype),
        compiler_params=_cparams(("arbitrary",)),
        name="scatter_rows",
    )(zero_flag, dest3, h2.reshape(n // SUBLANES, SUBLANES, w))


def _expert_kernel(be_ref, nu_ref, nxt_ref, par_ref, xs_ref, w1_hbm, b1_ref, w2_hbm, b2_ref, y_ref,
                   w1_f32, w2_f32, w1_scr, w2_scr, wsem):
    i = pl.program_id(0)
    e = be_ref[i]
    prev = be_ref[jnp.maximum(i - 1, 0)]

    def fetch(expert, slot):
        return (pltpu.make_async_copy(w1_hbm.at[expert], w1_f32.at[slot], wsem.at[slot]),
                pltpu.make_async_copy(w2_hbm.at[expert], w2_f32.at[slot], wsem.at[slot]))

    @pl.when((i < nu_ref[0]) & ((i == 0) | (e != prev)))
    def _():
        slot = par_ref[e]

        @pl.when(i == 0)
        def _():
            for c in fetch(e, slot):
                c.start()

        for c in fetch(e, slot):
            c.wait()

        @pl.when(nxt_ref[e] >= 0)
        def _():
            for c in fetch(nxt_ref[e], 1 - slot):
                c.start()

        w1_scr[...] = w1_f32[slot].astype(BF16)
        w2_scr[...] = w2_f32[slot].astype(BF16)

    @pl.when(i < nu_ref[0])
    def _():
        xb = xs_ref[...].astype(BF16)
        dff = w2_scr.shape[0]
        u = jnp.dot(xb, w1_scr[...], preferred_element_type=F32) + b1_ref[0]
        glu = jnp.minimum(u[:, :dff], SWIGLU_LIMIT)
        lin = jnp.clip(u[:, dff:], -SWIGLU_LIMIT, SWIGLU_LIMIT)
        act = glu * jax.nn.sigmoid(SWIGLU_ALPHA * glu) * (lin + 1.0)
        y = jnp.dot(act.astype(BF16), w2_scr[...], preferred_element_type=F32) + b2_ref[0]
        y_ref[...] = y

    @pl.when(i >= nu_ref[0])
    def _():
        y_ref[...] = jnp.zeros_like(y_ref)


def expert_ffn(x_sorted, block_expert, n_used, next_expert, run_parity, w1, b1, w2, b2):
    n_rows, d = x_sorted.shape
    w = d
    nblk = n_rows // MOE_ROWS
    dff2 = w1.shape[2]
    used = lambda i, nu: jnp.minimum(i, nu[0] - 1)
    grid_spec = pltpu.PrefetchScalarGridSpec(
        num_scalar_prefetch=4,
        grid=(nblk,),
        in_specs=[
            pl.BlockSpec((MOE_ROWS, w), lambda i, be, nu, nx, pa: (used(i, nu), 0)),
            pl.BlockSpec(memory_space=pl.ANY),
            pl.BlockSpec((1, 1, dff2), lambda i, be, nu, nx, pa: (be[used(i, nu)], 0, 0)),
            pl.BlockSpec(memory_space=pl.ANY),
            pl.BlockSpec((1, 1, d), lambda i, be, nu, nx, pa: (be[used(i, nu)], 0, 0)),
        ],
        out_specs=pl.BlockSpec((MOE_ROWS, w), lambda i, be, nu, nx, pa: (i, 0)),
        scratch_shapes=[pltpu.VMEM((2, d, dff2), F32), pltpu.VMEM((2, dff2 // 2, d), F32),
                        pltpu.VMEM((d, dff2), BF16), pltpu.VMEM((dff2 // 2, d), BF16),
                        pltpu.SemaphoreType.DMA((2,))],
    )
    return pl.pallas_call(
        _expert_kernel,
        grid_spec=grid_spec,
        out_shape=jax.ShapeDtypeStruct((n_rows, w), F32),
        compiler_params=_cparams(("arbitrary",)),
        name="expert_ffn",
    )(block_expert, n_used, next_expert, run_parity, x_sorted, w1, b1.reshape(N_EXPERTS, 1, dff2), w2,
      b2.reshape(N_EXPERTS, 1, d))


def _gather_assigned_rows(y_hbm, dest_ref, dst_ref, sem, tm):
    tiles = tm // SUBLANES

    def body(g, _):
        for u in range(SUBLANES):
            for kk in range(TOP_K):
                row = dest_ref[(g * SUBLANES + u) * TOP_K + kk]
                pltpu.make_async_copy(y_hbm.at[pl.ds(row, 1)], dst_ref.at[kk * tiles + g, pl.ds(u, 1)],
                                      sem).start(priority=kk % 2)
        return 0
    lax.fori_loop(0, tiles, body, 0)


def _combine_kernel(d0_ref, dn_ref, x1_ref, gate_ref, y_hbm, y_tiles_hbm, o_ref, ybuf, sems):
    i = pl.program_id(0)
    nblk = pl.num_programs(0)
    slot = lax.rem(i, 2)
    tm = o_ref.shape[0]
    tiles = tm // SUBLANES

    @pl.when(i == 0)
    def _():
        _gather_assigned_rows(y_hbm, d0_ref.at[0, 0], ybuf.at[0], sems.at[0], tm)

    @pl.when(i + 1 < nblk)
    def _():
        _gather_assigned_rows(y_hbm, dn_ref.at[0, 0], ybuf.at[1 - slot], sems.at[1 - slot], tm)

    pltpu.make_async_copy(y_tiles_hbm.at[pl.ds(0, TOP_K * tiles)], ybuf.at[slot], sems.at[slot]).wait()
    acc = x1_ref[...]
    for kk in range(TOP_K):
        rows = ybuf[slot, pl.ds(kk * tiles, tiles)].reshape(tm, ybuf.shape[-1])
        acc = acc + gate_ref[:, kk:kk + 1] * rows
    o_ref[...] = acc


def moe_combine(x1, gate, y_sorted, dest, tm=512):
    n, d = x1.shape
    n_rows, w = y_sorted.shape
    nblk = n // tm
    dest3 = dest.reshape(nblk, 1, tm * TOP_K)
    return pl.pallas_call(
        _combine_kernel,
        grid=(nblk,),
        in_specs=[pl.BlockSpec((1, 1, TOP_K * tm), lambda i: (0, 0, 0), memory_space=pltpu.SMEM),
                  pl.BlockSpec((1, 1, TOP_K * tm), lambda i: (jnp.minimum(i + 1, nblk - 1), 0, 0),
                               memory_space=pltpu.SMEM),
                  pl.BlockSpec((tm, d), lambda i: (i, 0)),
                  pl.BlockSpec((tm, TOP_K), lambda i: (i, 0)),
                  pl.BlockSpec(memory_space=pl.ANY),
                  pl.BlockSpec(memory_space=pl.ANY)],
        out_specs=pl.BlockSpec((tm, d), lambda i: (i, 0)),
        out_shape=jax.ShapeDtypeStruct((n, d), F32),
        scratch_shapes=[pltpu.VMEM((2, TOP_K * tm // SUBLANES, SUBLANES, w), y_sorted.dtype),
                        pltpu.SemaphoreType.DMA((2,))],
        compiler_params=_cparams(("arbitrary",)),
        name="moe_combine",
    )(dest3, dest3, x1, gate, y_sorted, y_sorted.reshape(n_rows // SUBLANES, SUBLANES, w))


def kernel(x, mem, positions, norm_mix, w_in, b_gate, a_q_gain, a_k_gain, r_mu, r_w0, r_w2, r_a0, r_a2,
           r_g2, r_k_k, r_k_a, r_r_k, r_ln_w, r_ln_b, mem_norm, w_mem_kv, c_q_gain, c_k_gain, w_branch,
           w_out, norm_ffn, router_w, router_b, exp_w1, exp_b1, exp_w2, exp_b2):
    b, s, d = x.shape
    n = b * s
    depth = norm_mix.shape[0]
    n_groups = len(A_GROUPS)
    qkv_cols = n_groups * A_WIDTH
    off_k, off_v, off_r = qkv_cols, 2 * qkv_cols, 3 * qkv_cols
    off_cq = off_r + R_COLS
    off_gate = off_cq + C_WIDTH

    x2d = x.reshape(n, d)
    pos_col = positions.reshape(n, 1).astype(jnp.int32)
    for l in range(depth):
        w_l = w_in[l]
        if l == 0:
            h, cos_t, sin_t = rmsnorm_rows(x2d, norm_mix[l], pos_col)
        else:
            h = rmsnorm_rows(x2d, norm_mix[l])
        qk_gains = rotary_gains(a_q_gain[l], a_k_gain[l])
        zr = project(h, w_l[:, off_r:off_cq], "plain", out_dtype=F32, tn=R_COLS // 2)
        cq = project(h, w_l[:, off_cq:off_gate], "headnorm", (c_q_gain[l].reshape(1, HEAD_DIM),))
        gates = project(h, w_l, "gate", (b_gate[l].reshape(1, -1),), tn=3 * MXU_COLS,
                        col0=off_gate, ncols=N_BRANCH * d)

        shp = lambda t: t.reshape(b, s, -1)
        outs, lses = [], []
        for g, (window, dilation) in enumerate(A_GROUPS):
            assert window // dilation == ATT_BLOCK
            qkv = project_qkv(h, w_l, g, n_groups, qk_gains, cos_t, sin_t, dilation)
            o, lse = band_attention_group(qkv, b, g, dilation)
            outs.append(o)
            lses.append(lse)

        yb = rwkv7_mix(shp(zr), r_mu[l], r_w0[l], r_w2[l], r_a0[l], r_a2[l], r_g2[l], r_k_k[l], r_k_a[l],
                       r_r_k[l].reshape(-1), r_ln_w[l], r_ln_b[l]).reshape(n, R_WIDTH)

        mlen = mem.shape[1]
        mem_n = rmsnorm_rows(mem.reshape(b * mlen, d), mem_norm[l])
        wkv = w_mem_kv[l]
        ck = project(mem_n, wkv[:, :C_WIDTH], "headnorm", (c_k_gain[l].reshape(1, HEAD_DIM),))
        cv = project(mem_n, wkv[:, C_WIDTH:], "plain")
        yc = cross_attention(shp(cq), ck.reshape(b, mlen, C_WIDTH), cv.reshape(b, mlen, C_WIDTH)).reshape(n, C_WIDTH)

        x1, h2p, top_idx, gate, rank, counts = merge_and_route(
            x2d, outs, lses, yb, yc, gates, w_branch[l], w_out[l], norm_ffn[l], router_w[l], router_b[l])
        pad_start, block_expert, zero_flag, n_used, next_expert, run_parity = block_layout(counts, n * TOP_K)
        dest = assignment_rows(top_idx, rank, pad_start)
        x_sorted = scatter_rows(h2p, dest, zero_flag, block_expert.shape[0] * MOE_ROWS)
        y_sorted = expert_ffn(x_sorted, block_expert, n_used, next_expert, run_parity,
                              exp_w1[l], exp_b1[l], exp_w2[l], exp_b2[l])
        x2d = moe_combine(x1, gate, y_sorted, dest)
    return x2d.reshape(b, s, d)
```
